```python
import math
import jax
import jax.numpy as jnp
from jax import lax
import numpy as np

D_MODEL = 1024
BATCH = 2
SEQ = 8192
DEPTH = 1
DEC_BATCH = 32
DEC_SEQ = 8
PAST_LEN = 8192
PAGE_SIZE = 128

SSD_HEADDIM = 64
SSD_INNER = D_MODEL
SSD_HEADS = SSD_INNER // SSD_HEADDIM
SSD_GROUPS = 2
SSD_STATE = 128
SSD_CONV = 4
SSD_CHUNK = 128
SSD_BC = SSD_GROUPS * SSD_STATE
CONV_DIM = SSD_INNER + 2 * SSD_BC
ATTN_HEADS = 8
ATTN_KV_HEADS = 4
HEAD_DIM = 64
ATTN_Q = ATTN_HEADS * HEAD_DIM
ATTN_KV = ATTN_KV_HEADS * HEAD_DIM
ROT_DIM = HEAD_DIM // 4
ROPE_THETA = 500000.0
MOBA_BLOCK = 256
MOBA_TOPK = 3
Q_BLOCK = 128
N_EGROUPS = 4
EXPERTS_PER_GROUP = 4
N_EXPERTS = N_EGROUPS * EXPERTS_PER_GROUP
EXPERT_TOPK = 2
D_EXPERT = D_MODEL // 2
IN_SIZES = (SSD_INNER, CONV_DIM, SSD_HEADS, ATTN_Q, ATTN_KV, ATTN_KV, D_MODEL, D_MODEL)
IN_DIM = SSD_INNER + CONV_DIM + SSD_HEADS + ATTN_Q + 2 * ATTN_KV + 2 * D_MODEL
EPS = 1e-6
F32 = jnp.float32

kernel_name = 'hybrid_ssd_moba_hmoe_decode_step'


def _split_points():
    pts, acc = [], 0
    for s in IN_SIZES[:-1]:
        acc += s
        pts.append(acc)
    return pts


def rmsnorm(x, w):
    xf = x.astype(F32)
    y = xf * lax.rsqrt(jnp.mean(xf * xf, axis=-1, keepdims=True) + EPS)
    return (y * w.astype(F32)).astype(x.dtype)


def rope(x, pos):
    half = ROT_DIM // 2
    inv_freq = jnp.exp(jnp.arange(half, dtype=F32) * (-2.0 * math.log(ROPE_THETA) / ROT_DIM))
    ang = pos.astype(F32)[:, None] * inv_freq[None, :]
    cos = jnp.cos(ang)[None, :, None, :]
    sin = jnp.sin(ang)[None, :, None, :]
    xf = x.astype(F32)
    x1 = xf[..., :half]
    x2 = xf[..., half:ROT_DIM]
    out = jnp.concatenate([x1 * cos - x2 * sin, x2 * cos + x1 * sin, xf[..., ROT_DIM:]], axis=-1)
    return out.astype(x.dtype)


def segsum(a):
    t = a.shape[-1]
    xr = jnp.broadcast_to(a[..., :, None], a.shape + (t,))
    strict = jnp.tril(jnp.ones((t, t), dtype=bool), -1)
    cs = jnp.cumsum(jnp.where(strict, xr, 0.0), axis=-2)
    return jnp.where(jnp.tril(jnp.ones((t, t), dtype=bool)), cs, -jnp.inf)


def ssd_scan(x, dt, a_neg, bm, cm, init_state):
    bsz, seqlen, nh, hd = x.shape
    e = nh // SSD_GROUPS
    cl = SSD_CHUNK if seqlen % SSD_CHUNK == 0 else seqlen
    nc = seqlen // cl
    xc = (x * dt[..., None]).reshape(bsz, nc, cl, SSD_GROUPS, e, hd)
    ac = (dt * a_neg).reshape(bsz, nc, cl, SSD_GROUPS, e).transpose(0, 3, 4, 1, 2)
    bc = bm.reshape(bsz, nc, cl, SSD_GROUPS, SSD_STATE)
    cc = cm.reshape(bsz, nc, cl, SSD_GROUPS, SSD_STATE)
    a_cs = jnp.cumsum(ac, axis=-1)
    l_mat = jnp.exp(segsum(ac))
    y_diag = jnp.einsum('bclgn,bcsgn,bgecls,bcsgep->bclgep', cc, bc, l_mat, xc)
    decay_states = jnp.exp(a_cs[..., -1:] - a_cs)
    states = jnp.einsum('bclgn,bgecl,bclgep->bcgepn', bc, decay_states, xc)
    init = init_state.reshape(bsz, 1, SSD_GROUPS, e, hd, SSD_STATE)
    states = jnp.concatenate([init, states], axis=1)
    decay_chunk = jnp.exp(segsum(jnp.pad(a_cs[..., -1], ((0, 0), (0, 0), (0, 0), (1, 0)))))
    new_states = jnp.einsum('bgezc,bcgepn->bzgepn', decay_chunk, states)
    y_off = jnp.einsum('bclgn,bcgepn,bgecl->bclgep', cc, new_states[:, :-1], jnp.exp(a_cs))
    y = (y_diag + y_off).reshape(bsz, seqlen, nh, hd)
    return y, new_states[:, -1].reshape(bsz, nh, hd, SSD_STATE)


def ssd_mixer(xbc_raw, z, dt_raw, conv_buf, ssm_state, conv_w, conv_b, dt_bias, a_log, d_skip, norm_w):
    bsz, seqlen, _ = xbc_raw.shape
    xbc_full = jnp.concatenate([conv_buf.astype(xbc_raw.dtype), xbc_raw], axis=1)
    conv_new = xbc_full[:, seqlen:]
    conv = lax.conv_general_dilated(xbc_full, conv_w.astype(xbc_raw.dtype)[:, None, :],
                                    window_strides=(1,), padding='VALID',
                                    dimension_numbers=('NWC', 'WIO', 'NWC'),
                                    feature_group_count=CONV_DIM)
    xbc = jax.nn.silu(conv.astype(F32) + conv_b.astype(F32))
    xs = xbc[..., :SSD_INNER].reshape(bsz, seqlen, SSD_HEADS, SSD_HEADDIM)
    bm = xbc[..., SSD_INNER:SSD_INNER + SSD_BC].reshape(bsz, seqlen, SSD_GROUPS, SSD_STATE)
    cm = xbc[..., SSD_INNER + SSD_BC:].reshape(bsz, seqlen, SSD_GROUPS, SSD_STATE)
    dt = jax.nn.softplus(dt_raw.astype(F32) + dt_bias.astype(F32))
    a_neg = -jnp.exp(a_log.astype(F32))
    y, state_new = ssd_scan(xs, dt, a_neg, bm, cm, ssm_state.astype(F32))
    y = (y + xs * d_skip.astype(F32)[:, None]).reshape(bsz, seqlen, SSD_INNER)
    y = y * jax.nn.silu(z.astype(F32))
    yg = y.reshape(bsz, seqlen, SSD_GROUPS, SSD_INNER // SSD_GROUPS)
    yg = yg * lax.rsqrt(jnp.mean(yg * yg, axis=-1, keepdims=True) + EPS)
    y = yg.reshape(bsz, seqlen, SSD_INNER) * norm_w.astype(F32)
    return y.astype(xbc_raw.dtype), conv_new, state_new.astype(ssm_state.dtype)


def moba_attention(q, k, v, q_pos):
    bsz, nq_tot = q.shape[0], q.shape[1]
    klen = k.shape[1]
    grp = ATTN_HEADS // ATTN_KV_HEADS
    nb = -(-klen // MOBA_BLOCK)
    pad = nb * MOBA_BLOCK - klen
    kb = jnp.pad(k, ((0, 0), (0, pad), (0, 0), (0, 0))).reshape(bsz, nb, MOBA_BLOCK, ATTN_KV_HEADS, HEAD_DIM)
    vb = jnp.pad(v, ((0, 0), (0, pad), (0, 0), (0, 0))).reshape(bsz, nb, MOBA_BLOCK, ATTN_KV_HEADS, HEAD_DIM)
    kmean = jnp.mean(kb.astype(F32), axis=2)
    qg = q.reshape(bsz, nq_tot, ATTN_KV_HEADS, grp, HEAD_DIM)
    score = jnp.einsum('bqkgd,bnkd->bqkgn', qg.astype(F32), kmean)
    own = q_pos // MOBA_BLOCK
    past_blk = (jnp.arange(nb)[None, :] < own[:, None])[None, :, None, None, :]
    score = jnp.where(past_blk, score, -jnp.inf)
    k_sel = min(MOBA_TOPK, max(nb - 1, 1))
    _, top_idx = lax.top_k(score, k_sel)
    own_idx = jnp.broadcast_to(own[None, :, None, None, None], top_idx.shape[:-1] + (1,)).astype(top_idx.dtype)
    blk_idx = jnp.concatenate([top_idx, own_idx], axis=-1)
    n_sel = k_sel + 1
    kbt = kb.transpose(0, 3, 1, 2, 4)
    vbt = vb.transpose(0, 3, 1, 2, 4)
    b_ix = jnp.arange(bsz)[:, None, None, None, None]
    h_ix = jnp.arange(ATTN_KV_HEADS)[None, None, :, None, None]
    is_past = (jnp.arange(n_sel) < k_sel)[:, None]
    scale = HEAD_DIM ** -0.5

    def attend(args):
        q_c, idx_c, pos_c = args
        kg = kbt[b_ix, h_ix, idx_c]
        vg = vbt[b_ix, h_ix, idx_c]
        logits = jnp.einsum('bqkgd,bqkgjsd->bqkgjs', q_c, kg).astype(F32) * scale
        key_pos = idx_c[..., None] * MOBA_BLOCK + jnp.arange(MOBA_BLOCK)
        past_ok = (jnp.arange(n_sel)[None, :] < (pos_c // MOBA_BLOCK)[:, None])[None, :, None, None, :, None]
        causal_ok = key_pos <= pos_c[None, :, None, None, None, None]
        mask = jnp.where(is_past, past_ok, causal_ok)
        probs = jax.nn.softmax(jnp.where(mask, logits, -jnp.inf), axis=(-2, -1))
        return jnp.einsum('bqkgjs,bqkgjsd->bqkgd', probs.astype(vg.dtype), vg)

    qc = Q_BLOCK if nq_tot % Q_BLOCK == 0 else nq_tot
    nchunks = nq_tot // qc
    q_ch = qg.reshape(bsz, nchunks, qc, ATTN_KV_HEADS, grp, HEAD_DIM).swapaxes(0, 1)
    idx_ch = blk_idx.reshape(bsz, nchunks, qc, ATTN_KV_HEADS, grp, n_sel).swapaxes(0, 1)
    pos_ch = q_pos.reshape(nchunks, qc)
    out = lax.map(attend, (q_ch, idx_ch, pos_ch))
    return out.swapaxes(0, 1).reshape(bsz, nq_tot, ATTN_Q)


def hier_moe(h, w_group, b_group, w_router, b_router, w_gate_e, w_up_e, w_down_e):
    shp = h.shape
    ht = h.reshape(-1, D_MODEL)
    g_prob = jax.nn.softmax((ht @ w_group).astype(F32) + b_group.astype(F32), axis=-1)
    g_idx = jnp.argmax(g_prob, axis=-1)
    g_w = jnp.max(g_prob, axis=-1)
    e_logits = ((ht @ w_router).astype(F32) + b_router.astype(F32)).reshape(-1, N_EGROUPS, EXPERTS_PER_GROUP)
    in_grp = jnp.take_along_axis(e_logits, g_idx[:, None, None], axis=1)[:, 0]
    top_v, top_i = lax.top_k(in_grp, EXPERT_TOPK)
    w_e = jax.nn.softmax(top_v, axis=-1) * g_w[:, None]
    e_id = g_idx[:, None] * EXPERTS_PER_GROUP + top_i
    combine = jnp.sum(jax.nn.one_hot(e_id, N_EXPERTS, dtype=F32) * w_e[..., None], axis=1)
    a = jnp.einsum('td,edf->tef', ht, w_gate_e)
    u = jnp.einsum('td,edf->tef', ht, w_up_e)
    act = (jax.nn.silu(a.astype(F32)) * u.astype(F32) * combine[:, :, None]).astype(ht.dtype)
    out = jnp.einsum('tef,efd->td', act, w_down_e)
    return out.reshape(shp)


def decoder_layer(x, c, conv_buf, ssm_state, k_past, v_past, lp):
    bsz, seqlen, _ = x.shape
    past = 0 if k_past is None else k_past.shape[1]
    pos = past + jnp.arange(seqlen, dtype=jnp.int32)
    mod = jax.nn.silu(c.astype(F32)) @ lp['w_ada'].astype(F32) + lp['b_ada'].astype(F32)
    sh1, sc1, gt1, sh2, sc2, gt2 = jnp.split(mod[:, None, :], 6, axis=-1)
    h = (rmsnorm(x, lp['norm1_w']).astype(F32) * (1.0 + sc1) + sh1).astype(x.dtype)
    proj = h @ lp['w_in']
    z, xbc, dt_raw, q, k, v, g_s, g_a = jnp.split(proj, _split_points(), axis=-1)
    y_ssd, conv_new, ssm_new = ssd_mixer(xbc, z, dt_raw, conv_buf, ssm_state, lp['conv_w'], lp['conv_b'],
                                         lp['dt_bias'], lp['a_log'], lp['d_skip'], lp['ssd_norm_w'])
    q = rope(q.reshape(bsz, seqlen, ATTN_HEADS, HEAD_DIM), pos)
    k = rope(k.reshape(bsz, seqlen, ATTN_KV_HEADS, HEAD_DIM), pos)
    v = v.reshape(bsz, seqlen, ATTN_KV_HEADS, HEAD_DIM)
    if k_past is None:
        k_all, v_all = k, v
    else:
        k_all = jnp.concatenate([k_past.astype(k.dtype), k], axis=1)
        v_all = jnp.concatenate([v_past.astype(v.dtype), v], axis=1)
    y_att = moba_attention(q, k_all, v_all, pos)
    merged = (jax.nn.sigmoid(g_s.astype(F32)) * (y_ssd @ lp['w_ssd_out']).astype(F32)
              + jax.nn.sigmoid(g_a.astype(F32)) * (y_att @ lp['w_attn_out']).astype(F32))
    mix = merged.astype(x.dtype) @ lp['w_o']
    x = (x.astype(F32) + gt1 * mix.astype(F32)).astype(x.dtype)
    h2 = (rmsnorm(x, lp['norm2_w']).astype(F32) * (1.0 + sc2) + sh2).astype(x.dtype)
    ffn = hier_moe(h2, lp['w_group'], lp['b_group'], lp['w_router'], lp['b_router'],
                   lp['w_gate_e'], lp['w_up_e'], lp['w_down_e'])
    x = (x.astype(F32) + gt2 * ffn.astype(F32)).astype(x.dtype)
    return x, k, v, conv_new, ssm_new


def setup_inputs(seed: int = 0) -> dict:
    key = jax.random.key(seed)
    ks = jax.random.split(key, 40)
    n_pages = PAST_LEN // PAGE_SIZE
    n_pool = (DEC_BATCH * n_pages * 5) // 4

    def nrm(k, shape, s):
        return jax.random.normal(k, shape, F32) * s

    dt0 = jnp.exp(jax.random.uniform(ks[15], (DEPTH, SSD_HEADS), F32,
                                     math.log(1e-3), math.log(1e-1)))
    return {
        'x_prompt': nrm(ks[0], (BATCH, SEQ, D_MODEL), 1.0),
        'x_sample': nrm(ks[1], (DEC_BATCH, DEC_SEQ, D_MODEL), 1.0),
        'cache_k': nrm(ks[2], (DEPTH, n_pool, PAGE_SIZE, ATTN_KV_HEADS, HEAD_DIM), 1.0),
        'cache_v': nrm(ks[3], (DEPTH, n_pool, PAGE_SIZE, ATTN_KV_HEADS, HEAD_DIM), 1.0),
        'state_conv': nrm(ks[4], (DEPTH, DEC_BATCH, SSD_CONV - 1, CONV_DIM), 1.0),
        'state_ssm': nrm(ks[5], (DEPTH, DEC_BATCH, SSD_HEADS, SSD_HEADDIM, SSD_STATE), 0.1),
        'page_table': jax.random.permutation(ks[6], n_pool)[:DEC_BATCH * n_pages]
                      .reshape(DEC_BATCH, n_pages).astype(jnp.int32),
        'c_prompt': nrm(ks[7], (BATCH, D_MODEL), 1.0),
        'c_sample': nrm(ks[8], (DEC_BATCH, D_MODEL), 1.0),
        'w_ada': nrm(ks[9], (DEPTH, D_MODEL, 6 * D_MODEL), 0.5 * D_MODEL ** -0.5),
        'b_ada': nrm(ks[10], (DEPTH, 6 * D_MODEL), 0.02),
        'norm1_w': 1.0 + nrm(ks[11], (DEPTH, D_MODEL), 0.05),
        'w_in': nrm(ks[12], (DEPTH, D_MODEL, IN_DIM), D_MODEL ** -0.5),
        'conv_w': nrm(ks[13], (DEPTH, SSD_CONV, CONV_DIM), SSD_CONV ** -0.5),
        'conv_b': nrm(ks[14], (DEPTH, CONV_DIM), 0.02),
        'dt_bias': dt0 + jnp.log(-jnp.expm1(-dt0)),
        'a_log': jnp.log(jax.random.uniform(ks[16], (DEPTH, SSD_HEADS), F32, 1.0, 16.0)),
        'd_skip': 1.0 + nrm(ks[17], (DEPTH, SSD_HEADS), 0.1),
        'ssd_norm_w': 1.0 + nrm(ks[18], (DEPTH, SSD_INNER), 0.05),
        'w_ssd_out': nrm(ks[19], (DEPTH, SSD_INNER, D_MODEL), SSD_INNER ** -0.5),
        'w_attn_out': nrm(ks[20], (DEPTH, ATTN_Q, D_MODEL), ATTN_Q ** -0.5),
        'w_o': nrm(ks[21], (DEPTH, D_MODEL, D_MODEL), D_MODEL ** -0.5),
        'norm2_w': 1.0 + nrm(ks[22], (DEPTH, D_MODEL), 0.05),
        'w_group': nrm(ks[23], (DEPTH, D_MODEL, N_EGROUPS), D_MODEL ** -0.5),
        'b_group': nrm(ks[24], (DEPTH, N_EGROUPS), 0.01),
        'w_router': nrm(ks[25], (DEPTH, D_MODEL, N_EXPERTS), D_MODEL ** -0.5),
        'b_router': nrm(ks[26], (DEPTH, N_EXPERTS), 0.01),
        'w_gate_e': nrm(ks[27], (DEPTH, N_EXPERTS, D_MODEL, D_EXPERT), D_MODEL ** -0.5),
        'w_up_e': nrm(ks[28], (DEPTH, N_EXPERTS, D_MODEL, D_EXPERT), D_MODEL ** -0.5),
        'w_down_e': nrm(ks[29], (DEPTH, N_EXPERTS, D_EXPERT, D_MODEL), D_EXPERT ** -0.5),
        'final_w': 1.0 + nrm(ks[30], (D_MODEL,), 0.05),
    }


def reference(x_prompt, x_sample, cache_k, cache_v, state_conv, state_ssm, page_table, c_prompt, c_sample,
              w_ada, b_ada, norm1_w, w_in, conv_w, conv_b, dt_bias, a_log, d_skip, ssd_norm_w,
              w_ssd_out, w_attn_out, w_o, norm2_w, w_group, b_group, w_router, b_router,
              w_gate_e, w_up_e, w_down_e, final_w):
    dec_b, n_pages = page_table.shape
    xp, xs = x_prompt, x_sample
    kp_l, vp_l, cp_l, sp_l = [], [], [], []
    ks_l, vs_l, cs_l, ss_l = [], [], [], []
    for l in range(DEPTH):
        lp = dict(w_ada=w_ada[l], b_ada=b_ada[l], norm1_w=norm1_w[l], w_in=w_in[l], conv_w=conv_w[l],
                  conv_b=conv_b[l], dt_bias=dt_bias[l], a_log=a_log[l], d_skip=d_skip[l],
                  ssd_norm_w=ssd_norm_w[l], w_ssd_out=w_ssd_out[l], w_attn_out=w_attn_out[l], w_o=w_o[l],
                  norm2_w=norm2_w[l], w_group=w_group[l], b_group=b_group[l], w_router=w_router[l],
                  b_router=b_router[l], w_gate_e=w_gate_e[l], w_up_e=w_up_e[l], w_down_e=w_down_e[l])
        k_past = cache_k[l][page_table].reshape(dec_b, n_pages * PAGE_SIZE, ATTN_KV_HEADS, HEAD_DIM)
        v_past = cache_v[l][page_table].reshape(dec_b, n_pages * PAGE_SIZE, ATTN_KV_HEADS, HEAD_DIM)
        conv0 = jnp.zeros((xp.shape[0], SSD_CONV - 1, CONV_DIM), xp.dtype)
        ssm0 = jnp.zeros((xp.shape[0], SSD_HEADS, SSD_HEADDIM, SSD_STATE), state_ssm.dtype)
        xp, kp, vp, cp, sp = decoder_layer(xp, c_prompt, conv0, ssm0, None, None, lp)
        xs, ks, vs, cs, ss = decoder_layer(xs, c_sample, state_conv[l], state_ssm[l], k_past, v_past, lp)
        kp_l.append(kp)
        vp_l.append(vp)
        cp_l.append(cp)
        sp_l.append(sp)
        ks_l.append(ks)
        vs_l.append(vs)
        cs_l.append(cs)
        ss_l.append(ss)
    y_prompt = rmsnorm(xp, final_w)
    y_sample = rmsnorm(xs, final_w)
    k_prompt = jnp.stack(kp_l)
    v_prompt = jnp.stack(vp_l)
    conv_prompt = jnp.stack(cp_l)
    ssm_prompt = jnp.stack(sp_l)
    k_sample = jnp.stack(ks_l)
    v_sample = jnp.stack(vs_l)
    conv_sample = jnp.stack(cs_l)
    ssm_sample = jnp.stack(ss_l)
    return (y_prompt, y_sample, k_prompt, v_prompt, conv_prompt, ssm_prompt, k_sample, v_sample, conv_sample, ssm_sample)
```

```python
import functools
import math

import numpy as np
import jax
import jax.numpy as jnp
from jax import lax
from jax.experimental import pallas as pl
from jax.experimental.pallas import tpu as pltpu

F32 = jnp.float32
BF16 = jnp.bfloat16
HIGHEST = lax.Precision.HIGHEST

D_MODEL = 1024
SSD_HEADDIM = 64
SSD_INNER = D_MODEL
SSD_HEADS = SSD_INNER // SSD_HEADDIM
SSD_GROUPS = 2
SSD_STATE = 128
SSD_CONV = 4
SSD_CHUNK = 128
SSD_BC = SSD_GROUPS * SSD_STATE
CONV_DIM = SSD_INNER + 2 * SSD_BC
ATTN_HEADS = 8
ATTN_KV_HEADS = 4
HEAD_DIM = 64
ATTN_Q = ATTN_HEADS * HEAD_DIM
ATTN_KV = ATTN_KV_HEADS * HEAD_DIM
ROT_DIM = HEAD_DIM // 4
ROPE_THETA = 500000.0
MOBA_BLOCK = 256
MOBA_TOPK = 3
N_EGROUPS = 4
EXPERTS_PER_GROUP = 4
N_EXPERTS = N_EGROUPS * EXPERTS_PER_GROUP
D_EXPERT = D_MODEL // 2
EPS = 1e-6

LANES = 128
SUBLANES = 8
VMEM_CAP = 56 << 20
NEG = -1e30
GRP = ATTN_HEADS // ATTN_KV_HEADS
ROUTE_OFF = N_EGROUPS

_SEG = {}
_off = 0
for _name, _w in (("z", SSD_INNER), ("xbc", CONV_DIM), ("q", ATTN_Q), ("k", ATTN_KV), ("v", ATTN_KV),
                  ("gs", D_MODEL), ("ga", D_MODEL), ("dt", LANES)):
    _SEG[_name] = (_off, _off + _w)
    _off += _w
IN_PAD = _off


def _cparams(semantics, vmem_bytes):
    return pltpu.CompilerParams(dimension_semantics=semantics,
                                vmem_limit_bytes=int(min(max(vmem_bytes, 16 << 20), VMEM_CAP)))


def _dot(a, b):
    return jnp.dot(a, b, preferred_element_type=F32)


def _dot_nt(a, b):
    return lax.dot_general(a, b, (((1,), (1,)), ((), ())), preferred_element_type=F32)


def _dot_exact(a, b):
    return jnp.dot(a, b, preferred_element_type=F32, precision=HIGHEST)


def _silu(x):
    return x * jax.nn.sigmoid(x)


def _mod_kernel(c_ref, w_ref, b_ref, o_ref):
    o_ref[...] = _dot_exact(_silu(c_ref[...]), w_ref[...]) + b_ref[...]


def _ada_mod(c_all, w_ada, b_ada):
    rows = c_all.shape[0]
    return pl.pallas_call(
        _mod_kernel,
        grid=(6,),
        in_specs=[pl.BlockSpec((rows, D_MODEL), lambda j: (0, 0)),
                  pl.BlockSpec((D_MODEL, D_MODEL), lambda j: (0, j)),
                  pl.BlockSpec((1, D_MODEL), lambda j: (0, j))],
        out_specs=pl.BlockSpec((rows, D_MODEL), lambda j: (0, j)),
        out_shape=jax.ShapeDtypeStruct((rows, 6 * D_MODEL), F32),
        compiler_params=_cparams(("arbitrary",), 24 << 20),
        name="ada_mod",
    )(c_all, w_ada, b_ada.reshape(1, -1))


def _rope_tables(pos):
    half = ROT_DIM // 2
    c = -2.0 * math.log(ROPE_THETA) / ROT_DIM
    c_hi = float(np.float32(c))
    j = jnp.arange(half, dtype=F32)
    inv_freq = jnp.exp(j * c_hi + j * (c - c_hi))
    ang = jnp.asarray(pos).astype(F32)[:, None] * inv_freq[None, :]
    cos, sin = jnp.cos(ang), jnp.sin(ang)
    n = pos.shape[0]
    one = jnp.ones((n, HEAD_DIM - ROT_DIM), F32)
    zero = jnp.zeros((n, half), F32)
    rest = jnp.zeros((n, HEAD_DIM - ROT_DIM), F32)
    a = jnp.concatenate([cos, cos, one], axis=1)
    s1 = jnp.concatenate([-sin, zero, rest], axis=1)
    s2 = jnp.concatenate([zero, sin, rest], axis=1)
    rep = LANES // HEAD_DIM
    return tuple(jnp.tile(t, (1, rep)) for t in (a, s1, s2))


def _inproj_kernel(*refs, attn_layout):
    (x_ref, sc_ref, sh_ref, nw_ref, w_ref, ra_ref, rs1_ref, rs2_ref,
     z_ref, xbc_ref, dt_ref, q_ref, k_ref, v_ref, gs_ref, ga_ref) = refs[:16]
    x = x_ref[0]
    h = x * lax.rsqrt(jnp.mean(x * x, axis=-1, keepdims=True) + EPS) * nw_ref[...]
    h = (h * (1.0 + sc_ref[0]) + sh_ref[0]).astype(BF16)

    def proj(name, out_ref):
        lo, hi = _SEG[name]
        step = 512
        for c in range(lo, hi, step):
            w = min(step, hi - c)
            out_ref[0, :, c - lo:c - lo + w] = _dot(h, w_ref[:, c:c + w])

    proj("z", z_ref)
    proj("xbc", xbc_ref)
    proj("gs", gs_ref)
    proj("ga", ga_ref)
    proj("dt", dt_ref)

    def rope(t):
        width = t.shape[1]
        reps = width // LANES
        a = jnp.concatenate([ra_ref[...]] * reps, axis=1)
        s1 = jnp.concatenate([rs1_ref[...]] * reps, axis=1)
        s2 = jnp.concatenate([rs2_ref[...]] * reps, axis=1)
        return t * a + pltpu.roll(t, width - ROT_DIM // 2, 1) * s1 + pltpu.roll(t, ROT_DIM // 2, 1) * s2

    lo, hi = _SEG["q"]
    q_ref[0] = rope(_dot(h, w_ref[:, lo:hi]))
    lo, hi = _SEG["k"]
    k = rope(_dot(h, w_ref[:, lo:hi]))
    k_ref[0] = k
    lo, hi = _SEG["v"]
    v = _dot(h, w_ref[:, lo:hi])
    v_ref[0] = v

    if attn_layout:
        kmean_ref, kaug_ref, vaug_ref = refs[16:19]
        tm = k.shape[0]
        kmean_ref[0, 0] = jnp.mean(k, axis=0, keepdims=True)
        lane = lax.broadcasted_iota(jnp.int32, (tm, HEAD_DIM), 1)
        blk_onehot = (lane == pl.program_id(1)).astype(F32)
        ones_col = (lane == 0).astype(F32)
        for kh in range(ATTN_KV_HEADS):
            sl = slice(kh * HEAD_DIM, (kh + 1) * HEAD_DIM)
            kaug_ref[0, kh] = jnp.concatenate([k[:, sl], blk_onehot], axis=1).astype(BF16)
            vaug_ref[0, kh] = jnp.concatenate([v[:, sl], ones_col], axis=1).astype(BF16)


def _in_proj(x, sc1, sh1, norm_w, w_in_r, tables, attn_layout):
    bsz, seqlen, _ = x.shape
    tm = MOBA_BLOCK
    assert seqlen % tm == 0
    nt = seqlen // tm
    mod_rows = sc1.shape[1]
    if mod_rows == 1:
        mod_spec = pl.BlockSpec((1, 1, D_MODEL), lambda b, i: (b, 0, 0))
    else:
        mod_spec = pl.BlockSpec((1, tm, D_MODEL), lambda b, i: (b, i, 0))
    tab_spec = pl.BlockSpec((tm, LANES), lambda b, i: (i, 0))

    def tok(width):
        return pl.BlockSpec((1, tm, width), lambda b, i: (b, i, 0))

    def shp(width):
        return jax.ShapeDtypeStruct((bsz, seqlen, width), F32)

    widths = (SSD_INNER, CONV_DIM, LANES, ATTN_Q, ATTN_KV, ATTN_KV, D_MODEL, D_MODEL)
    out_specs = [tok(w) for w in widths]
    out_shape = [shp(w) for w in widths]
    if attn_layout:
        assert nt <= HEAD_DIM
        out_specs += [pl.BlockSpec((1, 1, 1, ATTN_KV), lambda b, i: (b, i, 0, 0)),
                      pl.BlockSpec((1, ATTN_KV_HEADS, tm, LANES), lambda b, i: (b, 0, i, 0)),
                      pl.BlockSpec((1, ATTN_KV_HEADS, tm, LANES), lambda b, i: (b, 0, i, 0))]
        out_shape += [jax.ShapeDtypeStruct((bsz, nt, 1, ATTN_KV), F32),
                      jax.ShapeDtypeStruct((bsz, ATTN_KV_HEADS, seqlen, LANES), BF16),
                      jax.ShapeDtypeStruct((bsz, ATTN_KV_HEADS, seqlen, LANES), BF16)]
    vmem = 2 * (D_MODEL * IN_PAD * 2) + 2 * tm * (D_MODEL + IN_PAD + 4 * LANES) * 4 + (8 << 20)
    return pl.pallas_call(
        functools.partial(_inproj_kernel, attn_layout=attn_layout),
        grid=(bsz, nt),
        in_specs=[tok(D_MODEL), mod_spec, mod_spec,
                  pl.BlockSpec((1, D_MODEL), lambda b, i: (0, 0)),
                  pl.BlockSpec((D_MODEL, IN_PAD), lambda b, i: (0, 0)),
                  tab_spec, tab_spec, tab_spec],
        out_specs=out_specs,
        out_shape=out_shape,
        compiler_params=_cparams(("parallel", "parallel"), vmem),
        name="in_proj",
    )(x, sc1, sh1, norm_w.reshape(1, -1), w_in_r, *tables)


def _cumsum_rows(a):
    row = lax.broadcasted_iota(jnp.int32, a.shape, 0)
    s = 1
    while s < a.shape[0]:
        a = a + jnp.where(row >= s, pltpu.roll(a, s, 0), 0.0)
        s *= 2
    return a


def _ssd_kernel(xbc_ref, z_ref, dt_ref, cbuf_ref, s0_ref, cw_ref, cb_ref, dtb_ref, alog_ref, dsk_ref, nw_ref,
                y_ref, cnew_ref, sout_ref, xext, st, ysc, *, rows):
    ch = SSD_CHUNK
    pad = SUBLANES
    c = pl.program_id(1)
    last = pl.num_programs(1) - 1

    @pl.when(c == 0)
    def _():
        xext[0:pad, :] = jnp.zeros((pad, CONV_DIM), F32)
        xext[pad - (SSD_CONV - 1):pad, :] = cbuf_ref[0]
        st[...] = s0_ref[0].T

    xext[pad:pad + rows, :] = xbc_ref[0]
    if rows < ch:
        xext[pad + rows:pad + ch, :] = jnp.zeros((ch - rows, CONV_DIM), F32)

    conv = cb_ref[...] + xext[pad:pad + ch, :] * cw_ref[SSD_CONV - 1:SSD_CONV, :]
    for w in range(SSD_CONV - 1):
        off = pad - (SSD_CONV - 1) + w
        conv = conv + xext[off:off + ch, :] * cw_ref[w:w + 1, :]
    act = _silu(conv)

    @pl.when(c == last)
    def _():
        cnew_ref[0] = xext[pad + rows - (SSD_CONV - 1):pad + rows, :]

    xext[0:pad, :] = xext[ch:ch + pad, :]

    xs = act[:, :SSD_INNER]
    dt_raw = dt_ref[0]
    if rows < ch:
        dt_raw = jnp.concatenate([dt_raw, jnp.zeros((ch - rows, LANES), F32)], axis=0)
    dt_in = dt_raw + dtb_ref[...]
    dt = jnp.maximum(dt_in, 0.0) + jnp.log1p(jnp.exp(-jnp.abs(dt_in)))
    row = lax.broadcasted_iota(jnp.int32, (ch, LANES), 0)
    col = lax.broadcasted_iota(jnp.int32, (ch, LANES), 1)
    if rows < ch:
        dt = jnp.where(row < rows, dt, 0.0)
    a_cs = _cumsum_rows(dt * (-jnp.exp(alog_ref[...])))
    a_cs_t = a_cs.T
    dt_t = dt.T
    causal = row >= col

    for g in range(SSD_GROUPS):
        bg = act[:, SSD_INNER + g * SSD_STATE:SSD_INNER + (g + 1) * SSD_STATE]
        cg = act[:, SSD_INNER + SSD_BC + g * SSD_STATE:SSD_INNER + SSD_BC + (g + 1) * SSD_STATE]
        gmat = _dot_nt(cg.astype(BF16), bg.astype(BF16))
        bg_t = bg.T
        for e in range(SSD_HEADS // SSD_GROUPS):
            hd = g * (SSD_HEADS // SSD_GROUPS) + e
            sl = slice(hd * SSD_HEADDIM, (hd + 1) * SSD_HEADDIM)
            acs_col = a_cs[:, hd:hd + 1]
            acs_row = a_cs_t[hd:hd + 1, :]
            dt_row = dt_t[hd:hd + 1, :]
            acs_last = a_cs_t[hd:hd + 1, ch - 1:ch]
            lmat = jnp.where(causal, jnp.exp(acs_col - acs_row), 0.0)
            m = (gmat * lmat * dt_row).astype(BF16)
            x_h = xs[:, sl].astype(BF16)
            s_h = st[:, sl]
            c_exp = (cg * jnp.exp(acs_col)).astype(BF16)
            ysc[:, sl] = _dot(m, x_h) + _dot(c_exp, s_h.astype(BF16))
            b_w = (bg_t * (jnp.exp(acs_last - acs_row) * dt_row)).astype(BF16)
            st[:, sl] = jnp.exp(acs_last) * s_h + _dot(b_w, x_h)

    y = (ysc[0:rows, :] + xs[:rows] * dsk_ref[...]) * _silu(z_ref[0])
    gw = SSD_INNER // SSD_GROUPS
    for g in range(SSD_GROUPS):
        yg = y[:, g * gw:(g + 1) * gw]
        yg = yg * lax.rsqrt(jnp.mean(yg * yg, axis=-1, keepdims=True) + EPS)
        y_ref[0, :, g * gw:(g + 1) * gw] = yg * nw_ref[:, g * gw:(g + 1) * gw]

    @pl.when(c == last)
    def _():
        sout_ref[0] = st[...].T


def _ssd(xbc, z, dt, conv_buf, ssm_state, conv_w, conv_b, dt_bias, a_log, d_skip, norm_w):
    bsz, seqlen, _ = xbc.shape
    rows = SSD_CHUNK if seqlen % SSD_CHUNK == 0 else seqlen
    assert rows == seqlen or rows == SSD_CHUNK
    assert rows % SUBLANES == 0 and rows >= SSD_CONV - 1
    nc = seqlen // rows
    hp = SSD_HEADS * SSD_HEADDIM

    def tok(width):
        return pl.BlockSpec((1, rows, width), lambda b, c: (b, c, 0))

    def per_b(d1, d2):
        return pl.BlockSpec((1, d1, d2), lambda b, c: (b, 0, 0))

    def vec(width, r=1):
        return pl.BlockSpec((r, width), lambda b, c: (0, 0))

    def pad_lanes(v):
        return jnp.pad(v.astype(F32), (0, LANES - v.shape[0])).reshape(1, LANES)

    y, conv_new, s_new = pl.pallas_call(
        functools.partial(_ssd_kernel, rows=rows),
        grid=(bsz, nc),
        in_specs=[tok(CONV_DIM), tok(SSD_INNER), tok(LANES),
                  per_b(SSD_CONV - 1, CONV_DIM), per_b(hp, SSD_STATE),
                  vec(CONV_DIM, SSD_CONV), vec(CONV_DIM), vec(LANES), vec(LANES), vec(SSD_INNER), vec(SSD_INNER)],
        out_specs=[tok(SSD_INNER), per_b(SSD_CONV - 1, CONV_DIM), per_b(hp, SSD_STATE)],
        out_shape=[jax.ShapeDtypeStruct((bsz, seqlen, SSD_INNER), F32),
                   jax.ShapeDtypeStruct((bsz, SSD_CONV - 1, CONV_DIM), F32),
                   jax.ShapeDtypeStruct((bsz, hp, SSD_STATE), F32)],
        scratch_shapes=[pltpu.VMEM((SSD_CHUNK + SUBLANES, CONV_DIM), F32),
                        pltpu.VMEM((SSD_STATE, hp), F32),
                        pltpu.VMEM((SSD_CHUNK, SSD_INNER), F32)],
        compiler_params=_cparams(("parallel", "arbitrary"), 40 << 20),
        name="ssd",
    )(xbc, z, dt, conv_buf, ssm_state.reshape(bsz, hp, SSD_STATE),
      conv_w, conv_b.reshape(1, -1), pad_lanes(dt_bias), pad_lanes(a_log),
      jnp.repeat(d_skip.astype(F32), SSD_HEADDIM).reshape(1, -1), norm_w.reshape(1, -1))
    return y, conv_new, s_new.reshape(bsz, SSD_HEADS, SSD_HEADDIM, SSD_STATE)


def _select_topk(scores, lane, valid):
    sc = jnp.where(valid, scores, -jnp.inf)
    sel = jnp.zeros(scores.shape, F32)
    big = jnp.int32(1 << 20)
    for _ in range(MOBA_TOPK):
        m = jnp.max(sc, axis=1, keepdims=True)
        idx = jnp.min(jnp.where(sc == m, lane, big), axis=1, keepdims=True)
        idx = jnp.where(m > -jnp.inf, idx, -1)
        hit = lane == idx
        sel = jnp.where(hit, 1.0, sel)
        sc = jnp.where(hit, -jnp.inf, sc)
    return sel > 0.5


def _moba_prompt_kernel(q_ref, kmt_ref, kaug_ref, vaug_ref, o_ref):
    i = pl.program_id(2)
    tq = MOBA_BLOCK
    q2 = q_ref[0]
    qs = jnp.concatenate([q2[:, g * HEAD_DIM:(g + 1) * HEAD_DIM] for g in range(GRP)], axis=0)
    rows = qs.shape[0]
    lane = lax.broadcasted_iota(jnp.int32, (rows, LANES), 1)
    scores = _dot_exact(qs, kmt_ref[0, 0])
    sel = _select_topk(scores, lane, (lane >= HEAD_DIM) & (lane < HEAD_DIM + i))
    pen = jnp.where(sel | (lane == HEAD_DIM + i), 0.0, NEG)
    q_pad = jnp.concatenate([qs * (HEAD_DIM ** -0.5), jnp.zeros((rows, LANES - HEAD_DIM), F32)], axis=1)
    q_aug = jnp.where(lane < HEAD_DIM, q_pad, pen).astype(BF16)

    def update(carry, s, vb):
        m, acc = carry
        m_new = jnp.maximum(m, jnp.max(s, axis=1, keepdims=True))
        p = jnp.exp(s - m_new)
        acc = jnp.exp(m - m_new) * acc + _dot(p.astype(BF16), vb)
        return m_new, acc

    def body(n, carry):
        start = pl.multiple_of(n * tq, tq)
        kb = kaug_ref[0, 0, pl.ds(start, tq), :]
        vb = vaug_ref[0, 0, pl.ds(start, tq), :]
        return update(carry, _dot_nt(q_aug, kb), vb)

    init = (jnp.full((rows, 1), -jnp.inf, F32), jnp.zeros((rows, LANES), F32))
    carry = lax.fori_loop(0, i, body, init)
    start = pl.multiple_of(i * tq, tq)
    s = _dot_nt(q_aug, kaug_ref[0, 0, pl.ds(start, tq), :])
    r = lax.broadcasted_iota(jnp.int32, (rows, tq), 0)
    cidx = lax.broadcasted_iota(jnp.int32, (rows, tq), 1)
    qpos = r
    for g in range(1, GRP):
        qpos = jnp.where(r >= g * tq, r - g * tq, qpos)
    s = jnp.where(cidx <= qpos, s, NEG)
    _, acc = update(carry, s, vaug_ref[0, 0, pl.ds(start, tq), :])
    out = acc[:, :HEAD_DIM] / acc[:, HEAD_DIM:HEAD_DIM + 1]
    o_ref[0] = jnp.concatenate([out[g * tq:(g + 1) * tq] for g in range(GRP)], axis=1)


def _moba_prompt(q, kmean, kaug, vaug):
    bsz, seqlen, _ = q.shape
    nb = seqlen // MOBA_BLOCK
    assert GRP * HEAD_DIM == LANES and nb <= LANES - HEAD_DIM
    km = kmean.reshape(bsz, nb, ATTN_KV_HEADS, HEAD_DIM).transpose(0, 2, 3, 1)
    kmt = jnp.pad(km, ((0, 0), (0, 0), (0, 0), (HEAD_DIM, LANES - HEAD_DIM - nb)))
    kv_spec = pl.BlockSpec((1, 1, seqlen, LANES), lambda b, h, i: (b, h, 0, 0))
    vmem = 2 * 2 * seqlen * LANES * 2 + (24 << 20)
    return pl.pallas_call(
        _moba_prompt_kernel,
        grid=(bsz, ATTN_KV_HEADS, nb),
        in_specs=[pl.BlockSpec((1, MOBA_BLOCK, LANES), lambda b, h, i: (b, i, h)),
                  pl.BlockSpec((1, 1, HEAD_DIM, LANES), lambda b, h, i: (b, h, 0, 0)),
                  kv_spec, kv_spec],
        out_specs=pl.BlockSpec((1, MOBA_BLOCK, LANES), lambda b, h, i: (b, i, h)),
        out_shape=jax.ShapeDtypeStruct((bsz, seqlen, ATTN_Q), F32),
        compiler_params=_cparams(("parallel", "parallel", "arbitrary"), vmem),
        name="moba_prompt",
    )(q, kmt, kaug, vaug)


def _moba_score_kernel(pt_ref, qb_ref, k0_ref, k1_ref, sc_ref):
    del pt_ref
    n = pl.program_id(1)
    km = (jnp.sum(k0_ref[0], axis=0, keepdims=True) + jnp.sum(k1_ref[0], axis=0, keepdims=True)) * (1.0 / MOBA_BLOCK)
    colv = jnp.sum(qb_ref[0] * km, axis=1, keepdims=True)
    lane = lax.broadcasted_iota(jnp.int32, sc_ref.shape[1:], 1)

    @pl.when(n == 0)
    def _():
        sc_ref[0] = jnp.zeros(sc_ref.shape[1:], F32)

    sc_ref[0] = jnp.where(lane == n, colv, sc_ref[0])


def _moba_sample_kernel(pt_ref, sc_ref, qb_ref, k0_ref, k1_ref, v0_ref, v1_ref, kn_ref, vn_ref, o_ref,
                        pen_sc, m_sc, l_sc, acc_sc, *, nblk, dec_seq):
    del pt_ref
    n = pl.program_id(1)
    rows = qb_ref.shape[1]
    lane = lax.broadcasted_iota(jnp.int32, (rows, LANES), 1)

    @pl.when(n == 0)
    def _():
        sel = _select_topk(sc_ref[0], lane, lane < nblk)
        pen_sc[...] = jnp.where(sel, 0.0, NEG)
        m_sc[...] = jnp.full(m_sc.shape, -jnp.inf, F32)
        l_sc[...] = jnp.zeros(l_sc.shape, F32)
        acc_sc[...] = jnp.zeros(acc_sc.shape, F32)

    qb = (qb_ref[0] * (HEAD_DIM ** -0.5)).astype(BF16)

    def update(s, vs):
        m = m_sc[...]
        m_new = jnp.maximum(m, jnp.max(s, axis=1, keepdims=True))
        p = jnp.exp(s - m_new)
        alpha = jnp.exp(m - m_new)
        l_sc[...] = alpha * l_sc[...] + jnp.sum(p, axis=1, keepdims=True)
        pv = _dot(p[:, :LANES].astype(BF16), vs[0])
        for j in range(1, len(vs)):
            pv = pv + _dot(p[:, j * LANES:(j + 1) * LANES].astype(BF16), vs[j])
        acc_sc[...] = alpha * acc_sc[...] + pv
        m_sc[...] = m_new

    pen_col = jnp.sum(jnp.where(lane == n, pen_sc[...], 0.0), axis=1, keepdims=True)
    s = jnp.concatenate([_dot_nt(qb, k0_ref[0].astype(BF16)), _dot_nt(qb, k1_ref[0].astype(BF16))], axis=1) + pen_col
    update(s, (v0_ref[0].astype(BF16), v1_ref[0].astype(BF16)))

    @pl.when(n == nblk - 1)
    def _():
        zpad = jnp.zeros((LANES - dec_seq, kn_ref.shape[2]), F32)
        kn = jnp.concatenate([kn_ref[0], zpad], axis=0).astype(BF16)
        vn = jnp.concatenate([vn_ref[0], zpad], axis=0).astype(BF16)
        s_own = _dot_nt(qb, kn)
        r = lax.broadcasted_iota(jnp.int32, (rows, LANES), 0)
        qpos = r & (dec_seq - 1)
        s_own = jnp.where((lane <= qpos) & (lane < dec_seq), s_own, NEG)
        update(s_own, (vn,))
        o_ref[0] = acc_sc[...] / l_sc[...]


def _moba_sample(q, k_new, v_new, cache_k, cache_v, page_table, past_len):
    bsz, dec_seq, _ = q.shape
    page = cache_k.shape[1]
    assert MOBA_BLOCK == 2 * page and page == LANES
    assert past_len % MOBA_BLOCK == 0 and dec_seq <= SUBLANES and dec_seq & (dec_seq - 1) == 0
    nblk = past_len // MOBA_BLOCK
    assert nblk <= LANES and nblk >= MOBA_TOPK
    rows = ATTN_HEADS * dec_seq
    q4 = q.reshape(bsz, dec_seq, ATTN_HEADS, HEAD_DIM).transpose(0, 2, 1, 3)
    own = (np.arange(ATTN_HEADS)[:, None] // GRP == np.arange(ATTN_KV_HEADS)[None, :]).astype(np.float32)
    qb = (q4[:, :, :, None, :] * jnp.asarray(own)[None, :, None, :, None]).reshape(bsz, rows, ATTN_KV)

    def page_spec(j):
        return pl.BlockSpec((1, page, ATTN_KV), lambda b, n, pt: (pt[b, 2 * n + j], 0, 0))

    def per_b(d1, d2):
        return pl.BlockSpec((1, d1, d2), lambda b, n, pt: (b, 0, 0))

    scores = pl.pallas_call(
        _moba_score_kernel,
        grid_spec=pltpu.PrefetchScalarGridSpec(
            num_scalar_prefetch=1, grid=(bsz, nblk),
            in_specs=[per_b(rows, ATTN_KV), page_spec(0), page_spec(1)],
            out_specs=per_b(rows, LANES)),
        out_shape=jax.ShapeDtypeStruct((bsz, rows, LANES), F32),
        compiler_params=_cparams(("parallel", "arbitrary"), 16 << 20),
        name="moba_sample_scores",
    )(page_table, qb, cache_k, cache_k)

    out = pl.pallas_call(
        functools.partial(_moba_sample_kernel, nblk=nblk, dec_seq=dec_seq),
        grid_spec=pltpu.PrefetchScalarGridSpec(
            num_scalar_prefetch=1, grid=(bsz, nblk),
            in_specs=[per_b(rows, LANES), per_b(rows, ATTN_KV), page_spec(0), page_spec(1), page_spec(0), page_spec(1),
                      per_b(dec_seq, ATTN_KV), per_b(dec_seq, ATTN_KV)],
            out_specs=per_b(rows, ATTN_KV),
            scratch_shapes=[pltpu.VMEM((rows, LANES), F32), pltpu.VMEM((rows, 1), F32), pltpu.VMEM((rows, 1), F32),
                            pltpu.VMEM((rows, ATTN_KV), F32)]),
        out_shape=jax.ShapeDtypeStruct((bsz, rows, ATTN_KV), F32),
        compiler_params=_cparams(("parallel", "arbitrary"), 16 << 20),
        name="moba_sample_attn",
    )(page_table, scores, qb, cache_k, cache_k, cache_v, cache_v, k_new, v_new)
    o5 = out.reshape(bsz, ATTN_HEADS, dec_seq, ATTN_KV_HEADS, HEAD_DIM)
    o4 = jnp.sum(o5 * jnp.asarray(own)[None, :, None, :, None], axis=3)
    return o4.transpose(0, 2, 1, 3).reshape(bsz, dec_seq, ATTN_Q)


def _mix_kernel(x_ref, ys_ref, ya_ref, gs_ref, ga_ref, gt1_ref, sc2_ref, sh2_ref, nw_ref,
                wso_ref, wao_ref, wo_ref, wrt_ref, brt_ref, x1_ref, h2_ref, comb_ref):
    merged = (jax.nn.sigmoid(gs_ref[0]) * _dot(ys_ref[0].astype(BF16), wso_ref[...])
              + jax.nn.sigmoid(ga_ref[0]) * _dot(ya_ref[0].astype(BF16), wao_ref[...]))
    x1 = x_ref[0] + gt1_ref[0] * _dot(merged.astype(BF16), wo_ref[...])
    x1_ref[0] = x1
    h2 = x1 * lax.rsqrt(jnp.mean(x1 * x1, axis=-1, keepdims=True) + EPS) * nw_ref[...]
    h2 = h2 * (1.0 + sc2_ref[0]) + sh2_ref[0]
    h2_ref[0] = h2.astype(BF16)

    lg = _dot_exact(h2, wrt_ref[...]) + brt_ref[...]
    lane = lax.broadcasted_iota(jnp.int32, lg.shape, 1)
    big = jnp.int32(1 << 20)
    ninf = -jnp.inf
    gl = jnp.where(lane < N_EGROUPS, lg, ninf)
    gmax = jnp.max(gl, axis=1, keepdims=True)
    gidx = jnp.min(jnp.where(gl == gmax, lane, big), axis=1, keepdims=True)
    g_w = 1.0 / jnp.sum(jnp.exp(gl - gmax), axis=1, keepdims=True)
    lo = ROUTE_OFF + gidx * EXPERTS_PER_GROUP
    el = jnp.where((lane >= lo) & (lane < lo + EXPERTS_PER_GROUP), lg, ninf)
    m1 = jnp.max(el, axis=1, keepdims=True)
    i1 = jnp.min(jnp.where(el == m1, lane, big), axis=1, keepdims=True)
    el2 = jnp.where(lane == i1, ninf, el)
    m2 = jnp.max(el2, axis=1, keepdims=True)
    i2 = jnp.min(jnp.where(el2 == m2, lane, big), axis=1, keepdims=True)
    e2 = jnp.exp(m2 - m1)
    den = 1.0 / (1.0 + e2)
    comb_ref[0] = jnp.where(lane == i1, den * g_w, 0.0) + jnp.where(lane == i2, e2 * den * g_w, 0.0)


def _mix(x, y_ssd, y_att, gs, ga, gt1, sc2, sh2, norm_w, wso, wao, wo, wrt, brt):
    bsz, seqlen, _ = x.shape
    tm = 256
    assert seqlen % tm == 0
    nt = seqlen // tm
    mod_rows = gt1.shape[1]
    if mod_rows == 1:
        mod_spec = pl.BlockSpec((1, 1, D_MODEL), lambda b, i: (b, 0, 0))
    else:
        mod_spec = pl.BlockSpec((1, tm, D_MODEL), lambda b, i: (b, i, 0))

    def tok(width):
        return pl.BlockSpec((1, tm, width), lambda b, i: (b, i, 0))

    def full(d1, d2):
        return pl.BlockSpec((d1, d2), lambda b, i: (0, 0))

    vmem = 2 * 2 * (2 * D_MODEL * D_MODEL + ATTN_Q * D_MODEL) + 2 * tm * (8 * D_MODEL) * 4 + (12 << 20)
    return pl.pallas_call(
        _mix_kernel,
        grid=(bsz, nt),
        in_specs=[tok(D_MODEL), tok(SSD_INNER), tok(ATTN_Q), tok(D_MODEL), tok(D_MODEL),
                  mod_spec, mod_spec, mod_spec, full(1, D_MODEL),
                  full(SSD_INNER, D_MODEL), full(ATTN_Q, D_MODEL), full(D_MODEL, D_MODEL),
                  full(D_MODEL, LANES), full(1, LANES)],
        out_specs=[tok(D_MODEL), tok(D_MODEL), tok(LANES)],
        out_shape=[jax.ShapeDtypeStruct((bsz, seqlen, D_MODEL), F32),
                   jax.ShapeDtypeStruct((bsz, seqlen, D_MODEL), BF16),
                   jax.ShapeDtypeStruct((bsz, seqlen, LANES), F32)],
        compiler_params=_cparams(("parallel", "parallel"), vmem),
        name="mix",
    )(x, y_ssd, y_att, gs, ga, gt1, sc2, sh2, norm_w.reshape(1, -1), wso, wao, wo, wrt, brt)


def _moe_kernel(h2_ref, comb_ref, x1_ref, gt2_ref, fw_ref, wg_ref, wu_ref, wd_ref, y_ref, acc):
    e = pl.program_id(2)

    @pl.when(e == 0)
    def _():
        acc[...] = jnp.zeros(acc.shape, F32)

    h2 = h2_ref[0]
    lane = lax.broadcasted_iota(jnp.int32, comb_ref.shape[1:], 1)
    cw = jnp.sum(jnp.where(lane == e + ROUTE_OFF, comb_ref[0], 0.0), axis=1, keepdims=True)
    a = _dot(h2, wg_ref[0])
    u = _dot(h2, wu_ref[0])
    act = (_silu(a) * u * cw).astype(BF16)
    acc[...] += _dot(act, wd_ref[0])

    @pl.when(e == pl.num_programs(2) - 1)
    def _():
        x2 = x1_ref[0] + gt2_ref[0] * acc[...]
        y_ref[0] = x2 * lax.rsqrt(jnp.mean(x2 * x2, axis=-1, keepdims=True) + EPS) * fw_ref[...]


def _moe(h2, comb, x1, gt2, final_w, wg, wu, wd):
    bsz, seqlen, _ = h2.shape
    tm = 1024 if seqlen % 1024 == 0 else 256
    assert seqlen % tm == 0
    nt = seqlen // tm
    mod_rows = gt2.shape[1]
    if mod_rows == 1:
        mod_spec = pl.BlockSpec((1, 1, D_MODEL), lambda b, i, e: (b, 0, 0))
    else:
        mod_spec = pl.BlockSpec((1, tm, D_MODEL), lambda b, i, e: (b, i, 0))

    def tok(width):
        return pl.BlockSpec((1, tm, width), lambda b, i, e: (b, i, 0))

    vmem = 2 * tm * D_MODEL * (2 + 4 + 4) + tm * D_MODEL * 4 + 2 * 3 * D_MODEL * D_EXPERT * 2 + (16 << 20)
    return pl.pallas_call(
        _moe_kernel,
        grid=(bsz, nt, N_EXPERTS),
        in_specs=[tok(D_MODEL), tok(LANES), tok(D_MODEL), mod_spec,
                  pl.BlockSpec((1, D_MODEL), lambda b, i, e: (0, 0)),
                  pl.BlockSpec((1, D_MODEL, D_EXPERT), lambda b, i, e: (e, 0, 0)),
                  pl.BlockSpec((1, D_MODEL, D_EXPERT), lambda b, i, e: (e, 0, 0)),
                  pl.BlockSpec((1, D_EXPERT, D_MODEL), lambda b, i, e: (e, 0, 0))],
        out_specs=tok(D_MODEL),
        out_shape=jax.ShapeDtypeStruct((bsz, seqlen, D_MODEL), F32),
        scratch_shapes=[pltpu.VMEM((tm, D_MODEL), F32)],
        compiler_params=_cparams(("parallel", "parallel", "arbitrary"), vmem),
        name="moe",
    )(h2, comb, x1, gt2, final_w.reshape(1, -1), wg, wu, wd)


def _prep_weights(lp):
    w_in = lp["w_in"]
    sizes = (SSD_INNER, CONV_DIM, SSD_HEADS, ATTN_Q, ATTN_KV, ATTN_KV, D_MODEL, D_MODEL)
    pts = [int(p) for p in np.cumsum(sizes)[:-1]]
    z, xbc, dt, q, k, v, gs, ga = jnp.split(w_in, pts, axis=1)
    dt = jnp.pad(dt, ((0, 0), (0, LANES - SSD_HEADS)))
    w_in_r = jnp.concatenate([z, xbc, q, k, v, gs, ga, dt], axis=1).astype(BF16)
    wrt = jnp.concatenate([lp["w_group"], lp["w_router"]], axis=1)
    wrt = jnp.pad(wrt, ((0, 0), (0, LANES - wrt.shape[1]))).astype(F32)
    brt = jnp.concatenate([lp["b_group"], lp["b_router"]])
    brt = jnp.pad(brt, (0, LANES - brt.shape[0])).astype(F32).reshape(1, LANES)
    return dict(w_in_r=w_in_r, wrt=wrt, brt=brt,
                wso=lp["w_ssd_out"].astype(BF16), wao=lp["w_attn_out"].astype(BF16), wo=lp["w_o"].astype(BF16),
                wg=lp["w_gate_e"].astype(BF16), wu=lp["w_up_e"].astype(BF16), wd=lp["w_down_e"].astype(BF16))


def _mods(mod, per_row_repeat):
    parts = jnp.split(mod, 6, axis=-1)
    if per_row_repeat:
        return [jnp.repeat(p, per_row_repeat, axis=0)[None] for p in parts]
    return [p[:, None, :] for p in parts]


def _layer_prompt(x, mod, lp, pw, final_w):
    bsz, seqlen, _ = x.shape
    sh1, sc1, gt1, sh2, sc2, gt2 = _mods(mod, 0)
    tables = _rope_tables(np.arange(seqlen))
    (z, xbc, dt, q, k, v, gs, ga, kmean, kaug, vaug) = _in_proj(x, sc1, sh1, lp["norm1_w"], pw["w_in_r"], tables, True)
    conv0 = jnp.zeros((bsz, SSD_CONV - 1, CONV_DIM), F32)
    ssm0 = jnp.zeros((bsz, SSD_HEADS, SSD_HEADDIM, SSD_STATE), F32)
    y_ssd, conv_new, ssm_new = _ssd(xbc, z, dt, conv0, ssm0, lp["conv_w"], lp["conv_b"], lp["dt_bias"],
                                    lp["a_log"], lp["d_skip"], lp["ssd_norm_w"])
    y_att = _moba_prompt(q, kmean, kaug, vaug)
    x1, h2, comb = _mix(x, y_ssd, y_att, gs, ga, gt1, sc2, sh2, lp["norm2_w"],
                        pw["wso"], pw["wao"], pw["wo"], pw["wrt"], pw["brt"])
    y = _moe(h2, comb, x1, gt2, final_w, pw["wg"], pw["wu"], pw["wd"])
    kv_shape = (bsz, seqlen, ATTN_KV_HEADS, HEAD_DIM)
    return y, k.reshape(kv_shape), v.reshape(kv_shape), conv_new, ssm_new


def _layer_sample(x, mod, conv_buf, ssm_state, cache_k, cache_v, page_table, past_len, lp, pw, final_w):
    bsz, dec_seq, _ = x.shape
    ntok = bsz * dec_seq
    sh1, sc1, gt1, sh2, sc2, gt2 = _mods(mod, dec_seq)
    pos = past_len + (np.arange(ntok) % dec_seq)
    tables = _rope_tables(pos)
    xf = x.reshape(1, ntok, D_MODEL)
    z, xbc, dt, q, k, v, gs, ga = _in_proj(xf, sc1, sh1, lp["norm1_w"], pw["w_in_r"], tables, False)

    def per_b(t):
        return t.reshape(bsz, dec_seq, t.shape[-1])

    y_ssd, conv_new, ssm_new = _ssd(per_b(xbc), per_b(z), per_b(dt), conv_buf, ssm_state, lp["conv_w"],
                                    lp["conv_b"], lp["dt_bias"], lp["a_log"], lp["d_skip"], lp["ssd_norm_w"])
    n_pool, page = cache_k.shape[0], cache_k.shape[1]
    y_att = _moba_sample(per_b(q), per_b(k), per_b(v), cache_k.reshape(n_pool, page, ATTN_KV),
                         cache_v.reshape(n_pool, page, ATTN_KV), page_table, past_len)
    x1, h2, comb = _mix(xf, y_ssd.reshape(1, ntok, -1), y_att.reshape(1, ntok, -1), gs, ga, gt1, sc2, sh2,
                        lp["norm2_w"], pw["wso"], pw["wao"], pw["wo"], pw["wrt"], pw["brt"])
    y = _moe(h2, comb, x1, gt2, final_w, pw["wg"], pw["wu"], pw["wd"])
    kv_shape = (bsz, dec_seq, ATTN_KV_HEADS, HEAD_DIM)
    return y.reshape(bsz, dec_seq, D_MODEL), k.reshape(kv_shape), v.reshape(kv_shape), conv_new, ssm_new


def kernel(x_prompt, x_sample, cache_k, cache_v, state_conv, state_ssm, page_table, c_prompt, c_sample,
           w_ada, b_ada, norm1_w, w_in, conv_w, conv_b, dt_bias, a_log, d_skip, ssd_norm_w,
           w_ssd_out, w_attn_out, w_o, norm2_w, w_group, b_group, w_router, b_router,
           w_gate_e, w_up_e, w_down_e, final_w):
    depth = w_in.shape[0]
    assert depth == 1, "the final RMSNorm is fused into the last layer's MoE kernel"
    nb_p, nb_s = c_prompt.shape[0], c_sample.shape[0]
    past_len = page_table.shape[1] * cache_k.shape[2]
    l = 0
    lp = dict(norm1_w=norm1_w[l], w_in=w_in[l], conv_w=conv_w[l], conv_b=conv_b[l], dt_bias=dt_bias[l],
              a_log=a_log[l], d_skip=d_skip[l], ssd_norm_w=ssd_norm_w[l], w_ssd_out=w_ssd_out[l],
              w_attn_out=w_attn_out[l], w_o=w_o[l], norm2_w=norm2_w[l], w_group=w_group[l], b_group=b_group[l],
              w_router=w_router[l], b_router=b_router[l], w_gate_e=w_gate_e[l], w_up_e=w_up_e[l],
              w_down_e=w_down_e[l])
    pw = _prep_weights(lp)
    c_all = jnp.concatenate([c_prompt, c_sample], axis=0)
    pad_rows = (-c_all.shape[0]) % SUBLANES
    mod = _ada_mod(jnp.pad(c_all, ((0, pad_rows), (0, 0))), w_ada[l], b_ada[l])
    yp, kp, vp, cp, sp = _layer_prompt(x_prompt, mod[:nb_p], lp, pw, final_w)
    ys, ks, vs, cs, ss = _layer_sample(x_sample, mod[nb_p:nb_p + nb_s], state_conv[l], state_ssm[l],
                                       cache_k[l], cache_v[l], page_table, past_len, lp, pw, final_w)
    return (yp, ys, kp[None], vp[None], cp[None], sp[None], ks[None], vs[None], cs[None], ss[None])
```

```python
import functools
import math

import numpy as np
import jax
import jax.numpy as jnp
from jax import lax
from jax.experimental import pallas as pl
from jax.experimental.pallas import tpu as pltpu

F32 = jnp.float32
BF16 = jnp.bfloat16
HIGHEST = lax.Precision.HIGHEST

D_MODEL = 1024
SSD_HEADDIM = 64
SSD_INNER = D_MODEL
SSD_HEADS = SSD_INNER // SSD_HEADDIM
SSD_GROUPS = 2
SSD_STATE = 128
SSD_CONV = 4
SSD_CHUNK = 128
SSD_BC = SSD_GROUPS * SSD_STATE
CONV_DIM = SSD_INNER + 2 * SSD_BC
ATTN_HEADS = 8
ATTN_KV_HEADS = 4
HEAD_DIM = 64
ATTN_Q = ATTN_HEADS * HEAD_DIM
ATTN_KV = ATTN_KV_HEADS * HEAD_DIM
ROT_DIM = HEAD_DIM // 4
ROPE_THETA = 500000.0
MOBA_BLOCK = 256
MOBA_TOPK = 3
N_EGROUPS = 4
EXPERTS_PER_GROUP = 4
N_EXPERTS = N_EGROUPS * EXPERTS_PER_GROUP
D_EXPERT = D_MODEL // 2
EPS = 1e-6

LANES = 128
SUBLANES = 8
VMEM_CAP = 56 << 20
NEG = -1e30
GRP = ATTN_HEADS // ATTN_KV_HEADS
ROUTE_OFF = N_EGROUPS

_SEG = {}
_off = 0
for _name, _w in (("z", SSD_INNER), ("xbc", CONV_DIM), ("q", ATTN_Q), ("k", ATTN_KV), ("v", ATTN_KV),
                  ("gs", D_MODEL), ("ga", D_MODEL), ("dt", LANES)):
    _SEG[_name] = (_off, _off + _w)
    _off += _w
IN_PAD = _off


def _cparams(semantics, vmem_bytes):
    return pltpu.CompilerParams(dimension_semantics=semantics,
                                vmem_limit_bytes=int(min(max(vmem_bytes, 16 << 20), VMEM_CAP)))


def _dot(a, b):
    return jnp.dot(a, b, preferred_element_type=F32)


def _dot_nt(a, b):
    return lax.dot_general(a, b, (((1,), (1,)), ((), ())), preferred_element_type=F32)


def _dot_exact(a, b):
    return jnp.dot(a, b, preferred_element_type=F32, precision=HIGHEST)


def _silu(x):
    return x * jax.nn.sigmoid(x)


def _mod_kernel(c_ref, w_ref, b_ref, o_ref):
    o_ref[...] = _dot_exact(_silu(c_ref[...]), w_ref[...]) + b_ref[...]


def _ada_mod(c_all, w_ada, b_ada):
    rows = c_all.shape[0]
    return pl.pallas_call(
        _mod_kernel,
        grid=(6,),
        in_specs=[pl.BlockSpec((rows, D_MODEL), lambda j: (0, 0)),
                  pl.BlockSpec((D_MODEL, D_MODEL), lambda j: (0, j)),
                  pl.BlockSpec((1, D_MODEL), lambda j: (0, j))],
        out_specs=pl.BlockSpec((rows, D_MODEL), lambda j: (0, j)),
        out_shape=jax.ShapeDtypeStruct((rows, 6 * D_MODEL), F32),
        compiler_params=_cparams(("arbitrary",), 24 << 20),
        name="ada_mod",
    )(c_all, w_ada, b_ada.reshape(1, -1))


def _rope_tables(pos):
    half = ROT_DIM // 2
    c = -2.0 * math.log(ROPE_THETA) / ROT_DIM
    c_hi = float(np.float32(c))
    j = jnp.arange(half, dtype=F32)
    inv_freq = jnp.exp(j * c_hi + j * (c - c_hi))
    ang = jnp.asarray(pos).astype(F32)[:, None] * inv_freq[None, :]
    cos, sin = jnp.cos(ang), jnp.sin(ang)
    n = pos.shape[0]
    one = jnp.ones((n, HEAD_DIM - ROT_DIM), F32)
    zero = jnp.zeros((n, half), F32)
    rest = jnp.zeros((n, HEAD_DIM - ROT_DIM), F32)
    a = jnp.concatenate([cos, cos, one], axis=1)
    s1 = jnp.concatenate([-sin, zero, rest], axis=1)
    s2 = jnp.concatenate([zero, sin, rest], axis=1)
    rep = LANES // HEAD_DIM
    return tuple(jnp.tile(t, (1, rep)) for t in (a, s1, s2))


def _inproj_kernel(*refs, attn_layout):
    (x_ref, sc_ref, sh_ref, nw_ref, w_ref, ra_ref, rs1_ref, rs2_ref,
     z_ref, xbc_ref, dt_ref, q_ref, k_ref, v_ref, gs_ref, ga_ref) = refs[:16]
    x = x_ref[0]
    h = x * lax.rsqrt(jnp.mean(x * x, axis=-1, keepdims=True) + EPS) * nw_ref[...]
    h = (h * (1.0 + sc_ref[0]) + sh_ref[0]).astype(BF16)

    def proj(name, out_ref):
        lo, hi = _SEG[name]
        step = 512
        for c in range(lo, hi, step):
            w = min(step, hi - c)
            out_ref[0, :, c - lo:c - lo + w] = _dot(h, w_ref[:, c:c + w])

    proj("z", z_ref)
    proj("xbc", xbc_ref)
    proj("gs", gs_ref)
    proj("ga", ga_ref)
    proj("dt", dt_ref)

    def rope(t):
        width = t.shape[1]
        reps = width // LANES
        a = jnp.concatenate([ra_ref[...]] * reps, axis=1)
        s1 = jnp.concatenate([rs1_ref[...]] * reps, axis=1)
        s2 = jnp.concatenate([rs2_ref[...]] * reps, axis=1)
        return t * a + pltpu.roll(t, width - ROT_DIM // 2, 1) * s1 + pltpu.roll(t, ROT_DIM // 2, 1) * s2

    lo, hi = _SEG["q"]
    q_ref[0] = rope(_dot(h, w_ref[:, lo:hi]))
    lo, hi = _SEG["k"]
    k = rope(_dot(h, w_ref[:, lo:hi]))
    k_ref[0] = k
    lo, hi = _SEG["v"]
    v = _dot(h, w_ref[:, lo:hi])
    v_ref[0] = v

    if attn_layout:
        kmean_ref, kaug_ref, vaug_ref = refs[16:19]
        tm = k.shape[0]
        kmean_ref[0, 0] = jnp.mean(k, axis=0, keepdims=True)
        lane = lax.broadcasted_iota(jnp.int32, (tm, HEAD_DIM), 1)
        blk_onehot = (lane == pl.program_id(1)).astype(F32)
        ones_col = (lane == 0).astype(F32)
        for kh in range(ATTN_KV_HEADS):
            sl = slice(kh * HEAD_DIM, (kh + 1) * HEAD_DIM)
            kaug_ref[0, kh] = jnp.concatenate([k[:, sl], blk_onehot], axis=1).astype(BF16)
            vaug_ref[0, kh] = jnp.concatenate([v[:, sl], ones_col], axis=1).astype(BF16)


def _in_proj(x, sc1, sh1, norm_w, w_in_r, tables, attn_layout):
    bsz, seqlen, _ = x.shape
    tm = MOBA_BLOCK
    assert seqlen % tm == 0
    nt = seqlen // tm
    mod_rows = sc1.shape[1]
    if mod_rows == 1:
        mod_spec = pl.BlockSpec((1, 1, D_MODEL), lambda b, i: (b, 0, 0))
    else:
        mod_spec = pl.BlockSpec((1, tm, D_MODEL), lambda b, i: (b, i, 0))
    tab_spec = pl.BlockSpec((tm, LANES), lambda b, i: (i, 0))

    def tok(width):
        return pl.BlockSpec((1, tm, width), lambda b, i: (b, i, 0))

    def shp(width):
        return jax.ShapeDtypeStruct((bsz, seqlen, width), F32)

    widths = (SSD_INNER, CONV_DIM, LANES, ATTN_Q, ATTN_KV, ATTN_KV, D_MODEL, D_MODEL)
    out_specs = [tok(w) for w in widths]
    out_shape = [shp(w) for w in widths]
    if attn_layout:
        assert nt <= HEAD_DIM
        out_specs += [pl.BlockSpec((1, 1, 1, ATTN_KV), lambda b, i: (b, i, 0, 0)),
                      pl.BlockSpec((1, ATTN_KV_HEADS, tm, LANES), lambda b, i: (b, 0, i, 0)),
                      pl.BlockSpec((1, ATTN_KV_HEADS, tm, LANES), lambda b, i: (b, 0, i, 0))]
        out_shape += [jax.ShapeDtypeStruct((bsz, nt, 1, ATTN_KV), F32),
                      jax.ShapeDtypeStruct((bsz, ATTN_KV_HEADS, seqlen, LANES), BF16),
                      jax.ShapeDtypeStruct((bsz, ATTN_KV_HEADS, seqlen, LANES), BF16)]
    vmem = 2 * (D_MODEL * IN_PAD * 2) + 2 * tm * (D_MODEL + IN_PAD + 4 * LANES) * 4 + (8 << 20)
    return pl.pallas_call(
        functools.partial(_inproj_kernel, attn_layout=attn_layout),
        grid=(bsz, nt),
        in_specs=[tok(D_MODEL), mod_spec, mod_spec,
                  pl.BlockSpec((1, D_MODEL), lambda b, i: (0, 0)),
                  pl.BlockSpec((D_MODEL, IN_PAD), lambda b, i: (0, 0)),
                  tab_spec, tab_spec, tab_spec],
        out_specs=out_specs,
        out_shape=out_shape,
        compiler_params=_cparams(("parallel", "parallel"), vmem),
        name="in_proj",
    )(x, sc1, sh1, norm_w.reshape(1, -1), w_in_r, *tables)


def _cumsum_rows(a):
    row = lax.broadcasted_iota(jnp.int32, a.shape, 0)
    s = 1
    while s < a.shape[0]:
        a = a + jnp.where(row >= s, pltpu.roll(a, s, 0), 0.0)
        s *= 2
    return a


def _ssd_kernel(xbc_ref, z_ref, dt_ref, cbuf_ref, s0_ref, cw_ref, cb_ref, dtb_ref, alog_ref, dsk_ref, nw_ref,
                y_ref, cnew_ref, sout_ref, xext, st, ysc, *, rows):
    ch = SSD_CHUNK
    pad = SUBLANES
    c = pl.program_id(1)
    last = pl.num_programs(1) - 1

    @pl.when(c == 0)
    def _():
        xext[0:pad, :] = jnp.zeros((pad, CONV_DIM), F32)
        xext[pad - (SSD_CONV - 1):pad, :] = cbuf_ref[0]
        st[...] = s0_ref[0].T

    xext[pad:pad + rows, :] = xbc_ref[0]
    if rows < ch:
        xext[pad + rows:pad + ch, :] = jnp.zeros((ch - rows, CONV_DIM), F32)

    conv = cb_ref[...] + xext[pad:pad + ch, :] * cw_ref[SSD_CONV - 1:SSD_CONV, :]
    for w in range(SSD_CONV - 1):
        off = pad - (SSD_CONV - 1) + w
        conv = conv + xext[off:off + ch, :] * cw_ref[w:w + 1, :]
    act = _silu(conv)

    @pl.when(c == last)
    def _():
        cnew_ref[0] = xext[pad + rows - (SSD_CONV - 1):pad + rows, :]

    xext[0:pad, :] = xext[ch:ch + pad, :]

    xs = act[:, :SSD_INNER]
    dt_raw = dt_ref[0]
    if rows < ch:
        dt_raw = jnp.concatenate([dt_raw, jnp.zeros((ch - rows, LANES), F32)], axis=0)
    dt_in = dt_raw + dtb_ref[...]
    dt = jnp.maximum(dt_in, 0.0) + jnp.log1p(jnp.exp(-jnp.abs(dt_in)))
    row = lax.broadcasted_iota(jnp.int32, (ch, LANES), 0)
    col = lax.broadcasted_iota(jnp.int32, (ch, LANES), 1)
    if rows < ch:
        dt = jnp.where(row < rows, dt, 0.0)
    a_cs = _cumsum_rows(dt * (-jnp.exp(alog_ref[...])))
    a_cs_t = a_cs.T
    dt_t = dt.T
    causal = row >= col

    for g in range(SSD_GROUPS):
        bg = act[:, SSD_INNER + g * SSD_STATE:SSD_INNER + (g + 1) * SSD_STATE]
        cg = act[:, SSD_INNER + SSD_BC + g * SSD_STATE:SSD_INNER + SSD_BC + (g + 1) * SSD_STATE]
        gmat = _dot_nt(cg.astype(BF16), bg.astype(BF16))
        bg_t = bg.T
        for e in range(SSD_HEADS // SSD_GROUPS):
            hd = g * (SSD_HEADS // SSD_GROUPS) + e
            sl = slice(hd * SSD_HEADDIM, (hd + 1) * SSD_HEADDIM)
            acs_col = a_cs[:, hd:hd + 1]
            acs_row = a_cs_t[hd:hd + 1, :]
            dt_row = dt_t[hd:hd + 1, :]
            acs_last = a_cs_t[hd:hd + 1, ch - 1:ch]
            lmat = jnp.where(causal, jnp.exp(acs_col - acs_row), 0.0)
            m = (gmat * lmat * dt_row).astype(BF16)
            x_h = xs[:, sl].astype(BF16)
            s_h = st[:, sl]
            c_exp = (cg * jnp.exp(acs_col)).astype(BF16)
            ysc[:, sl] = _dot(m, x_h) + _dot(c_exp, s_h.astype(BF16))
            b_w = (bg_t * (jnp.exp(acs_last - acs_row) * dt_row)).astype(BF16)
            st[:, sl] = jnp.exp(acs_last) * s_h + _dot(b_w, x_h)

    y = (ysc[0:rows, :] + xs[:rows] * dsk_ref[...]) * _silu(z_ref[0])
    gw = SSD_INNER // SSD_GROUPS
    for g in range(SSD_GROUPS):
        yg = y[:, g * gw:(g + 1) * gw]
        yg = yg * lax.rsqrt(jnp.mean(yg * yg, axis=-1, keepdims=True) + EPS)
        y_ref[0, :, g * gw:(g + 1) * gw] = yg * nw_ref[:, g * gw:(g + 1) * gw]

    @pl.when(c == last)
    def _():
        sout_ref[0] = st[...].T


def _ssd(xbc, z, dt, conv_buf, ssm_state, conv_w, conv_b, dt_bias, a_log, d_skip, norm_w):
    bsz, seqlen, _ = xbc.shape
    rows = SSD_CHUNK if seqlen % SSD_CHUNK == 0 else seqlen
    assert rows == seqlen or rows == SSD_CHUNK
    assert rows % SUBLANES == 0 and rows >= SSD_CONV - 1
    nc = seqlen // rows
    hp = SSD_HEADS * SSD_HEADDIM

    def tok(width):
        return pl.BlockSpec((1, rows, width), lambda b, c: (b, c, 0))

    def per_b(d1, d2):
        return pl.BlockSpec((1, d1, d2), lambda b, c: (b, 0, 0))

    def vec(width, r=1):
        return pl.BlockSpec((r, width), lambda b, c: (0, 0))

    def pad_lanes(v):
        return jnp.pad(v.astype(F32), (0, LANES - v.shape[0])).reshape(1, LANES)

    y, conv_new, s_new = pl.pallas_call(
        functools.partial(_ssd_kernel, rows=rows),
        grid=(bsz, nc),
        in_specs=[tok(CONV_DIM), tok(SSD_INNER), tok(LANES),
                  per_b(SSD_CONV - 1, CONV_DIM), per_b(hp, SSD_STATE),
                  vec(CONV_DIM, SSD_CONV), vec(CONV_DIM), vec(LANES), vec(LANES), vec(SSD_INNER), vec(SSD_INNER)],
        out_specs=[tok(SSD_INNER), per_b(SSD_CONV - 1, CONV_DIM), per_b(hp, SSD_STATE)],
        out_shape=[jax.ShapeDtypeStruct((bsz, seqlen, SSD_INNER), F32),
                   jax.ShapeDtypeStruct((bsz, SSD_CONV - 1, CONV_DIM), F32),
                   jax.ShapeDtypeStruct((bsz, hp, SSD_STATE), F32)],
        scratch_shapes=[pltpu.VMEM((SSD_CHUNK + SUBLANES, CONV_DIM), F32),
                        pltpu.VMEM((SSD_STATE, hp), F32),
                        pltpu.VMEM((SSD_CHUNK, SSD_INNER), F32)],
        compiler_params=_cparams(("parallel", "arbitrary"), 40 << 20),
        name="ssd",
    )(xbc, z, dt, conv_buf, ssm_state.reshape(bsz, hp, SSD_STATE),
      conv_w, conv_b.reshape(1, -1), pad_lanes(dt_bias), pad_lanes(a_log),
      jnp.repeat(d_skip.astype(F32), SSD_HEADDIM).reshape(1, -1), norm_w.reshape(1, -1))
    return y, conv_new, s_new.reshape(bsz, SSD_HEADS, SSD_HEADDIM, SSD_STATE)


def _select_topk(scores, lane, valid):
    sc = jnp.where(valid, scores, -jnp.inf)
    sel = jnp.zeros(scores.shape, F32)
    lane_f = lane.astype(F32)
    for _ in range(MOBA_TOPK):
        m = jnp.max(sc, axis=1, keepdims=True)
        idx = jnp.min(jnp.where(sc == m, lane_f, float(1 << 20)), axis=1, keepdims=True)
        idx = jnp.where(m > -jnp.inf, idx, -1.0)
        hit = lane_f == idx
        sel = jnp.where(hit, 1.0, sel)
        sc = jnp.where(hit, -jnp.inf, sc)
    return sel > 0.5


def _moba_prompt_kernel(q_ref, kmt_ref, kaug_ref, vaug_ref, o_ref, *, chunk):
    i = pl.program_id(2)
    tq = MOBA_BLOCK
    q2 = q_ref[0]
    qs = jnp.concatenate([q2[:, g * HEAD_DIM:(g + 1) * HEAD_DIM] for g in range(GRP)], axis=0)
    rows = qs.shape[0]
    lane = lax.broadcasted_iota(jnp.int32, (rows, LANES), 1)
    scores = _dot_exact(qs, kmt_ref[0, 0])
    sel = _select_topk(scores, lane, (lane >= HEAD_DIM) & (lane < HEAD_DIM + i))
    pen = jnp.where(sel, 0.0, NEG)
    q_pad = jnp.concatenate([qs * (HEAD_DIM ** -0.5), jnp.zeros((rows, LANES - HEAD_DIM), F32)], axis=1)
    q_aug = jnp.where(lane < HEAD_DIM, q_pad, pen).astype(BF16)
    q_own = q_pad.astype(BF16)

    def update(m, acc, s, vb):
        m_new = jnp.maximum(m, jnp.max(s, axis=1, keepdims=True))
        p = jnp.exp(s - m_new)
        acc = jnp.exp(m - m_new) * acc + _dot(p.astype(BF16), vb)
        return m_new, acc

    nchunks = (i * tq + chunk - 1) // chunk
    last_chunk = kaug_ref.shape[2] // chunk - 1

    def logits(c):
        start = pl.multiple_of(jnp.minimum(c, last_chunk) * chunk, chunk)
        return _dot_nt(q_aug, kaug_ref[0, 0, pl.ds(start, chunk), :])

    def body(c, carry):
        m, acc, s = carry
        s_next = logits(c + 1)
        start = pl.multiple_of(c * chunk, chunk)
        m, acc = update(m, acc, s, vaug_ref[0, 0, pl.ds(start, chunk), :])
        return m, acc, s_next

    init = (jnp.full((rows, 1), -jnp.inf, F32), jnp.zeros((rows, LANES), F32), logits(0))
    m, acc, _ = lax.fori_loop(0, nchunks, body, init)
    start = pl.multiple_of(i * tq, tq)
    s = _dot_nt(q_own, kaug_ref[0, 0, pl.ds(start, tq), :])
    r = lax.broadcasted_iota(jnp.int32, (rows, tq), 0)
    cidx = lax.broadcasted_iota(jnp.int32, (rows, tq), 1)
    qpos = r
    for g in range(1, GRP):
        qpos = jnp.where(r >= g * tq, r - g * tq, qpos)
    s = jnp.where(cidx <= qpos, s, NEG)
    _, acc = update(m, acc, s, vaug_ref[0, 0, pl.ds(start, tq), :])
    out = acc[:, :HEAD_DIM] / acc[:, HEAD_DIM:HEAD_DIM + 1]
    o_ref[0] = jnp.concatenate([out[g * tq:(g + 1) * tq] for g in range(GRP)], axis=1)


def _moba_prompt(q, kmean, kaug, vaug):
    bsz, seqlen, _ = q.shape
    nb = seqlen // MOBA_BLOCK
    assert GRP * HEAD_DIM == LANES and nb <= LANES - HEAD_DIM
    km = kmean.reshape(bsz, nb, ATTN_KV_HEADS, HEAD_DIM).transpose(0, 2, 3, 1)
    kmt = jnp.pad(km, ((0, 0), (0, 0), (0, 0), (HEAD_DIM, LANES - HEAD_DIM - nb)))
    kv_spec = pl.BlockSpec((1, 1, seqlen, LANES), lambda b, h, i: (b, h, 0, 0))
    vmem = 2 * 2 * seqlen * LANES * 2 + (24 << 20)
    chunk = 2 * MOBA_BLOCK if seqlen % (2 * MOBA_BLOCK) == 0 else MOBA_BLOCK
    return pl.pallas_call(
        functools.partial(_moba_prompt_kernel, chunk=chunk),
        grid=(bsz, ATTN_KV_HEADS, nb),
        in_specs=[pl.BlockSpec((1, MOBA_BLOCK, LANES), lambda b, h, i: (b, i, h)),
                  pl.BlockSpec((1, 1, HEAD_DIM, LANES), lambda b, h, i: (b, h, 0, 0)),
                  kv_spec, kv_spec],
        out_specs=pl.BlockSpec((1, MOBA_BLOCK, LANES), lambda b, h, i: (b, i, h)),
        out_shape=jax.ShapeDtypeStruct((bsz, seqlen, ATTN_Q), F32),
        compiler_params=_cparams(("parallel", "parallel", "arbitrary"), vmem),
        name="moba_prompt",
    )(q, kmt, kaug, vaug)


def _moba_score_kernel(pt_ref, qbt_ref, *refs, pages_per_step):
    del pt_ref
    k_refs, sc_ref = refs[:pages_per_step], refs[pages_per_step]
    n = pl.program_id(1)
    pages_per_block = MOBA_BLOCK // LANES
    blocks_per_step = pages_per_step // pages_per_block
    qbt = qbt_ref[0]
    for jb in range(blocks_per_step):
        ksum = k_refs[jb * pages_per_block][0]
        for j in range(1, pages_per_block):
            ksum = ksum + k_refs[jb * pages_per_block + j][0]
        kmean = jnp.sum(ksum, axis=1, keepdims=True) * (1.0 / MOBA_BLOCK)
        sc_ref[0, pl.ds(n * blocks_per_step + jb, 1), :] = jnp.sum(qbt * kmean, axis=0, keepdims=True)


def _moba_sample_kernel(pt_ref, sc_ref, qb_ref, *refs, pages_per_step, nblk, dec_seq):
    del pt_ref
    k_refs = refs[:pages_per_step]
    v_refs = refs[pages_per_step:2 * pages_per_step]
    kn_ref, vn_ref, o_ref, pen_sc, m_sc, l_sc, acc_sc = refs[2 * pages_per_step:]
    n = pl.program_id(1)
    rows = qb_ref.shape[1]
    pages_per_block = MOBA_BLOCK // LANES
    blocks_per_step = pages_per_step // pages_per_block
    lane = lax.broadcasted_iota(jnp.int32, (rows, LANES), 1)

    @pl.when(n == 0)
    def _():
        sc_t = sc_ref[0]
        sc_sq = jnp.concatenate([sc_t, jnp.zeros((LANES - sc_t.shape[0], LANES), F32)], axis=0)
        scores = sc_sq.T[:rows]
        sel = _select_topk(scores, lane, lane < nblk)
        pen_sc[...] = jnp.where(sel, 0.0, NEG)
        m_sc[...] = jnp.full(m_sc.shape, -jnp.inf, F32)
        l_sc[...] = jnp.zeros(l_sc.shape, F32)
        acc_sc[...] = jnp.zeros(acc_sc.shape, F32)

    qb = (qb_ref[0] * (HEAD_DIM ** -0.5)).astype(BF16)

    def update(s, pv_fn):
        m = m_sc[...]
        m_new = jnp.maximum(m, jnp.max(s, axis=1, keepdims=True))
        p = jnp.exp(s - m_new)
        alpha = jnp.exp(m - m_new)
        l_sc[...] = alpha * l_sc[...] + jnp.sum(p, axis=1, keepdims=True)
        acc_sc[...] = alpha * acc_sc[...] + pv_fn(p.astype(BF16))
        m_sc[...] = m_new

    pen = pen_sc[...]
    parts = []
    for jb in range(blocks_per_step):
        pen_col = jnp.sum(jnp.where(lane == n * blocks_per_step + jb, pen, 0.0), axis=1, keepdims=True)
        for j in range(pages_per_block):
            parts.append(_dot(qb, k_refs[jb * pages_per_block + j][0].astype(BF16)) + pen_col)
    s = jnp.concatenate(parts, axis=1)

    def pv_past(p):
        pv = _dot_nt(p[:, :LANES], v_refs[0][0].astype(BF16))
        for j in range(1, pages_per_step):
            pv = pv + _dot_nt(p[:, j * LANES:(j + 1) * LANES], v_refs[j][0].astype(BF16))
        return pv

    update(s, pv_past)

    @pl.when(n == pl.num_programs(1) - 1)
    def _():
        zpad = jnp.zeros((LANES - dec_seq, kn_ref.shape[2]), F32)
        kn = jnp.concatenate([kn_ref[0], zpad], axis=0).astype(BF16)
        vn = jnp.concatenate([vn_ref[0], zpad], axis=0).astype(BF16)
        s_own = _dot_nt(qb, kn)
        r = lax.broadcasted_iota(jnp.int32, (rows, LANES), 0)
        qpos = r & (dec_seq - 1)
        s_own = jnp.where((lane <= qpos) & (lane < dec_seq), s_own, NEG)
        update(s_own, lambda p: _dot(p, vn))
        o_ref[0] = acc_sc[...] / l_sc[...]


def _moba_sample(q, k_new, v_new, cache_kt, cache_vt, page_table, past_len):
    bsz, dec_seq, _ = q.shape
    page = cache_kt.shape[2]
    assert page == LANES and MOBA_BLOCK % page == 0
    assert past_len % MOBA_BLOCK == 0 and dec_seq == SUBLANES
    nblk = past_len // MOBA_BLOCK
    assert MOBA_TOPK <= nblk <= LANES
    n_pages = past_len // page
    pages_per_step = 8
    while n_pages % pages_per_step:
        pages_per_step //= 2
    assert pages_per_step * page >= MOBA_BLOCK
    nsteps = n_pages // pages_per_step
    rows = ATTN_HEADS * dec_seq
    q4 = q.reshape(bsz, dec_seq, ATTN_HEADS, HEAD_DIM).transpose(0, 2, 1, 3)
    own = (np.arange(ATTN_HEADS)[:, None] // GRP == np.arange(ATTN_KV_HEADS)[None, :]).astype(np.float32)
    qb = (q4[:, :, :, None, :] * jnp.asarray(own)[None, :, None, :, None]).reshape(bsz, rows, ATTN_KV)
    qbt = jnp.pad(qb.transpose(0, 2, 1), ((0, 0), (0, 0), (0, LANES - rows)))

    def page_spec(j):
        return pl.BlockSpec((1, ATTN_KV, page), lambda b, n, pt: (pt[b, pages_per_step * n + j], 0, 0))

    def per_b(d1, d2):
        return pl.BlockSpec((1, d1, d2), lambda b, n, pt: (b, 0, 0))

    page_specs = [page_spec(j) for j in range(pages_per_step)]
    scores_t = pl.pallas_call(
        functools.partial(_moba_score_kernel, pages_per_step=pages_per_step),
        grid_spec=pltpu.PrefetchScalarGridSpec(
            num_scalar_prefetch=1, grid=(bsz, nsteps),
            in_specs=[per_b(ATTN_KV, LANES)] + page_specs,
            out_specs=per_b(nblk, LANES)),
        out_shape=jax.ShapeDtypeStruct((bsz, nblk, LANES), F32),
        compiler_params=_cparams(("parallel", "arbitrary"), 16 << 20),
        name="moba_sample_scores",
    )(page_table, qbt, *([cache_kt] * pages_per_step))

    out = pl.pallas_call(
        functools.partial(_moba_sample_kernel, pages_per_step=pages_per_step, nblk=nblk, dec_seq=dec_seq),
        grid_spec=pltpu.PrefetchScalarGridSpec(
            num_scalar_prefetch=1, grid=(bsz, nsteps),
            in_specs=[per_b(nblk, LANES), per_b(rows, ATTN_KV)] + page_specs + page_specs
                     + [per_b(dec_seq, ATTN_KV), per_b(dec_seq, ATTN_KV)],
            out_specs=per_b(rows, ATTN_KV),
            scratch_shapes=[pltpu.VMEM((rows, LANES), F32), pltpu.VMEM((rows, 1), F32), pltpu.VMEM((rows, 1), F32),
                            pltpu.VMEM((rows, ATTN_KV), F32)]),
        out_shape=jax.ShapeDtypeStruct((bsz, rows, ATTN_KV), F32),
        compiler_params=_cparams(("parallel", "arbitrary"), 24 << 20),
        name="moba_sample_attn",
    )(page_table, scores_t, qb, *([cache_kt] * pages_per_step), *([cache_vt] * pages_per_step), k_new, v_new)
    o5 = out.reshape(bsz, ATTN_HEADS, dec_seq, ATTN_KV_HEADS, HEAD_DIM)
    o4 = jnp.sum(o5 * jnp.asarray(own)[None, :, None, :, None], axis=3)
    return o4.transpose(0, 2, 1, 3).reshape(bsz, dec_seq, ATTN_Q)


def _mix_kernel(x_ref, ys_ref, ya_ref, gs_ref, ga_ref, gt1_ref, sc2_ref, sh2_ref, nw_ref,
                wso_ref, wao_ref, wo_ref, wrt_ref, brt_ref, x1_ref, h2_ref, comb_ref):
    merged = (jax.nn.sigmoid(gs_ref[0]) * _dot(ys_ref[0].astype(BF16), wso_ref[...])
              + jax.nn.sigmoid(ga_ref[0]) * _dot(ya_ref[0].astype(BF16), wao_ref[...]))
    x1 = x_ref[0] + gt1_ref[0] * _dot(merged.astype(BF16), wo_ref[...])
    x1_ref[0] = x1
    h2 = x1 * lax.rsqrt(jnp.mean(x1 * x1, axis=-1, keepdims=True) + EPS) * nw_ref[...]
    h2 = h2 * (1.0 + sc2_ref[0]) + sh2_ref[0]
    h2_ref[0] = h2.astype(BF16)

    lg = _dot_exact(h2, wrt_ref[...]) + brt_ref[...]
    lane = lax.broadcasted_iota(jnp.int32, lg.shape, 1).astype(F32)
    big = float(1 << 20)
    ninf = -jnp.inf
    gl = jnp.where(lane < N_EGROUPS, lg, ninf)
    gmax = jnp.max(gl, axis=1, keepdims=True)
    gidx = jnp.min(jnp.where(gl == gmax, lane, big), axis=1, keepdims=True)
    g_w = 1.0 / jnp.sum(jnp.exp(gl - gmax), axis=1, keepdims=True)
    lo = ROUTE_OFF + gidx * EXPERTS_PER_GROUP
    el = jnp.where((lane >= lo) & (lane < lo + EXPERTS_PER_GROUP), lg, ninf)
    m1 = jnp.max(el, axis=1, keepdims=True)
    i1 = jnp.min(jnp.where(el == m1, lane, big), axis=1, keepdims=True)
    el2 = jnp.where(lane == i1, ninf, el)
    m2 = jnp.max(el2, axis=1, keepdims=True)
    i2 = jnp.min(jnp.where(el2 == m2, lane, big), axis=1, keepdims=True)
    e2 = jnp.exp(m2 - m1)
    den = 1.0 / (1.0 + e2)
    comb_ref[0] = jnp.where(lane == i1, den * g_w, 0.0) + jnp.where(lane == i2, e2 * den * g_w, 0.0)


def _mix(x, y_ssd, y_att, gs, ga, gt1, sc2, sh2, norm_w, wso, wao, wo, wrt, brt):
    bsz, seqlen, _ = x.shape
    tm = 256
    assert seqlen % tm == 0
    nt = seqlen // tm
    mod_rows = gt1.shape[1]
    if mod_rows == 1:
        mod_spec = pl.BlockSpec((1, 1, D_MODEL), lambda b, i: (b, 0, 0))
    else:
        mod_spec = pl.BlockSpec((1, tm, D_MODEL), lambda b, i: (b, i, 0))

    def tok(width):
        return pl.BlockSpec((1, tm, width), lambda b, i: (b, i, 0))

    def full(d1, d2):
        return pl.BlockSpec((d1, d2), lambda b, i: (0, 0))

    vmem = 2 * 2 * (2 * D_MODEL * D_MODEL + ATTN_Q * D_MODEL) + 2 * tm * (8 * D_MODEL) * 4 + (12 << 20)
    return pl.pallas_call(
        _mix_kernel,
        grid=(bsz, nt),
        in_specs=[tok(D_MODEL), tok(SSD_INNER), tok(ATTN_Q), tok(D_MODEL), tok(D_MODEL),
                  mod_spec, mod_spec, mod_spec, full(1, D_MODEL),
                  full(SSD_INNER, D_MODEL), full(ATTN_Q, D_MODEL), full(D_MODEL, D_MODEL),
                  full(D_MODEL, LANES), full(1, LANES)],
        out_specs=[tok(D_MODEL), tok(D_MODEL), tok(LANES)],
        out_shape=[jax.ShapeDtypeStruct((bsz, seqlen, D_MODEL), F32),
                   jax.ShapeDtypeStruct((bsz, seqlen, D_MODEL), BF16),
                   jax.ShapeDtypeStruct((bsz, seqlen, LANES), F32)],
        compiler_params=_cparams(("parallel", "parallel"), vmem),
        name="mix",
    )(x, y_ssd, y_att, gs, ga, gt1, sc2, sh2, norm_w.reshape(1, -1), wso, wao, wo, wrt, brt)


def _moe_kernel(h2_ref, comb_ref, x1_ref, gt2_ref, fw_ref, wg_ref, wu_ref, wd_ref, y_ref, acc):
    e = pl.program_id(2)

    @pl.when(e == 0)
    def _():
        acc[...] = jnp.zeros(acc.shape, F32)

    h2 = h2_ref[0]
    lane = lax.broadcasted_iota(jnp.int32, comb_ref.shape[1:], 1)
    cw = jnp.sum(jnp.where(lane == e + ROUTE_OFF, comb_ref[0], 0.0), axis=1, keepdims=True)
    a = _dot(h2, wg_ref[0])
    u = _dot(h2, wu_ref[0])
    act = (_silu(a) * u * cw).astype(BF16)
    acc[...] += _dot(act, wd_ref[0])

    @pl.when(e == pl.num_programs(2) - 1)
    def _():
        x2 = x1_ref[0] + gt2_ref[0] * acc[...]
        y_ref[0] = x2 * lax.rsqrt(jnp.mean(x2 * x2, axis=-1, keepdims=True) + EPS) * fw_ref[...]


def _moe(h2, comb, x1, gt2, final_w, wg, wu, wd):
    bsz, seqlen, _ = h2.shape
    tm = 1024 if seqlen % 1024 == 0 else 256
    assert seqlen % tm == 0
    nt = seqlen // tm
    mod_rows = gt2.shape[1]
    if mod_rows == 1:
        mod_spec = pl.BlockSpec((1, 1, D_MODEL), lambda b, i, e: (b, 0, 0))
    else:
        mod_spec = pl.BlockSpec((1, tm, D_MODEL), lambda b, i, e: (b, i, 0))

    def tok(width):
        return pl.BlockSpec((1, tm, width), lambda b, i, e: (b, i, 0))

    vmem = 2 * tm * D_MODEL * (2 + 4 + 4) + tm * D_MODEL * 4 + 2 * 3 * D_MODEL * D_EXPERT * 2 + (16 << 20)
    return pl.pallas_call(
        _moe_kernel,
        grid=(bsz, nt, N_EXPERTS),
        in_specs=[tok(D_MODEL), tok(LANES), tok(D_MODEL), mod_spec,
                  pl.BlockSpec((1, D_MODEL), lambda b, i, e: (0, 0)),
                  pl.BlockSpec((1, D_MODEL, D_EXPERT), lambda b, i, e: (e, 0, 0)),
                  pl.BlockSpec((1, D_MODEL, D_EXPERT), lambda b, i, e: (e, 0, 0)),
                  pl.BlockSpec((1, D_EXPERT, D_MODEL), lambda b, i, e: (e, 0, 0))],
        out_specs=tok(D_MODEL),
        out_shape=jax.ShapeDtypeStruct((bsz, seqlen, D_MODEL), F32),
        scratch_shapes=[pltpu.VMEM((tm, D_MODEL), F32)],
        compiler_params=_cparams(("parallel", "parallel", "arbitrary"), vmem),
        name="moe",
    )(h2, comb, x1, gt2, final_w.reshape(1, -1), wg, wu, wd)


def _prep_weights(lp):
    w_in = lp["w_in"]
    sizes = (SSD_INNER, CONV_DIM, SSD_HEADS, ATTN_Q, ATTN_KV, ATTN_KV, D_MODEL, D_MODEL)
    pts = [int(p) for p in np.cumsum(sizes)[:-1]]
    z, xbc, dt, q, k, v, gs, ga = jnp.split(w_in, pts, axis=1)
    dt = jnp.pad(dt, ((0, 0), (0, LANES - SSD_HEADS)))
    w_in_r = jnp.concatenate([z, xbc, q, k, v, gs, ga, dt], axis=1).astype(BF16)
    wrt = jnp.concatenate([lp["w_group"], lp["w_router"]], axis=1)
    wrt = jnp.pad(wrt, ((0, 0), (0, LANES - wrt.shape[1]))).astype(F32)
    brt = jnp.concatenate([lp["b_group"], lp["b_router"]])
    brt = jnp.pad(brt, (0, LANES - brt.shape[0])).astype(F32).reshape(1, LANES)
    return dict(w_in_r=w_in_r, wrt=wrt, brt=brt,
                wso=lp["w_ssd_out"].astype(BF16), wao=lp["w_attn_out"].astype(BF16), wo=lp["w_o"].astype(BF16),
                wg=lp["w_gate_e"].astype(BF16), wu=lp["w_up_e"].astype(BF16), wd=lp["w_down_e"].astype(BF16))


def _mods(mod, per_row_repeat):
    parts = jnp.split(mod, 6, axis=-1)
    if per_row_repeat:
        return [jnp.repeat(p, per_row_repeat, axis=0)[None] for p in parts]
    return [p[:, None, :] for p in parts]


def _layer_prompt(x, mod, lp, pw, final_w):
    bsz, seqlen, _ = x.shape
    sh1, sc1, gt1, sh2, sc2, gt2 = _mods(mod, 0)
    tables = _rope_tables(np.arange(seqlen))
    (z, xbc, dt, q, k, v, gs, ga, kmean, kaug, vaug) = _in_proj(x, sc1, sh1, lp["norm1_w"], pw["w_in_r"], tables, True)
    conv0 = jnp.zeros((bsz, SSD_CONV - 1, CONV_DIM), F32)
    ssm0 = jnp.zeros((bsz, SSD_HEADS, SSD_HEADDIM, SSD_STATE), F32)
    y_ssd, conv_new, ssm_new = _ssd(xbc, z, dt, conv0, ssm0, lp["conv_w"], lp["conv_b"], lp["dt_bias"],
                                    lp["a_log"], lp["d_skip"], lp["ssd_norm_w"])
    y_att = _moba_prompt(q, kmean, kaug, vaug)
    x1, h2, comb = _mix(x, y_ssd, y_att, gs, ga, gt1, sc2, sh2, lp["norm2_w"],
                        pw["wso"], pw["wao"], pw["wo"], pw["wrt"], pw["brt"])
    y = _moe(h2, comb, x1, gt2, final_w, pw["wg"], pw["wu"], pw["wd"])
    kv_shape = (bsz, seqlen, ATTN_KV_HEADS, HEAD_DIM)
    return y, k.reshape(kv_shape), v.reshape(kv_shape), conv_new, ssm_new


def _layer_sample(x, mod, conv_buf, ssm_state, cache_k, cache_v, page_table, past_len, lp, pw, final_w):
    bsz, dec_seq, _ = x.shape
    ntok = bsz * dec_seq
    sh1, sc1, gt1, sh2, sc2, gt2 = _mods(mod, dec_seq)
    pos = past_len + (np.arange(ntok) % dec_seq)
    tables = _rope_tables(pos)
    xf = x.reshape(1, ntok, D_MODEL)
    z, xbc, dt, q, k, v, gs, ga = _in_proj(xf, sc1, sh1, lp["norm1_w"], pw["w_in_r"], tables, False)

    def per_b(t):
        return t.reshape(bsz, dec_seq, t.shape[-1])

    y_ssd, conv_new, ssm_new = _ssd(per_b(xbc), per_b(z), per_b(dt), conv_buf, ssm_state, lp["conv_w"],
                                    lp["conv_b"], lp["dt_bias"], lp["a_log"], lp["d_skip"], lp["ssd_norm_w"])
    n_pool, page = cache_k.shape[0], cache_k.shape[1]

    def position_minor(cache):
        return cache.transpose(0, 2, 3, 1).reshape(n_pool, ATTN_KV, page)

    y_att = _moba_sample(per_b(q), per_b(k), per_b(v), position_minor(cache_k), position_minor(cache_v),
                         page_table, past_len)
    x1, h2, comb = _mix(xf, y_ssd.reshape(1, ntok, -1), y_att.reshape(1, ntok, -1), gs, ga, gt1, sc2, sh2,
                        lp["norm2_w"], pw["wso"], pw["wao"], pw["wo"], pw["wrt"], pw["brt"])
    y = _moe(h2, comb, x1, gt2, final_w, pw["wg"], pw["wu"], pw["wd"])
    kv_shape = (bsz, dec_seq, ATTN_KV_HEADS, HEAD_DIM)
    return y.reshape(bsz, dec_seq, D_MODEL), k.reshape(kv_shape), v.reshape(kv_shape), conv_new, ssm_new


def kernel(x_prompt, x_sample, cache_k, cache_v, state_conv, state_ssm, page_table, c_prompt, c_sample,
           w_ada, b_ada, norm1_w, w_in, conv_w, conv_b, dt_bias, a_log, d_skip, ssd_norm_w,
           w_ssd_out, w_attn_out, w_o, norm2_w, w_group, b_group, w_router, b_router,
           w_gate_e, w_up_e, w_down_e, final_w):
    depth = w_in.shape[0]
    assert depth == 1, "the final RMSNorm is fused into the last layer's MoE kernel"
    nb_p, nb_s = c_prompt.shape[0], c_sample.shape[0]
    past_len = page_table.shape[1] * cache_k.shape[2]
    l = 0
    lp = dict(norm1_w=norm1_w[l], w_in=w_in[l], conv_w=conv_w[l], conv_b=conv_b[l], dt_bias=dt_bias[l],
              a_log=a_log[l], d_skip=d_skip[l], ssd_norm_w=ssd_norm_w[l], w_ssd_out=w_ssd_out[l],
              w_attn_out=w_attn_out[l], w_o=w_o[l], norm2_w=norm2_w[l], w_group=w_group[l], b_group=b_group[l],
              w_router=w_router[l], b_router=b_router[l], w_gate_e=w_gate_e[l], w_up_e=w_up_e[l],
              w_down_e=w_down_e[l])
    pw = _prep_weights(lp)
    c_all = jnp.concatenate([c_prompt, c_sample], axis=0)
    pad_rows = (-c_all.shape[0]) % SUBLANES
    mod = _ada_mod(jnp.pad(c_all, ((0, pad_rows), (0, 0))), w_ada[l], b_ada[l])
    yp, kp, vp, cp, sp = _layer_prompt(x_prompt, mod[:nb_p], lp, pw, final_w)
    ys, ks, vs, cs, ss = _layer_sample(x_sample, mod[nb_p:nb_p + nb_s], state_conv[l], state_ssm[l],
                                       cache_k[l], cache_v[l], page_table, past_len, lp, pw, final_w)
    return (yp, ys, kp[None], vp[None], cp[None], sp[None], ks[None], vs[None], cs[None], ss[None])
```

```python
import functools
import math

import numpy as np
import jax
import jax.numpy as jnp
from jax import lax
from jax.experimental import pallas as pl
from jax.experimental.pallas import tpu as pltpu

F32 = jnp.float32
BF16 = jnp.bfloat16
HIGHEST = lax.Precision.HIGHEST

D_MODEL = 1024
SSD_HEADDIM = 64
SSD_INNER = D_MODEL
SSD_HEADS = SSD_INNER // SSD_HEADDIM
SSD_GROUPS = 2
SSD_STATE = 128
SSD_CONV = 4
SSD_CHUNK = 128
SSD_BC = SSD_GROUPS * SSD_STATE
CONV_DIM = SSD_INNER + 2 * SSD_BC
ATTN_HEADS = 8
ATTN_KV_HEADS = 4
HEAD_DIM = 64
ATTN_Q = ATTN_HEADS * HEAD_DIM
ATTN_KV = ATTN_KV_HEADS * HEAD_DIM
ROT_DIM = HEAD_DIM // 4
ROPE_THETA = 500000.0
MOBA_BLOCK = 256
MOBA_TOPK = 3
N_EGROUPS = 4
EXPERTS_PER_GROUP = 4
N_EXPERTS = N_EGROUPS * EXPERTS_PER_GROUP
D_EXPERT = D_MODEL // 2
EPS = 1e-6

LANES = 128
SUBLANES = 8
VMEM_CAP = 56 << 20
NEG = -1e30
GRP = ATTN_HEADS // ATTN_KV_HEADS
ROUTE_OFF = N_EGROUPS

_SEG = {}
_off = 0
for _name, _w in (("z", SSD_INNER), ("xbc", CONV_DIM), ("q", ATTN_Q), ("k", ATTN_KV), ("v", ATTN_KV),
                  ("gs", D_MODEL), ("ga", D_MODEL), ("dt", LANES)):
    _SEG[_name] = (_off, _off + _w)
    _off += _w
IN_PAD = _off


def _cparams(semantics, vmem_bytes):
    return pltpu.CompilerParams(dimension_semantics=semantics,
                                vmem_limit_bytes=int(min(max(vmem_bytes, 16 << 20), VMEM_CAP)))


def _dot(a, b):
    return jnp.dot(a, b, preferred_element_type=F32)


def _dot_nt(a, b):
    return lax.dot_general(a, b, (((1,), (1,)), ((), ())), preferred_element_type=F32)


def _dot_exact(a, b):
    return jnp.dot(a, b, preferred_element_type=F32, precision=HIGHEST)


def _silu(x):
    return x * jax.nn.sigmoid(x)


def _mod_kernel(c_ref, w_ref, b_ref, o_ref):
    o_ref[...] = _dot_exact(_silu(c_ref[...]), w_ref[...]) + b_ref[...]


def _ada_mod(c_all, w_ada, b_ada):
    rows = c_all.shape[0]
    return pl.pallas_call(
        _mod_kernel,
        grid=(6,),
        in_specs=[pl.BlockSpec((rows, D_MODEL), lambda j: (0, 0)),
                  pl.BlockSpec((D_MODEL, D_MODEL), lambda j: (0, j)),
                  pl.BlockSpec((1, D_MODEL), lambda j: (0, j))],
        out_specs=pl.BlockSpec((rows, D_MODEL), lambda j: (0, j)),
        out_shape=jax.ShapeDtypeStruct((rows, 6 * D_MODEL), F32),
        compiler_params=_cparams(("arbitrary",), 24 << 20),
        name="ada_mod",
    )(c_all, w_ada, b_ada.reshape(1, -1))


def _rope_tables(pos):
    half = ROT_DIM // 2
    c = -2.0 * math.log(ROPE_THETA) / ROT_DIM
    c_hi = float(np.float32(c))
    j = jnp.arange(half, dtype=F32)
    inv_freq = jnp.exp(j * c_hi + j * (c - c_hi))
    ang = jnp.asarray(pos).astype(F32)[:, None] * inv_freq[None, :]
    cos, sin = jnp.cos(ang), jnp.sin(ang)
    n = pos.shape[0]
    one = jnp.ones((n, HEAD_DIM - ROT_DIM), F32)
    zero = jnp.zeros((n, half), F32)
    rest = jnp.zeros((n, HEAD_DIM - ROT_DIM), F32)
    a = jnp.concatenate([cos, cos, one], axis=1)
    s1 = jnp.concatenate([-sin, zero, rest], axis=1)
    s2 = jnp.concatenate([zero, sin, rest], axis=1)
    rep = LANES // HEAD_DIM
    return tuple(jnp.tile(t, (1, rep)) for t in (a, s1, s2))


def _inproj_kernel(*refs, attn_layout):
    (x_ref, sc_ref, sh_ref, nw_ref, w_ref, ra_ref, rs1_ref, rs2_ref,
     z_ref, xbc_ref, dt_ref, q_ref, k_ref, v_ref, gs_ref, ga_ref) = refs[:16]
    x = x_ref[0]
    h = x * lax.rsqrt(jnp.mean(x * x, axis=-1, keepdims=True) + EPS) * nw_ref[...]
    h = (h * (1.0 + sc_ref[0]) + sh_ref[0]).astype(BF16)

    def proj(name, out_ref):
        lo, hi = _SEG[name]
        step = 512
        for c in range(lo, hi, step):
            w = min(step, hi - c)
            out_ref[0, :, c - lo:c - lo + w] = _dot(h, w_ref[:, c:c + w])

    proj("z", z_ref)
    proj("xbc", xbc_ref)
    proj("gs", gs_ref)
    proj("ga", ga_ref)
    proj("dt", dt_ref)

    def rope(t):
        width = t.shape[1]
        reps = width // LANES
        a = jnp.concatenate([ra_ref[...]] * reps, axis=1)
        s1 = jnp.concatenate([rs1_ref[...]] * reps, axis=1)
        s2 = jnp.concatenate([rs2_ref[...]] * reps, axis=1)
        return t * a + pltpu.roll(t, width - ROT_DIM // 2, 1) * s1 + pltpu.roll(t, ROT_DIM // 2, 1) * s2

    lo, hi = _SEG["q"]
    q_ref[0] = rope(_dot(h, w_ref[:, lo:hi]))
    lo, hi = _SEG["k"]
    k = rope(_dot(h, w_ref[:, lo:hi]))
    k_ref[0] = k
    lo, hi = _SEG["v"]
    v = _dot(h, w_ref[:, lo:hi])
    v_ref[0] = v

    if attn_layout:
        kmean_ref, kaug_ref, vaug_ref = refs[16:19]
        tm = k.shape[0]
        kmean_ref[0, 0] = jnp.mean(k, axis=0, keepdims=True)
        lane = lax.broadcasted_iota(jnp.int32, (tm, HEAD_DIM), 1)
        blk_onehot = (lane == pl.program_id(1)).astype(F32)
        ones_col = (lane == 0).astype(F32)
        for kh in range(ATTN_KV_HEADS):
            sl = slice(kh * HEAD_DIM, (kh + 1) * HEAD_DIM)
            kaug_ref[0, kh] = jnp.concatenate([k[:, sl], blk_onehot], axis=1).astype(BF16)
            vaug_ref[0, kh] = jnp.concatenate([v[:, sl], ones_col], axis=1).astype(BF16)


def _in_proj(x, sc1, sh1, norm_w, w_in_r, tables, attn_layout):
    bsz, seqlen, _ = x.shape
    tm = MOBA_BLOCK
    assert seqlen % tm == 0
    nt = seqlen // tm
    mod_rows = sc1.shape[1]
    if mod_rows == 1:
        mod_spec = pl.BlockSpec((1, 1, D_MODEL), lambda b, i: (b, 0, 0))
    else:
        mod_spec = pl.BlockSpec((1, tm, D_MODEL), lambda b, i: (b, i, 0))
    tab_spec = pl.BlockSpec((tm, LANES), lambda b, i: (i, 0))

    def tok(width):
        return pl.BlockSpec((1, tm, width), lambda b, i: (b, i, 0))

    def shp(width):
        return jax.ShapeDtypeStruct((bsz, seqlen, width), F32)

    widths = (SSD_INNER, CONV_DIM, LANES, ATTN_Q, ATTN_KV, ATTN_KV, D_MODEL, D_MODEL)
    out_specs = [tok(w) for w in widths]
    out_shape = [shp(w) for w in widths]
    if attn_layout:
        assert nt <= HEAD_DIM
        out_specs += [pl.BlockSpec((1, 1, 1, ATTN_KV), lambda b, i: (b, i, 0, 0)),
                      pl.BlockSpec((1, ATTN_KV_HEADS, tm, LANES), lambda b, i: (b, 0, i, 0)),
                      pl.BlockSpec((1, ATTN_KV_HEADS, tm, LANES), lambda b, i: (b, 0, i, 0))]
        out_shape += [jax.ShapeDtypeStruct((bsz, nt, 1, ATTN_KV), F32),
                      jax.ShapeDtypeStruct((bsz, ATTN_KV_HEADS, seqlen, LANES), BF16),
                      jax.ShapeDtypeStruct((bsz, ATTN_KV_HEADS, seqlen, LANES), BF16)]
    vmem = 2 * (D_MODEL * IN_PAD * 2) + 2 * tm * (D_MODEL + IN_PAD + 4 * LANES) * 4 + (8 << 20)
    return pl.pallas_call(
        functools.partial(_inproj_kernel, attn_layout=attn_layout),
        grid=(bsz, nt),
        in_specs=[tok(D_MODEL), mod_spec, mod_spec,
                  pl.BlockSpec((1, D_MODEL), lambda b, i: (0, 0)),
                  pl.BlockSpec((D_MODEL, IN_PAD), lambda b, i: (0, 0)),
                  tab_spec, tab_spec, tab_spec],
        out_specs=out_specs,
        out_shape=out_shape,
        compiler_params=_cparams(("parallel", "parallel"), vmem),
        name="in_proj",
    )(x, sc1, sh1, norm_w.reshape(1, -1), w_in_r, *tables)


def _cumsum_rows(a):
    row = lax.broadcasted_iota(jnp.int32, a.shape, 0)
    s = 1
    while s < a.shape[0]:
        a = a + jnp.where(row >= s, pltpu.roll(a, s, 0), 0.0)
        s *= 2
    return a


def _ssd_kernel(xbc_ref, z_ref, dt_ref, cbuf_ref, s0_ref, cw_ref, cb_ref, dtb_ref, alog_ref, dsk_ref, nw_ref,
                y_ref, cnew_ref, sout_ref, xext, st, ysc, *, rows):
    ch = SSD_CHUNK
    pad = SUBLANES
    c = pl.program_id(1)
    last = pl.num_programs(1) - 1

    @pl.when(c == 0)
    def _():
        xext[0:pad, :] = jnp.zeros((pad, CONV_DIM), F32)
        xext[pad - (SSD_CONV - 1):pad, :] = cbuf_ref[0]
        st[...] = s0_ref[0].T

    xext[pad:pad + rows, :] = xbc_ref[0]
    if rows < ch:
        xext[pad + rows:pad + ch, :] = jnp.zeros((ch - rows, CONV_DIM), F32)

    conv = cb_ref[...] + xext[pad:pad + ch, :] * cw_ref[SSD_CONV - 1:SSD_CONV, :]
    for w in range(SSD_CONV - 1):
        off = pad - (SSD_CONV - 1) + w
        conv = conv + xext[off:off + ch, :] * cw_ref[w:w + 1, :]
    act = _silu(conv)

    @pl.when(c == last)
    def _():
        cnew_ref[0] = xext[pad + rows - (SSD_CONV - 1):pad + rows, :]

    xext[0:pad, :] = xext[ch:ch + pad, :]

    xs = act[:, :SSD_INNER]
    dt_raw = dt_ref[0]
    if rows < ch:
        dt_raw = jnp.concatenate([dt_raw, jnp.zeros((ch - rows, LANES), F32)], axis=0)
    dt_in = dt_raw + dtb_ref[...]
    dt = jnp.maximum(dt_in, 0.0) + jnp.log1p(jnp.exp(-jnp.abs(dt_in)))
    row = lax.broadcasted_iota(jnp.int32, (ch, LANES), 0)
    col = lax.broadcasted_iota(jnp.int32, (ch, LANES), 1)
    if rows < ch:
        dt = jnp.where(row < rows, dt, 0.0)
    a_cs = _cumsum_rows(dt * (-jnp.exp(alog_ref[...])))
    a_cs_t = a_cs.T
    dt_t = dt.T
    causal = row >= col

    heads_per_pair = LANES // SSD_HEADDIM
    assert heads_per_pair == 2 and SSD_STATE == ch
    first_of_pair = col < SSD_HEADDIM
    for g in range(SSD_GROUPS):
        bg = act[:, SSD_INNER + g * SSD_STATE:SSD_INNER + (g + 1) * SSD_STATE]
        cg = act[:, SSD_INNER + SSD_BC + g * SSD_STATE:SSD_INNER + SSD_BC + (g + 1) * SSD_STATE]
        gmat = _dot_nt(cg.astype(BF16), bg.astype(BF16))
        bg_t = bg.T
        for pair in range(SSD_HEADS // SSD_GROUPS // heads_per_pair):
            hd0 = g * (SSD_HEADS // SSD_GROUPS) + pair * heads_per_pair
            sl = slice(hd0 * SSD_HEADDIM, hd0 * SSD_HEADDIM + LANES)
            x_pair = xs[:, sl]
            s_pair = st[:, sl]
            x_diag = [jnp.where(first_of_pair, x_pair, 0.0).astype(BF16),
                      jnp.where(first_of_pair, 0.0, x_pair).astype(BF16)]
            s_diag = [jnp.where(first_of_pair, s_pair, 0.0).astype(BF16),
                      jnp.where(first_of_pair, 0.0, s_pair).astype(BF16)]
            m_parts, c_parts, b_parts, decay = [], [], [], []
            for hd in range(hd0, hd0 + heads_per_pair):
                acs_col = a_cs[:, hd:hd + 1]
                acs_row = a_cs_t[hd:hd + 1, :]
                dt_row = dt_t[hd:hd + 1, :]
                acs_last = a_cs_t[hd:hd + 1, ch - 1:ch]
                lmat = jnp.where(causal, jnp.exp(acs_col - acs_row), 0.0)
                m_parts.append((gmat * lmat * dt_row).astype(BF16))
                c_parts.append((cg * jnp.exp(acs_col)).astype(BF16))
                b_parts.append((bg_t * (jnp.exp(acs_last - acs_row) * dt_row)).astype(BF16))
                decay.append(jnp.exp(acs_last))
            ysc[:, sl] = _dot(jnp.concatenate(m_parts + c_parts, axis=1), jnp.concatenate(x_diag + s_diag, axis=0))
            st[:, sl] = (jnp.where(first_of_pair[:1], decay[0], decay[1]) * s_pair
                         + _dot(jnp.concatenate(b_parts, axis=1), jnp.concatenate(x_diag, axis=0)))

    y = (ysc[0:rows, :] + xs[:rows] * dsk_ref[...]) * _silu(z_ref[0])
    gw = SSD_INNER // SSD_GROUPS
    for g in range(SSD_GROUPS):
        yg = y[:, g * gw:(g + 1) * gw]
        yg = yg * lax.rsqrt(jnp.mean(yg * yg, axis=-1, keepdims=True) + EPS)
        y_ref[0, :, g * gw:(g + 1) * gw] = yg * nw_ref[:, g * gw:(g + 1) * gw]

    @pl.when(c == last)
    def _():
        sout_ref[0] = st[...].T


def _ssd(xbc, z, dt, conv_buf, ssm_state, conv_w, conv_b, dt_bias, a_log, d_skip, norm_w):
    bsz, seqlen, _ = xbc.shape
    rows = SSD_CHUNK if seqlen % SSD_CHUNK == 0 else seqlen
    assert rows == seqlen or rows == SSD_CHUNK
    assert rows % SUBLANES == 0 and rows >= SSD_CONV - 1
    nc = seqlen // rows
    hp = SSD_HEADS * SSD_HEADDIM

    def tok(width):
        return pl.BlockSpec((1, rows, width), lambda b, c: (b, c, 0))

    def per_b(d1, d2):
        return pl.BlockSpec((1, d1, d2), lambda b, c: (b, 0, 0))

    def vec(width, r=1):
        return pl.BlockSpec((r, width), lambda b, c: (0, 0))

    def pad_lanes(v):
        return jnp.pad(v.astype(F32), (0, LANES - v.shape[0])).reshape(1, LANES)

    y, conv_new, s_new = pl.pallas_call(
        functools.partial(_ssd_kernel, rows=rows),
        grid=(bsz, nc),
        in_specs=[tok(CONV_DIM), tok(SSD_INNER), tok(LANES),
                  per_b(SSD_CONV - 1, CONV_DIM), per_b(hp, SSD_STATE),
                  vec(CONV_DIM, SSD_CONV), vec(CONV_DIM), vec(LANES), vec(LANES), vec(SSD_INNER), vec(SSD_INNER)],
        out_specs=[tok(SSD_INNER), per_b(SSD_CONV - 1, CONV_DIM), per_b(hp, SSD_STATE)],
        out_shape=[jax.ShapeDtypeStruct((bsz, seqlen, SSD_INNER), F32),
                   jax.ShapeDtypeStruct((bsz, SSD_CONV - 1, CONV_DIM), F32),
                   jax.ShapeDtypeStruct((bsz, hp, SSD_STATE), F32)],
        scratch_shapes=[pltpu.VMEM((SSD_CHUNK + SUBLANES, CONV_DIM), F32),
                        pltpu.VMEM((SSD_STATE, hp), F32),
                        pltpu.VMEM((SSD_CHUNK, SSD_INNER), F32)],
        compiler_params=_cparams(("parallel", "arbitrary"), 40 << 20),
        name="ssd",
    )(xbc, z, dt, conv_buf, ssm_state.reshape(bsz, hp, SSD_STATE),
      conv_w, conv_b.reshape(1, -1), pad_lanes(dt_bias), pad_lanes(a_log),
      jnp.repeat(d_skip.astype(F32), SSD_HEADDIM).reshape(1, -1), norm_w.reshape(1, -1))
    return y, conv_new, s_new.reshape(bsz, SSD_HEADS, SSD_HEADDIM, SSD_STATE)


def _select_topk(scores, lane, valid):
    sc = jnp.where(valid, scores, -jnp.inf)
    sel = jnp.zeros(scores.shape, F32)
    lane_f = lane.astype(F32)
    for _ in range(MOBA_TOPK):
        m = jnp.max(sc, axis=1, keepdims=True)
        idx = jnp.min(jnp.where(sc == m, lane_f, float(1 << 20)), axis=1, keepdims=True)
        idx = jnp.where(m > -jnp.inf, idx, -1.0)
        hit = lane_f == idx
        sel = jnp.where(hit, 1.0, sel)
        sc = jnp.where(hit, -jnp.inf, sc)
    return sel > 0.5


def _moba_prompt_kernel(q_ref, kmt_ref, kaug_ref, vaug_ref, o_ref, *, chunk):
    i = pl.program_id(2)
    tq = MOBA_BLOCK
    q2 = q_ref[0]
    qs = jnp.concatenate([q2[:, g * HEAD_DIM:(g + 1) * HEAD_DIM] for g in range(GRP)], axis=0)
    rows = qs.shape[0]
    lane = lax.broadcasted_iota(jnp.int32, (rows, LANES), 1)
    scores = _dot_exact(qs, kmt_ref[0, 0])
    sel = _select_topk(scores, lane, (lane >= HEAD_DIM) & (lane < HEAD_DIM + i))
    pen = jnp.where(sel, 0.0, NEG)
    q_pad = jnp.concatenate([qs * (HEAD_DIM ** -0.5), jnp.zeros((rows, LANES - HEAD_DIM), F32)], axis=1)
    q_aug = jnp.where(lane < HEAD_DIM, q_pad, pen).astype(BF16)
    q_own = q_pad.astype(BF16)

    def update(m, acc, s, vb):
        m_new = jnp.maximum(m, jnp.max(s, axis=1, keepdims=True))
        p = jnp.exp(s - m_new)
        acc = jnp.exp(m - m_new) * acc + _dot(p.astype(BF16), vb)
        return m_new, acc

    nchunks = (i * tq + chunk - 1) // chunk
    last_chunk = kaug_ref.shape[2] // chunk - 1

    def logits(c):
        start = pl.multiple_of(jnp.minimum(c, last_chunk) * chunk, chunk)
        return _dot_nt(q_aug, kaug_ref[0, 0, pl.ds(start, chunk), :])

    def body(c, carry):
        m, acc, s = carry
        s_next = logits(c + 1)
        start = pl.multiple_of(c * chunk, chunk)
        m, acc = update(m, acc, s, vaug_ref[0, 0, pl.ds(start, chunk), :])
        return m, acc, s_next

    init = (jnp.full((rows, 1), -jnp.inf, F32), jnp.zeros((rows, LANES), F32), logits(0))
    m, acc, _ = lax.fori_loop(0, nchunks, body, init)
    start = pl.multiple_of(i * tq, tq)
    s = _dot_nt(q_own, kaug_ref[0, 0, pl.ds(start, tq), :])
    r = lax.broadcasted_iota(jnp.int32, (rows, tq), 0)
    cidx = lax.broadcasted_iota(jnp.int32, (rows, tq), 1)
    qpos = r
    for g in range(1, GRP):
        qpos = jnp.where(r >= g * tq, r - g * tq, qpos)
    s = jnp.where(cidx <= qpos, s, NEG)
    _, acc = update(m, acc, s, vaug_ref[0, 0, pl.ds(start, tq), :])
    out = acc[:, :HEAD_DIM] / acc[:, HEAD_DIM:HEAD_DIM + 1]
    o_ref[0] = jnp.concatenate([out[g * tq:(g + 1) * tq] for g in range(GRP)], axis=1)


def _moba_prompt(q, kmean, kaug, vaug):
    bsz, seqlen, _ = q.shape
    nb = seqlen // MOBA_BLOCK
    assert GRP * HEAD_DIM == LANES and nb <= LANES - HEAD_DIM
    km = kmean.reshape(bsz, nb, ATTN_KV_HEADS, HEAD_DIM).transpose(0, 2, 3, 1)
    kmt = jnp.pad(km, ((0, 0), (0, 0), (0, 0), (HEAD_DIM, LANES - HEAD_DIM - nb)))
    kv_spec = pl.BlockSpec((1, 1, seqlen, LANES), lambda b, h, i: (b, h, 0, 0))
    vmem = 2 * 2 * seqlen * LANES * 2 + (24 << 20)
    chunk = 2 * MOBA_BLOCK if seqlen % (2 * MOBA_BLOCK) == 0 else MOBA_BLOCK
    return pl.pallas_call(
        functools.partial(_moba_prompt_kernel, chunk=chunk),
        grid=(bsz, ATTN_KV_HEADS, nb),
        in_specs=[pl.BlockSpec((1, MOBA_BLOCK, LANES), lambda b, h, i: (b, i, h)),
                  pl.BlockSpec((1, 1, HEAD_DIM, LANES), lambda b, h, i: (b, h, 0, 0)),
                  kv_spec, kv_spec],
        out_specs=pl.BlockSpec((1, MOBA_BLOCK, LANES), lambda b, h, i: (b, i, h)),
        out_shape=jax.ShapeDtypeStruct((bsz, seqlen, ATTN_Q), F32),
        compiler_params=_cparams(("parallel", "parallel", "arbitrary"), vmem),
        name="moba_prompt",
    )(q, kmt, kaug, vaug)


def _moba_score_kernel(pt_ref, qbt_ref, *refs, pages_per_step):
    del pt_ref
    k_refs, sc_ref = refs[:pages_per_step], refs[pages_per_step]
    n = pl.program_id(1)
    pages_per_block = MOBA_BLOCK // LANES
    blocks_per_step = pages_per_step // pages_per_block
    qbt = qbt_ref[0]
    for jb in range(blocks_per_step):
        ksum = k_refs[jb * pages_per_block][0]
        for j in range(1, pages_per_block):
            ksum = ksum + k_refs[jb * pages_per_block + j][0]
        kmean = jnp.sum(ksum, axis=1, keepdims=True) * (1.0 / MOBA_BLOCK)
        sc_ref[0, pl.ds(n * blocks_per_step + jb, 1), :] = jnp.sum(qbt * kmean, axis=0, keepdims=True)


def _moba_sample_kernel(pt_ref, sc_ref, qb_ref, *refs, pages_per_step, nblk, dec_seq):
    del pt_ref
    k_refs = refs[:pages_per_step]
    v_refs = refs[pages_per_step:2 * pages_per_step]
    kn_ref, vn_ref, o_ref, pen_sc, m_sc, l_sc, acc_sc = refs[2 * pages_per_step:]
    n = pl.program_id(1)
    rows = qb_ref.shape[1]
    pages_per_block = MOBA_BLOCK // LANES
    blocks_per_step = pages_per_step // pages_per_block
    lane = lax.broadcasted_iota(jnp.int32, (rows, LANES), 1)

    @pl.when(n == 0)
    def _():
        sc_t = sc_ref[0]
        sc_sq = jnp.concatenate([sc_t, jnp.zeros((LANES - sc_t.shape[0], LANES), F32)], axis=0)
        scores = sc_sq.T[:rows]
        sel = _select_topk(scores, lane, lane < nblk)
        pen_sc[...] = jnp.where(sel, 0.0, NEG)
        m_sc[...] = jnp.full(m_sc.shape, -jnp.inf, F32)
        l_sc[...] = jnp.zeros(l_sc.shape, F32)
        acc_sc[...] = jnp.zeros(acc_sc.shape, F32)

    qb = (qb_ref[0] * (HEAD_DIM ** -0.5)).astype(BF16)

    def update(s, pv_fn):
        m = m_sc[...]
        m_new = jnp.maximum(m, jnp.max(s, axis=1, keepdims=True))
        p = jnp.exp(s - m_new)
        alpha = jnp.exp(m - m_new)
        l_sc[...] = alpha * l_sc[...] + jnp.sum(p, axis=1, keepdims=True)
        acc_sc[...] = alpha * acc_sc[...] + pv_fn(p.astype(BF16))
        m_sc[...] = m_new

    pen = pen_sc[...]
    parts = []
    for jb in range(blocks_per_step):
        pen_col = jnp.sum(jnp.where(lane == n * blocks_per_step + jb, pen, 0.0), axis=1, keepdims=True)
        for j in range(pages_per_block):
            parts.append(_dot(qb, k_refs[jb * pages_per_block + j][0].astype(BF16)) + pen_col)
    s = jnp.concatenate(parts, axis=1)

    def pv_past(p):
        pv = _dot_nt(p[:, :LANES], v_refs[0][0].astype(BF16))
        for j in range(1, pages_per_step):
            pv = pv + _dot_nt(p[:, j * LANES:(j + 1) * LANES], v_refs[j][0].astype(BF16))
        return pv

    update(s, pv_past)

    @pl.when(n == pl.num_programs(1) - 1)
    def _():
        zpad = jnp.zeros((LANES - dec_seq, kn_ref.shape[2]), F32)
        kn = jnp.concatenate([kn_ref[0], zpad], axis=0).astype(BF16)
        vn = jnp.concatenate([vn_ref[0], zpad], axis=0).astype(BF16)
        s_own = _dot_nt(qb, kn)
        r = lax.broadcasted_iota(jnp.int32, (rows, LANES), 0)
        qpos = r & (dec_seq - 1)
        s_own = jnp.where((lane <= qpos) & (lane < dec_seq), s_own, NEG)
        update(s_own, lambda p: _dot(p, vn))
        o_ref[0] = acc_sc[...] / l_sc[...]


def _moba_sample(q, k_new, v_new, cache_kt, cache_vt, page_table, past_len):
    bsz, dec_seq, _ = q.shape
    page = cache_kt.shape[2]
    assert page == LANES and MOBA_BLOCK % page == 0
    assert past_len % MOBA_BLOCK == 0 and dec_seq == SUBLANES
    nblk = past_len // MOBA_BLOCK
    assert MOBA_TOPK <= nblk <= LANES
    n_pages = past_len // page
    pages_per_step = 16
    while n_pages % pages_per_step:
        pages_per_step //= 2
    assert pages_per_step * page >= MOBA_BLOCK
    nsteps = n_pages // pages_per_step
    rows = ATTN_HEADS * dec_seq
    q4 = q.reshape(bsz, dec_seq, ATTN_HEADS, HEAD_DIM).transpose(0, 2, 1, 3)
    own = (np.arange(ATTN_HEADS)[:, None] // GRP == np.arange(ATTN_KV_HEADS)[None, :]).astype(np.float32)
    qb = (q4[:, :, :, None, :] * jnp.asarray(own)[None, :, None, :, None]).reshape(bsz, rows, ATTN_KV)
    qbt = jnp.pad(qb.transpose(0, 2, 1), ((0, 0), (0, 0), (0, LANES - rows)))

    def page_spec(j):
        return pl.BlockSpec((1, ATTN_KV, page), lambda b, n, pt: (pt[b, pages_per_step * n + j], 0, 0))

    def per_b(d1, d2):
        return pl.BlockSpec((1, d1, d2), lambda b, n, pt: (b, 0, 0))

    page_specs = [page_spec(j) for j in range(pages_per_step)]
    scores_t = pl.pallas_call(
        functools.partial(_moba_score_kernel, pages_per_step=pages_per_step),
        grid_spec=pltpu.PrefetchScalarGridSpec(
            num_scalar_prefetch=1, grid=(bsz, nsteps),
            in_specs=[per_b(ATTN_KV, LANES)] + page_specs,
            out_specs=per_b(nblk, LANES)),
        out_shape=jax.ShapeDtypeStruct((bsz, nblk, LANES), F32),
        compiler_params=_cparams(("parallel", "arbitrary"), 16 << 20),
        name="moba_sample_scores",
    )(page_table, qbt, *([cache_kt] * pages_per_step))

    out = pl.pallas_call(
        functools.partial(_moba_sample_kernel, pages_per_step=pages_per_step, nblk=nblk, dec_seq=dec_seq),
        grid_spec=pltpu.PrefetchScalarGridSpec(
            num_scalar_prefetch=1, grid=(bsz, nsteps),
            in_specs=[per_b(nblk, LANES), per_b(rows, ATTN_KV)] + page_specs + page_specs
                     + [per_b(dec_seq, ATTN_KV), per_b(dec_seq, ATTN_KV)],
            out_specs=per_b(rows, ATTN_KV),
            scratch_shapes=[pltpu.VMEM((rows, LANES), F32), pltpu.VMEM((rows, 1), F32), pltpu.VMEM((rows, 1), F32),
                            pltpu.VMEM((rows, ATTN_KV), F32)]),
        out_shape=jax.ShapeDtypeStruct((bsz, rows, ATTN_KV), F32),
        compiler_params=_cparams(("parallel", "arbitrary"), 24 << 20),
        name="moba_sample_attn",
    )(page_table, scores_t, qb, *([cache_kt] * pages_per_step), *([cache_vt] * pages_per_step), k_new, v_new)
    o5 = out.reshape(bsz, ATTN_HEADS, dec_seq, ATTN_KV_HEADS, HEAD_DIM)
    o4 = jnp.sum(o5 * jnp.asarray(own)[None, :, None, :, None], axis=3)
    return o4.transpose(0, 2, 1, 3).reshape(bsz, dec_seq, ATTN_Q)


def _mix_kernel(x_ref, ys_ref, ya_ref, gs_ref, ga_ref, gt1_ref, sc2_ref, sh2_ref, nw_ref,
                wso_ref, wao_ref, wo_ref, wrt_ref, brt_ref, x1_ref, h2_ref, comb_ref):
    merged = (jax.nn.sigmoid(gs_ref[0]) * _dot(ys_ref[0].astype(BF16), wso_ref[...])
              + jax.nn.sigmoid(ga_ref[0]) * _dot(ya_ref[0].astype(BF16), wao_ref[...]))
    x1 = x_ref[0] + gt1_ref[0] * _dot(merged.astype(BF16), wo_ref[...])
    x1_ref[0] = x1
    h2 = x1 * lax.rsqrt(jnp.mean(x1 * x1, axis=-1, keepdims=True) + EPS) * nw_ref[...]
    h2 = h2 * (1.0 + sc2_ref[0]) + sh2_ref[0]
    h2_ref[0] = h2.astype(BF16)

    h_hi = h2.astype(BF16)
    h_lo = (h2 - h_hi.astype(F32)).astype(BF16)
    hw = _dot(h_hi, wrt_ref[...])
    lg = hw[:, :LANES] + hw[:, LANES:] + _dot(h_lo, wrt_ref[:, :LANES]) + brt_ref[...]
    lane = lax.broadcasted_iota(jnp.int32, lg.shape, 1).astype(F32)
    big = float(1 << 20)
    ninf = -jnp.inf
    gl = jnp.where(lane < N_EGROUPS, lg, ninf)
    gmax = jnp.max(gl, axis=1, keepdims=True)
    gidx = jnp.min(jnp.where(gl == gmax, lane, big), axis=1, keepdims=True)
    g_w = 1.0 / jnp.sum(jnp.exp(gl - gmax), axis=1, keepdims=True)
    lo = ROUTE_OFF + gidx * EXPERTS_PER_GROUP
    el = jnp.where((lane >= lo) & (lane < lo + EXPERTS_PER_GROUP), lg, ninf)
    m1 = jnp.max(el, axis=1, keepdims=True)
    i1 = jnp.min(jnp.where(el == m1, lane, big), axis=1, keepdims=True)
    el2 = jnp.where(lane == i1, ninf, el)
    m2 = jnp.max(el2, axis=1, keepdims=True)
    i2 = jnp.min(jnp.where(el2 == m2, lane, big), axis=1, keepdims=True)
    e2 = jnp.exp(m2 - m1)
    den = 1.0 / (1.0 + e2)
    comb_ref[0] = jnp.where(lane == i1, den * g_w, 0.0) + jnp.where(lane == i2, e2 * den * g_w, 0.0)


def _mix(x, y_ssd, y_att, gs, ga, gt1, sc2, sh2, norm_w, wso, wao, wo, wrt, brt):
    bsz, seqlen, _ = x.shape
    tm = 256
    assert seqlen % tm == 0
    nt = seqlen // tm
    mod_rows = gt1.shape[1]
    if mod_rows == 1:
        mod_spec = pl.BlockSpec((1, 1, D_MODEL), lambda b, i: (b, 0, 0))
    else:
        mod_spec = pl.BlockSpec((1, tm, D_MODEL), lambda b, i: (b, i, 0))

    def tok(width):
        return pl.BlockSpec((1, tm, width), lambda b, i: (b, i, 0))

    def full(d1, d2):
        return pl.BlockSpec((d1, d2), lambda b, i: (0, 0))

    vmem = 2 * 2 * (2 * D_MODEL * D_MODEL + ATTN_Q * D_MODEL) + 2 * tm * (8 * D_MODEL) * 4 + (12 << 20)
    return pl.pallas_call(
        _mix_kernel,
        grid=(bsz, nt),
        in_specs=[tok(D_MODEL), tok(SSD_INNER), tok(ATTN_Q), tok(D_MODEL), tok(D_MODEL),
                  mod_spec, mod_spec, mod_spec, full(1, D_MODEL),
                  full(SSD_INNER, D_MODEL), full(ATTN_Q, D_MODEL), full(D_MODEL, D_MODEL),
                  full(D_MODEL, 2 * LANES), full(1, LANES)],
        out_specs=[tok(D_MODEL), tok(D_MODEL), tok(LANES)],
        out_shape=[jax.ShapeDtypeStruct((bsz, seqlen, D_MODEL), F32),
                   jax.ShapeDtypeStruct((bsz, seqlen, D_MODEL), BF16),
                   jax.ShapeDtypeStruct((bsz, seqlen, LANES), F32)],
        compiler_params=_cparams(("parallel", "parallel"), vmem),
        name="mix",
    )(x, y_ssd, y_att, gs, ga, gt1, sc2, sh2, norm_w.reshape(1, -1), wso, wao, wo, wrt, brt)


def _moe_kernel(h2_ref, comb_ref, x1_ref, gt2_ref, fw_ref, wg_ref, wu_ref, wd_ref, y_ref, acc):
    e = pl.program_id(2)

    @pl.when(e == 0)
    def _():
        acc[...] = jnp.zeros(acc.shape, F32)

    h2 = h2_ref[0]
    lane = lax.broadcasted_iota(jnp.int32, comb_ref.shape[1:], 1)
    cw = jnp.sum(jnp.where(lane == e + ROUTE_OFF, comb_ref[0], 0.0), axis=1, keepdims=True)
    a = _dot(h2, wg_ref[0])
    u = _dot(h2, wu_ref[0])
    act = (_silu(a) * u * cw).astype(BF16)
    acc[...] += _dot(act, wd_ref[0])

    @pl.when(e == pl.num_programs(2) - 1)
    def _():
        x2 = x1_ref[0] + gt2_ref[0] * acc[...]
        y_ref[0] = x2 * lax.rsqrt(jnp.mean(x2 * x2, axis=-1, keepdims=True) + EPS) * fw_ref[...]


def _moe(h2, comb, x1, gt2, final_w, wg, wu, wd):
    bsz, seqlen, _ = h2.shape
    tm = 1024 if seqlen % 1024 == 0 else 256
    assert seqlen % tm == 0
    nt = seqlen // tm
    mod_rows = gt2.shape[1]
    if mod_rows == 1:
        mod_spec = pl.BlockSpec((1, 1, D_MODEL), lambda b, i, e: (b, 0, 0))
    else:
        mod_spec = pl.BlockSpec((1, tm, D_MODEL), lambda b, i, e: (b, i, 0))

    def tok(width):
        return pl.BlockSpec((1, tm, width), lambda b, i, e: (b, i, 0))

    vmem = 2 * tm * D_MODEL * (2 + 4 + 4) + tm * D_MODEL * 4 + 2 * 3 * D_MODEL * D_EXPERT * 2 + (16 << 20)
    return pl.pallas_call(
        _moe_kernel,
        grid=(bsz, nt, N_EXPERTS),
        in_specs=[tok(D_MODEL), tok(LANES), tok(D_MODEL), mod_spec,
                  pl.BlockSpec((1, D_MODEL), lambda b, i, e: (0, 0)),
                  pl.BlockSpec((1, D_MODEL, D_EXPERT), lambda b, i, e: (e, 0, 0)),
                  pl.BlockSpec((1, D_MODEL, D_EXPERT), lambda b, i, e: (e, 0, 0)),
                  pl.BlockSpec((1, D_EXPERT, D_MODEL), lambda b, i, e: (e, 0, 0))],
        out_specs=tok(D_MODEL),
        out_shape=jax.ShapeDtypeStruct((bsz, seqlen, D_MODEL), F32),
        scratch_shapes=[pltpu.VMEM((tm, D_MODEL), F32)],
        compiler_params=_cparams(("parallel", "parallel", "arbitrary"), vmem),
        name="moe",
    )(h2, comb, x1, gt2, final_w.reshape(1, -1), wg, wu, wd)


def _prep_weights(lp):
    w_in = lp["w_in"]
    sizes = (SSD_INNER, CONV_DIM, SSD_HEADS, ATTN_Q, ATTN_KV, ATTN_KV, D_MODEL, D_MODEL)
    pts = [int(p) for p in np.cumsum(sizes)[:-1]]
    z, xbc, dt, q, k, v, gs, ga = jnp.split(w_in, pts, axis=1)
    dt = jnp.pad(dt, ((0, 0), (0, LANES - SSD_HEADS)))
    w_in_r = jnp.concatenate([z, xbc, q, k, v, gs, ga, dt], axis=1).astype(BF16)
    wrt = jnp.concatenate([lp["w_group"], lp["w_router"]], axis=1)
    wrt = jnp.pad(wrt, ((0, 0), (0, LANES - wrt.shape[1]))).astype(F32)
    wrt_hi = wrt.astype(BF16)
    wrt = jnp.concatenate([wrt_hi, (wrt - wrt_hi.astype(F32)).astype(BF16)], axis=1)
    brt = jnp.concatenate([lp["b_group"], lp["b_router"]])
    brt = jnp.pad(brt, (0, LANES - brt.shape[0])).astype(F32).reshape(1, LANES)
    return dict(w_in_r=w_in_r, wrt=wrt, brt=brt,
                wso=lp["w_ssd_out"].astype(BF16), wao=lp["w_attn_out"].astype(BF16), wo=lp["w_o"].astype(BF16),
                wg=lp["w_gate_e"].astype(BF16), wu=lp["w_up_e"].astype(BF16), wd=lp["w_down_e"].astype(BF16))


def _mods(mod, per_row_repeat):
    parts = jnp.split(mod, 6, axis=-1)
    if per_row_repeat:
        return [jnp.repeat(p, per_row_repeat, axis=0)[None] for p in parts]
    return [p[:, None, :] for p in parts]


def _layer_prompt(x, mod, lp, pw, final_w):
    bsz, seqlen, _ = x.shape
    sh1, sc1, gt1, sh2, sc2, gt2 = _mods(mod, 0)
    tables = _rope_tables(np.arange(seqlen))
    (z, xbc, dt, q, k, v, gs, ga, kmean, kaug, vaug) = _in_proj(x, sc1, sh1, lp["norm1_w"], pw["w_in_r"], tables, True)
    conv0 = jnp.zeros((bsz, SSD_CONV - 1, CONV_DIM), F32)
    ssm0 = jnp.zeros((bsz, SSD_HEADS, SSD_HEADDIM, SSD_STATE), F32)
    y_ssd, conv_new, ssm_new = _ssd(xbc, z, dt, conv0, ssm0, lp["conv_w"], lp["conv_b"], lp["dt_bias"],
                                    lp["a_log"], lp["d_skip"], lp["ssd_norm_w"])
    y_att = _moba_prompt(q, kmean, kaug, vaug)
    x1, h2, comb = _mix(x, y_ssd, y_att, gs, ga, gt1, sc2, sh2, lp["norm2_w"],
                        pw["wso"], pw["wao"], pw["wo"], pw["wrt"], pw["brt"])
    y = _moe(h2, comb, x1, gt2, final_w, pw["wg"], pw["wu"], pw["wd"])
    kv_shape = (bsz, seqlen, ATTN_KV_HEADS, HEAD_DIM)
    return y, k.reshape(kv_shape), v.reshape(kv_shape), conv_new, ssm_new


def _layer_sample(x, mod, conv_buf, ssm_state, cache_k, cache_v, page_table, past_len, lp, pw, final_w):
    bsz, dec_seq, _ = x.shape
    ntok = bsz * dec_seq
    sh1, sc1, gt1, sh2, sc2, gt2 = _mods(mod, dec_seq)
    pos = past_len + (np.arange(ntok) % dec_seq)
    tables = _rope_tables(pos)
    xf = x.reshape(1, ntok, D_MODEL)
    z, xbc, dt, q, k, v, gs, ga = _in_proj(xf, sc1, sh1, lp["norm1_w"], pw["w_in_r"], tables, False)

    def per_b(t):
        return t.reshape(bsz, dec_seq, t.shape[-1])

    y_ssd, conv_new, ssm_new = _ssd(per_b(xbc), per_b(z), per_b(dt), conv_buf, ssm_state, lp["conv_w"],
                                    lp["conv_b"], lp["dt_bias"], lp["a_log"], lp["d_skip"], lp["ssd_norm_w"])
    n_pool, page = cache_k.shape[0], cache_k.shape[1]

    def position_minor(cache):
        return cache.transpose(0, 2, 3, 1).reshape(n_pool, ATTN_KV, page)

    y_att = _moba_sample(per_b(q), per_b(k), per_b(v), position_minor(cache_k), position_minor(cache_v),
                         page_table, past_len)
    x1, h2, comb = _mix(xf, y_ssd.reshape(1, ntok, -1), y_att.reshape(1, ntok, -1), gs, ga, gt1, sc2, sh2,
                        lp["norm2_w"], pw["wso"], pw["wao"], pw["wo"], pw["wrt"], pw["brt"])
    y = _moe(h2, comb, x1, gt2, final_w, pw["wg"], pw["wu"], pw["wd"])
    kv_shape = (bsz, dec_seq, ATTN_KV_HEADS, HEAD_DIM)
    return y.reshape(bsz, dec_seq, D_MODEL), k.reshape(kv_shape), v.reshape(kv_shape), conv_new, ssm_new


def kernel(x_prompt, x_sample, cache_k, cache_v, state_conv, state_ssm, page_table, c_prompt, c_sample,
           w_ada, b_ada, norm1_w, w_in, conv_w, conv_b, dt_bias, a_log, d_skip, ssd_norm_w,
           w_ssd_out, w_attn_out, w_o, norm2_w, w_group, b_group, w_router, b_router,
           w_gate_e, w_up_e, w_down_e, final_w):
    depth = w_in.shape[0]
    assert depth == 1, "the final RMSNorm is fused into the last layer's MoE kernel"
    nb_p, nb_s = c_prompt.shape[0], c_sample.shape[0]
    past_len = page_table.shape[1] * cache_k.shape[2]
    l = 0
    lp = dict(norm1_w=norm1_w[l], w_in=w_in[l], conv_w=conv_w[l], conv_b=conv_b[l], dt_bias=dt_bias[l],
              a_log=a_log[l], d_skip=d_skip[l], ssd_norm_w=ssd_norm_w[l], w_ssd_out=w_ssd_out[l],
              w_attn_out=w_attn_out[l], w_o=w_o[l], norm2_w=norm2_w[l], w_group=w_group[l], b_group=b_group[l],
              w_router=w_router[l], b_router=b_router[l], w_gate_e=w_gate_e[l], w_up_e=w_up_e[l],
              w_down_e=w_down_e[l])
    pw = _prep_weights(lp)
    c_all = jnp.concatenate([c_prompt, c_sample], axis=0)
    pad_rows = (-c_all.shape[0]) % SUBLANES
    mod = _ada_mod(jnp.pad(c_all, ((0, pad_rows), (0, 0))), w_ada[l], b_ada[l])
    yp, kp, vp, cp, sp = _layer_prompt(x_prompt, mod[:nb_p], lp, pw, final_w)
    ys, ks, vs, cs, ss = _layer_sample(x_sample, mod[nb_p:nb_p + nb_s], state_conv[l], state_ssm[l],
                                       cache_k[l], cache_v[l], page_table, past_len, lp, pw, final_w)
    return (yp, ys, kp[None], vp[None], cp[None], sp[None], ks[None], vs[None], cs[None], ss[None])
```

```python
import functools
import math

import numpy as np
import jax
import jax.numpy as jnp
from jax import lax
from jax.experimental import pallas as pl
from jax.experimental.pallas import tpu as pltpu

F32 = jnp.float32
BF16 = jnp.bfloat16
HIGHEST = lax.Precision.HIGHEST

D_MODEL = 1024
SSD_HEADDIM = 64
SSD_INNER = D_MODEL
SSD_HEADS = SSD_INNER // SSD_HEADDIM
SSD_GROUPS = 2
SSD_STATE = 128
SSD_CONV = 4
SSD_CHUNK = 128
SSD_BC = SSD_GROUPS * SSD_STATE
CONV_DIM = SSD_INNER + 2 * SSD_BC
ATTN_HEADS = 8
ATTN_KV_HEADS = 4
HEAD_DIM = 64
ATTN_Q = ATTN_HEADS * HEAD_DIM
ATTN_KV = ATTN_KV_HEADS * HEAD_DIM
ROT_DIM = HEAD_DIM // 4
ROPE_THETA = 500000.0
MOBA_BLOCK = 256
MOBA_TOPK = 3
N_EGROUPS = 4
EXPERTS_PER_GROUP = 4
N_EXPERTS = N_EGROUPS * EXPERTS_PER_GROUP
D_EXPERT = D_MODEL // 2
EPS = 1e-6

LANES = 128
SUBLANES = 8
VMEM_CAP = 56 << 20
NEG = -1e30
GRP = ATTN_HEADS // ATTN_KV_HEADS
ROUTE_OFF = N_EGROUPS

_SEG = {}
_off = 0
for _name, _w in (("z", SSD_INNER), ("xbc", CONV_DIM), ("q", ATTN_Q), ("k", ATTN_KV), ("v", ATTN_KV),
                  ("gs", D_MODEL), ("ga", D_MODEL), ("dt", LANES)):
    _SEG[_name] = (_off, _off + _w)
    _off += _w
IN_PAD = _off


def _cparams(semantics, vmem_bytes):
    return pltpu.CompilerParams(dimension_semantics=semantics,
                                vmem_limit_bytes=int(min(max(vmem_bytes, 16 << 20), VMEM_CAP)))


def _dot(a, b):
    return jnp.dot(a, b, preferred_element_type=F32)


def _dot_nt(a, b):
    return lax.dot_general(a, b, (((1,), (1,)), ((), ())), preferred_element_type=F32)


def _dot_exact(a, b):
    return jnp.dot(a, b, preferred_element_type=F32, precision=HIGHEST)


def _silu(x):
    return x * jax.nn.sigmoid(x)


def _mod_kernel(c_ref, w_ref, b_ref, o_ref):
    o_ref[...] = _dot_exact(_silu(c_ref[...]), w_ref[...]) + b_ref[...]


def _ada_mod(c_all, w_ada, b_ada):
    rows = c_all.shape[0]
    return pl.pallas_call(
        _mod_kernel,
        grid=(6,),
        in_specs=[pl.BlockSpec((rows, D_MODEL), lambda j: (0, 0)),
                  pl.BlockSpec((D_MODEL, D_MODEL), lambda j: (0, j)),
                  pl.BlockSpec((1, D_MODEL), lambda j: (0, j))],
        out_specs=pl.BlockSpec((rows, D_MODEL), lambda j: (0, j)),
        out_shape=jax.ShapeDtypeStruct((rows, 6 * D_MODEL), F32),
        compiler_params=_cparams(("arbitrary",), 24 << 20),
        name="ada_mod",
    )(c_all, w_ada, b_ada.reshape(1, -1))


def _rope_tables(pos):
    half = ROT_DIM // 2
    c = -2.0 * math.log(ROPE_THETA) / ROT_DIM
    c_hi = float(np.float32(c))
    j = jnp.arange(half, dtype=F32)
    inv_freq = jnp.exp(j * c_hi + j * (c - c_hi))
    ang = jnp.asarray(pos).astype(F32)[:, None] * inv_freq[None, :]
    cos, sin = jnp.cos(ang), jnp.sin(ang)
    n = pos.shape[0]
    one = jnp.ones((n, HEAD_DIM - ROT_DIM), F32)
    zero = jnp.zeros((n, half), F32)
    rest = jnp.zeros((n, HEAD_DIM - ROT_DIM), F32)
    a = jnp.concatenate([cos, cos, one], axis=1)
    s1 = jnp.concatenate([-sin, zero, rest], axis=1)
    s2 = jnp.concatenate([zero, sin, rest], axis=1)
    rep = LANES // HEAD_DIM
    return tuple(jnp.tile(t, (1, rep)) for t in (a, s1, s2))


def _inproj_kernel(*refs, attn_layout):
    (x_ref, sc_ref, sh_ref, nw_ref, w_ref, ra_ref, rs1_ref, rs2_ref,
     z_ref, xbc_ref, dt_ref, q_ref, k_ref, v_ref, gs_ref, ga_ref) = refs[:16]
    x = x_ref[0]
    h = x * lax.rsqrt(jnp.mean(x * x, axis=-1, keepdims=True) + EPS) * nw_ref[...]
    h = (h * (1.0 + sc_ref[0]) + sh_ref[0]).astype(BF16)

    def proj(name, out_ref):
        lo, hi = _SEG[name]
        step = 512
        for c in range(lo, hi, step):
            w = min(step, hi - c)
            out_ref[0, :, c - lo:c - lo + w] = _dot(h, w_ref[:, c:c + w])

    proj("z", z_ref)
    proj("xbc", xbc_ref)
    proj("gs", gs_ref)
    proj("ga", ga_ref)
    proj("dt", dt_ref)

    def rope(t):
        width = t.shape[1]
        reps = width // LANES
        a = jnp.concatenate([ra_ref[...]] * reps, axis=1)
        s1 = jnp.concatenate([rs1_ref[...]] * reps, axis=1)
        s2 = jnp.concatenate([rs2_ref[...]] * reps, axis=1)
        return t * a + pltpu.roll(t, width - ROT_DIM // 2, 1) * s1 + pltpu.roll(t, ROT_DIM // 2, 1) * s2

    lo, hi = _SEG["q"]
    q_ref[0] = rope(_dot(h, w_ref[:, lo:hi]))
    lo, hi = _SEG["k"]
    k = rope(_dot(h, w_ref[:, lo:hi]))
    k_ref[0] = k
    lo, hi = _SEG["v"]
    v = _dot(h, w_ref[:, lo:hi])
    v_ref[0] = v

    if attn_layout:
        kmean_ref, kaug_ref, vaug_ref = refs[16:19]
        tm = k.shape[0]
        kmean_ref[0, 0] = jnp.mean(k, axis=0, keepdims=True)
        lane = lax.broadcasted_iota(jnp.int32, (tm, HEAD_DIM), 1)
        blk_onehot = (lane == pl.program_id(1)).astype(F32)
        ones_col = (lane == 0).astype(F32)
        for kh in range(ATTN_KV_HEADS):
            sl = slice(kh * HEAD_DIM, (kh + 1) * HEAD_DIM)
            kaug_ref[0, kh] = jnp.concatenate([k[:, sl], blk_onehot], axis=1).astype(BF16)
            vaug_ref[0, kh] = jnp.concatenate([v[:, sl], ones_col], axis=1).astype(BF16)


def _in_proj(x, sc1, sh1, norm_w, w_in_r, tables, attn_layout):
    bsz, seqlen, _ = x.shape
    tm = MOBA_BLOCK
    assert seqlen % tm == 0
    nt = seqlen // tm
    mod_rows = sc1.shape[1]
    if mod_rows == 1:
        mod_spec = pl.BlockSpec((1, 1, D_MODEL), lambda b, i: (b, 0, 0))
    else:
        mod_spec = pl.BlockSpec((1, tm, D_MODEL), lambda b, i: (b, i, 0))
    tab_spec = pl.BlockSpec((tm, LANES), lambda b, i: (i, 0))

    def tok(width):
        return pl.BlockSpec((1, tm, width), lambda b, i: (b, i, 0))

    def shp(width):
        return jax.ShapeDtypeStruct((bsz, seqlen, width), F32)

    widths = (SSD_INNER, CONV_DIM, LANES, ATTN_Q, ATTN_KV, ATTN_KV, D_MODEL, D_MODEL)
    out_specs = [tok(w) for w in widths]
    out_shape = [shp(w) for w in widths]
    if attn_layout:
        assert nt <= HEAD_DIM
        out_specs += [pl.BlockSpec((1, 1, 1, ATTN_KV), lambda b, i: (b, i, 0, 0)),
                      pl.BlockSpec((1, ATTN_KV_HEADS, tm, LANES), lambda b, i: (b, 0, i, 0)),
                      pl.BlockSpec((1, ATTN_KV_HEADS, tm, LANES), lambda b, i: (b, 0, i, 0))]
        out_shape += [jax.ShapeDtypeStruct((bsz, nt, 1, ATTN_KV), F32),
                      jax.ShapeDtypeStruct((bsz, ATTN_KV_HEADS, seqlen, LANES), BF16),
                      jax.ShapeDtypeStruct((bsz, ATTN_KV_HEADS, seqlen, LANES), BF16)]
    vmem = 2 * (D_MODEL * IN_PAD * 2) + 2 * tm * (D_MODEL + IN_PAD + 4 * LANES) * 4 + (8 << 20)
    return pl.pallas_call(
        functools.partial(_inproj_kernel, attn_layout=attn_layout),
        grid=(bsz, nt),
        in_specs=[tok(D_MODEL), mod_spec, mod_spec,
                  pl.BlockSpec((1, D_MODEL), lambda b, i: (0, 0)),
                  pl.BlockSpec((D_MODEL, IN_PAD), lambda b, i: (0, 0)),
                  tab_spec, tab_spec, tab_spec],
        out_specs=out_specs,
        out_shape=out_shape,
        compiler_params=_cparams(("parallel", "parallel"), vmem),
        name="in_proj",
    )(x, sc1, sh1, norm_w.reshape(1, -1), w_in_r, *tables)


def _cumsum_rows(a):
    row = lax.broadcasted_iota(jnp.int32, a.shape, 0)
    s = 1
    while s < a.shape[0]:
        a = a + jnp.where(row >= s, pltpu.roll(a, s, 0), 0.0)
        s *= 2
    return a


def _ssd_kernel(xbc_ref, z_ref, dt_ref, cbuf_ref, s0_ref, cw_ref, cb_ref, dtb_ref, alog_ref, dsk_ref, nw_ref,
                y_ref, cnew_ref, sout_ref, xext, st, ysc, *, rows):
    ch = SSD_CHUNK
    pad = SUBLANES
    c = pl.program_id(1)
    last = pl.num_programs(1) - 1

    @pl.when(c == 0)
    def _():
        xext[0:pad, :] = jnp.zeros((pad, CONV_DIM), F32)
        xext[pad - (SSD_CONV - 1):pad, :] = cbuf_ref[0]
        st[...] = s0_ref[0].T

    xext[pad:pad + rows, :] = xbc_ref[0]
    if rows < ch:
        xext[pad + rows:pad + ch, :] = jnp.zeros((ch - rows, CONV_DIM), F32)

    conv = cb_ref[...] + xext[pad:pad + ch, :] * cw_ref[SSD_CONV - 1:SSD_CONV, :]
    for w in range(SSD_CONV - 1):
        off = pad - (SSD_CONV - 1) + w
        conv = conv + xext[off:off + ch, :] * cw_ref[w:w + 1, :]
    act = _silu(conv)

    @pl.when(c == last)
    def _():
        cnew_ref[0] = xext[pad + rows - (SSD_CONV - 1):pad + rows, :]

    xext[0:pad, :] = xext[ch:ch + pad, :]

    xs = act[:, :SSD_INNER]
    dt_raw = dt_ref[0]
    if rows < ch:
        dt_raw = jnp.concatenate([dt_raw, jnp.zeros((ch - rows, LANES), F32)], axis=0)
    dt_in = dt_raw + dtb_ref[...]
    dt = jnp.maximum(dt_in, 0.0) + jnp.log1p(jnp.exp(-jnp.abs(dt_in)))
    row = lax.broadcasted_iota(jnp.int32, (ch, LANES), 0)
    col = lax.broadcasted_iota(jnp.int32, (ch, LANES), 1)
    if rows < ch:
        dt = jnp.where(row < rows, dt, 0.0)
    a_cs = _cumsum_rows(dt * (-jnp.exp(alog_ref[...])))
    a_cs_t = a_cs.T
    dt_t = dt.T
    causal = row >= col

    heads_per_pair = LANES // SSD_HEADDIM
    assert heads_per_pair == 2 and SSD_STATE == ch
    first_of_pair = col < SSD_HEADDIM
    for g in range(SSD_GROUPS):
        bg = act[:, SSD_INNER + g * SSD_STATE:SSD_INNER + (g + 1) * SSD_STATE]
        cg = act[:, SSD_INNER + SSD_BC + g * SSD_STATE:SSD_INNER + SSD_BC + (g + 1) * SSD_STATE]
        gmat = _dot_nt(cg.astype(BF16), bg.astype(BF16))
        bg_t = bg.T
        for pair in range(SSD_HEADS // SSD_GROUPS // heads_per_pair):
            hd0 = g * (SSD_HEADS // SSD_GROUPS) + pair * heads_per_pair
            sl = slice(hd0 * SSD_HEADDIM, hd0 * SSD_HEADDIM + LANES)
            x_pair = xs[:, sl]
            s_pair = st[:, sl]
            x_diag = [jnp.where(first_of_pair, x_pair, 0.0).astype(BF16),
                      jnp.where(first_of_pair, 0.0, x_pair).astype(BF16)]
            s_diag = [jnp.where(first_of_pair, s_pair, 0.0).astype(BF16),
                      jnp.where(first_of_pair, 0.0, s_pair).astype(BF16)]
            m_parts, c_parts, b_parts, decay = [], [], [], []
            for hd in range(hd0, hd0 + heads_per_pair):
                acs_col = a_cs[:, hd:hd + 1]
                acs_row = a_cs_t[hd:hd + 1, :]
                dt_row = dt_t[hd:hd + 1, :]
                acs_last = a_cs_t[hd:hd + 1, ch - 1:ch]
                lmat = jnp.where(causal, jnp.exp(acs_col - acs_row), 0.0)
                m_parts.append((gmat * lmat * dt_row).astype(BF16))
                c_parts.append((cg * jnp.exp(acs_col)).astype(BF16))
                b_parts.append((bg_t * (jnp.exp(acs_last - acs_row) * dt_row)).astype(BF16))
                decay.append(jnp.exp(acs_last))
            ysc[:, sl] = _dot(jnp.concatenate(m_parts + c_parts, axis=1), jnp.concatenate(x_diag + s_diag, axis=0))
            st[:, sl] = (jnp.where(first_of_pair[:1], decay[0], decay[1]) * s_pair
                         + _dot(jnp.concatenate(b_parts, axis=1), jnp.concatenate(x_diag, axis=0)))

    y = (ysc[0:rows, :] + xs[:rows] * dsk_ref[...]) * _silu(z_ref[0])
    gw = SSD_INNER // SSD_GROUPS
    for g in range(SSD_GROUPS):
        yg = y[:, g * gw:(g + 1) * gw]
        yg = yg * lax.rsqrt(jnp.mean(yg * yg, axis=-1, keepdims=True) + EPS)
        y_ref[0, :, g * gw:(g + 1) * gw] = yg * nw_ref[:, g * gw:(g + 1) * gw]

    @pl.when(c == last)
    def _():
        sout_ref[0] = st[...].T


def _ssd(xbc, z, dt, conv_buf, ssm_state, conv_w, conv_b, dt_bias, a_log, d_skip, norm_w):
    bsz, seqlen, _ = xbc.shape
    rows = SSD_CHUNK if seqlen % SSD_CHUNK == 0 else seqlen
    assert rows == seqlen or rows == SSD_CHUNK
    assert rows % SUBLANES == 0 and rows >= SSD_CONV - 1
    nc = seqlen // rows
    hp = SSD_HEADS * SSD_HEADDIM

    def tok(width):
        return pl.BlockSpec((1, rows, width), lambda b, c: (b, c, 0))

    def per_b(d1, d2):
        return pl.BlockSpec((1, d1, d2), lambda b, c: (b, 0, 0))

    def vec(width, r=1):
        return pl.BlockSpec((r, width), lambda b, c: (0, 0))

    def pad_lanes(v):
        return jnp.pad(v.astype(F32), (0, LANES - v.shape[0])).reshape(1, LANES)

    y, conv_new, s_new = pl.pallas_call(
        functools.partial(_ssd_kernel, rows=rows),
        grid=(bsz, nc),
        in_specs=[tok(CONV_DIM), tok(SSD_INNER), tok(LANES),
                  per_b(SSD_CONV - 1, CONV_DIM), per_b(hp, SSD_STATE),
                  vec(CONV_DIM, SSD_CONV), vec(CONV_DIM), vec(LANES), vec(LANES), vec(SSD_INNER), vec(SSD_INNER)],
        out_specs=[tok(SSD_INNER), per_b(SSD_CONV - 1, CONV_DIM), per_b(hp, SSD_STATE)],
        out_shape=[jax.ShapeDtypeStruct((bsz, seqlen, SSD_INNER), F32),
                   jax.ShapeDtypeStruct((bsz, SSD_CONV - 1, CONV_DIM), F32),
                   jax.ShapeDtypeStruct((bsz, hp, SSD_STATE), F32)],
        scratch_shapes=[pltpu.VMEM((SSD_CHUNK + SUBLANES, CONV_DIM), F32),
                        pltpu.VMEM((SSD_STATE, hp), F32),
                        pltpu.VMEM((SSD_CHUNK, SSD_INNER), F32)],
        compiler_params=_cparams(("parallel", "arbitrary"), 40 << 20),
        name="ssd",
    )(xbc, z, dt, conv_buf, ssm_state.reshape(bsz, hp, SSD_STATE),
      conv_w, conv_b.reshape(1, -1), pad_lanes(dt_bias), pad_lanes(a_log),
      jnp.repeat(d_skip.astype(F32), SSD_HEADDIM).reshape(1, -1), norm_w.reshape(1, -1))
    return y, conv_new, s_new.reshape(bsz, SSD_HEADS, SSD_HEADDIM, SSD_STATE)


def _select_topk(scores, valid, axis):
    sc = scores if valid is None else jnp.where(valid, scores, -jnp.inf)
    sel = jnp.zeros(scores.shape, F32)
    pos_f = lax.broadcasted_iota(jnp.int32, scores.shape, axis).astype(F32)
    for _ in range(MOBA_TOPK):
        m = jnp.max(sc, axis=axis, keepdims=True)
        idx = jnp.min(jnp.where(sc == m, pos_f, float(1 << 20)), axis=axis, keepdims=True)
        idx = jnp.where(m > -jnp.inf, idx, -1.0)
        hit = pos_f == idx
        sel = jnp.where(hit, 1.0, sel)
        sc = jnp.where(hit, -jnp.inf, sc)
    return sel > 0.5


def _moba_prompt_kernel(q_ref, km_ref, kaug_ref, vaug_ref, o_ref, s_a, s_b, *, chunk):
    i = pl.program_id(2)
    tq = MOBA_BLOCK
    q2 = q_ref[0]
    qs = jnp.concatenate([q2[:, g * HEAD_DIM:(g + 1) * HEAD_DIM] for g in range(GRP)], axis=0)
    rows = qs.shape[0]
    lane = lax.broadcasted_iota(jnp.int32, (rows, LANES), 1)
    scores_t = lax.dot_general(km_ref[0, 0], qs, (((1,), (1,)), ((), ())), preferred_element_type=F32,
                               precision=HIGHEST)
    nbp = scores_t.shape[0]
    blk = lax.broadcasted_iota(jnp.int32, scores_t.shape, 0)
    sel_t = _select_topk(scores_t, blk < i, 0)
    pen_t = jnp.concatenate([jnp.zeros((HEAD_DIM, rows), F32), jnp.where(sel_t, 0.0, NEG),
                             jnp.full((LANES - HEAD_DIM - nbp, rows), NEG, F32)], axis=0)
    pen = pen_t.T
    q_pad = jnp.concatenate([qs * (HEAD_DIM ** -0.5), jnp.zeros((rows, LANES - HEAD_DIM), F32)], axis=1)
    q_aug = jnp.where(lane < HEAD_DIM, q_pad, pen).astype(BF16)
    q_own = q_pad.astype(BF16)

    def update(m, acc, s, vb):
        m_new = jnp.maximum(m, jnp.max(s, axis=1, keepdims=True))
        p = jnp.exp(s - m_new)
        acc = jnp.exp(m - m_new) * acc + _dot(p.astype(BF16), vb)
        return m_new, acc

    nchunks = (i * tq + chunk - 1) // chunk
    total_chunks = kaug_ref.shape[2] // chunk
    npairs = (nchunks + 1) // 2

    def logits(c):
        start = pl.multiple_of(jnp.minimum(c, total_chunks - 1) * chunk, chunk)
        return _dot_nt(q_aug, kaug_ref[0, 0, pl.ds(start, chunk), :])

    def values(c):
        return vaug_ref[0, 0, pl.ds(pl.multiple_of(c * chunk, chunk), chunk), :]

    def body(t, carry):
        m, acc = carry
        c0 = 2 * t
        s_b[...] = logits(c0 + 1)
        m, acc = update(m, acc, s_a[...], values(c0))
        s_a[...] = logits(c0 + 2)
        m, acc = update(m, acc, s_b[...], values(c0 + 1))
        return m, acc

    start = pl.multiple_of(i * tq, tq)
    s_own = _dot_nt(q_own, kaug_ref[0, 0, pl.ds(start, tq), :])
    r = lax.broadcasted_iota(jnp.int32, (rows, tq), 0)
    cidx = lax.broadcasted_iota(jnp.int32, (rows, tq), 1)
    qpos = r
    for g in range(1, GRP):
        qpos = jnp.where(r >= g * tq, r - g * tq, qpos)
    s_own = jnp.where(cidx <= qpos, s_own, NEG)

    s_a[...] = logits(0)
    init = (jnp.full((rows, 1), -jnp.inf, F32), jnp.zeros((rows, LANES), F32))
    m, acc = lax.fori_loop(0, npairs, body, init)
    _, acc = update(m, acc, s_own, vaug_ref[0, 0, pl.ds(start, tq), :])
    out = acc[:, :HEAD_DIM] / acc[:, HEAD_DIM:HEAD_DIM + 1]
    o_ref[0] = jnp.concatenate([out[g * tq:(g + 1) * tq] for g in range(GRP)], axis=1)


def _moba_prompt(q, kmean, kaug, vaug):
    bsz, seqlen, _ = q.shape
    nb = seqlen // MOBA_BLOCK
    assert GRP * HEAD_DIM == LANES and nb <= LANES - HEAD_DIM
    nbp = -(-nb // SUBLANES) * SUBLANES
    km = kmean.reshape(bsz, nb, ATTN_KV_HEADS, HEAD_DIM).transpose(0, 2, 1, 3)
    km = jnp.pad(km, ((0, 0), (0, 0), (0, nbp - nb), (0, 0)))
    kv_spec = pl.BlockSpec((1, 1, seqlen, LANES), lambda b, h, i: (b, h, 0, 0))
    vmem = 2 * 2 * seqlen * LANES * 2 + (24 << 20)
    chunk = 2 * MOBA_BLOCK
    assert seqlen % (2 * chunk) == 0, "the sweep walks the keys two chunks at a time"
    rows = GRP * MOBA_BLOCK
    return pl.pallas_call(
        functools.partial(_moba_prompt_kernel, chunk=chunk),
        scratch_shapes=[pltpu.VMEM((rows, chunk), F32), pltpu.VMEM((rows, chunk), F32)],
        grid=(bsz, ATTN_KV_HEADS, nb),
        in_specs=[pl.BlockSpec((1, MOBA_BLOCK, LANES), lambda b, h, i: (b, i, h)),
                  pl.BlockSpec((1, 1, nbp, HEAD_DIM), lambda b, h, i: (b, h, 0, 0)),
                  kv_spec, kv_spec],
        out_specs=pl.BlockSpec((1, MOBA_BLOCK, LANES), lambda b, h, i: (b, i, h)),
        out_shape=jax.ShapeDtypeStruct((bsz, seqlen, ATTN_Q), F32),
        compiler_params=_cparams(("parallel", "parallel", "arbitrary"), vmem),
        name="moba_prompt",
    )(q, km, kaug, vaug)


def _moba_score_kernel(pt_ref, qbt_ref, *refs, pages_per_step):
    del pt_ref
    k_refs, sc_ref = refs[:pages_per_step], refs[pages_per_step]
    n = pl.program_id(1)
    pages_per_block = MOBA_BLOCK // LANES
    blocks_per_step = pages_per_step // pages_per_block
    qbt = qbt_ref[0]
    for jb in range(blocks_per_step):
        ksum = k_refs[jb * pages_per_block][0]
        for j in range(1, pages_per_block):
            ksum = ksum + k_refs[jb * pages_per_block + j][0]
        kmean = jnp.sum(ksum, axis=1, keepdims=True) * (1.0 / MOBA_BLOCK)
        sc_ref[0, pl.ds(n * blocks_per_step + jb, 1), :] = jnp.sum(qbt * kmean, axis=0, keepdims=True)


def _moba_sample_kernel(pt_ref, sc_ref, qb_ref, *refs, pages_per_step, nblk, dec_seq):
    del pt_ref
    k_refs = refs[:pages_per_step]
    v_refs = refs[pages_per_step:2 * pages_per_step]
    kn_ref, vn_ref, o_ref, pen_sc, m_sc, l_sc, acc_sc = refs[2 * pages_per_step:]
    n = pl.program_id(1)
    rows = qb_ref.shape[1]
    pages_per_block = MOBA_BLOCK // LANES
    blocks_per_step = pages_per_step // pages_per_block
    lane = lax.broadcasted_iota(jnp.int32, (rows, LANES), 1)

    @pl.when(n == 0)
    def _():
        sc_t = sc_ref[0]
        sel_t = _select_topk(sc_t, None, 0)
        pen_t = jnp.concatenate([jnp.where(sel_t, 0.0, NEG), jnp.full((LANES - nblk, LANES), NEG, F32)], axis=0)
        pen_sc[...] = pen_t.T[:rows]
        m_sc[...] = jnp.full(m_sc.shape, -jnp.inf, F32)
        l_sc[...] = jnp.zeros(l_sc.shape, F32)
        acc_sc[...] = jnp.zeros(acc_sc.shape, F32)

    qb = (qb_ref[0] * (HEAD_DIM ** -0.5)).astype(BF16)

    def update(s, pv_fn):
        m = m_sc[...]
        m_new = jnp.maximum(m, jnp.max(s, axis=1, keepdims=True))
        p = jnp.exp(s - m_new)
        alpha = jnp.exp(m - m_new)
        l_sc[...] = alpha * l_sc[...] + jnp.sum(p, axis=1, keepdims=True)
        acc_sc[...] = alpha * acc_sc[...] + pv_fn(p.astype(BF16))
        m_sc[...] = m_new

    pen = pen_sc[...]
    parts = []
    for jb in range(blocks_per_step):
        pen_col = jnp.sum(jnp.where(lane == n * blocks_per_step + jb, pen, 0.0), axis=1, keepdims=True)
        for j in range(pages_per_block):
            parts.append(_dot(qb, k_refs[jb * pages_per_block + j][0].astype(BF16)) + pen_col)
    s = jnp.concatenate(parts, axis=1)

    def pv_past(p):
        pv = _dot_nt(p[:, :LANES], v_refs[0][0].astype(BF16))
        for j in range(1, pages_per_step):
            pv = pv + _dot_nt(p[:, j * LANES:(j + 1) * LANES], v_refs[j][0].astype(BF16))
        return pv

    update(s, pv_past)

    @pl.when(n == pl.num_programs(1) - 1)
    def _():
        zpad = jnp.zeros((LANES - dec_seq, kn_ref.shape[2]), F32)
        kn = jnp.concatenate([kn_ref[0], zpad], axis=0).astype(BF16)
        vn = jnp.concatenate([vn_ref[0], zpad], axis=0).astype(BF16)
        s_own = _dot_nt(qb, kn)
        r = lax.broadcasted_iota(jnp.int32, (rows, LANES), 0)
        qpos = r & (dec_seq - 1)
        s_own = jnp.where((lane <= qpos) & (lane < dec_seq), s_own, NEG)
        update(s_own, lambda p: _dot(p, vn))
        o_ref[0] = acc_sc[...] / l_sc[...]


def _moba_sample(q, k_new, v_new, cache_kt, cache_vt, page_table, past_len):
    bsz, dec_seq, _ = q.shape
    page = cache_kt.shape[2]
    assert page == LANES and MOBA_BLOCK % page == 0
    assert past_len % MOBA_BLOCK == 0 and dec_seq == SUBLANES
    nblk = past_len // MOBA_BLOCK
    assert MOBA_TOPK <= nblk <= LANES
    n_pages = past_len // page
    pages_per_step = 16
    while n_pages % pages_per_step:
        pages_per_step //= 2
    assert pages_per_step * page >= MOBA_BLOCK
    nsteps = n_pages // pages_per_step
    rows = ATTN_HEADS * dec_seq
    q4 = q.reshape(bsz, dec_seq, ATTN_HEADS, HEAD_DIM).transpose(0, 2, 1, 3)
    own = (np.arange(ATTN_HEADS)[:, None] // GRP == np.arange(ATTN_KV_HEADS)[None, :]).astype(np.float32)
    qb = (q4[:, :, :, None, :] * jnp.asarray(own)[None, :, None, :, None]).reshape(bsz, rows, ATTN_KV)
    qbt = jnp.pad(qb.transpose(0, 2, 1), ((0, 0), (0, 0), (0, LANES - rows)))

    def page_spec(j):
        return pl.BlockSpec((1, ATTN_KV, page), lambda b, n, pt: (pt[b, pages_per_step * n + j], 0, 0))

    def per_b(d1, d2):
        return pl.BlockSpec((1, d1, d2), lambda b, n, pt: (b, 0, 0))

    page_specs = [page_spec(j) for j in range(pages_per_step)]
    scores_t = pl.pallas_call(
        functools.partial(_moba_score_kernel, pages_per_step=pages_per_step),
        grid_spec=pltpu.PrefetchScalarGridSpec(
            num_scalar_prefetch=1, grid=(bsz, nsteps),
            in_specs=[per_b(ATTN_KV, LANES)] + page_specs,
            out_specs=per_b(nblk, LANES)),
        out_shape=jax.ShapeDtypeStruct((bsz, nblk, LANES), F32),
        compiler_params=_cparams(("parallel", "arbitrary"), 16 << 20),
        name="moba_sample_scores",
    )(page_table, qbt, *([cache_kt] * pages_per_step))

    out = pl.pallas_call(
        functools.partial(_moba_sample_kernel, pages_per_step=pages_per_step, nblk=nblk, dec_seq=dec_seq),
        grid_spec=pltpu.PrefetchScalarGridSpec(
            num_scalar_prefetch=1, grid=(bsz, nsteps),
            in_specs=[per_b(nblk, LANES), per_b(rows, ATTN_KV)] + page_specs + page_specs
                     + [per_b(dec_seq, ATTN_KV), per_b(dec_seq, ATTN_KV)],
            out_specs=per_b(rows, ATTN_KV),
            scratch_shapes=[pltpu.VMEM((rows, LANES), F32), pltpu.VMEM((rows, 1), F32), pltpu.VMEM((rows, 1), F32),
                            pltpu.VMEM((rows, ATTN_KV), F32)]),
        out_shape=jax.ShapeDtypeStruct((bsz, rows, ATTN_KV), F32),
        compiler_params=_cparams(("parallel", "arbitrary"), 24 << 20),
        name="moba_sample_attn",
    )(page_table, scores_t, qb, *([cache_kt] * pages_per_step), *([cache_vt] * pages_per_step), k_new, v_new)
    o5 = out.reshape(bsz, ATTN_HEADS, dec_seq, ATTN_KV_HEADS, HEAD_DIM)
    o4 = jnp.sum(o5 * jnp.asarray(own)[None, :, None, :, None], axis=3)
    return o4.transpose(0, 2, 1, 3).reshape(bsz, dec_seq, ATTN_Q)


def _mix_kernel(x_ref, ys_ref, ya_ref, gs_ref, ga_ref, gt1_ref, sc2_ref, sh2_ref, nw_ref,
                wso_ref, wao_ref, wo_ref, wrt_ref, brt_ref, x1_ref, h2_ref, comb_ref):
    merged = (jax.nn.sigmoid(gs_ref[0]) * _dot(ys_ref[0].astype(BF16), wso_ref[...])
              + jax.nn.sigmoid(ga_ref[0]) * _dot(ya_ref[0].astype(BF16), wao_ref[...]))
    x1 = x_ref[0] + gt1_ref[0] * _dot(merged.astype(BF16), wo_ref[...])
    x1_ref[0] = x1
    h2 = x1 * lax.rsqrt(jnp.mean(x1 * x1, axis=-1, keepdims=True) + EPS) * nw_ref[...]
    h2 = h2 * (1.0 + sc2_ref[0]) + sh2_ref[0]
    h2_ref[0] = h2.astype(BF16)

    h_hi = h2.astype(BF16)
    h_lo = (h2 - h_hi.astype(F32)).astype(BF16)
    hw = _dot(h_hi, wrt_ref[...])
    lg = hw[:, :LANES] + hw[:, LANES:] + _dot(h_lo, wrt_ref[:, :LANES]) + brt_ref[...]
    lane = lax.broadcasted_iota(jnp.int32, lg.shape, 1).astype(F32)
    big = float(1 << 20)
    ninf = -jnp.inf
    gl = jnp.where(lane < N_EGROUPS, lg, ninf)
    gmax = jnp.max(gl, axis=1, keepdims=True)
    gidx = jnp.min(jnp.where(gl == gmax, lane, big), axis=1, keepdims=True)
    g_w = 1.0 / jnp.sum(jnp.exp(gl - gmax), axis=1, keepdims=True)
    lo = ROUTE_OFF + gidx * EXPERTS_PER_GROUP
    el = jnp.where((lane >= lo) & (lane < lo + EXPERTS_PER_GROUP), lg, ninf)
    m1 = jnp.max(el, axis=1, keepdims=True)
    i1 = jnp.min(jnp.where(el == m1, lane, big), axis=1, keepdims=True)
    el2 = jnp.where(lane == i1, ninf, el)
    m2 = jnp.max(el2, axis=1, keepdims=True)
    i2 = jnp.min(jnp.where(el2 == m2, lane, big), axis=1, keepdims=True)
    e2 = jnp.exp(m2 - m1)
    den = 1.0 / (1.0 + e2)
    comb_ref[0] = jnp.where(lane == i1, den * g_w, 0.0) + jnp.where(lane == i2, e2 * den * g_w, 0.0)


def _mix(x, y_ssd, y_att, gs, ga, gt1, sc2, sh2, norm_w, wso, wao, wo, wrt, brt):
    bsz, seqlen, _ = x.shape
    tm = 256
    assert seqlen % tm == 0
    nt = seqlen // tm
    mod_rows = gt1.shape[1]
    if mod_rows == 1:
        mod_spec = pl.BlockSpec((1, 1, D_MODEL), lambda b, i: (b, 0, 0))
    else:
        mod_spec = pl.BlockSpec((1, tm, D_MODEL), lambda b, i: (b, i, 0))

    def tok(width):
        return pl.BlockSpec((1, tm, width), lambda b, i: (b, i, 0))

    def full(d1, d2):
        return pl.BlockSpec((d1, d2), lambda b, i: (0, 0))

    vmem = 2 * 2 * (2 * D_MODEL * D_MODEL + ATTN_Q * D_MODEL) + 2 * tm * (8 * D_MODEL) * 4 + (12 << 20)
    return pl.pallas_call(
        _mix_kernel,
        grid=(bsz, nt),
        in_specs=[tok(D_MODEL), tok(SSD_INNER), tok(ATTN_Q), tok(D_MODEL), tok(D_MODEL),
                  mod_spec, mod_spec, mod_spec, full(1, D_MODEL),
                  full(SSD_INNER, D_MODEL), full(ATTN_Q, D_MODEL), full(D_MODEL, D_MODEL),
                  full(D_MODEL, 2 * LANES), full(1, LANES)],
        out_specs=[tok(D_MODEL), tok(D_MODEL), tok(LANES)],
        out_shape=[jax.ShapeDtypeStruct((bsz, seqlen, D_MODEL), F32),
                   jax.ShapeDtypeStruct((bsz, seqlen, D_MODEL), BF16),
                   jax.ShapeDtypeStruct((bsz, seqlen, LANES), F32)],
        compiler_params=_cparams(("parallel", "parallel"), vmem),
        name="mix",
    )(x, y_ssd, y_att, gs, ga, gt1, sc2, sh2, norm_w.reshape(1, -1), wso, wao, wo, wrt, brt)


def _moe_kernel(h2_ref, comb_ref, x1_ref, gt2_ref, fw_ref, wg_ref, wu_ref, wd_ref, y_ref, acc):
    e = pl.program_id(2)

    @pl.when(e == 0)
    def _():
        acc[...] = jnp.zeros(acc.shape, F32)

    h2 = h2_ref[0]
    lane = lax.broadcasted_iota(jnp.int32, comb_ref.shape[1:], 1)
    cw = jnp.sum(jnp.where(lane == e + ROUTE_OFF, comb_ref[0], 0.0), axis=1, keepdims=True)
    a = _dot(h2, wg_ref[0])
    u = _dot(h2, wu_ref[0])
    act = (_silu(a) * u * cw).astype(BF16)
    acc[...] += _dot(act, wd_ref[0])

    @pl.when(e == pl.num_programs(2) - 1)
    def _():
        x2 = x1_ref[0] + gt2_ref[0] * acc[...]
        y_ref[0] = x2 * lax.rsqrt(jnp.mean(x2 * x2, axis=-1, keepdims=True) + EPS) * fw_ref[...]


def _moe(h2, comb, x1, gt2, final_w, wg, wu, wd):
    bsz, seqlen, _ = h2.shape
    tm = 1024 if seqlen % 1024 == 0 else 256
    assert seqlen % tm == 0
    nt = seqlen // tm
    mod_rows = gt2.shape[1]
    if mod_rows == 1:
        mod_spec = pl.BlockSpec((1, 1, D_MODEL), lambda b, i, e: (b, 0, 0))
    else:
        mod_spec = pl.BlockSpec((1, tm, D_MODEL), lambda b, i, e: (b, i, 0))

    def tok(width):
        return pl.BlockSpec((1, tm, width), lambda b, i, e: (b, i, 0))

    vmem = 2 * tm * D_MODEL * (2 + 4 + 4) + tm * D_MODEL * 4 + 2 * 3 * D_MODEL * D_EXPERT * 2 + (16 << 20)
    return pl.pallas_call(
        _moe_kernel,
        grid=(bsz, nt, N_EXPERTS),
        in_specs=[tok(D_MODEL), tok(LANES), tok(D_MODEL), mod_spec,
                  pl.BlockSpec((1, D_MODEL), lambda b, i, e: (0, 0)),
                  pl.BlockSpec((1, D_MODEL, D_EXPERT), lambda b, i, e: (e, 0, 0)),
                  pl.BlockSpec((1, D_MODEL, D_EXPERT), lambda b, i, e: (e, 0, 0)),
                  pl.BlockSpec((1, D_EXPERT, D_MODEL), lambda b, i, e: (e, 0, 0))],
        out_specs=tok(D_MODEL),
        out_shape=jax.ShapeDtypeStruct((bsz, seqlen, D_MODEL), F32),
        scratch_shapes=[pltpu.VMEM((tm, D_MODEL), F32)],
        compiler_params=_cparams(("parallel", "parallel", "arbitrary"), vmem),
        name="moe",
    )(h2, comb, x1, gt2, final_w.reshape(1, -1), wg, wu, wd)


def _prep_weights(lp):
    w_in = lp["w_in"]
    sizes = (SSD_INNER, CONV_DIM, SSD_HEADS, ATTN_Q, ATTN_KV, ATTN_KV, D_MODEL, D_MODEL)
    pts = [int(p) for p in np.cumsum(sizes)[:-1]]
    z, xbc, dt, q, k, v, gs, ga = jnp.split(w_in, pts, axis=1)
    dt = jnp.pad(dt, ((0, 0), (0, LANES - SSD_HEADS)))
    w_in_r = jnp.concatenate([z, xbc, q, k, v, gs, ga, dt], axis=1).astype(BF16)
    wrt = jnp.concatenate([lp["w_group"], lp["w_router"]], axis=1)
    wrt = jnp.pad(wrt, ((0, 0), (0, LANES - wrt.shape[1]))).astype(F32)
    wrt_hi = wrt.astype(BF16)
    wrt = jnp.concatenate([wrt_hi, (wrt - wrt_hi.astype(F32)).astype(BF16)], axis=1)
    brt = jnp.concatenate([lp["b_group"], lp["b_router"]])
    brt = jnp.pad(brt, (0, LANES - brt.shape[0])).astype(F32).reshape(1, LANES)
    return dict(w_in_r=w_in_r, wrt=wrt, brt=brt,
                wso=lp["w_ssd_out"].astype(BF16), wao=lp["w_attn_out"].astype(BF16), wo=lp["w_o"].astype(BF16),
                wg=lp["w_gate_e"].astype(BF16), wu=lp["w_up_e"].astype(BF16), wd=lp["w_down_e"].astype(BF16))


def _mods(mod, per_row_repeat):
    parts = jnp.split(mod, 6, axis=-1)
    if per_row_repeat:
        return [jnp.repeat(p, per_row_repeat, axis=0)[None] for p in parts]
    return [p[:, None, :] for p in parts]


def _layer_prompt(x, mod, lp, pw, final_w):
    bsz, seqlen, _ = x.shape
    sh1, sc1, gt1, sh2, sc2, gt2 = _mods(mod, 0)
    tables = _rope_tables(np.arange(seqlen))
    (z, xbc, dt, q, k, v, gs, ga, kmean, kaug, vaug) = _in_proj(x, sc1, sh1, lp["norm1_w"], pw["w_in_r"], tables, True)
    conv0 = jnp.zeros((bsz, SSD_CONV - 1, CONV_DIM), F32)
    ssm0 = jnp.zeros((bsz, SSD_HEADS, SSD_HEADDIM, SSD_STATE), F32)
    y_ssd, conv_new, ssm_new = _ssd(xbc, z, dt, conv0, ssm0, lp["conv_w"], lp["conv_b"], lp["dt_bias"],
                                    lp["a_log"], lp["d_skip"], lp["ssd_norm_w"])
    y_att = _moba_prompt(q, kmean, kaug, vaug)
    x1, h2, comb = _mix(x, y_ssd, y_att, gs, ga, gt1, sc2, sh2, lp["norm2_w"],
                        pw["wso"], pw["wao"], pw["wo"], pw["wrt"], pw["brt"])
    y = _moe(h2, comb, x1, gt2, final_w, pw["wg"], pw["wu"], pw["wd"])
    kv_shape = (bsz, seqlen, ATTN_KV_HEADS, HEAD_DIM)
    return y, k.reshape(kv_shape), v.reshape(kv_shape), conv_new, ssm_new


def _layer_sample(x, mod, conv_buf, ssm_state, cache_k, cache_v, page_table, past_len, lp, pw, final_w):
    bsz, dec_seq, _ = x.shape
    ntok = bsz * dec_seq
    sh1, sc1, gt1, sh2, sc2, gt2 = _mods(mod, dec_seq)
    pos = past_len + (np.arange(ntok) % dec_seq)
    tables = _rope_tables(pos)
    xf = x.reshape(1, ntok, D_MODEL)
    z, xbc, dt, q, k, v, gs, ga = _in_proj(xf, sc1, sh1, lp["norm1_w"], pw["w_in_r"], tables, False)

    def per_b(t):
        return t.reshape(bsz, dec_seq, t.shape[-1])

    y_ssd, conv_new, ssm_new = _ssd(per_b(xbc), per_b(z), per_b(dt), conv_buf, ssm_state, lp["conv_w"],
                                    lp["conv_b"], lp["dt_bias"], lp["a_log"], lp["d_skip"], lp["ssd_norm_w"])
    n_pool, page = cache_k.shape[0], cache_k.shape[1]

    def position_minor(cache):
        return cache.transpose(0, 2, 3, 1).reshape(n_pool, ATTN_KV, page)

    y_att = _moba_sample(per_b(q), per_b(k), per_b(v), position_minor(cache_k), position_minor(cache_v),
                         page_table, past_len)
    x1, h2, comb = _mix(xf, y_ssd.reshape(1, ntok, -1), y_att.reshape(1, ntok, -1), gs, ga, gt1, sc2, sh2,
                        lp["norm2_w"], pw["wso"], pw["wao"], pw["wo"], pw["wrt"], pw["brt"])
    y = _moe(h2, comb, x1, gt2, final_w, pw["wg"], pw["wu"], pw["wd"])
    kv_shape = (bsz, dec_seq, ATTN_KV_HEADS, HEAD_DIM)
    return y.reshape(bsz, dec_seq, D_MODEL), k.reshape(kv_shape), v.reshape(kv_shape), conv_new, ssm_new


def kernel(x_prompt, x_sample, cache_k, cache_v, state_conv, state_ssm, page_table, c_prompt, c_sample,
           w_ada, b_ada, norm1_w, w_in, conv_w, conv_b, dt_bias, a_log, d_skip, ssd_norm_w,
           w_ssd_out, w_attn_out, w_o, norm2_w, w_group, b_group, w_router, b_router,
           w_gate_e, w_up_e, w_down_e, final_w):
    depth = w_in.shape[0]
    assert depth == 1, "the final RMSNorm is fused into the last layer's MoE kernel"
    nb_p, nb_s = c_prompt.shape[0], c_sample.shape[0]
    past_len = page_table.shape[1] * cache_k.shape[2]
    l = 0
    lp = dict(norm1_w=norm1_w[l], w_in=w_in[l], conv_w=conv_w[l], conv_b=conv_b[l], dt_bias=dt_bias[l],
              a_log=a_log[l], d_skip=d_skip[l], ssd_norm_w=ssd_norm_w[l], w_ssd_out=w_ssd_out[l],
              w_attn_out=w_attn_out[l], w_o=w_o[l], norm2_w=norm2_w[l], w_group=w_group[l], b_group=b_group[l],
              w_router=w_router[l], b_router=b_router[l], w_gate_e=w_gate_e[l], w_up_e=w_up_e[l],
              w_down_e=w_down_e[l])
    pw = _prep_weights(lp)
    c_all = jnp.concatenate([c_prompt, c_sample], axis=0)
    pad_rows = (-c_all.shape[0]) % SUBLANES
    mod = _ada_mod(jnp.pad(c_all, ((0, pad_rows), (0, 0))), w_ada[l], b_ada[l])
    yp, kp, vp, cp, sp = _layer_prompt(x_prompt, mod[:nb_p], lp, pw, final_w)
    ys, ks, vs, cs, ss = _layer_sample(x_sample, mod[nb_p:nb_p + nb_s], state_conv[l], state_ssm[l],
                                       cache_k[l], cache_v[l], page_table, past_len, lp, pw, final_w)
    return (yp, ys, kp[None], vp[None], cp[None], sp[None], ks[None], vs[None], cs[None], ss[None])
```

```python
import functools
import math

import numpy as np
import jax
import jax.numpy as jnp
from jax import lax
from jax.experimental import pallas as pl
from jax.experimental.pallas import tpu as pltpu

F32 = jnp.float32
BF16 = jnp.bfloat16
HIGHEST = lax.Precision.HIGHEST

D_MODEL = 1024
SSD_HEADDIM = 64
SSD_INNER = D_MODEL
SSD_HEADS = SSD_INNER // SSD_HEADDIM
SSD_GROUPS = 2
SSD_STATE = 128
SSD_CONV = 4
SSD_CHUNK = 128
SSD_BC = SSD_GROUPS * SSD_STATE
CONV_DIM = SSD_INNER + 2 * SSD_BC
ATTN_HEADS = 8
ATTN_KV_HEADS = 4
HEAD_DIM = 64
ATTN_Q = ATTN_HEADS * HEAD_DIM
ATTN_KV = ATTN_KV_HEADS * HEAD_DIM
ROT_DIM = HEAD_DIM // 4
ROPE_THETA = 500000.0
MOBA_BLOCK = 256
MOBA_TOPK = 3
N_EGROUPS = 4
EXPERTS_PER_GROUP = 4
N_EXPERTS = N_EGROUPS * EXPERTS_PER_GROUP
D_EXPERT = D_MODEL // 2
EPS = 1e-6

LANES = 128
SUBLANES = 8
VMEM_CAP = 56 << 20
NEG = -1e30
GRP = ATTN_HEADS // ATTN_KV_HEADS
ROUTE_OFF = N_EGROUPS

_SEG = {}
_off = 0
for _name, _w in (("z", SSD_INNER), ("xbc", CONV_DIM), ("q", ATTN_Q), ("k", ATTN_KV), ("v", ATTN_KV),
                  ("gs", D_MODEL), ("ga", D_MODEL), ("dt", LANES)):
    _SEG[_name] = (_off, _off + _w)
    _off += _w
IN_PAD = _off


def _cparams(semantics, vmem_bytes):
    return pltpu.CompilerParams(dimension_semantics=semantics,
                                vmem_limit_bytes=int(min(max(vmem_bytes, 16 << 20), VMEM_CAP)))


def _dot(a, b):
    return jnp.dot(a, b, preferred_element_type=F32)


def _dot_nt(a, b):
    return lax.dot_general(a, b, (((1,), (1,)), ((), ())), preferred_element_type=F32)


def _dot_exact(a, b):
    return jnp.dot(a, b, preferred_element_type=F32, precision=HIGHEST)


def _silu(x):
    return x * jax.nn.sigmoid(x)


def _mod_kernel(c_ref, w_ref, b_ref, o_ref):
    o_ref[...] = _dot_exact(_silu(c_ref[...]), w_ref[...]) + b_ref[...]


def _ada_mod(c_all, w_ada, b_ada):
    rows = c_all.shape[0]
    return pl.pallas_call(
        _mod_kernel,
        grid=(6,),
        in_specs=[pl.BlockSpec((rows, D_MODEL), lambda j: (0, 0)),
                  pl.BlockSpec((D_MODEL, D_MODEL), lambda j: (0, j)),
                  pl.BlockSpec((1, D_MODEL), lambda j: (0, j))],
        out_specs=pl.BlockSpec((rows, D_MODEL), lambda j: (0, j)),
        out_shape=jax.ShapeDtypeStruct((rows, 6 * D_MODEL), F32),
        compiler_params=_cparams(("arbitrary",), 24 << 20),
        name="ada_mod",
    )(c_all, w_ada, b_ada.reshape(1, -1))


def _rope_tables(pos):
    half = ROT_DIM // 2
    c = -2.0 * math.log(ROPE_THETA) / ROT_DIM
    c_hi = float(np.float32(c))
    j = jnp.arange(half, dtype=F32)
    inv_freq = jnp.exp(j * c_hi + j * (c - c_hi))
    ang = jnp.asarray(pos).astype(F32)[:, None] * inv_freq[None, :]
    cos, sin = jnp.cos(ang), jnp.sin(ang)
    n = pos.shape[0]
    one = jnp.ones((n, HEAD_DIM - ROT_DIM), F32)
    zero = jnp.zeros((n, half), F32)
    rest = jnp.zeros((n, HEAD_DIM - ROT_DIM), F32)
    a = jnp.concatenate([cos, cos, one], axis=1)
    s1 = jnp.concatenate([-sin, zero, rest], axis=1)
    s2 = jnp.concatenate([zero, sin, rest], axis=1)
    rep = LANES // HEAD_DIM
    return tuple(jnp.tile(t, (1, rep)) for t in (a, s1, s2))


def _inproj_kernel(*refs, attn_layout):
    (x_ref, sc_ref, sh_ref, nw_ref, w_ref, ra_ref, rs1_ref, rs2_ref,
     z_ref, xbc_ref, dt_ref, q_ref, k_ref, v_ref, gs_ref, ga_ref) = refs[:16]
    x = x_ref[0]
    h = x * lax.rsqrt(jnp.mean(x * x, axis=-1, keepdims=True) + EPS) * nw_ref[...]
    h = (h * (1.0 + sc_ref[0]) + sh_ref[0]).astype(BF16)

    def proj(name, out_ref):
        lo, hi = _SEG[name]
        step = 512
        for c in range(lo, hi, step):
            w = min(step, hi - c)
            out_ref[0, :, c - lo:c - lo + w] = _dot(h, w_ref[:, c:c + w])

    proj("z", z_ref)
    proj("xbc", xbc_ref)
    proj("gs", gs_ref)
    proj("ga", ga_ref)
    proj("dt", dt_ref)

    def rope(t):
        width = t.shape[1]
        reps = width // LANES
        a = jnp.concatenate([ra_ref[...]] * reps, axis=1)
        s1 = jnp.concatenate([rs1_ref[...]] * reps, axis=1)
        s2 = jnp.concatenate([rs2_ref[...]] * reps, axis=1)
        return t * a + pltpu.roll(t, width - ROT_DIM // 2, 1) * s1 + pltpu.roll(t, ROT_DIM // 2, 1) * s2

    lo, hi = _SEG["q"]
    q_ref[0] = rope(_dot(h, w_ref[:, lo:hi]))
    lo, hi = _SEG["k"]
    k = rope(_dot(h, w_ref[:, lo:hi]))
    k_ref[0] = k
    lo, hi = _SEG["v"]
    v = _dot(h, w_ref[:, lo:hi])
    v_ref[0] = v

    if attn_layout:
        kmean_ref, kaug_ref, vaug_ref = refs[16:19]
        tm = k.shape[0]
        kmean_ref[0, 0] = jnp.mean(k, axis=0, keepdims=True)
        lane = lax.broadcasted_iota(jnp.int32, (tm, HEAD_DIM), 1)
        blk_onehot = (lane == pl.program_id(1)).astype(F32)
        ones_col = (lane == 0).astype(F32)
        for kh in range(ATTN_KV_HEADS):
            sl = slice(kh * HEAD_DIM, (kh + 1) * HEAD_DIM)
            kaug_ref[0, kh] = jnp.concatenate([k[:, sl], blk_onehot], axis=1).astype(BF16)
            vaug_ref[0, kh] = jnp.concatenate([v[:, sl], ones_col], axis=1).astype(BF16)


def _in_proj(x, sc1, sh1, norm_w, w_in_r, tables, attn_layout):
    bsz, seqlen, _ = x.shape
    tm = MOBA_BLOCK
    assert seqlen % tm == 0
    nt = seqlen // tm
    mod_rows = sc1.shape[1]
    if mod_rows == 1:
        mod_spec = pl.BlockSpec((1, 1, D_MODEL), lambda b, i: (b, 0, 0))
    else:
        mod_spec = pl.BlockSpec((1, tm, D_MODEL), lambda b, i: (b, i, 0))
    tab_spec = pl.BlockSpec((tm, LANES), lambda b, i: (i, 0))

    def tok(width):
        return pl.BlockSpec((1, tm, width), lambda b, i: (b, i, 0))

    def shp(width):
        return jax.ShapeDtypeStruct((bsz, seqlen, width), F32)

    widths = (SSD_INNER, CONV_DIM, LANES, ATTN_Q, ATTN_KV, ATTN_KV, D_MODEL, D_MODEL)
    out_specs = [tok(w) for w in widths]
    out_shape = [shp(w) for w in widths]
    if attn_layout:
        assert nt <= HEAD_DIM
        out_specs += [pl.BlockSpec((1, 1, 1, ATTN_KV), lambda b, i: (b, i, 0, 0)),
                      pl.BlockSpec((1, ATTN_KV_HEADS, tm, LANES), lambda b, i: (b, 0, i, 0)),
                      pl.BlockSpec((1, ATTN_KV_HEADS, tm, LANES), lambda b, i: (b, 0, i, 0))]
        out_shape += [jax.ShapeDtypeStruct((bsz, nt, 1, ATTN_KV), F32),
                      jax.ShapeDtypeStruct((bsz, ATTN_KV_HEADS, seqlen, LANES), BF16),
                      jax.ShapeDtypeStruct((bsz, ATTN_KV_HEADS, seqlen, LANES), BF16)]
    vmem = 2 * (D_MODEL * IN_PAD * 2) + 2 * tm * (D_MODEL + IN_PAD + 4 * LANES) * 4 + (8 << 20)
    return pl.pallas_call(
        functools.partial(_inproj_kernel, attn_layout=attn_layout),
        grid=(bsz, nt),
        in_specs=[tok(D_MODEL), mod_spec, mod_spec,
                  pl.BlockSpec((1, D_MODEL), lambda b, i: (0, 0)),
                  pl.BlockSpec((D_MODEL, IN_PAD), lambda b, i: (0, 0)),
                  tab_spec, tab_spec, tab_spec],
        out_specs=out_specs,
        out_shape=out_shape,
        compiler_params=_cparams(("parallel", "parallel"), vmem),
        name="in_proj",
    )(x, sc1, sh1, norm_w.reshape(1, -1), w_in_r, *tables)


def _cumsum_rows(a):
    row = lax.broadcasted_iota(jnp.int32, a.shape, 0)
    s = 1
    while s < a.shape[0]:
        a = a + jnp.where(row >= s, pltpu.roll(a, s, 0), 0.0)
        s *= 2
    return a


def _ssd_kernel(xbc_ref, z_ref, dt_ref, cbuf_ref, s0_ref, cw_ref, cb_ref, dtb_ref, alog_ref, dsk_ref, nw_ref,
                y_ref, cnew_ref, sout_ref, xext, st, ysc, *, rows):
    ch = SSD_CHUNK
    pad = SUBLANES
    c = pl.program_id(1)
    last = pl.num_programs(1) - 1

    @pl.when(c == 0)
    def _():
        xext[0:pad, :] = jnp.zeros((pad, CONV_DIM), F32)
        xext[pad - (SSD_CONV - 1):pad, :] = cbuf_ref[0]
        st[...] = s0_ref[0].T

    xext[pad:pad + rows, :] = xbc_ref[0]
    if rows < ch:
        xext[pad + rows:pad + ch, :] = jnp.zeros((ch - rows, CONV_DIM), F32)

    conv = cb_ref[...] + xext[pad:pad + ch, :] * cw_ref[SSD_CONV - 1:SSD_CONV, :]
    for w in range(SSD_CONV - 1):
        off = pad - (SSD_CONV - 1) + w
        conv = conv + xext[off:off + ch, :] * cw_ref[w:w + 1, :]
    act = _silu(conv)

    @pl.when(c == last)
    def _():
        cnew_ref[0] = xext[pad + rows - (SSD_CONV - 1):pad + rows, :]

    xext[0:pad, :] = xext[ch:ch + pad, :]

    xs = act[:, :SSD_INNER]
    dt_raw = dt_ref[0]
    if rows < ch:
        dt_raw = jnp.concatenate([dt_raw, jnp.zeros((ch - rows, LANES), F32)], axis=0)
    dt_in = dt_raw + dtb_ref[...]
    dt = jnp.maximum(dt_in, 0.0) + jnp.log1p(jnp.exp(-jnp.abs(dt_in)))
    row = lax.broadcasted_iota(jnp.int32, (ch, LANES), 0)
    col = lax.broadcasted_iota(jnp.int32, (ch, LANES), 1)
    if rows < ch:
        dt = jnp.where(row < rows, dt, 0.0)
    a_cs = _cumsum_rows(dt * (-jnp.exp(alog_ref[...])))
    a_cs_t = a_cs.T
    dt_t = dt.T
    causal = row >= col

    heads_per_pair = LANES // SSD_HEADDIM
    assert heads_per_pair == 2 and SSD_STATE == ch
    first_of_pair = col < SSD_HEADDIM
    for g in range(SSD_GROUPS):
        bg = act[:, SSD_INNER + g * SSD_STATE:SSD_INNER + (g + 1) * SSD_STATE]
        cg = act[:, SSD_INNER + SSD_BC + g * SSD_STATE:SSD_INNER + SSD_BC + (g + 1) * SSD_STATE]
        gmat = _dot_nt(cg.astype(BF16), bg.astype(BF16))
        bg_t = bg.T
        for pair in range(SSD_HEADS // SSD_GROUPS // heads_per_pair):
            hd0 = g * (SSD_HEADS // SSD_GROUPS) + pair * heads_per_pair
            sl = slice(hd0 * SSD_HEADDIM, hd0 * SSD_HEADDIM + LANES)
            x_pair = xs[:, sl]
            s_pair = st[:, sl]
            x_diag = [jnp.where(first_of_pair, x_pair, 0.0).astype(BF16),
                      jnp.where(first_of_pair, 0.0, x_pair).astype(BF16)]
            s_diag = [jnp.where(first_of_pair, s_pair, 0.0).astype(BF16),
                      jnp.where(first_of_pair, 0.0, s_pair).astype(BF16)]
            m_parts, c_parts, b_parts, decay = [], [], [], []
            for hd in range(hd0, hd0 + heads_per_pair):
                acs_col = a_cs[:, hd:hd + 1]
                acs_row = a_cs_t[hd:hd + 1, :]
                dt_row = dt_t[hd:hd + 1, :]
                acs_last = a_cs_t[hd:hd + 1, ch - 1:ch]
                lmat = jnp.where(causal, jnp.exp(acs_col - acs_row), 0.0)
                m_parts.append((gmat * lmat * dt_row).astype(BF16))
                c_parts.append((cg * jnp.exp(acs_col)).astype(BF16))
                b_parts.append((bg_t * (jnp.exp(acs_last - acs_row) * dt_row)).astype(BF16))
                decay.append(jnp.exp(acs_last))
            ysc[:, sl] = _dot(jnp.concatenate(m_parts + c_parts, axis=1), jnp.concatenate(x_diag + s_diag, axis=0))
            st[:, sl] = (jnp.where(first_of_pair[:1], decay[0], decay[1]) * s_pair
                         + _dot(jnp.concatenate(b_parts, axis=1), jnp.concatenate(x_diag, axis=0)))

    y = (ysc[0:rows, :] + xs[:rows] * dsk_ref[...]) * _silu(z_ref[0])
    gw = SSD_INNER // SSD_GROUPS
    for g in range(SSD_GROUPS):
        yg = y[:, g * gw:(g + 1) * gw]
        yg = yg * lax.rsqrt(jnp.mean(yg * yg, axis=-1, keepdims=True) + EPS)
        y_ref[0, :, g * gw:(g + 1) * gw] = yg * nw_ref[:, g * gw:(g + 1) * gw]

    @pl.when(c == last)
    def _():
        sout_ref[0] = st[...].T


def _ssd(xbc, z, dt, conv_buf, ssm_state, conv_w, conv_b, dt_bias, a_log, d_skip, norm_w):
    bsz, seqlen, _ = xbc.shape
    rows = SSD_CHUNK if seqlen % SSD_CHUNK == 0 else seqlen
    assert rows == seqlen or rows == SSD_CHUNK
    assert rows % SUBLANES == 0 and rows >= SSD_CONV - 1
    nc = seqlen // rows
    hp = SSD_HEADS * SSD_HEADDIM

    def tok(width):
        return pl.BlockSpec((1, rows, width), lambda b, c: (b, c, 0))

    def per_b(d1, d2):
        return pl.BlockSpec((1, d1, d2), lambda b, c: (b, 0, 0))

    def vec(width, r=1):
        return pl.BlockSpec((r, width), lambda b, c: (0, 0))

    def pad_lanes(v):
        return jnp.pad(v.astype(F32), (0, LANES - v.shape[0])).reshape(1, LANES)

    y, conv_new, s_new = pl.pallas_call(
        functools.partial(_ssd_kernel, rows=rows),
        grid=(bsz, nc),
        in_specs=[tok(CONV_DIM), tok(SSD_INNER), tok(LANES),
                  per_b(SSD_CONV - 1, CONV_DIM), per_b(hp, SSD_STATE),
                  vec(CONV_DIM, SSD_CONV), vec(CONV_DIM), vec(LANES), vec(LANES), vec(SSD_INNER), vec(SSD_INNER)],
        out_specs=[tok(SSD_INNER), per_b(SSD_CONV - 1, CONV_DIM), per_b(hp, SSD_STATE)],
        out_shape=[jax.ShapeDtypeStruct((bsz, seqlen, SSD_INNER), F32),
                   jax.ShapeDtypeStruct((bsz, SSD_CONV - 1, CONV_DIM), F32),
                   jax.ShapeDtypeStruct((bsz, hp, SSD_STATE), F32)],
        scratch_shapes=[pltpu.VMEM((SSD_CHUNK + SUBLANES, CONV_DIM), F32),
                        pltpu.VMEM((SSD_STATE, hp), F32),
                        pltpu.VMEM((SSD_CHUNK, SSD_INNER), F32)],
        compiler_params=_cparams(("parallel", "arbitrary"), 40 << 20),
        name="ssd",
    )(xbc, z, dt, conv_buf, ssm_state.reshape(bsz, hp, SSD_STATE),
      conv_w, conv_b.reshape(1, -1), pad_lanes(dt_bias), pad_lanes(a_log),
      jnp.repeat(d_skip.astype(F32), SSD_HEADDIM).reshape(1, -1), norm_w.reshape(1, -1))
    return y, conv_new, s_new.reshape(bsz, SSD_HEADS, SSD_HEADDIM, SSD_STATE)


def _select_topk(scores, valid, axis):
    sc = scores if valid is None else jnp.where(valid, scores, -jnp.inf)
    sel = jnp.zeros(scores.shape, F32)
    pos_f = lax.broadcasted_iota(jnp.int32, scores.shape, axis).astype(F32)
    for _ in range(MOBA_TOPK):
        m = jnp.max(sc, axis=axis, keepdims=True)
        idx = jnp.min(jnp.where(sc == m, pos_f, float(1 << 20)), axis=axis, keepdims=True)
        idx = jnp.where(m > -jnp.inf, idx, -1.0)
        hit = pos_f == idx
        sel = jnp.where(hit, 1.0, sel)
        sc = jnp.where(hit, -jnp.inf, sc)
    return sel > 0.5


def _moba_prompt_kernel(q_ref, km_ref, kaug_ref, vaug_ref, o_ref, s_a, s_b, *, chunk):
    i = pl.program_id(2)
    tq = MOBA_BLOCK
    q2 = q_ref[0]
    qs = jnp.concatenate([q2[:, g * HEAD_DIM:(g + 1) * HEAD_DIM] for g in range(GRP)], axis=0)
    rows = qs.shape[0]
    lane = lax.broadcasted_iota(jnp.int32, (rows, LANES), 1)
    scores_t = lax.dot_general(km_ref[0, 0], qs, (((1,), (1,)), ((), ())), preferred_element_type=F32,
                               precision=HIGHEST)
    nbp = scores_t.shape[0]
    blk = lax.broadcasted_iota(jnp.int32, scores_t.shape, 0)
    sel_t = _select_topk(scores_t, blk < i, 0)
    pen_t = jnp.concatenate([jnp.zeros((HEAD_DIM, rows), F32), jnp.where(sel_t, 0.0, NEG),
                             jnp.full((LANES - HEAD_DIM - nbp, rows), NEG, F32)], axis=0)
    pen = pen_t.T
    q_pad = jnp.concatenate([qs * (HEAD_DIM ** -0.5), jnp.zeros((rows, LANES - HEAD_DIM), F32)], axis=1)
    q_aug = jnp.where(lane < HEAD_DIM, q_pad, pen).astype(BF16)
    q_own = q_pad.astype(BF16)

    def update(m, acc, s, vb):
        m_new = jnp.maximum(m, jnp.max(s, axis=1, keepdims=True))
        p = jnp.exp(s - m_new)
        acc = jnp.exp(m - m_new) * acc + _dot(p.astype(BF16), vb)
        return m_new, acc

    nchunks = (i * tq + chunk - 1) // chunk
    total_chunks = kaug_ref.shape[2] // chunk
    npairs = (nchunks + 1) // 2

    def logits(c):
        start = pl.multiple_of(jnp.minimum(c, total_chunks - 1) * chunk, chunk)
        return _dot_nt(q_aug, kaug_ref[0, 0, pl.ds(start, chunk), :])

    def values(c):
        return vaug_ref[0, 0, pl.ds(pl.multiple_of(c * chunk, chunk), chunk), :]

    def body(t, carry):
        m, acc = carry
        c0 = 2 * t
        s_b[...] = logits(c0 + 1)
        m, acc = update(m, acc, s_a[...], values(c0))
        s_a[...] = logits(c0 + 2)
        m, acc = update(m, acc, s_b[...], values(c0 + 1))
        return m, acc

    start = pl.multiple_of(i * tq, tq)
    s_own = _dot_nt(q_own, kaug_ref[0, 0, pl.ds(start, tq), :])
    r = lax.broadcasted_iota(jnp.int32, (rows, tq), 0)
    cidx = lax.broadcasted_iota(jnp.int32, (rows, tq), 1)
    qpos = r
    for g in range(1, GRP):
        qpos = jnp.where(r >= g * tq, r - g * tq, qpos)
    s_own = jnp.where(cidx <= qpos, s_own, NEG)

    s_a[...] = logits(0)
    init = (jnp.full((rows, 1), -jnp.inf, F32), jnp.zeros((rows, LANES), F32))
    m, acc = lax.fori_loop(0, npairs, body, init)
    _, acc = update(m, acc, s_own, vaug_ref[0, 0, pl.ds(start, tq), :])
    out = acc[:, :HEAD_DIM] / acc[:, HEAD_DIM:HEAD_DIM + 1]
    o_ref[0] = jnp.concatenate([out[g * tq:(g + 1) * tq] for g in range(GRP)], axis=1)


def _moba_prompt(q, kmean, kaug, vaug):
    bsz, seqlen, _ = q.shape
    nb = seqlen // MOBA_BLOCK
    assert GRP * HEAD_DIM == LANES and nb <= LANES - HEAD_DIM
    nbp = -(-nb // SUBLANES) * SUBLANES
    km = kmean.reshape(bsz, nb, ATTN_KV_HEADS, HEAD_DIM).transpose(0, 2, 1, 3)
    km = jnp.pad(km, ((0, 0), (0, 0), (0, nbp - nb), (0, 0)))
    kv_spec = pl.BlockSpec((1, 1, seqlen, LANES), lambda b, h, i: (b, h, 0, 0))
    vmem = 2 * 2 * seqlen * LANES * 2 + (24 << 20)
    chunk = 2 * MOBA_BLOCK
    assert seqlen % (2 * chunk) == 0, "the sweep walks the keys two chunks at a time"
    rows = GRP * MOBA_BLOCK
    return pl.pallas_call(
        functools.partial(_moba_prompt_kernel, chunk=chunk),
        scratch_shapes=[pltpu.VMEM((rows, chunk), F32), pltpu.VMEM((rows, chunk), F32)],
        grid=(bsz, ATTN_KV_HEADS, nb),
        in_specs=[pl.BlockSpec((1, MOBA_BLOCK, LANES), lambda b, h, i: (b, i, h)),
                  pl.BlockSpec((1, 1, nbp, HEAD_DIM), lambda b, h, i: (b, h, 0, 0)),
                  kv_spec, kv_spec],
        out_specs=pl.BlockSpec((1, MOBA_BLOCK, LANES), lambda b, h, i: (b, i, h)),
        out_shape=jax.ShapeDtypeStruct((bsz, seqlen, ATTN_Q), F32),
        compiler_params=_cparams(("parallel", "parallel", "arbitrary"), vmem),
        name="moba_prompt",
    )(q, km, kaug, vaug)


def _moba_score_kernel(pt_ref, qbt_ref, *refs, pages_per_step):
    del pt_ref
    k_refs, sc_ref = refs[:pages_per_step], refs[pages_per_step]
    n = pl.program_id(1)
    pages_per_block = MOBA_BLOCK // LANES
    blocks_per_step = pages_per_step // pages_per_block
    qbt = qbt_ref[0]
    for jb in range(blocks_per_step):
        ksum = k_refs[jb * pages_per_block][0]
        for j in range(1, pages_per_block):
            ksum = ksum + k_refs[jb * pages_per_block + j][0]
        kmean = jnp.sum(ksum, axis=1, keepdims=True) * (1.0 / MOBA_BLOCK)
        sc_ref[0, pl.ds(n * blocks_per_step + jb, 1), :] = jnp.sum(qbt * kmean, axis=0, keepdims=True)


def _moba_sample_kernel(pt_ref, sc_ref, qb_ref, *refs, pages_per_step, nblk, dec_seq):
    del pt_ref
    k_refs = refs[:pages_per_step]
    v_refs = refs[pages_per_step:2 * pages_per_step]
    kn_ref, vn_ref, o_ref, pen_sc, m_sc, l_sc, acc_sc = refs[2 * pages_per_step:]
    n = pl.program_id(1)
    rows = qb_ref.shape[1]
    pages_per_block = MOBA_BLOCK // LANES
    blocks_per_step = pages_per_step // pages_per_block
    lane = lax.broadcasted_iota(jnp.int32, (rows, LANES), 1)

    @pl.when(n == 0)
    def _():
        sc_t = sc_ref[0]
        sel_t = _select_topk(sc_t, None, 0)
        pen_t = jnp.concatenate([jnp.where(sel_t, 0.0, NEG), jnp.full((LANES - nblk, LANES), NEG, F32)], axis=0)
        pen_sc[...] = pen_t.T[:rows]
        m_sc[...] = jnp.full(m_sc.shape, -jnp.inf, F32)
        l_sc[...] = jnp.zeros(l_sc.shape, F32)
        acc_sc[...] = jnp.zeros(acc_sc.shape, F32)

    qb = (qb_ref[0] * (HEAD_DIM ** -0.5)).astype(BF16)

    def update(s, pv_fn):
        m = m_sc[...]
        m_new = jnp.maximum(m, jnp.max(s, axis=1, keepdims=True))
        p = jnp.exp(s - m_new)
        alpha = jnp.exp(m - m_new)
        l_sc[...] = alpha * l_sc[...] + jnp.sum(p, axis=1, keepdims=True)
        acc_sc[...] = alpha * acc_sc[...] + pv_fn(p.astype(BF16))
        m_sc[...] = m_new

    pen = pen_sc[...]
    parts = []
    for jb in range(blocks_per_step):
        pen_col = jnp.sum(jnp.where(lane == n * blocks_per_step + jb, pen, 0.0), axis=1, keepdims=True)
        for j in range(pages_per_block):
            parts.append(_dot(qb, k_refs[jb * pages_per_block + j][0].astype(BF16)) + pen_col)
    s = jnp.concatenate(parts, axis=1)

    def pv_past(p):
        pv = _dot_nt(p[:, :LANES], v_refs[0][0].astype(BF16))
        for j in range(1, pages_per_step):
            pv = pv + _dot_nt(p[:, j * LANES:(j + 1) * LANES], v_refs[j][0].astype(BF16))
        return pv

    update(s, pv_past)

    @pl.when(n == pl.num_programs(1) - 1)
    def _():
        zpad = jnp.zeros((LANES - dec_seq, kn_ref.shape[2]), F32)
        kn = jnp.concatenate([kn_ref[0], zpad], axis=0).astype(BF16)
        vn = jnp.concatenate([vn_ref[0], zpad], axis=0).astype(BF16)
        s_own = _dot_nt(qb, kn)
        r = lax.broadcasted_iota(jnp.int32, (rows, LANES), 0)
        qpos = r & (dec_seq - 1)
        s_own = jnp.where((lane <= qpos) & (lane < dec_seq), s_own, NEG)
        update(s_own, lambda p: _dot(p, vn))
        o_ref[0] = acc_sc[...] / l_sc[...]


def _moba_sample(q, k_new, v_new, cache_kt, cache_vt, page_table, past_len):
    bsz, dec_seq, _ = q.shape
    page = cache_kt.shape[2]
    assert page == LANES and MOBA_BLOCK % page == 0
    assert past_len % MOBA_BLOCK == 0 and dec_seq == SUBLANES
    nblk = past_len // MOBA_BLOCK
    assert MOBA_TOPK <= nblk <= LANES
    n_pages = past_len // page
    pages_per_step = 16
    while n_pages % pages_per_step:
        pages_per_step //= 2
    assert pages_per_step * page >= MOBA_BLOCK
    nsteps = n_pages // pages_per_step
    rows = ATTN_HEADS * dec_seq
    q4 = q.reshape(bsz, dec_seq, ATTN_HEADS, HEAD_DIM).transpose(0, 2, 1, 3)
    own = (np.arange(ATTN_HEADS)[:, None] // GRP == np.arange(ATTN_KV_HEADS)[None, :]).astype(np.float32)
    qb = (q4[:, :, :, None, :] * jnp.asarray(own)[None, :, None, :, None]).reshape(bsz, rows, ATTN_KV)
    qbt = jnp.pad(qb.transpose(0, 2, 1), ((0, 0), (0, 0), (0, LANES - rows)))

    def page_spec(j):
        return pl.BlockSpec((1, ATTN_KV, page), lambda b, n, pt: (pt[b, pages_per_step * n + j], 0, 0))

    def per_b(d1, d2):
        return pl.BlockSpec((1, d1, d2), lambda b, n, pt: (b, 0, 0))

    page_specs = [page_spec(j) for j in range(pages_per_step)]
    scores_t = pl.pallas_call(
        functools.partial(_moba_score_kernel, pages_per_step=pages_per_step),
        grid_spec=pltpu.PrefetchScalarGridSpec(
            num_scalar_prefetch=1, grid=(bsz, nsteps),
            in_specs=[per_b(ATTN_KV, LANES)] + page_specs,
            out_specs=per_b(nblk, LANES)),
        out_shape=jax.ShapeDtypeStruct((bsz, nblk, LANES), F32),
        compiler_params=_cparams(("parallel", "arbitrary"), 16 << 20),
        name="moba_sample_scores",
    )(page_table, qbt, *([cache_kt] * pages_per_step))

    out = pl.pallas_call(
        functools.partial(_moba_sample_kernel, pages_per_step=pages_per_step, nblk=nblk, dec_seq=dec_seq),
        grid_spec=pltpu.PrefetchScalarGridSpec(
            num_scalar_prefetch=1, grid=(bsz, nsteps),
            in_specs=[per_b(nblk, LANES), per_b(rows, ATTN_KV)] + page_specs + page_specs
                     + [per_b(dec_seq, ATTN_KV), per_b(dec_seq, ATTN_KV)],
            out_specs=per_b(rows, ATTN_KV),
            scratch_shapes=[pltpu.VMEM((rows, LANES), F32), pltpu.VMEM((rows, 1), F32), pltpu.VMEM((rows, 1), F32),
                            pltpu.VMEM((rows, ATTN_KV), F32)]),
        out_shape=jax.ShapeDtypeStruct((bsz, rows, ATTN_KV), F32),
        compiler_params=_cparams(("parallel", "arbitrary"), 24 << 20),
        name="moba_sample_attn",
    )(page_table, scores_t, qb, *([cache_kt] * pages_per_step), *([cache_vt] * pages_per_step), k_new, v_new)
    o5 = out.reshape(bsz, ATTN_HEADS, dec_seq, ATTN_KV_HEADS, HEAD_DIM)
    o4 = jnp.sum(o5 * jnp.asarray(own)[None, :, None, :, None], axis=3)
    return o4.transpose(0, 2, 1, 3).reshape(bsz, dec_seq, ATTN_Q)


def _mix_kernel(x_ref, ys_ref, ya_ref, gs_ref, ga_ref, gt1_ref, sc2_ref, sh2_ref, nw_ref,
                wso_ref, wao_ref, wo_ref, wrt_ref, brt_ref, x1_ref, hc_ref):
    merged = (jax.nn.sigmoid(gs_ref[0]) * _dot(ys_ref[0].astype(BF16), wso_ref[...])
              + jax.nn.sigmoid(ga_ref[0]) * _dot(ya_ref[0].astype(BF16), wao_ref[...]))
    x1 = x_ref[0] + gt1_ref[0] * _dot(merged.astype(BF16), wo_ref[...])
    x1_ref[0] = x1
    h2 = x1 * lax.rsqrt(jnp.mean(x1 * x1, axis=-1, keepdims=True) + EPS) * nw_ref[...]
    h2 = h2 * (1.0 + sc2_ref[0]) + sh2_ref[0]
    hc_ref[0, :, :D_MODEL] = h2

    h_hi = h2.astype(BF16)
    h_lo = (h2 - h_hi.astype(F32)).astype(BF16)
    hw = _dot(h_hi, wrt_ref[...])
    lg = hw[:, :LANES] + hw[:, LANES:] + _dot(h_lo, wrt_ref[:, :LANES]) + brt_ref[...]
    lane = lax.broadcasted_iota(jnp.int32, lg.shape, 1).astype(F32)
    big = float(1 << 20)
    ninf = -jnp.inf
    gl = jnp.where(lane < N_EGROUPS, lg, ninf)
    gmax = jnp.max(gl, axis=1, keepdims=True)
    gidx = jnp.min(jnp.where(gl == gmax, lane, big), axis=1, keepdims=True)
    g_w = 1.0 / jnp.sum(jnp.exp(gl - gmax), axis=1, keepdims=True)
    lo = ROUTE_OFF + gidx * EXPERTS_PER_GROUP
    el = jnp.where((lane >= lo) & (lane < lo + EXPERTS_PER_GROUP), lg, ninf)
    m1 = jnp.max(el, axis=1, keepdims=True)
    i1 = jnp.min(jnp.where(el == m1, lane, big), axis=1, keepdims=True)
    el2 = jnp.where(lane == i1, ninf, el)
    m2 = jnp.max(el2, axis=1, keepdims=True)
    i2 = jnp.min(jnp.where(el2 == m2, lane, big), axis=1, keepdims=True)
    e2 = jnp.exp(m2 - m1)
    den = 1.0 / (1.0 + e2)
    comb = jnp.where(lane == i1, den * g_w, 0.0) + jnp.where(lane == i2, e2 * den * g_w, 0.0)
    hc_ref[0, :, D_MODEL:] = jnp.where(lane == 0.0, gidx, comb)


def _mix(x, y_ssd, y_att, gs, ga, gt1, sc2, sh2, norm_w, wso, wao, wo, wrt, brt):
    bsz, seqlen, _ = x.shape
    tm = 256
    assert seqlen % tm == 0
    nt = seqlen // tm
    mod_rows = gt1.shape[1]
    if mod_rows == 1:
        mod_spec = pl.BlockSpec((1, 1, D_MODEL), lambda b, i: (b, 0, 0))
    else:
        mod_spec = pl.BlockSpec((1, tm, D_MODEL), lambda b, i: (b, i, 0))

    def tok(width):
        return pl.BlockSpec((1, tm, width), lambda b, i: (b, i, 0))

    def full(d1, d2):
        return pl.BlockSpec((d1, d2), lambda b, i: (0, 0))

    vmem = 2 * 2 * (2 * D_MODEL * D_MODEL + ATTN_Q * D_MODEL) + 2 * tm * (8 * D_MODEL) * 4 + (12 << 20)
    return pl.pallas_call(
        _mix_kernel,
        grid=(bsz, nt),
        in_specs=[tok(D_MODEL), tok(SSD_INNER), tok(ATTN_Q), tok(D_MODEL), tok(D_MODEL),
                  mod_spec, mod_spec, mod_spec, full(1, D_MODEL),
                  full(SSD_INNER, D_MODEL), full(ATTN_Q, D_MODEL), full(D_MODEL, D_MODEL),
                  full(D_MODEL, 2 * LANES), full(1, LANES)],
        out_specs=[tok(D_MODEL), tok(D_MODEL + LANES)],
        out_shape=[jax.ShapeDtypeStruct((bsz, seqlen, D_MODEL), F32),
                   jax.ShapeDtypeStruct((bsz, seqlen, D_MODEL + LANES), F32)],
        compiler_params=_cparams(("parallel", "parallel"), vmem),
        name="mix",
    )(x, y_ssd, y_att, gs, ga, gt1, sc2, sh2, norm_w.reshape(1, -1), wso, wao, wo, wrt, brt)


def _expert_step(h2, comb, expert, wg_ref, wu_ref, wd_ref):
    lane = lax.broadcasted_iota(jnp.int32, comb.shape, 1)
    cw = jnp.sum(jnp.where(lane == expert + ROUTE_OFF, comb, 0.0), axis=1, keepdims=True)
    a = _dot(h2, wg_ref[0])
    u = _dot(h2, wu_ref[0])
    act = (_silu(a) * u * cw).astype(BF16)
    return _dot(act, wd_ref[0])


def _final_norm(x1, gt2, ffn, fw):
    x2 = x1 + gt2 * ffn
    return x2 * lax.rsqrt(jnp.mean(x2 * x2, axis=-1, keepdims=True) + EPS) * fw


def _moe_kernel(hc_ref, x1_ref, gt2_ref, fw_ref, wg_ref, wu_ref, wd_ref, y_ref, acc):
    e = pl.program_id(2)

    @pl.when(e == 0)
    def _():
        acc[...] = jnp.zeros(acc.shape, F32)

    acc[...] += _expert_step(hc_ref[0, :, :D_MODEL].astype(BF16), hc_ref[0, :, D_MODEL:], e,
                             wg_ref, wu_ref, wd_ref)

    @pl.when(e == pl.num_programs(2) - 1)
    def _():
        y_ref[0] = _final_norm(x1_ref[0], gt2_ref[0], acc[...], fw_ref[...])


def _moe(hc, x1, gt2, final_w, wg, wu, wd):
    bsz, seqlen, _ = x1.shape
    tm = 1024 if seqlen % 1024 == 0 else 256
    assert seqlen % tm == 0
    nt = seqlen // tm
    mod_rows = gt2.shape[1]
    if mod_rows == 1:
        mod_spec = pl.BlockSpec((1, 1, D_MODEL), lambda b, i, e: (b, 0, 0))
    else:
        mod_spec = pl.BlockSpec((1, tm, D_MODEL), lambda b, i, e: (b, i, 0))

    def tok(width):
        return pl.BlockSpec((1, tm, width), lambda b, i, e: (b, i, 0))

    vmem = 2 * tm * D_MODEL * (2 + 4 + 4) + tm * D_MODEL * 4 + 2 * 3 * D_MODEL * D_EXPERT * 2 + (16 << 20)
    return pl.pallas_call(
        _moe_kernel,
        grid=(bsz, nt, N_EXPERTS),
        in_specs=[tok(D_MODEL + LANES), tok(D_MODEL), mod_spec,
                  pl.BlockSpec((1, D_MODEL), lambda b, i, e: (0, 0)),
                  pl.BlockSpec((1, D_MODEL, D_EXPERT), lambda b, i, e: (e, 0, 0)),
                  pl.BlockSpec((1, D_MODEL, D_EXPERT), lambda b, i, e: (e, 0, 0)),
                  pl.BlockSpec((1, D_EXPERT, D_MODEL), lambda b, i, e: (e, 0, 0))],
        out_specs=tok(D_MODEL),
        out_shape=jax.ShapeDtypeStruct((bsz, seqlen, D_MODEL), F32),
        scratch_shapes=[pltpu.VMEM((tm, D_MODEL), F32)],
        compiler_params=_cparams(("parallel", "parallel", "arbitrary"), vmem),
        name="moe",
    )(hc, x1, gt2, final_w.reshape(1, -1), wg, wu, wd)


ROUTE_TILE = 512
DISPATCH_TILE = 256


def _move_rows(src_of, dst_of, sem, whole_src, whole_dst):
    def body(r, carry):
        pltpu.make_async_copy(src_of(r), dst_of(r), sem).start()
        return carry

    lax.fori_loop(0, DISPATCH_TILE, body, 0, unroll=8)
    pltpu.make_async_copy(whole_src, whole_dst, sem).wait()


def _dispatch_kernel(pos_ref, hc_ref, xs_in, xs_ref, sem):
    del xs_in
    base = pl.program_id(0) * DISPATCH_TILE
    _move_rows(lambda r: hc_ref.at[pl.ds(r, 1), :], lambda r: xs_ref.at[pl.ds(pos_ref[base + r], 1), :],
               sem.at[0], hc_ref, xs_ref.at[pl.ds(0, DISPATCH_TILE), :])


def _moe_routed_kernel(tg_ref, xs_ref, wg_ref, wu_ref, wd_ref, ys_ref, xb, acc):
    t, e = pl.program_id(0), pl.program_id(1)

    @pl.when(e == 0)
    def _():
        xb[...] = xs_ref[:, :D_MODEL].astype(BF16)
        acc[...] = jnp.zeros(acc.shape, F32)

    expert = tg_ref[t] * EXPERTS_PER_GROUP + e
    acc[...] += _expert_step(xb[...], xs_ref[:, D_MODEL:], expert, wg_ref, wu_ref, wd_ref)

    @pl.when(e == pl.num_programs(1) - 1)
    def _():
        ys_ref[...] = acc[...]


def _combine_kernel(pos_ref, x1_ref, gt2_ref, fw_ref, ys_ref, y_ref, ybuf, sem):
    base = pl.program_id(0) * DISPATCH_TILE
    _move_rows(lambda r: ys_ref.at[pl.ds(pos_ref[base + r], 1), :], lambda r: ybuf.at[pl.ds(r, 1), :],
               sem.at[0], ys_ref.at[pl.ds(0, DISPATCH_TILE), :], ybuf)
    y_ref[...] = _final_norm(x1_ref[...], gt2_ref[0], ybuf[...], fw_ref[...])


def _moe_routed(hc, x1, gt2, final_w, wg, wu, wd):
    bsz, seqlen, width = hc.shape
    ntok = bsz * seqlen
    assert seqlen % DISPATCH_TILE == 0 and ntok % ROUTE_TILE == 0 and gt2.shape[1] == 1
    hcf, x1f = hc.reshape(ntok, width), x1.reshape(ntok, D_MODEL)
    gidx = hcf[:, D_MODEL].astype(jnp.int32)
    onehot = (gidx[:, None] == jnp.arange(N_EGROUPS, dtype=jnp.int32)[None, :]).astype(jnp.int32)
    csum = jnp.cumsum(onehot, axis=0)
    padded = (csum[-1] + ROUTE_TILE - 1) // ROUTE_TILE * ROUTE_TILE
    ends = jnp.cumsum(padded)
    pos = jnp.sum(onehot * (csum - 1 + (ends - padded)[None, :]), axis=1).astype(jnp.int32)
    n_tiles = ntok // ROUTE_TILE + N_EGROUPS
    tile_start = jnp.arange(n_tiles, dtype=jnp.int32) * ROUTE_TILE
    tile_group = jnp.minimum(jnp.sum((tile_start[:, None] >= ends[None, :]).astype(jnp.int32), axis=1),
                             N_EGROUPS - 1).astype(jnp.int32)
    nrows = n_tiles * ROUTE_TILE
    nsteps = ntok // DISPATCH_TILE
    any_spec = pl.BlockSpec(memory_space=pl.ANY)

    xs = pl.pallas_call(
        _dispatch_kernel,
        grid_spec=pltpu.PrefetchScalarGridSpec(
            num_scalar_prefetch=1, grid=(nsteps,),
            in_specs=[pl.BlockSpec((DISPATCH_TILE, width), lambda i, pos: (i, 0)), any_spec],
            out_specs=any_spec,
            scratch_shapes=[pltpu.SemaphoreType.DMA((1,))]),
        out_shape=jax.ShapeDtypeStruct((nrows, width), F32),
        input_output_aliases={2: 0},
        compiler_params=_cparams(("arbitrary",), 16 << 20),
        name="moe_dispatch",
    )(pos, hcf, jnp.zeros((nrows, width), F32))

    def w_spec(d1, d2):
        return pl.BlockSpec((1, d1, d2), lambda t, e, tg: (tg[t] * EXPERTS_PER_GROUP + e, 0, 0))

    vmem = (2 * ROUTE_TILE * (width + D_MODEL) * 4 + ROUTE_TILE * D_MODEL * (2 + 4)
            + 2 * 3 * D_MODEL * D_EXPERT * 2 + (16 << 20))
    ys = pl.pallas_call(
        _moe_routed_kernel,
        grid_spec=pltpu.PrefetchScalarGridSpec(
            num_scalar_prefetch=1, grid=(n_tiles, EXPERTS_PER_GROUP),
            in_specs=[pl.BlockSpec((ROUTE_TILE, width), lambda t, e, tg: (t, 0)),
                      w_spec(D_MODEL, D_EXPERT), w_spec(D_MODEL, D_EXPERT), w_spec(D_EXPERT, D_MODEL)],
            out_specs=pl.BlockSpec((ROUTE_TILE, D_MODEL), lambda t, e, tg: (t, 0)),
            scratch_shapes=[pltpu.VMEM((ROUTE_TILE, D_MODEL), BF16), pltpu.VMEM((ROUTE_TILE, D_MODEL), F32)]),
        out_shape=jax.ShapeDtypeStruct((nrows, D_MODEL), F32),
        compiler_params=_cparams(("parallel", "arbitrary"), vmem),
        name="moe_routed",
    )(tile_group, xs, wg, wu, wd)

    steps_per_batch = seqlen // DISPATCH_TILE
    y = pl.pallas_call(
        _combine_kernel,
        grid_spec=pltpu.PrefetchScalarGridSpec(
            num_scalar_prefetch=1, grid=(nsteps,),
            in_specs=[pl.BlockSpec((DISPATCH_TILE, D_MODEL), lambda i, pos: (i, 0)),
                      pl.BlockSpec((1, 1, D_MODEL), lambda i, pos: (i // steps_per_batch, 0, 0)),
                      pl.BlockSpec((1, D_MODEL), lambda i, pos: (0, 0)), any_spec],
            out_specs=pl.BlockSpec((DISPATCH_TILE, D_MODEL), lambda i, pos: (i, 0)),
            scratch_shapes=[pltpu.VMEM((DISPATCH_TILE, D_MODEL), F32), pltpu.SemaphoreType.DMA((1,))]),
        out_shape=jax.ShapeDtypeStruct((ntok, D_MODEL), F32),
        compiler_params=_cparams(("arbitrary",), 16 << 20),
        name="moe_combine",
    )(pos, x1f, gt2, final_w.reshape(1, -1), ys)
    return y.reshape(bsz, seqlen, D_MODEL)


def _prep_weights(lp):
    w_in = lp["w_in"]
    sizes = (SSD_INNER, CONV_DIM, SSD_HEADS, ATTN_Q, ATTN_KV, ATTN_KV, D_MODEL, D_MODEL)
    pts = [int(p) for p in np.cumsum(sizes)[:-1]]
    z, xbc, dt, q, k, v, gs, ga = jnp.split(w_in, pts, axis=1)
    dt = jnp.pad(dt, ((0, 0), (0, LANES - SSD_HEADS)))
    w_in_r = jnp.concatenate([z, xbc, q, k, v, gs, ga, dt], axis=1).astype(BF16)
    wrt = jnp.concatenate([lp["w_group"], lp["w_router"]], axis=1)
    wrt = jnp.pad(wrt, ((0, 0), (0, LANES - wrt.shape[1]))).astype(F32)
    wrt_hi = wrt.astype(BF16)
    wrt = jnp.concatenate([wrt_hi, (wrt - wrt_hi.astype(F32)).astype(BF16)], axis=1)
    brt = jnp.concatenate([lp["b_group"], lp["b_router"]])
    brt = jnp.pad(brt, (0, LANES - brt.shape[0])).astype(F32).reshape(1, LANES)
    return dict(w_in_r=w_in_r, wrt=wrt, brt=brt,
                wso=lp["w_ssd_out"].astype(BF16), wao=lp["w_attn_out"].astype(BF16), wo=lp["w_o"].astype(BF16),
                wg=lp["w_gate_e"].astype(BF16), wu=lp["w_up_e"].astype(BF16), wd=lp["w_down_e"].astype(BF16))


def _mods(mod, per_row_repeat):
    parts = jnp.split(mod, 6, axis=-1)
    if per_row_repeat:
        return [jnp.repeat(p, per_row_repeat, axis=0)[None] for p in parts]
    return [p[:, None, :] for p in parts]


def _layer_prompt(x, mod, lp, pw, final_w):
    bsz, seqlen, _ = x.shape
    sh1, sc1, gt1, sh2, sc2, gt2 = _mods(mod, 0)
    tables = _rope_tables(np.arange(seqlen))
    (z, xbc, dt, q, k, v, gs, ga, kmean, kaug, vaug) = _in_proj(x, sc1, sh1, lp["norm1_w"], pw["w_in_r"], tables, True)
    conv0 = jnp.zeros((bsz, SSD_CONV - 1, CONV_DIM), F32)
    ssm0 = jnp.zeros((bsz, SSD_HEADS, SSD_HEADDIM, SSD_STATE), F32)
    y_ssd, conv_new, ssm_new = _ssd(xbc, z, dt, conv0, ssm0, lp["conv_w"], lp["conv_b"], lp["dt_bias"],
                                    lp["a_log"], lp["d_skip"], lp["ssd_norm_w"])
    y_att = _moba_prompt(q, kmean, kaug, vaug)
    x1, hc = _mix(x, y_ssd, y_att, gs, ga, gt1, sc2, sh2, lp["norm2_w"],
                  pw["wso"], pw["wao"], pw["wo"], pw["wrt"], pw["brt"])
    y = _moe_routed(hc, x1, gt2, final_w, pw["wg"], pw["wu"], pw["wd"])
    kv_shape = (bsz, seqlen, ATTN_KV_HEADS, HEAD_DIM)
    return y, k.reshape(kv_shape), v.reshape(kv_shape), conv_new, ssm_new


def _layer_sample(x, mod, conv_buf, ssm_state, cache_k, cache_v, page_table, past_len, lp, pw, final_w):
    bsz, dec_seq, _ = x.shape
    ntok = bsz * dec_seq
    sh1, sc1, gt1, sh2, sc2, gt2 = _mods(mod, dec_seq)
    pos = past_len + (np.arange(ntok) % dec_seq)
    tables = _rope_tables(pos)
    xf = x.reshape(1, ntok, D_MODEL)
    z, xbc, dt, q, k, v, gs, ga = _in_proj(xf, sc1, sh1, lp["norm1_w"], pw["w_in_r"], tables, False)

    def per_b(t):
        return t.reshape(bsz, dec_seq, t.shape[-1])

    y_ssd, conv_new, ssm_new = _ssd(per_b(xbc), per_b(z), per_b(dt), conv_buf, ssm_state, lp["conv_w"],
                                    lp["conv_b"], lp["dt_bias"], lp["a_log"], lp["d_skip"], lp["ssd_norm_w"])
    n_pool, page = cache_k.shape[0], cache_k.shape[1]

    def position_minor(cache):
        return cache.transpose(0, 2, 3, 1).reshape(n_pool, ATTN_KV, page)

    y_att = _moba_sample(per_b(q), per_b(k), per_b(v), position_minor(cache_k), position_minor(cache_v),
                         page_table, past_len)
    x1, hc = _mix(xf, y_ssd.reshape(1, ntok, -1), y_att.reshape(1, ntok, -1), gs, ga, gt1, sc2, sh2,
                  lp["norm2_w"], pw["wso"], pw["wao"], pw["wo"], pw["wrt"], pw["brt"])
    y = _moe(hc, x1, gt2, final_w, pw["wg"], pw["wu"], pw["wd"])
    kv_shape = (bsz, dec_seq, ATTN_KV_HEADS, HEAD_DIM)
    return y.reshape(bsz, dec_seq, D_MODEL), k.reshape(kv_shape), v.reshape(kv_shape), conv_new, ssm_new


def kernel(x_prompt, x_sample, cache_k, cache_v, state_conv, state_ssm, page_table, c_prompt, c_sample,
           w_ada, b_ada, norm1_w, w_in, conv_w, conv_b, dt_bias, a_log, d_skip, ssd_norm_w,
           w_ssd_out, w_attn_out, w_o, norm2_w, w_group, b_group, w_router, b_router,
           w_gate_e, w_up_e, w_down_e, final_w):
    depth = w_in.shape[0]
    assert depth == 1, "the final RMSNorm is fused into the last layer's MoE kernel"
    nb_p, nb_s = c_prompt.shape[0], c_sample.shape[0]
    past_len = page_table.shape[1] * cache_k.shape[2]
    l = 0
    lp = dict(norm1_w=norm1_w[l], w_in=w_in[l], conv_w=conv_w[l], conv_b=conv_b[l], dt_bias=dt_bias[l],
              a_log=a_log[l], d_skip=d_skip[l], ssd_norm_w=ssd_norm_w[l], w_ssd_out=w_ssd_out[l],
              w_attn_out=w_attn_out[l], w_o=w_o[l], norm2_w=norm2_w[l], w_group=w_group[l], b_group=b_group[l],
              w_router=w_router[l], b_router=b_router[l], w_gate_e=w_gate_e[l], w_up_e=w_up_e[l],
              w_down_e=w_down_e[l])
    pw = _prep_weights(lp)
    c_all = jnp.concatenate([c_prompt, c_sample], axis=0)
    pad_rows = (-c_all.shape[0]) % SUBLANES
    mod = _ada_mod(jnp.pad(c_all, ((0, pad_rows), (0, 0))), w_ada[l], b_ada[l])
    yp, kp, vp, cp, sp = _layer_prompt(x_prompt, mod[:nb_p], lp, pw, final_w)
    ys, ks, vs, cs, ss = _layer_sample(x_sample, mod[nb_p:nb_p + nb_s], state_conv[l], state_ssm[l],
                                       cache_k[l], cache_v[l], page_table, past_len, lp, pw, final_w)
    return (yp, ys, kp[None], vp[None], cp[None], sp[None], ks[None], vs[None], cs[None], ss[None])
```

```python
import functools
import math

import numpy as np
import jax
import jax.numpy as jnp
from jax import lax
from jax.experimental import pallas as pl
from jax.experimental.pallas import tpu as pltpu

F32 = jnp.float32
BF16 = jnp.bfloat16
HIGHEST = lax.Precision.HIGHEST

D_MODEL = 1024
SSD_HEADDIM = 64
SSD_INNER = D_MODEL
SSD_HEADS = SSD_INNER // SSD_HEADDIM
SSD_GROUPS = 2
SSD_STATE = 128
SSD_CONV = 4
SSD_CHUNK = 128
SSD_BC = SSD_GROUPS * SSD_STATE
CONV_DIM = SSD_INNER + 2 * SSD_BC
ATTN_HEADS = 8
ATTN_KV_HEADS = 4
HEAD_DIM = 64
ATTN_Q = ATTN_HEADS * HEAD_DIM
ATTN_KV = ATTN_KV_HEADS * HEAD_DIM
ROT_DIM = HEAD_DIM // 4
ROPE_THETA = 500000.0
MOBA_BLOCK = 256
MOBA_TOPK = 3
N_EGROUPS = 4
EXPERTS_PER_GROUP = 4
N_EXPERTS = N_EGROUPS * EXPERTS_PER_GROUP
D_EXPERT = D_MODEL // 2
EPS = 1e-6

LANES = 128
SUBLANES = 8
VMEM_CAP = 56 << 20
NEG = -1e30
GRP = ATTN_HEADS // ATTN_KV_HEADS
ROUTE_OFF = N_EGROUPS

_SEG = {}
_off = 0
for _name, _w in (("z", SSD_INNER), ("xbc", CONV_DIM), ("q", ATTN_Q), ("k", ATTN_KV), ("v", ATTN_KV),
                  ("gs", D_MODEL), ("ga", D_MODEL), ("dt", LANES)):
    _SEG[_name] = (_off, _off + _w)
    _off += _w
IN_PAD = _off


def _cparams(semantics, vmem_bytes):
    return pltpu.CompilerParams(dimension_semantics=semantics,
                                vmem_limit_bytes=int(min(max(vmem_bytes, 16 << 20), VMEM_CAP)))


def _dot(a, b):
    return jnp.dot(a, b, preferred_element_type=F32)


def _dot_nt(a, b):
    return lax.dot_general(a, b, (((1,), (1,)), ((), ())), preferred_element_type=F32)


def _dot_exact(a, b):
    return jnp.dot(a, b, preferred_element_type=F32, precision=HIGHEST)


def _silu(x):
    return x * jax.nn.sigmoid(x)


def _mod_kernel(c_ref, w_ref, b_ref, o_ref):
    o_ref[...] = _dot_exact(_silu(c_ref[...]), w_ref[...]) + b_ref[...]


def _ada_mod(c_all, w_ada, b_ada):
    rows = c_all.shape[0]
    return pl.pallas_call(
        _mod_kernel,
        grid=(6,),
        in_specs=[pl.BlockSpec((rows, D_MODEL), lambda j: (0, 0)),
                  pl.BlockSpec((D_MODEL, D_MODEL), lambda j: (0, j)),
                  pl.BlockSpec((1, D_MODEL), lambda j: (0, j))],
        out_specs=pl.BlockSpec((rows, D_MODEL), lambda j: (0, j)),
        out_shape=jax.ShapeDtypeStruct((rows, 6 * D_MODEL), F32),
        compiler_params=_cparams(("arbitrary",), 24 << 20),
        name="ada_mod",
    )(c_all, w_ada, b_ada.reshape(1, -1))


def _rope_tables(pos):
    half = ROT_DIM // 2
    c = -2.0 * math.log(ROPE_THETA) / ROT_DIM
    c_hi = float(np.float32(c))
    j = jnp.arange(half, dtype=F32)
    inv_freq = jnp.exp(j * c_hi + j * (c - c_hi))
    ang = jnp.asarray(pos).astype(F32)[:, None] * inv_freq[None, :]
    cos, sin = jnp.cos(ang), jnp.sin(ang)
    n = pos.shape[0]
    one = jnp.ones((n, HEAD_DIM - ROT_DIM), F32)
    zero = jnp.zeros((n, half), F32)
    rest = jnp.zeros((n, HEAD_DIM - ROT_DIM), F32)
    a = jnp.concatenate([cos, cos, one], axis=1)
    s1 = jnp.concatenate([-sin, zero, rest], axis=1)
    s2 = jnp.concatenate([zero, sin, rest], axis=1)
    rep = LANES // HEAD_DIM
    return tuple(jnp.tile(t, (1, rep)) for t in (a, s1, s2))


def _inproj_kernel(*refs, attn_layout):
    (x_ref, sc_ref, sh_ref, nw_ref, w_ref, ra_ref, rs1_ref, rs2_ref,
     z_ref, xbc_ref, dt_ref, q_ref, k_ref, v_ref, gs_ref, ga_ref) = refs[:16]
    x = x_ref[0]
    h = x * lax.rsqrt(jnp.mean(x * x, axis=-1, keepdims=True) + EPS) * nw_ref[...]
    h = (h * (1.0 + sc_ref[0]) + sh_ref[0]).astype(BF16)

    def proj(name, out_ref):
        lo, hi = _SEG[name]
        step = 512
        for c in range(lo, hi, step):
            w = min(step, hi - c)
            out_ref[0, :, c - lo:c - lo + w] = _dot(h, w_ref[:, c:c + w])

    proj("z", z_ref)
    proj("xbc", xbc_ref)
    proj("gs", gs_ref)
    proj("ga", ga_ref)
    proj("dt", dt_ref)

    def rope(t):
        width = t.shape[1]
        reps = width // LANES
        a = jnp.concatenate([ra_ref[...]] * reps, axis=1)
        s1 = jnp.concatenate([rs1_ref[...]] * reps, axis=1)
        s2 = jnp.concatenate([rs2_ref[...]] * reps, axis=1)
        return t * a + pltpu.roll(t, width - ROT_DIM // 2, 1) * s1 + pltpu.roll(t, ROT_DIM // 2, 1) * s2

    lo, hi = _SEG["q"]
    q_ref[0] = rope(_dot(h, w_ref[:, lo:hi]))
    lo, hi = _SEG["k"]
    k = rope(_dot(h, w_ref[:, lo:hi]))
    k_ref[0] = k
    lo, hi = _SEG["v"]
    v = _dot(h, w_ref[:, lo:hi])
    v_ref[0] = v

    if attn_layout:
        kmean_ref, kaug_ref, vaug_ref = refs[16:19]
        tm = k.shape[0]
        kmean_ref[0, 0] = jnp.mean(k, axis=0, keepdims=True)
        lane = lax.broadcasted_iota(jnp.int32, (tm, HEAD_DIM), 1)
        blk_onehot = (lane == pl.program_id(1)).astype(F32)
        ones_col = (lane == 0).astype(F32)
        for kh in range(ATTN_KV_HEADS):
            sl = slice(kh * HEAD_DIM, (kh + 1) * HEAD_DIM)
            kaug_ref[0, kh] = jnp.concatenate([k[:, sl], blk_onehot], axis=1).astype(BF16)
            vaug_ref[0, kh] = jnp.concatenate([v[:, sl], ones_col], axis=1).astype(BF16)


def _in_proj(x, sc1, sh1, norm_w, w_in_r, tables, attn_layout):
    bsz, seqlen, _ = x.shape
    tm = MOBA_BLOCK
    assert seqlen % tm == 0
    nt = seqlen // tm
    mod_rows = sc1.shape[1]
    if mod_rows == 1:
        mod_spec = pl.BlockSpec((1, 1, D_MODEL), lambda b, i: (b, 0, 0))
    else:
        mod_spec = pl.BlockSpec((1, tm, D_MODEL), lambda b, i: (b, i, 0))
    tab_spec = pl.BlockSpec((tm, LANES), lambda b, i: (i, 0))

    def tok(width):
        return pl.BlockSpec((1, tm, width), lambda b, i: (b, i, 0))

    def shp(width):
        return jax.ShapeDtypeStruct((bsz, seqlen, width), F32)

    widths = (SSD_INNER, CONV_DIM, LANES, ATTN_Q, ATTN_KV, ATTN_KV, D_MODEL, D_MODEL)
    out_specs = [tok(w) for w in widths]
    out_shape = [shp(w) for w in widths]
    if attn_layout:
        assert nt <= HEAD_DIM
        out_specs += [pl.BlockSpec((1, 1, 1, ATTN_KV), lambda b, i: (b, i, 0, 0)),
                      pl.BlockSpec((1, ATTN_KV_HEADS, tm, LANES), lambda b, i: (b, 0, i, 0)),
                      pl.BlockSpec((1, ATTN_KV_HEADS, tm, LANES), lambda b, i: (b, 0, i, 0))]
        out_shape += [jax.ShapeDtypeStruct((bsz, nt, 1, ATTN_KV), F32),
                      jax.ShapeDtypeStruct((bsz, ATTN_KV_HEADS, seqlen, LANES), BF16),
                      jax.ShapeDtypeStruct((bsz, ATTN_KV_HEADS, seqlen, LANES), BF16)]
    vmem = 2 * (D_MODEL * IN_PAD * 2) + 2 * tm * (D_MODEL + IN_PAD + 4 * LANES) * 4 + (8 << 20)
    return pl.pallas_call(
        functools.partial(_inproj_kernel, attn_layout=attn_layout),
        grid=(bsz, nt),
        in_specs=[tok(D_MODEL), mod_spec, mod_spec,
                  pl.BlockSpec((1, D_MODEL), lambda b, i: (0, 0)),
                  pl.BlockSpec((D_MODEL, IN_PAD), lambda b, i: (0, 0)),
                  tab_spec, tab_spec, tab_spec],
        out_specs=out_specs,
        out_shape=out_shape,
        compiler_params=_cparams(("parallel", "parallel"), vmem),
        name="in_proj",
    )(x, sc1, sh1, norm_w.reshape(1, -1), w_in_r, *tables)


def _cumsum_rows(a):
    row = lax.broadcasted_iota(jnp.int32, a.shape, 0)
    s = 1
    while s < a.shape[0]:
        a = a + jnp.where(row >= s, pltpu.roll(a, s, 0), 0.0)
        s *= 2
    return a


def _ssd_kernel(xbc_ref, z_ref, dt_ref, cbuf_ref, s0_ref, cw_ref, cb_ref, dtb_ref, alog_ref, dsk_ref, nw_ref,
                y_ref, cnew_ref, sout_ref, xext, st, ysc, *, rows):
    ch = SSD_CHUNK
    pad = SUBLANES
    c = pl.program_id(1)
    last = pl.num_programs(1) - 1

    @pl.when(c == 0)
    def _():
        xext[0:pad, :] = jnp.zeros((pad, CONV_DIM), F32)
        xext[pad - (SSD_CONV - 1):pad, :] = cbuf_ref[0]
        st[...] = s0_ref[0].T

    xext[pad:pad + rows, :] = xbc_ref[0]
    if rows < ch:
        xext[pad + rows:pad + ch, :] = jnp.zeros((ch - rows, CONV_DIM), F32)

    conv = cb_ref[...] + xext[pad:pad + ch, :] * cw_ref[SSD_CONV - 1:SSD_CONV, :]
    for w in range(SSD_CONV - 1):
        off = pad - (SSD_CONV - 1) + w
        conv = conv + xext[off:off + ch, :] * cw_ref[w:w + 1, :]
    act = _silu(conv)

    @pl.when(c == last)
    def _():
        cnew_ref[0] = xext[pad + rows - (SSD_CONV - 1):pad + rows, :]

    xext[0:pad, :] = xext[ch:ch + pad, :]

    xs = act[:, :SSD_INNER]
    dt_raw = dt_ref[0]
    if rows < ch:
        dt_raw = jnp.concatenate([dt_raw, jnp.zeros((ch - rows, LANES), F32)], axis=0)
    dt_in = dt_raw + dtb_ref[...]
    dt = jnp.maximum(dt_in, 0.0) + jnp.log1p(jnp.exp(-jnp.abs(dt_in)))
    row = lax.broadcasted_iota(jnp.int32, (ch, LANES), 0)
    col = lax.broadcasted_iota(jnp.int32, (ch, LANES), 1)
    if rows < ch:
        dt = jnp.where(row < rows, dt, 0.0)
    a_cs = _cumsum_rows(dt * (-jnp.exp(alog_ref[...])))
    a_cs_t = a_cs.T
    dt_t = dt.T
    causal = row >= col

    heads_per_pair = LANES // SSD_HEADDIM
    assert heads_per_pair == 2 and SSD_STATE == ch
    first_of_pair = col < SSD_HEADDIM
    for g in range(SSD_GROUPS):
        bg = act[:, SSD_INNER + g * SSD_STATE:SSD_INNER + (g + 1) * SSD_STATE]
        cg = act[:, SSD_INNER + SSD_BC + g * SSD_STATE:SSD_INNER + SSD_BC + (g + 1) * SSD_STATE]
        gmat = _dot_nt(cg.astype(BF16), bg.astype(BF16))
        bg_t = bg.T
        for pair in range(SSD_HEADS // SSD_GROUPS // heads_per_pair):
            hd0 = g * (SSD_HEADS // SSD_GROUPS) + pair * heads_per_pair
            sl = slice(hd0 * SSD_HEADDIM, hd0 * SSD_HEADDIM + LANES)
            x_pair = xs[:, sl]
            s_pair = st[:, sl]
            x_diag = [jnp.where(first_of_pair, x_pair, 0.0).astype(BF16),
                      jnp.where(first_of_pair, 0.0, x_pair).astype(BF16)]
            s_diag = [jnp.where(first_of_pair, s_pair, 0.0).astype(BF16),
                      jnp.where(first_of_pair, 0.0, s_pair).astype(BF16)]
            m_parts, c_parts, b_parts, decay = [], [], [], []
            for hd in range(hd0, hd0 + heads_per_pair):
                acs_col = a_cs[:, hd:hd + 1]
                acs_row = a_cs_t[hd:hd + 1, :]
                dt_row = dt_t[hd:hd + 1, :]
                acs_last = a_cs_t[hd:hd + 1, ch - 1:ch]
                lmat = jnp.where(causal, jnp.exp(acs_col - acs_row), 0.0)
                m_parts.append((gmat * lmat * dt_row).astype(BF16))
                c_parts.append((cg * jnp.exp(acs_col)).astype(BF16))
                b_parts.append((bg_t * (jnp.exp(acs_last - acs_row) * dt_row)).astype(BF16))
                decay.append(jnp.exp(acs_last))
            ysc[:, sl] = _dot(jnp.concatenate(m_parts + c_parts, axis=1), jnp.concatenate(x_diag + s_diag, axis=0))
            st[:, sl] = (jnp.where(first_of_pair[:1], decay[0], decay[1]) * s_pair
                         + _dot(jnp.concatenate(b_parts, axis=1), jnp.concatenate(x_diag, axis=0)))

    y = (ysc[0:rows, :] + xs[:rows] * dsk_ref[...]) * _silu(z_ref[0])
    gw = SSD_INNER // SSD_GROUPS
    for g in range(SSD_GROUPS):
        yg = y[:, g * gw:(g + 1) * gw]
        yg = yg * lax.rsqrt(jnp.mean(yg * yg, axis=-1, keepdims=True) + EPS)
        y_ref[0, :, g * gw:(g + 1) * gw] = yg * nw_ref[:, g * gw:(g + 1) * gw]

    @pl.when(c == last)
    def _():
        sout_ref[0] = st[...].T


def _ssd(xbc, z, dt, conv_buf, ssm_state, conv_w, conv_b, dt_bias, a_log, d_skip, norm_w):
    bsz, seqlen, _ = xbc.shape
    rows = SSD_CHUNK if seqlen % SSD_CHUNK == 0 else seqlen
    assert rows == seqlen or rows == SSD_CHUNK
    assert rows % SUBLANES == 0 and rows >= SSD_CONV - 1
    nc = seqlen // rows
    hp = SSD_HEADS * SSD_HEADDIM

    def tok(width):
        return pl.BlockSpec((1, rows, width), lambda b, c: (b, c, 0))

    def per_b(d1, d2):
        return pl.BlockSpec((1, d1, d2), lambda b, c: (b, 0, 0))

    def vec(width, r=1):
        return pl.BlockSpec((r, width), lambda b, c: (0, 0))

    def pad_lanes(v):
        return jnp.pad(v.astype(F32), (0, LANES - v.shape[0])).reshape(1, LANES)

    y, conv_new, s_new = pl.pallas_call(
        functools.partial(_ssd_kernel, rows=rows),
        grid=(bsz, nc),
        in_specs=[tok(CONV_DIM), tok(SSD_INNER), tok(LANES),
                  per_b(SSD_CONV - 1, CONV_DIM), per_b(hp, SSD_STATE),
                  vec(CONV_DIM, SSD_CONV), vec(CONV_DIM), vec(LANES), vec(LANES), vec(SSD_INNER), vec(SSD_INNER)],
        out_specs=[tok(SSD_INNER), per_b(SSD_CONV - 1, CONV_DIM), per_b(hp, SSD_STATE)],
        out_shape=[jax.ShapeDtypeStruct((bsz, seqlen, SSD_INNER), F32),
                   jax.ShapeDtypeStruct((bsz, SSD_CONV - 1, CONV_DIM), F32),
                   jax.ShapeDtypeStruct((bsz, hp, SSD_STATE), F32)],
        scratch_shapes=[pltpu.VMEM((SSD_CHUNK + SUBLANES, CONV_DIM), F32),
                        pltpu.VMEM((SSD_STATE, hp), F32),
                        pltpu.VMEM((SSD_CHUNK, SSD_INNER), F32)],
        compiler_params=_cparams(("parallel", "arbitrary"), 40 << 20),
        name="ssd",
    )(xbc, z, dt, conv_buf, ssm_state.reshape(bsz, hp, SSD_STATE),
      conv_w, conv_b.reshape(1, -1), pad_lanes(dt_bias), pad_lanes(a_log),
      jnp.repeat(d_skip.astype(F32), SSD_HEADDIM).reshape(1, -1), norm_w.reshape(1, -1))
    return y, conv_new, s_new.reshape(bsz, SSD_HEADS, SSD_HEADDIM, SSD_STATE)


def _select_topk(scores, valid, axis):
    sc = scores if valid is None else jnp.where(valid, scores, -jnp.inf)
    sel = jnp.zeros(scores.shape, F32)
    pos_f = lax.broadcasted_iota(jnp.int32, scores.shape, axis).astype(F32)
    for _ in range(MOBA_TOPK):
        m = jnp.max(sc, axis=axis, keepdims=True)
        idx = jnp.min(jnp.where(sc == m, pos_f, float(1 << 20)), axis=axis, keepdims=True)
        idx = jnp.where(m > -jnp.inf, idx, -1.0)
        hit = pos_f == idx
        sel = jnp.where(hit, 1.0, sel)
        sc = jnp.where(hit, -jnp.inf, sc)
    return sel > 0.5


def _moba_prompt_kernel(q_ref, km_ref, kaug_ref, vaug_ref, o_ref, s_a, s_b, *, qblocks):
    j = pl.program_id(2)
    tq = MOBA_BLOCK
    chunk = qblocks * tq
    half = GRP * tq
    q2 = q_ref[0]
    qs = jnp.concatenate([q2[b * tq:(b + 1) * tq, g * HEAD_DIM:(g + 1) * HEAD_DIM]
                          for b in range(qblocks) for g in range(GRP)], axis=0)
    rows = qs.shape[0]
    lane = lax.broadcasted_iota(jnp.int32, (rows, LANES), 1)
    scores_t = lax.dot_general(km_ref[0, 0], qs, (((1,), (1,)), ((), ())), preferred_element_type=F32,
                               precision=HIGHEST)
    nbp = scores_t.shape[0]
    blk = lax.broadcasted_iota(jnp.int32, scores_t.shape, 0)
    col = lax.broadcasted_iota(jnp.int32, scores_t.shape, 1)
    own = qblocks * j
    for b in range(1, qblocks):
        own = own + (col >= b * half).astype(jnp.int32)
    sel_t = _select_topk(scores_t, blk < own, 0)
    pen_t = jnp.concatenate([jnp.zeros((HEAD_DIM, rows), F32), jnp.where(sel_t, 0.0, NEG),
                             jnp.full((LANES - HEAD_DIM - nbp, rows), NEG, F32)], axis=0)
    pen = pen_t.T
    q_pad = jnp.concatenate([qs * (HEAD_DIM ** -0.5), jnp.zeros((rows, LANES - HEAD_DIM), F32)], axis=1)
    q_aug = jnp.where(lane < HEAD_DIM, q_pad, pen).astype(BF16)
    q_own = q_pad.astype(BF16)

    def update(m, acc, s, pv):
        m_new = jnp.maximum(m, jnp.max(s, axis=1, keepdims=True))
        p = jnp.exp(s - m_new)
        acc = jnp.exp(m - m_new) * acc + pv(p.astype(BF16))
        return m_new, acc

    total_chunks = kaug_ref.shape[2] // chunk
    npairs = (j + 2) // 2

    def logits(c):
        start = pl.multiple_of(jnp.minimum(c, total_chunks - 1) * chunk, chunk)
        return _dot_nt(q_aug, kaug_ref[0, 0, pl.ds(start, chunk), :])

    def values(c):
        vb = vaug_ref[0, 0, pl.ds(pl.multiple_of(c * chunk, chunk), chunk), :]
        return lambda p: _dot(p, vb)

    def body(t, carry):
        m, acc = carry
        c0 = 2 * t
        s_b[...] = logits(c0 + 1)
        m, acc = update(m, acc, s_a[...], values(c0))
        s_a[...] = logits(c0 + 2)
        m, acc = update(m, acc, s_b[...], values(c0 + 1))
        return m, acc

    def own_rows(ref, b):
        return ref[0, 0, pl.ds(pl.multiple_of((qblocks * j + b) * tq, tq), tq), :]

    s_own = jnp.concatenate([_dot_nt(q_own[b * half:(b + 1) * half], own_rows(kaug_ref, b))
                             for b in range(qblocks)], axis=0)
    r = lax.broadcasted_iota(jnp.int32, (rows, tq), 0)
    cidx = lax.broadcasted_iota(jnp.int32, (rows, tq), 1)
    s_own = jnp.where(cidx <= (r & (tq - 1)), s_own, NEG)

    def own_values(p):
        return jnp.concatenate([_dot(p[b * half:(b + 1) * half], own_rows(vaug_ref, b))
                                for b in range(qblocks)], axis=0)

    s_a[...] = logits(0)
    init = (jnp.full((rows, 1), -jnp.inf, F32), jnp.zeros((rows, LANES), F32))
    m, acc = lax.fori_loop(0, npairs, body, init)
    _, acc = update(m, acc, s_own, own_values)
    out = acc[:, :HEAD_DIM] / acc[:, HEAD_DIM:HEAD_DIM + 1]
    o_ref[0] = jnp.concatenate(
        [jnp.concatenate([out[(b * GRP + g) * tq:(b * GRP + g + 1) * tq] for g in range(GRP)], axis=1)
         for b in range(qblocks)], axis=0)


def _moba_prompt(q, kmean, kaug, vaug):
    bsz, seqlen, _ = q.shape
    nb = seqlen // MOBA_BLOCK
    assert GRP * HEAD_DIM == LANES and nb <= LANES - HEAD_DIM and MOBA_BLOCK & (MOBA_BLOCK - 1) == 0
    nbp = -(-nb // SUBLANES) * SUBLANES
    km = kmean.reshape(bsz, nb, ATTN_KV_HEADS, HEAD_DIM).transpose(0, 2, 1, 3)
    km = jnp.pad(km, ((0, 0), (0, 0), (0, nbp - nb), (0, 0)))
    kv_spec = pl.BlockSpec((1, 1, seqlen, LANES), lambda b, h, i: (b, h, 0, 0))
    qblocks = 2
    chunk = qblocks * MOBA_BLOCK
    assert seqlen % (2 * chunk) == 0, "the sweep walks the keys two chunks at a time"
    rows = qblocks * GRP * MOBA_BLOCK
    vmem = 2 * 2 * seqlen * LANES * 2 + 2 * rows * chunk * 4 + (24 << 20)
    return pl.pallas_call(
        functools.partial(_moba_prompt_kernel, qblocks=qblocks),
        scratch_shapes=[pltpu.VMEM((rows, chunk), F32), pltpu.VMEM((rows, chunk), F32)],
        grid=(bsz, ATTN_KV_HEADS, nb // qblocks),
        in_specs=[pl.BlockSpec((1, chunk, LANES), lambda b, h, i: (b, i, h)),
                  pl.BlockSpec((1, 1, nbp, HEAD_DIM), lambda b, h, i: (b, h, 0, 0)),
                  kv_spec, kv_spec],
        out_specs=pl.BlockSpec((1, chunk, LANES), lambda b, h, i: (b, i, h)),
        out_shape=jax.ShapeDtypeStruct((bsz, seqlen, ATTN_Q), F32),
        compiler_params=_cparams(("parallel", "parallel", "arbitrary"), vmem),
        name="moba_prompt",
    )(q, km, kaug, vaug)


def _moba_score_kernel(pt_ref, qbt_ref, *refs, pages_per_step):
    del pt_ref
    k_refs, sc_ref = refs[:pages_per_step], refs[pages_per_step]
    n = pl.program_id(1)
    pages_per_block = MOBA_BLOCK // LANES
    blocks_per_step = pages_per_step // pages_per_block
    qbt = qbt_ref[0]
    for jb in range(blocks_per_step):
        ksum = k_refs[jb * pages_per_block][0]
        for j in range(1, pages_per_block):
            ksum = ksum + k_refs[jb * pages_per_block + j][0]
        kmean = jnp.sum(ksum, axis=1, keepdims=True) * (1.0 / MOBA_BLOCK)
        sc_ref[0, pl.ds(n * blocks_per_step + jb, 1), :] = jnp.sum(qbt * kmean, axis=0, keepdims=True)


def _moba_sample_kernel(pt_ref, sc_ref, qb_ref, *refs, pages_per_step, nblk, dec_seq):
    del pt_ref
    k_refs = refs[:pages_per_step]
    v_refs = refs[pages_per_step:2 * pages_per_step]
    kn_ref, vn_ref, o_ref, pen_sc, m_sc, l_sc, acc_sc = refs[2 * pages_per_step:]
    n = pl.program_id(1)
    rows = qb_ref.shape[1]
    pages_per_block = MOBA_BLOCK // LANES
    blocks_per_step = pages_per_step // pages_per_block
    lane = lax.broadcasted_iota(jnp.int32, (rows, LANES), 1)

    @pl.when(n == 0)
    def _():
        sc_t = sc_ref[0]
        sel_t = _select_topk(sc_t, None, 0)
        pen_t = jnp.concatenate([jnp.where(sel_t, 0.0, NEG), jnp.full((LANES - nblk, LANES), NEG, F32)], axis=0)
        pen_sc[...] = pen_t.T[:rows]
        m_sc[...] = jnp.full(m_sc.shape, -jnp.inf, F32)
        l_sc[...] = jnp.zeros(l_sc.shape, F32)
        acc_sc[...] = jnp.zeros(acc_sc.shape, F32)

    qb = (qb_ref[0] * (HEAD_DIM ** -0.5)).astype(BF16)

    def update(s, pv_fn):
        m = m_sc[...]
        m_new = jnp.maximum(m, jnp.max(s, axis=1, keepdims=True))
        p = jnp.exp(s - m_new)
        alpha = jnp.exp(m - m_new)
        l_sc[...] = alpha * l_sc[...] + jnp.sum(p, axis=1, keepdims=True)
        acc_sc[...] = alpha * acc_sc[...] + pv_fn(p.astype(BF16))
        m_sc[...] = m_new

    pen = pen_sc[...]
    parts = []
    for jb in range(blocks_per_step):
        pen_col = jnp.sum(jnp.where(lane == n * blocks_per_step + jb, pen, 0.0), axis=1, keepdims=True)
        for j in range(pages_per_block):
            parts.append(_dot(qb, k_refs[jb * pages_per_block + j][0].astype(BF16)) + pen_col)
    s = jnp.concatenate(parts, axis=1)

    def pv_past(p):
        pv = _dot_nt(p[:, :LANES], v_refs[0][0].astype(BF16))
        for j in range(1, pages_per_step):
            pv = pv + _dot_nt(p[:, j * LANES:(j + 1) * LANES], v_refs[j][0].astype(BF16))
        return pv

    update(s, pv_past)

    @pl.when(n == pl.num_programs(1) - 1)
    def _():
        zpad = jnp.zeros((LANES - dec_seq, kn_ref.shape[2]), F32)
        kn = jnp.concatenate([kn_ref[0], zpad], axis=0).astype(BF16)
        vn = jnp.concatenate([vn_ref[0], zpad], axis=0).astype(BF16)
        s_own = _dot_nt(qb, kn)
        r = lax.broadcasted_iota(jnp.int32, (rows, LANES), 0)
        qpos = r & (dec_seq - 1)
        s_own = jnp.where((lane <= qpos) & (lane < dec_seq), s_own, NEG)
        update(s_own, lambda p: _dot(p, vn))
        o_ref[0] = acc_sc[...] / l_sc[...]


def _moba_sample(q, k_new, v_new, cache_kt, cache_vt, page_table, past_len):
    bsz, dec_seq, _ = q.shape
    page = cache_kt.shape[2]
    assert page == LANES and MOBA_BLOCK % page == 0
    assert past_len % MOBA_BLOCK == 0 and dec_seq == SUBLANES
    nblk = past_len // MOBA_BLOCK
    assert MOBA_TOPK <= nblk <= LANES
    n_pages = past_len // page
    pages_per_step = 16
    while n_pages % pages_per_step:
        pages_per_step //= 2
    assert pages_per_step * page >= MOBA_BLOCK
    nsteps = n_pages // pages_per_step
    rows = ATTN_HEADS * dec_seq
    q4 = q.reshape(bsz, dec_seq, ATTN_HEADS, HEAD_DIM).transpose(0, 2, 1, 3)
    own = (np.arange(ATTN_HEADS)[:, None] // GRP == np.arange(ATTN_KV_HEADS)[None, :]).astype(np.float32)
    qb = (q4[:, :, :, None, :] * jnp.asarray(own)[None, :, None, :, None]).reshape(bsz, rows, ATTN_KV)
    qbt = jnp.pad(qb.transpose(0, 2, 1), ((0, 0), (0, 0), (0, LANES - rows)))

    def page_spec(j):
        return pl.BlockSpec((1, ATTN_KV, page), lambda b, n, pt: (pt[b, pages_per_step * n + j], 0, 0))

    def per_b(d1, d2):
        return pl.BlockSpec((1, d1, d2), lambda b, n, pt: (b, 0, 0))

    page_specs = [page_spec(j) for j in range(pages_per_step)]
    scores_t = pl.pallas_call(
        functools.partial(_moba_score_kernel, pages_per_step=pages_per_step),
        grid_spec=pltpu.PrefetchScalarGridSpec(
            num_scalar_prefetch=1, grid=(bsz, nsteps),
            in_specs=[per_b(ATTN_KV, LANES)] + page_specs,
            out_specs=per_b(nblk, LANES)),
        out_shape=jax.ShapeDtypeStruct((bsz, nblk, LANES), F32),
        compiler_params=_cparams(("parallel", "arbitrary"), 16 << 20),
        name="moba_sample_scores",
    )(page_table, qbt, *([cache_kt] * pages_per_step))

    out = pl.pallas_call(
        functools.partial(_moba_sample_kernel, pages_per_step=pages_per_step, nblk=nblk, dec_seq=dec_seq),
        grid_spec=pltpu.PrefetchScalarGridSpec(
            num_scalar_prefetch=1, grid=(bsz, nsteps),
            in_specs=[per_b(nblk, LANES), per_b(rows, ATTN_KV)] + page_specs + page_specs
                     + [per_b(dec_seq, ATTN_KV), per_b(dec_seq, ATTN_KV)],
            out_specs=per_b(rows, ATTN_KV),
            scratch_shapes=[pltpu.VMEM((rows, LANES), F32), pltpu.VMEM((rows, 1), F32), pltpu.VMEM((rows, 1), F32),
                            pltpu.VMEM((rows, ATTN_KV), F32)]),
        out_shape=jax.ShapeDtypeStruct((bsz, rows, ATTN_KV), F32),
        compiler_params=_cparams(("parallel", "arbitrary"), 24 << 20),
        name="moba_sample_attn",
    )(page_table, scores_t, qb, *([cache_kt] * pages_per_step), *([cache_vt] * pages_per_step), k_new, v_new)
    o5 = out.reshape(bsz, ATTN_HEADS, dec_seq, ATTN_KV_HEADS, HEAD_DIM)
    o4 = jnp.sum(o5 * jnp.asarray(own)[None, :, None, :, None], axis=3)
    return o4.transpose(0, 2, 1, 3).reshape(bsz, dec_seq, ATTN_Q)


def _mix_kernel(x_ref, ys_ref, ya_ref, gs_ref, ga_ref, gt1_ref, sc2_ref, sh2_ref, nw_ref,
                wso_ref, wao_ref, wo_ref, wrt_ref, brt_ref, x1_ref, hc_ref):
    merged = (jax.nn.sigmoid(gs_ref[0]) * _dot(ys_ref[0].astype(BF16), wso_ref[...])
              + jax.nn.sigmoid(ga_ref[0]) * _dot(ya_ref[0].astype(BF16), wao_ref[...]))
    x1 = x_ref[0] + gt1_ref[0] * _dot(merged.astype(BF16), wo_ref[...])
    x1_ref[0] = x1
    h2 = x1 * lax.rsqrt(jnp.mean(x1 * x1, axis=-1, keepdims=True) + EPS) * nw_ref[...]
    h2 = h2 * (1.0 + sc2_ref[0]) + sh2_ref[0]
    hc_ref[0, :, :D_MODEL] = h2

    h_hi = h2.astype(BF16)
    h_lo = (h2 - h_hi.astype(F32)).astype(BF16)
    hw = _dot(h_hi, wrt_ref[...])
    lg = hw[:, :LANES] + hw[:, LANES:] + _dot(h_lo, wrt_ref[:, :LANES]) + brt_ref[...]
    lane = lax.broadcasted_iota(jnp.int32, lg.shape, 1).astype(F32)
    big = float(1 << 20)
    ninf = -jnp.inf
    gl = jnp.where(lane < N_EGROUPS, lg, ninf)
    gmax = jnp.max(gl, axis=1, keepdims=True)
    gidx = jnp.min(jnp.where(gl == gmax, lane, big), axis=1, keepdims=True)
    g_w = 1.0 / jnp.sum(jnp.exp(gl - gmax), axis=1, keepdims=True)
    lo = ROUTE_OFF + gidx * EXPERTS_PER_GROUP
    el = jnp.where((lane >= lo) & (lane < lo + EXPERTS_PER_GROUP), lg, ninf)
    m1 = jnp.max(el, axis=1, keepdims=True)
    i1 = jnp.min(jnp.where(el == m1, lane, big), axis=1, keepdims=True)
    el2 = jnp.where(lane == i1, ninf, el)
    m2 = jnp.max(el2, axis=1, keepdims=True)
    i2 = jnp.min(jnp.where(el2 == m2, lane, big), axis=1, keepdims=True)
    e2 = jnp.exp(m2 - m1)
    den = 1.0 / (1.0 + e2)
    comb = jnp.where(lane == i1, den * g_w, 0.0) + jnp.where(lane == i2, e2 * den * g_w, 0.0)
    hc_ref[0, :, D_MODEL:] = jnp.where(lane == 0.0, gidx, comb)


def _mix(x, y_ssd, y_att, gs, ga, gt1, sc2, sh2, norm_w, wso, wao, wo, wrt, brt):
    bsz, seqlen, _ = x.shape
    tm = 256
    assert seqlen % tm == 0
    nt = seqlen // tm
    mod_rows = gt1.shape[1]
    if mod_rows == 1:
        mod_spec = pl.BlockSpec((1, 1, D_MODEL), lambda b, i: (b, 0, 0))
    else:
        mod_spec = pl.BlockSpec((1, tm, D_MODEL), lambda b, i: (b, i, 0))

    def tok(width):
        return pl.BlockSpec((1, tm, width), lambda b, i: (b, i, 0))

    def full(d1, d2):
        return pl.BlockSpec((d1, d2), lambda b, i: (0, 0))

    vmem = 2 * 2 * (2 * D_MODEL * D_MODEL + ATTN_Q * D_MODEL) + 2 * tm * (8 * D_MODEL) * 4 + (12 << 20)
    return pl.pallas_call(
        _mix_kernel,
        grid=(bsz, nt),
        in_specs=[tok(D_MODEL), tok(SSD_INNER), tok(ATTN_Q), tok(D_MODEL), tok(D_MODEL),
                  mod_spec, mod_spec, mod_spec, full(1, D_MODEL),
                  full(SSD_INNER, D_MODEL), full(ATTN_Q, D_MODEL), full(D_MODEL, D_MODEL),
                  full(D_MODEL, 2 * LANES), full(1, LANES)],
        out_specs=[tok(D_MODEL), tok(D_MODEL + LANES)],
        out_shape=[jax.ShapeDtypeStruct((bsz, seqlen, D_MODEL), F32),
                   jax.ShapeDtypeStruct((bsz, seqlen, D_MODEL + LANES), F32)],
        compiler_params=_cparams(("parallel", "parallel"), vmem),
        name="mix",
    )(x, y_ssd, y_att, gs, ga, gt1, sc2, sh2, norm_w.reshape(1, -1), wso, wao, wo, wrt, brt)


def _expert_step(h2, comb, expert, wg_ref, wu_ref, wd_ref):
    lane = lax.broadcasted_iota(jnp.int32, comb.shape, 1)
    cw = jnp.sum(jnp.where(lane == expert + ROUTE_OFF, comb, 0.0), axis=1, keepdims=True)
    a = _dot(h2, wg_ref[0])
    u = _dot(h2, wu_ref[0])
    act = (_silu(a) * u * cw).astype(BF16)
    return _dot(act, wd_ref[0])


def _final_norm(x1, gt2, ffn, fw):
    x2 = x1 + gt2 * ffn
    return x2 * lax.rsqrt(jnp.mean(x2 * x2, axis=-1, keepdims=True) + EPS) * fw


def _moe_kernel(hc_ref, x1_ref, gt2_ref, fw_ref, wg_ref, wu_ref, wd_ref, y_ref, acc):
    e = pl.program_id(2)

    @pl.when(e == 0)
    def _():
        acc[...] = jnp.zeros(acc.shape, F32)

    acc[...] += _expert_step(hc_ref[0, :, :D_MODEL].astype(BF16), hc_ref[0, :, D_MODEL:], e,
                             wg_ref, wu_ref, wd_ref)

    @pl.when(e == pl.num_programs(2) - 1)
    def _():
        y_ref[0] = _final_norm(x1_ref[0], gt2_ref[0], acc[...], fw_ref[...])


def _moe(hc, x1, gt2, final_w, wg, wu, wd):
    bsz, seqlen, _ = x1.shape
    tm = 1024 if seqlen % 1024 == 0 else 256
    assert seqlen % tm == 0
    nt = seqlen // tm
    mod_rows = gt2.shape[1]
    if mod_rows == 1:
        mod_spec = pl.BlockSpec((1, 1, D_MODEL), lambda b, i, e: (b, 0, 0))
    else:
        mod_spec = pl.BlockSpec((1, tm, D_MODEL), lambda b, i, e: (b, i, 0))

    def tok(width):
        return pl.BlockSpec((1, tm, width), lambda b, i, e: (b, i, 0))

    vmem = 2 * tm * D_MODEL * (2 + 4 + 4) + tm * D_MODEL * 4 + 2 * 3 * D_MODEL * D_EXPERT * 2 + (16 << 20)
    return pl.pallas_call(
        _moe_kernel,
        grid=(bsz, nt, N_EXPERTS),
        in_specs=[tok(D_MODEL + LANES), tok(D_MODEL), mod_spec,
                  pl.BlockSpec((1, D_MODEL), lambda b, i, e: (0, 0)),
                  pl.BlockSpec((1, D_MODEL, D_EXPERT), lambda b, i, e: (e, 0, 0)),
                  pl.BlockSpec((1, D_MODEL, D_EXPERT), lambda b, i, e: (e, 0, 0)),
                  pl.BlockSpec((1, D_EXPERT, D_MODEL), lambda b, i, e: (e, 0, 0))],
        out_specs=tok(D_MODEL),
        out_shape=jax.ShapeDtypeStruct((bsz, seqlen, D_MODEL), F32),
        scratch_shapes=[pltpu.VMEM((tm, D_MODEL), F32)],
        compiler_params=_cparams(("parallel", "parallel", "arbitrary"), vmem),
        name="moe",
    )(hc, x1, gt2, final_w.reshape(1, -1), wg, wu, wd)


ROUTE_TILE = 512
DISPATCH_TILE = 256


def _move_rows(src_of, dst_of, sem, whole_src, whole_dst):
    def body(r, carry):
        pltpu.make_async_copy(src_of(r), dst_of(r), sem).start()
        return carry

    lax.fori_loop(0, DISPATCH_TILE, body, 0, unroll=8)
    pltpu.make_async_copy(whole_src, whole_dst, sem).wait()


def _dispatch_kernel(pos_ref, hc_ref, xs_in, xs_ref, sem):
    del xs_in
    base = pl.program_id(0) * DISPATCH_TILE
    _move_rows(lambda r: hc_ref.at[pl.ds(r, 1), :], lambda r: xs_ref.at[pl.ds(pos_ref[base + r], 1), :],
               sem.at[0], hc_ref, xs_ref.at[pl.ds(0, DISPATCH_TILE), :])


def _moe_routed_kernel(tg_ref, xs_ref, wg_ref, wu_ref, wd_ref, ys_ref, xb, acc):
    t, e = pl.program_id(0), pl.program_id(1)

    @pl.when(e == 0)
    def _():
        xb[...] = xs_ref[:, :D_MODEL].astype(BF16)
        acc[...] = jnp.zeros(acc.shape, F32)

    expert = tg_ref[t] * EXPERTS_PER_GROUP + e
    acc[...] += _expert_step(xb[...], xs_ref[:, D_MODEL:], expert, wg_ref, wu_ref, wd_ref)

    @pl.when(e == pl.num_programs(1) - 1)
    def _():
        ys_ref[...] = acc[...]


def _combine_kernel(pos_ref, x1_ref, gt2_ref, fw_ref, ys_ref, y_ref, ybuf, sem):
    i = pl.program_id(0)
    slot = i % 2

    def request(step, buf):
        base = step * DISPATCH_TILE

        def body(r, carry):
            pltpu.make_async_copy(ys_ref.at[pl.ds(pos_ref[base + r], 1), :], ybuf.at[buf, pl.ds(r, 1), :],
                                  sem.at[buf]).start()
            return carry

        lax.fori_loop(0, DISPATCH_TILE, body, 0, unroll=8)

    @pl.when(i == 0)
    def _():
        request(0, 0)

    @pl.when(i + 1 < pl.num_programs(0))
    def _():
        request(i + 1, 1 - slot)

    pltpu.make_async_copy(ys_ref.at[pl.ds(0, DISPATCH_TILE), :], ybuf.at[slot], sem.at[slot]).wait()
    y_ref[...] = _final_norm(x1_ref[...], gt2_ref[0], ybuf[slot], fw_ref[...])


def _moe_routed(hc, x1, gt2, final_w, wg, wu, wd):
    bsz, seqlen, width = hc.shape
    ntok = bsz * seqlen
    assert seqlen % DISPATCH_TILE == 0 and ntok % ROUTE_TILE == 0 and gt2.shape[1] == 1
    hcf, x1f = hc.reshape(ntok, width), x1.reshape(ntok, D_MODEL)
    gidx = hcf[:, D_MODEL].astype(jnp.int32)
    onehot = (gidx[:, None] == jnp.arange(N_EGROUPS, dtype=jnp.int32)[None, :]).astype(jnp.int32)
    csum = jnp.cumsum(onehot, axis=0)
    padded = (csum[-1] + ROUTE_TILE - 1) // ROUTE_TILE * ROUTE_TILE
    ends = jnp.cumsum(padded)
    pos = jnp.sum(onehot * (csum - 1 + (ends - padded)[None, :]), axis=1).astype(jnp.int32)
    n_tiles = ntok // ROUTE_TILE + N_EGROUPS
    tile_start = jnp.arange(n_tiles, dtype=jnp.int32) * ROUTE_TILE
    tile_group = jnp.minimum(jnp.sum((tile_start[:, None] >= ends[None, :]).astype(jnp.int32), axis=1),
                             N_EGROUPS - 1).astype(jnp.int32)
    nrows = n_tiles * ROUTE_TILE
    nsteps = ntok // DISPATCH_TILE
    any_spec = pl.BlockSpec(memory_space=pl.ANY)

    xs = pl.pallas_call(
        _dispatch_kernel,
        grid_spec=pltpu.PrefetchScalarGridSpec(
            num_scalar_prefetch=1, grid=(nsteps,),
            in_specs=[pl.BlockSpec((DISPATCH_TILE, width), lambda i, pos: (i, 0)), any_spec],
            out_specs=any_spec,
            scratch_shapes=[pltpu.SemaphoreType.DMA((1,))]),
        out_shape=jax.ShapeDtypeStruct((nrows, width), F32),
        input_output_aliases={2: 0},
        compiler_params=_cparams(("arbitrary",), 16 << 20),
        name="moe_dispatch",
    )(pos, hcf, jnp.zeros((nrows, width), F32))

    def w_spec(d1, d2):
        return pl.BlockSpec((1, d1, d2), lambda t, e, tg: (tg[t] * EXPERTS_PER_GROUP + e, 0, 0))

    vmem = (2 * ROUTE_TILE * (width + D_MODEL) * 4 + ROUTE_TILE * D_MODEL * (2 + 4)
            + 2 * 3 * D_MODEL * D_EXPERT * 2 + (16 << 20))
    ys = pl.pallas_call(
        _moe_routed_kernel,
        grid_spec=pltpu.PrefetchScalarGridSpec(
            num_scalar_prefetch=1, grid=(n_tiles, EXPERTS_PER_GROUP),
            in_specs=[pl.BlockSpec((ROUTE_TILE, width), lambda t, e, tg: (t, 0)),
                      w_spec(D_MODEL, D_EXPERT), w_spec(D_MODEL, D_EXPERT), w_spec(D_EXPERT, D_MODEL)],
            out_specs=pl.BlockSpec((ROUTE_TILE, D_MODEL), lambda t, e, tg: (t, 0)),
            scratch_shapes=[pltpu.VMEM((ROUTE_TILE, D_MODEL), BF16), pltpu.VMEM((ROUTE_TILE, D_MODEL), F32)]),
        out_shape=jax.ShapeDtypeStruct((nrows, D_MODEL), F32),
        compiler_params=_cparams(("parallel", "arbitrary"), vmem),
        name="moe_routed",
    )(tile_group, xs, wg, wu, wd)

    steps_per_batch = seqlen // DISPATCH_TILE
    y = pl.pallas_call(
        _combine_kernel,
        grid_spec=pltpu.PrefetchScalarGridSpec(
            num_scalar_prefetch=1, grid=(nsteps,),
            in_specs=[pl.BlockSpec((DISPATCH_TILE, D_MODEL), lambda i, pos: (i, 0)),
                      pl.BlockSpec((1, 1, D_MODEL), lambda i, pos: (i // steps_per_batch, 0, 0)),
                      pl.BlockSpec((1, D_MODEL), lambda i, pos: (0, 0)), any_spec],
            out_specs=pl.BlockSpec((DISPATCH_TILE, D_MODEL), lambda i, pos: (i, 0)),
            scratch_shapes=[pltpu.VMEM((2, DISPATCH_TILE, D_MODEL), F32), pltpu.SemaphoreType.DMA((2,))]),
        out_shape=jax.ShapeDtypeStruct((ntok, D_MODEL), F32),
        compiler_params=_cparams(("arbitrary",), 16 << 20),
        name="moe_combine",
    )(pos, x1f, gt2, final_w.reshape(1, -1), ys)
    return y.reshape(bsz, seqlen, D_MODEL)


def _prep_weights(lp):
    w_in = lp["w_in"]
    sizes = (SSD_INNER, CONV_DIM, SSD_HEADS, ATTN_Q, ATTN_KV, ATTN_KV, D_MODEL, D_MODEL)
    pts = [int(p) for p in np.cumsum(sizes)[:-1]]
    z, xbc, dt, q, k, v, gs, ga = jnp.split(w_in, pts, axis=1)
    dt = jnp.pad(dt, ((0, 0), (0, LANES - SSD_HEADS)))
    w_in_r = jnp.concatenate([z, xbc, q, k, v, gs, ga, dt], axis=1).astype(BF16)
    wrt = jnp.concatenate([lp["w_group"], lp["w_router"]], axis=1)
    wrt = jnp.pad(wrt, ((0, 0), (0, LANES - wrt.shape[1]))).astype(F32)
    wrt_hi = wrt.astype(BF16)
    wrt = jnp.concatenate([wrt_hi, (wrt - wrt_hi.astype(F32)).astype(BF16)], axis=1)
    brt = jnp.concatenate([lp["b_group"], lp["b_router"]])
    brt = jnp.pad(brt, (0, LANES - brt.shape[0])).astype(F32).reshape(1, LANES)
    return dict(w_in_r=w_in_r, wrt=wrt, brt=brt,
                wso=lp["w_ssd_out"].astype(BF16), wao=lp["w_attn_out"].astype(BF16), wo=lp["w_o"].astype(BF16),
                wg=lp["w_gate_e"].astype(BF16), wu=lp["w_up_e"].astype(BF16), wd=lp["w_down_e"].astype(BF16))


def _mods(mod, per_row_repeat):
    parts = jnp.split(mod, 6, axis=-1)
    if per_row_repeat:
        return [jnp.repeat(p, per_row_repeat, axis=0)[None] for p in parts]
    return [p[:, None, :] for p in parts]


def _layer_prompt(x, mod, lp, pw, final_w):
    bsz, seqlen, _ = x.shape
    sh1, sc1, gt1, sh2, sc2, gt2 = _mods(mod, 0)
    tables = _rope_tables(np.arange(seqlen))
    (z, xbc, dt, q, k, v, gs, ga, kmean, kaug, vaug) = _in_proj(x, sc1, sh1, lp["norm1_w"], pw["w_in_r"], tables, True)
    conv0 = jnp.zeros((bsz, SSD_CONV - 1, CONV_DIM), F32)
    ssm0 = jnp.zeros((bsz, SSD_HEADS, SSD_HEADDIM, SSD_STATE), F32)
    y_ssd, conv_new, ssm_new = _ssd(xbc, z, dt, conv0, ssm0, lp["conv_w"], lp["conv_b"], lp["dt_bias"],
                                    lp["a_log"], lp["d_skip"], lp["ssd_norm_w"])
    y_att = _moba_prompt(q, kmean, kaug, vaug)
    x1, hc = _mix(x, y_ssd, y_att, gs, ga, gt1, sc2, sh2, lp["norm2_w"],
                  pw["wso"], pw["wao"], pw["wo"], pw["wrt"], pw["brt"])
    y = _moe_routed(hc, x1, gt2, final_w, pw["wg"], pw["wu"], pw["wd"])
    kv_shape = (bsz, seqlen, ATTN_KV_HEADS, HEAD_DIM)
    return y, k.reshape(kv_shape), v.reshape(kv_shape), conv_new, ssm_new


def _layer_sample(x, mod, conv_buf, ssm_state, cache_k, cache_v, page_table, past_len, lp, pw, final_w):
    bsz, dec_seq, _ = x.shape
    ntok = bsz * dec_seq
    sh1, sc1, gt1, sh2, sc2, gt2 = _mods(mod, dec_seq)
    pos = past_len + (np.arange(ntok) % dec_seq)
    tables = _rope_tables(pos)
    xf = x.reshape(1, ntok, D_MODEL)
    z, xbc, dt, q, k, v, gs, ga = _in_proj(xf, sc1, sh1, lp["norm1_w"], pw["w_in_r"], tables, False)

    def per_b(t):
        return t.reshape(bsz, dec_seq, t.shape[-1])

    y_ssd, conv_new, ssm_new = _ssd(per_b(xbc), per_b(z), per_b(dt), conv_buf, ssm_state, lp["conv_w"],
                                    lp["conv_b"], lp["dt_bias"], lp["a_log"], lp["d_skip"], lp["ssd_norm_w"])
    n_pool, page = cache_k.shape[0], cache_k.shape[1]

    def position_minor(cache):
        return cache.transpose(0, 2, 3, 1).reshape(n_pool, ATTN_KV, page)

    y_att = _moba_sample(per_b(q), per_b(k), per_b(v), position_minor(cache_k), position_minor(cache_v),
                         page_table, past_len)
    x1, hc = _mix(xf, y_ssd.reshape(1, ntok, -1), y_att.reshape(1, ntok, -1), gs, ga, gt1, sc2, sh2,
                  lp["norm2_w"], pw["wso"], pw["wao"], pw["wo"], pw["wrt"], pw["brt"])
    y = _moe(hc, x1, gt2, final_w, pw["wg"], pw["wu"], pw["wd"])
    kv_shape = (bsz, dec_seq, ATTN_KV_HEADS, HEAD_DIM)
    return y.reshape(bsz, dec_seq, D_MODEL), k.reshape(kv_shape), v.reshape(kv_shape), conv_new, ssm_new


def kernel(x_prompt, x_sample, cache_k, cache_v, state_conv, state_ssm, page_table, c_prompt, c_sample,
           w_ada, b_ada, norm1_w, w_in, conv_w, conv_b, dt_bias, a_log, d_skip, ssd_norm_w,
           w_ssd_out, w_attn_out, w_o, norm2_w, w_group, b_group, w_router, b_router,
           w_gate_e, w_up_e, w_down_e, final_w):
    depth = w_in.shape[0]
    assert depth == 1, "the final RMSNorm is fused into the last layer's MoE kernel"
    nb_p, nb_s = c_prompt.shape[0], c_sample.shape[0]
    past_len = page_table.shape[1] * cache_k.shape[2]
    l = 0
    lp = dict(norm1_w=norm1_w[l], w_in=w_in[l], conv_w=conv_w[l], conv_b=conv_b[l], dt_bias=dt_bias[l],
              a_log=a_log[l], d_skip=d_skip[l], ssd_norm_w=ssd_norm_w[l], w_ssd_out=w_ssd_out[l],
              w_attn_out=w_attn_out[l], w_o=w_o[l], norm2_w=norm2_w[l], w_group=w_group[l], b_group=b_group[l],
              w_router=w_router[l], b_router=b_router[l], w_gate_e=w_gate_e[l], w_up_e=w_up_e[l],
              w_down_e=w_down_e[l])
    pw = _prep_weights(lp)
    c_all = jnp.concatenate([c_prompt, c_sample], axis=0)
    pad_rows = (-c_all.shape[0]) % SUBLANES
    mod = _ada_mod(jnp.pad(c_all, ((0, pad_rows), (0, 0))), w_ada[l], b_ada[l])
    yp, kp, vp, cp, sp = _layer_prompt(x_prompt, mod[:nb_p], lp, pw, final_w)
    ys, ks, vs, cs, ss = _layer_sample(x_sample, mod[nb_p:nb_p + nb_s], state_conv[l], state_ssm[l],
                                       cache_k[l], cache_v[l], page_table, past_len, lp, pw, final_w)
    return (yp, ys, kp[None], vp[None], cp[None], sp[None], ks[None], vs[None], cs[None], ss[None])
```

```python
import functools
import math

import numpy as np
import jax
import jax.numpy as jnp
from jax import lax
from jax.experimental import pallas as pl
from jax.experimental.pallas import tpu as pltpu

F32 = jnp.float32
BF16 = jnp.bfloat16
HIGHEST = lax.Precision.HIGHEST

D_MODEL = 1024
SSD_HEADDIM = 64
SSD_INNER = D_MODEL
SSD_HEADS = SSD_INNER // SSD_HEADDIM
SSD_GROUPS = 2
SSD_STATE = 128
SSD_CONV = 4
SSD_CHUNK = 128
SSD_BC = SSD_GROUPS * SSD_STATE
CONV_DIM = SSD_INNER + 2 * SSD_BC
ATTN_HEADS = 8
ATTN_KV_HEADS = 4
HEAD_DIM = 64
ATTN_Q = ATTN_HEADS * HEAD_DIM
ATTN_KV = ATTN_KV_HEADS * HEAD_DIM
ROT_DIM = HEAD_DIM // 4
ROPE_THETA = 500000.0
MOBA_BLOCK = 256
MOBA_TOPK = 3
N_EGROUPS = 4
EXPERTS_PER_GROUP = 4
N_EXPERTS = N_EGROUPS * EXPERTS_PER_GROUP
D_EXPERT = D_MODEL // 2
EPS = 1e-6

LANES = 128
SUBLANES = 8
VMEM_CAP = 56 << 20
NEG = -1e30
GRP = ATTN_HEADS // ATTN_KV_HEADS
ROUTE_OFF = N_EGROUPS

_SEG = {}
_off = 0
for _name, _w in (("z", SSD_INNER), ("xbc", CONV_DIM), ("q", ATTN_Q), ("k", ATTN_KV), ("v", ATTN_KV),
                  ("gs", D_MODEL), ("ga", D_MODEL), ("dt", LANES)):
    _SEG[_name] = (_off, _off + _w)
    _off += _w
IN_PAD = _off


def _cparams(semantics, vmem_bytes):
    return pltpu.CompilerParams(dimension_semantics=semantics,
                                vmem_limit_bytes=int(min(max(vmem_bytes, 16 << 20), VMEM_CAP)))


def _dot(a, b):
    return jnp.dot(a, b, preferred_element_type=F32)


def _dot_nt(a, b):
    return lax.dot_general(a, b, (((1,), (1,)), ((), ())), preferred_element_type=F32)


def _dot_exact(a, b):
    return jnp.dot(a, b, preferred_element_type=F32, precision=HIGHEST)


def _silu(x):
    return x * jax.nn.sigmoid(x)


def _mod_kernel(c_ref, w_ref, b_ref, o_ref):
    o_ref[...] = _dot_exact(_silu(c_ref[...]), w_ref[...]) + b_ref[...]


def _ada_mod(c_all, w_ada, b_ada):
    rows = c_all.shape[0]
    return pl.pallas_call(
        _mod_kernel,
        grid=(6,),
        in_specs=[pl.BlockSpec((rows, D_MODEL), lambda j: (0, 0)),
                  pl.BlockSpec((D_MODEL, D_MODEL), lambda j: (0, j)),
                  pl.BlockSpec((1, D_MODEL), lambda j: (0, j))],
        out_specs=pl.BlockSpec((rows, D_MODEL), lambda j: (0, j)),
        out_shape=jax.ShapeDtypeStruct((rows, 6 * D_MODEL), F32),
        compiler_params=_cparams(("arbitrary",), 24 << 20),
        name="ada_mod",
    )(c_all, w_ada, b_ada.reshape(1, -1))


def _rope_tables(pos):
    half = ROT_DIM // 2
    c = -2.0 * math.log(ROPE_THETA) / ROT_DIM
    c_hi = float(np.float32(c))
    j = jnp.arange(half, dtype=F32)
    inv_freq = jnp.exp(j * c_hi + j * (c - c_hi))
    ang = jnp.asarray(pos).astype(F32)[:, None] * inv_freq[None, :]
    cos, sin = jnp.cos(ang), jnp.sin(ang)
    n = pos.shape[0]
    one = jnp.ones((n, HEAD_DIM - ROT_DIM), F32)
    zero = jnp.zeros((n, half), F32)
    rest = jnp.zeros((n, HEAD_DIM - ROT_DIM), F32)
    a = jnp.concatenate([cos, cos, one], axis=1)
    s1 = jnp.concatenate([-sin, zero, rest], axis=1)
    s2 = jnp.concatenate([zero, sin, rest], axis=1)
    rep = LANES // HEAD_DIM
    return tuple(jnp.tile(t, (1, rep)) for t in (a, s1, s2))


def _inproj_kernel(*refs, attn_layout):
    (x_ref, sc_ref, sh_ref, nw_ref, w_ref, ra_ref, rs1_ref, rs2_ref,
     z_ref, xbc_ref, dt_ref, q_ref, k_ref, v_ref, gs_ref, ga_ref) = refs[:16]
    x = x_ref[0]
    h = x * lax.rsqrt(jnp.mean(x * x, axis=-1, keepdims=True) + EPS) * nw_ref[...]
    h = (h * (1.0 + sc_ref[0]) + sh_ref[0]).astype(BF16)

    def proj(name, out_ref):
        lo, hi = _SEG[name]
        step = 512
        for c in range(lo, hi, step):
            w = min(step, hi - c)
            out_ref[0, :, c - lo:c - lo + w] = _dot(h, w_ref[:, c:c + w])

    proj("z", z_ref)
    proj("xbc", xbc_ref)
    proj("gs", gs_ref)
    proj("ga", ga_ref)
    proj("dt", dt_ref)

    def rope(t):
        width = t.shape[1]
        reps = width // LANES
        a = jnp.concatenate([ra_ref[...]] * reps, axis=1)
        s1 = jnp.concatenate([rs1_ref[...]] * reps, axis=1)
        s2 = jnp.concatenate([rs2_ref[...]] * reps, axis=1)
        return t * a + pltpu.roll(t, width - ROT_DIM // 2, 1) * s1 + pltpu.roll(t, ROT_DIM // 2, 1) * s2

    lo, hi = _SEG["q"]
    q_ref[0] = rope(_dot(h, w_ref[:, lo:hi]))
    lo, hi = _SEG["k"]
    k = rope(_dot(h, w_ref[:, lo:hi]))
    k_ref[0] = k
    lo, hi = _SEG["v"]
    v = _dot(h, w_ref[:, lo:hi])
    v_ref[0] = v

    if attn_layout:
        kmean_ref, kaug_ref, vaug_ref = refs[16:19]
        tm = k.shape[0]
        kmean_ref[0, 0] = jnp.mean(k, axis=0, keepdims=True)
        lane = lax.broadcasted_iota(jnp.int32, (tm, HEAD_DIM), 1)
        blk_onehot = (lane == pl.program_id(1)).astype(F32)
        ones_col = (lane == 0).astype(F32)
        for kh in range(ATTN_KV_HEADS):
            sl = slice(kh * HEAD_DIM, (kh + 1) * HEAD_DIM)
            kaug_ref[0, kh] = jnp.concatenate([k[:, sl], blk_onehot], axis=1).astype(BF16)
            vaug_ref[0, kh] = jnp.concatenate([v[:, sl], ones_col], axis=1).astype(BF16)


def _in_proj(x, sc1, sh1, norm_w, w_in_r, tables, attn_layout):
    bsz, seqlen, _ = x.shape
    tm = MOBA_BLOCK
    assert seqlen % tm == 0
    nt = seqlen // tm
    mod_rows = sc1.shape[1]
    if mod_rows == 1:
        mod_spec = pl.BlockSpec((1, 1, D_MODEL), lambda b, i: (b, 0, 0))
    else:
        mod_spec = pl.BlockSpec((1, tm, D_MODEL), lambda b, i: (b, i, 0))
    tab_spec = pl.BlockSpec((tm, LANES), lambda b, i: (i, 0))

    def tok(width):
        return pl.BlockSpec((1, tm, width), lambda b, i: (b, i, 0))

    def shp(width):
        return jax.ShapeDtypeStruct((bsz, seqlen, width), F32)

    widths = (SSD_INNER, CONV_DIM, LANES, ATTN_Q, ATTN_KV, ATTN_KV, D_MODEL, D_MODEL)
    out_specs = [tok(w) for w in widths]
    out_shape = [shp(w) for w in widths]
    if attn_layout:
        assert nt <= HEAD_DIM
        out_specs += [pl.BlockSpec((1, 1, 1, ATTN_KV), lambda b, i: (b, i, 0, 0)),
                      pl.BlockSpec((1, ATTN_KV_HEADS, tm, LANES), lambda b, i: (b, 0, i, 0)),
                      pl.BlockSpec((1, ATTN_KV_HEADS, tm, LANES), lambda b, i: (b, 0, i, 0))]
        out_shape += [jax.ShapeDtypeStruct((bsz, nt, 1, ATTN_KV), F32),
                      jax.ShapeDtypeStruct((bsz, ATTN_KV_HEADS, seqlen, LANES), BF16),
                      jax.ShapeDtypeStruct((bsz, ATTN_KV_HEADS, seqlen, LANES), BF16)]
    vmem = 2 * (D_MODEL * IN_PAD * 2) + 2 * tm * (D_MODEL + IN_PAD + 4 * LANES) * 4 + (8 << 20)
    return pl.pallas_call(
        functools.partial(_inproj_kernel, attn_layout=attn_layout),
        grid=(bsz, nt),
        in_specs=[tok(D_MODEL), mod_spec, mod_spec,
                  pl.BlockSpec((1, D_MODEL), lambda b, i: (0, 0)),
                  pl.BlockSpec((D_MODEL, IN_PAD), lambda b, i: (0, 0)),
                  tab_spec, tab_spec, tab_spec],
        out_specs=out_specs,
        out_shape=out_shape,
        compiler_params=_cparams(("parallel", "parallel"), vmem),
        name="in_proj",
    )(x, sc1, sh1, norm_w.reshape(1, -1), w_in_r, *tables)


def _cumsum_rows(a):
    row = lax.broadcasted_iota(jnp.int32, a.shape, 0)
    s = 1
    while s < a.shape[0]:
        a = a + jnp.where(row >= s, pltpu.roll(a, s, 0), 0.0)
        s *= 2
    return a


def _ssd_kernel(xbc_ref, z_ref, dt_ref, cbuf_ref, s0_ref, cw_ref, cb_ref, dtb_ref, alog_ref, dsk_ref, nw_ref,
                y_ref, cnew_ref, sout_ref, xext, st, ysc, *, rows):
    ch = SSD_CHUNK
    pad = SUBLANES
    c = pl.program_id(1)
    last = pl.num_programs(1) - 1

    @pl.when(c == 0)
    def _():
        xext[0:pad, :] = jnp.zeros((pad, CONV_DIM), F32)
        xext[pad - (SSD_CONV - 1):pad, :] = cbuf_ref[0]
        st[...] = s0_ref[0].T

    xext[pad:pad + rows, :] = xbc_ref[0]
    if rows < ch:
        xext[pad + rows:pad + ch, :] = jnp.zeros((ch - rows, CONV_DIM), F32)

    conv = cb_ref[...] + xext[pad:pad + ch, :] * cw_ref[SSD_CONV - 1:SSD_CONV, :]
    for w in range(SSD_CONV - 1):
        off = pad - (SSD_CONV - 1) + w
        conv = conv + xext[off:off + ch, :] * cw_ref[w:w + 1, :]
    act = _silu(conv)

    @pl.when(c == last)
    def _():
        cnew_ref[0] = xext[pad + rows - (SSD_CONV - 1):pad + rows, :]

    xext[0:pad, :] = xext[ch:ch + pad, :]

    xs = act[:, :SSD_INNER]
    dt_raw = dt_ref[0]
    if rows < ch:
        dt_raw = jnp.concatenate([dt_raw, jnp.zeros((ch - rows, LANES), F32)], axis=0)
    dt_in = dt_raw + dtb_ref[...]
    dt = jnp.maximum(dt_in, 0.0) + jnp.log1p(jnp.exp(-jnp.abs(dt_in)))
    row = lax.broadcasted_iota(jnp.int32, (ch, LANES), 0)
    col = lax.broadcasted_iota(jnp.int32, (ch, LANES), 1)
    if rows < ch:
        dt = jnp.where(row < rows, dt, 0.0)
    a_cs = _cumsum_rows(dt * (-jnp.exp(alog_ref[...])))
    a_cs_t = a_cs.T
    dt_t = dt.T
    causal = row >= col

    heads_per_pair = LANES // SSD_HEADDIM
    assert heads_per_pair == 2 and SSD_STATE == ch
    first_of_pair = col < SSD_HEADDIM
    for g in range(SSD_GROUPS):
        bg = act[:, SSD_INNER + g * SSD_STATE:SSD_INNER + (g + 1) * SSD_STATE]
        cg = act[:, SSD_INNER + SSD_BC + g * SSD_STATE:SSD_INNER + SSD_BC + (g + 1) * SSD_STATE]
        gmat = _dot_nt(cg.astype(BF16), bg.astype(BF16))
        bg_t = bg.T
        for pair in range(SSD_HEADS // SSD_GROUPS // heads_per_pair):
            hd0 = g * (SSD_HEADS // SSD_GROUPS) + pair * heads_per_pair
            sl = slice(hd0 * SSD_HEADDIM, hd0 * SSD_HEADDIM + LANES)
            x_pair = xs[:, sl]
            s_pair = st[:, sl]
            x_diag = [jnp.where(first_of_pair, x_pair, 0.0).astype(BF16),
                      jnp.where(first_of_pair, 0.0, x_pair).astype(BF16)]
            s_diag = [jnp.where(first_of_pair, s_pair, 0.0).astype(BF16),
                      jnp.where(first_of_pair, 0.0, s_pair).astype(BF16)]
            m_parts, c_parts, b_parts, decay = [], [], [], []
            for hd in range(hd0, hd0 + heads_per_pair):
                acs_col = a_cs[:, hd:hd + 1]
                acs_row = a_cs_t[hd:hd + 1, :]
                dt_row = dt_t[hd:hd + 1, :]
                acs_last = a_cs_t[hd:hd + 1, ch - 1:ch]
                lmat = jnp.where(causal, jnp.exp(acs_col - acs_row), 0.0)
                m_parts.append((gmat * lmat * dt_row).astype(BF16))
                c_parts.append((cg * jnp.exp(acs_col)).astype(BF16))
                b_parts.append((bg_t * (jnp.exp(acs_last - acs_row) * dt_row)).astype(BF16))
                decay.append(jnp.exp(acs_last))
            ysc[:, sl] = _dot(jnp.concatenate(m_parts + c_parts, axis=1), jnp.concatenate(x_diag + s_diag, axis=0))
            st[:, sl] = (jnp.where(first_of_pair[:1], decay[0], decay[1]) * s_pair
                         + _dot(jnp.concatenate(b_parts, axis=1), jnp.concatenate(x_diag, axis=0)))

    y = (ysc[0:rows, :] + xs[:rows] * dsk_ref[...]) * _silu(z_ref[0])
    gw = SSD_INNER // SSD_GROUPS
    for g in range(SSD_GROUPS):
        yg = y[:, g * gw:(g + 1) * gw]
        yg = yg * lax.rsqrt(jnp.mean(yg * yg, axis=-1, keepdims=True) + EPS)
        y_ref[0, :, g * gw:(g + 1) * gw] = yg * nw_ref[:, g * gw:(g + 1) * gw]

    @pl.when(c == last)
    def _():
        sout_ref[0] = st[...].T


def _ssd(xbc, z, dt, conv_buf, ssm_state, conv_w, conv_b, dt_bias, a_log, d_skip, norm_w):
    bsz, seqlen, _ = xbc.shape
    rows = SSD_CHUNK if seqlen % SSD_CHUNK == 0 else seqlen
    assert rows == seqlen or rows == SSD_CHUNK
    assert rows % SUBLANES == 0 and rows >= SSD_CONV - 1
    nc = seqlen // rows
    hp = SSD_HEADS * SSD_HEADDIM

    def tok(width):
        return pl.BlockSpec((1, rows, width), lambda b, c: (b, c, 0))

    def per_b(d1, d2):
        return pl.BlockSpec((1, d1, d2), lambda b, c: (b, 0, 0))

    def vec(width, r=1):
        return pl.BlockSpec((r, width), lambda b, c: (0, 0))

    def pad_lanes(v):
        return jnp.pad(v.astype(F32), (0, LANES - v.shape[0])).reshape(1, LANES)

    y, conv_new, s_new = pl.pallas_call(
        functools.partial(_ssd_kernel, rows=rows),
        grid=(bsz, nc),
        in_specs=[tok(CONV_DIM), tok(SSD_INNER), tok(LANES),
                  per_b(SSD_CONV - 1, CONV_DIM), per_b(hp, SSD_STATE),
                  vec(CONV_DIM, SSD_CONV), vec(CONV_DIM), vec(LANES), vec(LANES), vec(SSD_INNER), vec(SSD_INNER)],
        out_specs=[tok(SSD_INNER), per_b(SSD_CONV - 1, CONV_DIM), per_b(hp, SSD_STATE)],
        out_shape=[jax.ShapeDtypeStruct((bsz, seqlen, SSD_INNER), F32),
                   jax.ShapeDtypeStruct((bsz, SSD_CONV - 1, CONV_DIM), F32),
                   jax.ShapeDtypeStruct((bsz, hp, SSD_STATE), F32)],
        scratch_shapes=[pltpu.VMEM((SSD_CHUNK + SUBLANES, CONV_DIM), F32),
                        pltpu.VMEM((SSD_STATE, hp), F32),
                        pltpu.VMEM((SSD_CHUNK, SSD_INNER), F32)],
        compiler_params=_cparams(("parallel", "arbitrary"), 40 << 20),
        name="ssd",
    )(xbc, z, dt, conv_buf, ssm_state.reshape(bsz, hp, SSD_STATE),
      conv_w, conv_b.reshape(1, -1), pad_lanes(dt_bias), pad_lanes(a_log),
      jnp.repeat(d_skip.astype(F32), SSD_HEADDIM).reshape(1, -1), norm_w.reshape(1, -1))
    return y, conv_new, s_new.reshape(bsz, SSD_HEADS, SSD_HEADDIM, SSD_STATE)


def _select_topk(scores, valid, axis):
    sc = scores if valid is None else jnp.where(valid, scores, -jnp.inf)
    sel = jnp.zeros(scores.shape, F32)
    pos_f = lax.broadcasted_iota(jnp.int32, scores.shape, axis).astype(F32)
    for _ in range(MOBA_TOPK):
        m = jnp.max(sc, axis=axis, keepdims=True)
        idx = jnp.min(jnp.where(sc == m, pos_f, float(1 << 20)), axis=axis, keepdims=True)
        idx = jnp.where(m > -jnp.inf, idx, -1.0)
        hit = pos_f == idx
        sel = jnp.where(hit, 1.0, sel)
        sc = jnp.where(hit, -jnp.inf, sc)
    return sel > 0.5


def _moba_prompt_kernel(q_ref, km_ref, kaug_ref, vaug_ref, o_ref, s_a, s_b, *, qblocks):
    j = pl.program_id(2)
    tq = MOBA_BLOCK
    chunk = qblocks * tq
    half = GRP * tq
    q2 = q_ref[0]
    qs = jnp.concatenate([q2[b * tq:(b + 1) * tq, g * HEAD_DIM:(g + 1) * HEAD_DIM]
                          for b in range(qblocks) for g in range(GRP)], axis=0)
    rows = qs.shape[0]
    lane = lax.broadcasted_iota(jnp.int32, (rows, LANES), 1)
    scores_t = lax.dot_general(km_ref[0, 0], qs, (((1,), (1,)), ((), ())), preferred_element_type=F32,
                               precision=HIGHEST)
    nbp = scores_t.shape[0]
    blk = lax.broadcasted_iota(jnp.int32, scores_t.shape, 0)
    col = lax.broadcasted_iota(jnp.int32, scores_t.shape, 1)
    own = qblocks * j
    for b in range(1, qblocks):
        own = own + (col >= b * half).astype(jnp.int32)
    sel_t = _select_topk(scores_t, blk < own, 0)
    pen_t = jnp.concatenate([jnp.zeros((HEAD_DIM, rows), F32), jnp.where(sel_t, 0.0, NEG),
                             jnp.full((LANES - HEAD_DIM - nbp, rows), NEG, F32)], axis=0)
    pen = pen_t.T
    q_pad = jnp.concatenate([qs * (HEAD_DIM ** -0.5), jnp.zeros((rows, LANES - HEAD_DIM), F32)], axis=1)
    q_aug = jnp.where(lane < HEAD_DIM, q_pad, pen).astype(BF16)
    q_own = q_pad.astype(BF16)

    def update(m, acc, s, pv):
        m_new = jnp.maximum(m, jnp.max(s, axis=1, keepdims=True))
        p = jnp.exp(s - m_new)
        acc = jnp.exp(m - m_new) * acc + pv(p.astype(BF16))
        return m_new, acc

    total_chunks = kaug_ref.shape[2] // chunk
    npairs = (j + 2) // 2

    def logits(c):
        start = pl.multiple_of(jnp.minimum(c, total_chunks - 1) * chunk, chunk)
        return _dot_nt(q_aug, kaug_ref[0, 0, pl.ds(start, chunk), :])

    def values(c):
        vb = vaug_ref[0, 0, pl.ds(pl.multiple_of(c * chunk, chunk), chunk), :]
        return lambda p: _dot(p, vb)

    def body(t, carry):
        m, acc = carry
        c0 = 2 * t
        s_b[...] = logits(c0 + 1)
        m, acc = update(m, acc, s_a[...], values(c0))
        s_a[...] = logits(c0 + 2)
        m, acc = update(m, acc, s_b[...], values(c0 + 1))
        return m, acc

    def own_rows(ref, b):
        return ref[0, 0, pl.ds(pl.multiple_of((qblocks * j + b) * tq, tq), tq), :]

    s_own = jnp.concatenate([_dot_nt(q_own[b * half:(b + 1) * half], own_rows(kaug_ref, b))
                             for b in range(qblocks)], axis=0)
    r = lax.broadcasted_iota(jnp.int32, (rows, tq), 0)
    cidx = lax.broadcasted_iota(jnp.int32, (rows, tq), 1)
    s_own = jnp.where(cidx <= (r & (tq - 1)), s_own, NEG)

    def own_values(p):
        return jnp.concatenate([_dot(p[b * half:(b + 1) * half], own_rows(vaug_ref, b))
                                for b in range(qblocks)], axis=0)

    s_a[...] = logits(0)
    init = (jnp.full((rows, 1), -jnp.inf, F32), jnp.zeros((rows, LANES), F32))
    m, acc = lax.fori_loop(0, npairs, body, init)
    _, acc = update(m, acc, s_own, own_values)
    out = acc[:, :HEAD_DIM] / acc[:, HEAD_DIM:HEAD_DIM + 1]
    o_ref[0] = jnp.concatenate(
        [jnp.concatenate([out[(b * GRP + g) * tq:(b * GRP + g + 1) * tq] for g in range(GRP)], axis=1)
         for b in range(qblocks)], axis=0)


def _moba_prompt(q, kmean, kaug, vaug):
    bsz, seqlen, _ = q.shape
    nb = seqlen // MOBA_BLOCK
    assert GRP * HEAD_DIM == LANES and nb <= LANES - HEAD_DIM and MOBA_BLOCK & (MOBA_BLOCK - 1) == 0
    nbp = -(-nb // SUBLANES) * SUBLANES
    km = kmean.reshape(bsz, nb, ATTN_KV_HEADS, HEAD_DIM).transpose(0, 2, 1, 3)
    km = jnp.pad(km, ((0, 0), (0, 0), (0, nbp - nb), (0, 0)))
    kv_spec = pl.BlockSpec((1, 1, seqlen, LANES), lambda b, h, i: (b, h, 0, 0))
    qblocks = 4
    chunk = qblocks * MOBA_BLOCK
    assert seqlen % (2 * chunk) == 0, "the sweep walks the keys two chunks at a time"
    rows = qblocks * GRP * MOBA_BLOCK
    vmem = 2 * 2 * seqlen * LANES * 2 + 2 * rows * chunk * 4 + (24 << 20)
    return pl.pallas_call(
        functools.partial(_moba_prompt_kernel, qblocks=qblocks),
        scratch_shapes=[pltpu.VMEM((rows, chunk), F32), pltpu.VMEM((rows, chunk), F32)],
        grid=(bsz, ATTN_KV_HEADS, nb // qblocks),
        in_specs=[pl.BlockSpec((1, chunk, LANES), lambda b, h, i: (b, i, h)),
                  pl.BlockSpec((1, 1, nbp, HEAD_DIM), lambda b, h, i: (b, h, 0, 0)),
                  kv_spec, kv_spec],
        out_specs=pl.BlockSpec((1, chunk, LANES), lambda b, h, i: (b, i, h)),
        out_shape=jax.ShapeDtypeStruct((bsz, seqlen, ATTN_Q), F32),
        compiler_params=_cparams(("parallel", "parallel", "arbitrary"), vmem),
        name="moba_prompt",
    )(q, km, kaug, vaug)


def _moba_sample_kernel(pt_ref, qbt_ref, qb_ref, *refs, pages_per_step, nsteps, nblk, dec_seq):
    del pt_ref
    k_refs = refs[:pages_per_step]
    v_refs = refs[pages_per_step:2 * pages_per_step]
    kn_ref, vn_ref, o_ref, kt_sc, sct_sc, pen_sc, m_sc, l_sc, acc_sc = refs[2 * pages_per_step:]
    step = pl.program_id(1)
    rows = qb_ref.shape[1]
    pages_per_block = MOBA_BLOCK // LANES
    blocks_per_step = pages_per_step // pages_per_block

    @pl.when(step < nsteps)
    def _():
        qbt = qbt_ref[0]
        for jb in range(blocks_per_step):
            ksum = None
            for j in range(pages_per_block):
                idx = jb * pages_per_block + j
                kpage = k_refs[idx][0]
                kt_sc[step * pages_per_step + idx] = kpage.astype(BF16)
                ksum = kpage if ksum is None else ksum + kpage
            kmean = jnp.sum(ksum, axis=1, keepdims=True) * (1.0 / MOBA_BLOCK)
            sct_sc[pl.ds(step * blocks_per_step + jb, 1), :] = jnp.sum(qbt * kmean, axis=0, keepdims=True)

    @pl.when(step >= nsteps)
    def _():
        n = step - nsteps
        lane = lax.broadcasted_iota(jnp.int32, (rows, LANES), 1)

        @pl.when(n == 0)
        def _():
            sel_t = _select_topk(sct_sc[...], None, 0)
            pen_t = jnp.concatenate([jnp.where(sel_t, 0.0, NEG), jnp.full((LANES - nblk, LANES), NEG, F32)],
                                    axis=0)
            pen_sc[...] = pen_t.T[:rows]
            m_sc[...] = jnp.full(m_sc.shape, -jnp.inf, F32)
            l_sc[...] = jnp.zeros(l_sc.shape, F32)
            acc_sc[...] = jnp.zeros(acc_sc.shape, F32)

        qb = (qb_ref[0] * (HEAD_DIM ** -0.5)).astype(BF16)

        def update(s, pv_fn):
            m = m_sc[...]
            m_new = jnp.maximum(m, jnp.max(s, axis=1, keepdims=True))
            p = jnp.exp(s - m_new)
            alpha = jnp.exp(m - m_new)
            l_sc[...] = alpha * l_sc[...] + jnp.sum(p, axis=1, keepdims=True)
            acc_sc[...] = alpha * acc_sc[...] + pv_fn(p.astype(BF16))
            m_sc[...] = m_new

        pen = pen_sc[...]
        pen_cols = []
        for jb in range(blocks_per_step):
            pen_col = jnp.sum(jnp.where(lane == n * blocks_per_step + jb, pen, 0.0), axis=1, keepdims=True)
            pen_cols.append(jnp.broadcast_to(pen_col, (rows, MOBA_BLOCK)))
        kt_all = jnp.concatenate([kt_sc[n * pages_per_step + j] for j in range(pages_per_step)], axis=1)
        vt_all = jnp.concatenate([v_ref[0].astype(BF16) for v_ref in v_refs], axis=1)
        update(_dot(qb, kt_all) + jnp.concatenate(pen_cols, axis=1), lambda p: _dot_nt(p, vt_all))

        @pl.when(n == nsteps - 1)
        def _():
            zpad = jnp.zeros((LANES - dec_seq, kn_ref.shape[2]), F32)
            kn = jnp.concatenate([kn_ref[0], zpad], axis=0).astype(BF16)
            vn = jnp.concatenate([vn_ref[0], zpad], axis=0).astype(BF16)
            s_own = _dot_nt(qb, kn)
            r = lax.broadcasted_iota(jnp.int32, (rows, LANES), 0)
            qpos = r & (dec_seq - 1)
            s_own = jnp.where((lane <= qpos) & (lane < dec_seq), s_own, NEG)
            update(s_own, lambda p: _dot(p, vn))
            o_ref[0] = acc_sc[...] / l_sc[...]


def _moba_sample(q, k_new, v_new, cache_kt, cache_vt, page_table, past_len):
    bsz, dec_seq, _ = q.shape
    page = cache_kt.shape[2]
    assert page == LANES and MOBA_BLOCK % page == 0
    assert past_len % MOBA_BLOCK == 0 and dec_seq == SUBLANES
    nblk = past_len // MOBA_BLOCK
    assert MOBA_TOPK <= nblk <= LANES and nblk % SUBLANES == 0
    n_pages = past_len // page
    pages_per_step = 16
    while n_pages % pages_per_step:
        pages_per_step //= 2
    assert pages_per_step * page >= MOBA_BLOCK
    nsteps = n_pages // pages_per_step
    rows = ATTN_HEADS * dec_seq
    q4 = q.reshape(bsz, dec_seq, ATTN_HEADS, HEAD_DIM).transpose(0, 2, 1, 3)
    own = (np.arange(ATTN_HEADS)[:, None] // GRP == np.arange(ATTN_KV_HEADS)[None, :]).astype(np.float32)
    qb = (q4[:, :, :, None, :] * jnp.asarray(own)[None, :, None, :, None]).reshape(bsz, rows, ATTN_KV)
    qbt = jnp.pad(qb.transpose(0, 2, 1), ((0, 0), (0, 0), (0, LANES - rows)))

    def k_spec(j):
        return pl.BlockSpec((1, ATTN_KV, page),
                            lambda b, s, pt: (pt[b, pages_per_step * jnp.minimum(s, nsteps - 1) + j], 0, 0))

    def v_spec(j):
        return pl.BlockSpec((1, ATTN_KV, page),
                            lambda b, s, pt: (pt[b, pages_per_step * jnp.maximum(s - nsteps, 0) + j], 0, 0))

    def per_b(d1, d2):
        return pl.BlockSpec((1, d1, d2), lambda b, s, pt: (b, 0, 0))

    out = pl.pallas_call(
        functools.partial(_moba_sample_kernel, pages_per_step=pages_per_step, nsteps=nsteps, nblk=nblk,
                          dec_seq=dec_seq),
        grid_spec=pltpu.PrefetchScalarGridSpec(
            num_scalar_prefetch=1, grid=(bsz, 2 * nsteps),
            in_specs=[per_b(ATTN_KV, LANES), per_b(rows, ATTN_KV)]
                     + [k_spec(j) for j in range(pages_per_step)] + [v_spec(j) for j in range(pages_per_step)]
                     + [per_b(dec_seq, ATTN_KV), per_b(dec_seq, ATTN_KV)],
            out_specs=per_b(rows, ATTN_KV),
            scratch_shapes=[pltpu.VMEM((n_pages, ATTN_KV, page), BF16), pltpu.VMEM((nblk, LANES), F32),
                            pltpu.VMEM((rows, LANES), F32), pltpu.VMEM((rows, 1), F32), pltpu.VMEM((rows, 1), F32),
                            pltpu.VMEM((rows, ATTN_KV), F32)]),
        out_shape=jax.ShapeDtypeStruct((bsz, rows, ATTN_KV), F32),
        compiler_params=_cparams(("parallel", "arbitrary"), 32 << 20),
        name="moba_sample_attn",
    )(page_table, qbt, qb, *([cache_kt] * pages_per_step), *([cache_vt] * pages_per_step), k_new, v_new)
    o5 = out.reshape(bsz, ATTN_HEADS, dec_seq, ATTN_KV_HEADS, HEAD_DIM)
    o4 = jnp.sum(o5 * jnp.asarray(own)[None, :, None, :, None], axis=3)
    return o4.transpose(0, 2, 1, 3).reshape(bsz, dec_seq, ATTN_Q)


def _mix_kernel(x_ref, ys_ref, ya_ref, gs_ref, ga_ref, gt1_ref, sc2_ref, sh2_ref, nw_ref,
                wso_ref, wao_ref, wo_ref, wrt_ref, brt_ref, x1_ref, hc_ref):
    merged = (jax.nn.sigmoid(gs_ref[0]) * _dot(ys_ref[0].astype(BF16), wso_ref[...])
              + jax.nn.sigmoid(ga_ref[0]) * _dot(ya_ref[0].astype(BF16), wao_ref[...]))
    x1 = x_ref[0] + gt1_ref[0] * _dot(merged.astype(BF16), wo_ref[...])
    x1_ref[0] = x1
    h2 = x1 * lax.rsqrt(jnp.mean(x1 * x1, axis=-1, keepdims=True) + EPS) * nw_ref[...]
    h2 = h2 * (1.0 + sc2_ref[0]) + sh2_ref[0]
    hc_ref[0, :, :D_MODEL] = h2

    h_hi = h2.astype(BF16)
    h_lo = (h2 - h_hi.astype(F32)).astype(BF16)
    hw = _dot(h_hi, wrt_ref[...])
    lg = hw[:, :LANES] + hw[:, LANES:] + _dot(h_lo, wrt_ref[:, :LANES]) + brt_ref[...]
    lane = lax.broadcasted_iota(jnp.int32, lg.shape, 1).astype(F32)
    big = float(1 << 20)
    ninf = -jnp.inf
    gl = jnp.where(lane < N_EGROUPS, lg, ninf)
    gmax = jnp.max(gl, axis=1, keepdims=True)
    gidx = jnp.min(jnp.where(gl == gmax, lane, big), axis=1, keepdims=True)
    g_w = 1.0 / jnp.sum(jnp.exp(gl - gmax), axis=1, keepdims=True)
    lo = ROUTE_OFF + gidx * EXPERTS_PER_GROUP
    el = jnp.where((lane >= lo) & (lane < lo + EXPERTS_PER_GROUP), lg, ninf)
    m1 = jnp.max(el, axis=1, keepdims=True)
    i1 = jnp.min(jnp.where(el == m1, lane, big), axis=1, keepdims=True)
    el2 = jnp.where(lane == i1, ninf, el)
    m2 = jnp.max(el2, axis=1, keepdims=True)
    i2 = jnp.min(jnp.where(el2 == m2, lane, big), axis=1, keepdims=True)
    e2 = jnp.exp(m2 - m1)
    den = 1.0 / (1.0 + e2)
    comb = jnp.where(lane == i1, den * g_w, 0.0) + jnp.where(lane == i2, e2 * den * g_w, 0.0)
    hc_ref[0, :, D_MODEL:] = jnp.where(lane == 0.0, gidx, comb)


def _mix(x, y_ssd, y_att, gs, ga, gt1, sc2, sh2, norm_w, wso, wao, wo, wrt, brt):
    bsz, seqlen, _ = x.shape
    tm = 256
    assert seqlen % tm == 0
    nt = seqlen // tm
    mod_rows = gt1.shape[1]
    if mod_rows == 1:
        mod_spec = pl.BlockSpec((1, 1, D_MODEL), lambda b, i: (b, 0, 0))
    else:
        mod_spec = pl.BlockSpec((1, tm, D_MODEL), lambda b, i: (b, i, 0))

    def tok(width):
        return pl.BlockSpec((1, tm, width), lambda b, i: (b, i, 0))

    def full(d1, d2):
        return pl.BlockSpec((d1, d2), lambda b, i: (0, 0))

    vmem = 2 * 2 * (2 * D_MODEL * D_MODEL + ATTN_Q * D_MODEL) + 2 * tm * (8 * D_MODEL) * 4 + (12 << 20)
    return pl.pallas_call(
        _mix_kernel,
        grid=(bsz, nt),
        in_specs=[tok(D_MODEL), tok(SSD_INNER), tok(ATTN_Q), tok(D_MODEL), tok(D_MODEL),
                  mod_spec, mod_spec, mod_spec, full(1, D_MODEL),
                  full(SSD_INNER, D_MODEL), full(ATTN_Q, D_MODEL), full(D_MODEL, D_MODEL),
                  full(D_MODEL, 2 * LANES), full(1, LANES)],
        out_specs=[tok(D_MODEL), tok(D_MODEL + LANES)],
        out_shape=[jax.ShapeDtypeStruct((bsz, seqlen, D_MODEL), F32),
                   jax.ShapeDtypeStruct((bsz, seqlen, D_MODEL + LANES), F32)],
        compiler_params=_cparams(("parallel", "parallel"), vmem),
        name="mix",
    )(x, y_ssd, y_att, gs, ga, gt1, sc2, sh2, norm_w.reshape(1, -1), wso, wao, wo, wrt, brt)


def _expert_step(h2, comb, expert, wg_ref, wu_ref, wd_ref):
    lane = lax.broadcasted_iota(jnp.int32, comb.shape, 1)
    cw = jnp.sum(jnp.where(lane == expert + ROUTE_OFF, comb, 0.0), axis=1, keepdims=True)
    a = _dot(h2, wg_ref[0])
    u = _dot(h2, wu_ref[0])
    act = (_silu(a) * u * cw).astype(BF16)
    return _dot(act, wd_ref[0])


def _final_norm(x1, gt2, ffn, fw):
    x2 = x1 + gt2 * ffn
    return x2 * lax.rsqrt(jnp.mean(x2 * x2, axis=-1, keepdims=True) + EPS) * fw


def _moe_kernel(hc_ref, x1_ref, gt2_ref, fw_ref, wg_ref, wu_ref, wd_ref, y_ref, acc):
    e = pl.program_id(2)

    @pl.when(e == 0)
    def _():
        acc[...] = jnp.zeros(acc.shape, F32)

    acc[...] += _expert_step(hc_ref[0, :, :D_MODEL].astype(BF16), hc_ref[0, :, D_MODEL:], e,
                             wg_ref, wu_ref, wd_ref)

    @pl.when(e == pl.num_programs(2) - 1)
    def _():
        y_ref[0] = _final_norm(x1_ref[0], gt2_ref[0], acc[...], fw_ref[...])


def _moe(hc, x1, gt2, final_w, wg, wu, wd):
    bsz, seqlen, _ = x1.shape
    tm = 1024 if seqlen % 1024 == 0 else 256
    assert seqlen % tm == 0
    nt = seqlen // tm
    mod_rows = gt2.shape[1]
    if mod_rows == 1:
        mod_spec = pl.BlockSpec((1, 1, D_MODEL), lambda b, i, e: (b, 0, 0))
    else:
        mod_spec = pl.BlockSpec((1, tm, D_MODEL), lambda b, i, e: (b, i, 0))

    def tok(width):
        return pl.BlockSpec((1, tm, width), lambda b, i, e: (b, i, 0))

    vmem = 2 * tm * D_MODEL * (2 + 4 + 4) + tm * D_MODEL * 4 + 2 * 3 * D_MODEL * D_EXPERT * 2 + (16 << 20)
    return pl.pallas_call(
        _moe_kernel,
        grid=(bsz, nt, N_EXPERTS),
        in_specs=[tok(D_MODEL + LANES), tok(D_MODEL), mod_spec,
                  pl.BlockSpec((1, D_MODEL), lambda b, i, e: (0, 0)),
                  pl.BlockSpec((1, D_MODEL, D_EXPERT), lambda b, i, e: (e, 0, 0)),
                  pl.BlockSpec((1, D_MODEL, D_EXPERT), lambda b, i, e: (e, 0, 0)),
                  pl.BlockSpec((1, D_EXPERT, D_MODEL), lambda b, i, e: (e, 0, 0))],
        out_specs=tok(D_MODEL),
        out_shape=jax.ShapeDtypeStruct((bsz, seqlen, D_MODEL), F32),
        scratch_shapes=[pltpu.VMEM((tm, D_MODEL), F32)],
        compiler_params=_cparams(("parallel", "parallel", "arbitrary"), vmem),
        name="moe",
    )(hc, x1, gt2, final_w.reshape(1, -1), wg, wu, wd)


ROUTE_TILE = 512
DISPATCH_TILE = 256


def _move_rows(src_of, dst_of, sem, whole_src, whole_dst):
    def body(r, carry):
        pltpu.make_async_copy(src_of(r), dst_of(r), sem).start()
        return carry

    lax.fori_loop(0, DISPATCH_TILE, body, 0, unroll=8)
    pltpu.make_async_copy(whole_src, whole_dst, sem).wait()


def _dispatch_kernel(pos_ref, hc_ref, xs_in, xs_ref, sem):
    del xs_in
    base = pl.program_id(0) * DISPATCH_TILE
    _move_rows(lambda r: hc_ref.at[pl.ds(r, 1), :], lambda r: xs_ref.at[pl.ds(pos_ref[base + r], 1), :],
               sem.at[0], hc_ref, xs_ref.at[pl.ds(0, DISPATCH_TILE), :])


def _moe_routed_kernel(tg_ref, xs_ref, wg_ref, wu_ref, wd_ref, ys_ref, xb, acc):
    t, e = pl.program_id(0), pl.program_id(1)

    @pl.when(e == 0)
    def _():
        xb[...] = xs_ref[:, :D_MODEL].astype(BF16)
        acc[...] = jnp.zeros(acc.shape, F32)

    expert = tg_ref[t] * EXPERTS_PER_GROUP + e
    acc[...] += _expert_step(xb[...], xs_ref[:, D_MODEL:], expert, wg_ref, wu_ref, wd_ref)

    @pl.when(e == pl.num_programs(1) - 1)
    def _():
        ys_ref[...] = acc[...]


def _combine_kernel(pos_ref, x1_ref, gt2_ref, fw_ref, ys_ref, y_ref, ybuf, sem):
    i = pl.program_id(0)
    slot = i % 2

    def request(step, buf):
        base = step * DISPATCH_TILE

        def body(r, carry):
            pltpu.make_async_copy(ys_ref.at[pl.ds(pos_ref[base + r], 1), :], ybuf.at[buf, pl.ds(r, 1), :],
                                  sem.at[buf]).start()
            return carry

        lax.fori_loop(0, DISPATCH_TILE, body, 0, unroll=8)

    @pl.when(i == 0)
    def _():
        request(0, 0)

    @pl.when(i + 1 < pl.num_programs(0))
    def _():
        request(i + 1, 1 - slot)

    pltpu.make_async_copy(ys_ref.at[pl.ds(0, DISPATCH_TILE), :], ybuf.at[slot], sem.at[slot]).wait()
    y_ref[...] = _final_norm(x1_ref[...], gt2_ref[0], ybuf[slot], fw_ref[...])


def _moe_routed(hc, x1, gt2, final_w, wg, wu, wd):
    bsz, seqlen, width = hc.shape
    ntok = bsz * seqlen
    assert seqlen % DISPATCH_TILE == 0 and ntok % ROUTE_TILE == 0 and gt2.shape[1] == 1
    hcf, x1f = hc.reshape(ntok, width), x1.reshape(ntok, D_MODEL)
    gidx = hcf[:, D_MODEL].astype(jnp.int32)
    onehot = (gidx[:, None] == jnp.arange(N_EGROUPS, dtype=jnp.int32)[None, :]).astype(jnp.int32)
    csum = jnp.cumsum(onehot, axis=0)
    padded = (csum[-1] + ROUTE_TILE - 1) // ROUTE_TILE * ROUTE_TILE
    ends = jnp.cumsum(padded)
    pos = jnp.sum(onehot * (csum - 1 + (ends - padded)[None, :]), axis=1).astype(jnp.int32)
    n_tiles = ntok // ROUTE_TILE + N_EGROUPS
    tile_start = jnp.arange(n_tiles, dtype=jnp.int32) * ROUTE_TILE
    tile_group = jnp.minimum(jnp.sum((tile_start[:, None] >= ends[None, :]).astype(jnp.int32), axis=1),
                             N_EGROUPS - 1).astype(jnp.int32)
    nrows = n_tiles * ROUTE_TILE
    nsteps = ntok // DISPATCH_TILE
    any_spec = pl.BlockSpec(memory_space=pl.ANY)

    xs = pl.pallas_call(
        _dispatch_kernel,
        grid_spec=pltpu.PrefetchScalarGridSpec(
            num_scalar_prefetch=1, grid=(nsteps,),
            in_specs=[pl.BlockSpec((DISPATCH_TILE, width), lambda i, pos: (i, 0)), any_spec],
            out_specs=any_spec,
            scratch_shapes=[pltpu.SemaphoreType.DMA((1,))]),
        out_shape=jax.ShapeDtypeStruct((nrows, width), F32),
        input_output_aliases={2: 0},
        compiler_params=_cparams(("arbitrary",), 16 << 20),
        name="moe_dispatch",
    )(pos, hcf, jnp.zeros((nrows, width), F32))

    def w_spec(d1, d2):
        return pl.BlockSpec((1, d1, d2), lambda t, e, tg: (tg[t] * EXPERTS_PER_GROUP + e, 0, 0))

    vmem = (2 * ROUTE_TILE * (width + D_MODEL) * 4 + ROUTE_TILE * D_MODEL * (2 + 4)
            + 2 * 3 * D_MODEL * D_EXPERT * 2 + (16 << 20))
    ys = pl.pallas_call(
        _moe_routed_kernel,
        grid_spec=pltpu.PrefetchScalarGridSpec(
            num_scalar_prefetch=1, grid=(n_tiles, EXPERTS_PER_GROUP),
            in_specs=[pl.BlockSpec((ROUTE_TILE, width), lambda t, e, tg: (t, 0)),
                      w_spec(D_MODEL, D_EXPERT), w_spec(D_MODEL, D_EXPERT), w_spec(D_EXPERT, D_MODEL)],
            out_specs=pl.BlockSpec((ROUTE_TILE, D_MODEL), lambda t, e, tg: (t, 0)),
            scratch_shapes=[pltpu.VMEM((ROUTE_TILE, D_MODEL), BF16), pltpu.VMEM((ROUTE_TILE, D_MODEL), F32)]),
        out_shape=jax.ShapeDtypeStruct((nrows, D_MODEL), F32),
        compiler_params=_cparams(("parallel", "arbitrary"), vmem),
        name="moe_routed",
    )(tile_group, xs, wg, wu, wd)

    steps_per_batch = seqlen // DISPATCH_TILE
    y = pl.pallas_call(
        _combine_kernel,
        grid_spec=pltpu.PrefetchScalarGridSpec(
            num_scalar_prefetch=1, grid=(nsteps,),
            in_specs=[pl.BlockSpec((DISPATCH_TILE, D_MODEL), lambda i, pos: (i, 0)),
                      pl.BlockSpec((1, 1, D_MODEL), lambda i, pos: (i // steps_per_batch, 0, 0)),
                      pl.BlockSpec((1, D_MODEL), lambda i, pos: (0, 0)), any_spec],
            out_specs=pl.BlockSpec((DISPATCH_TILE, D_MODEL), lambda i, pos: (i, 0)),
            scratch_shapes=[pltpu.VMEM((2, DISPATCH_TILE, D_MODEL), F32), pltpu.SemaphoreType.DMA((2,))]),
        out_shape=jax.ShapeDtypeStruct((ntok, D_MODEL), F32),
        compiler_params=_cparams(("arbitrary",), 16 << 20),
        name="moe_combine",
    )(pos, x1f, gt2, final_w.reshape(1, -1), ys)
    return y.reshape(bsz, seqlen, D_MODEL)


def _prep_weights(lp):
    w_in = lp["w_in"]
    sizes = (SSD_INNER, CONV_DIM, SSD_HEADS, ATTN_Q, ATTN_KV, ATTN_KV, D_MODEL, D_MODEL)
    pts = [int(p) for p in np.cumsum(sizes)[:-1]]
    z, xbc, dt, q, k, v, gs, ga = jnp.split(w_in, pts, axis=1)
    dt = jnp.pad(dt, ((0, 0), (0, LANES - SSD_HEADS)))
    w_in_r = jnp.concatenate([z, xbc, q, k, v, gs, ga, dt], axis=1).astype(BF16)
    wrt = jnp.concatenate([lp["w_group"], lp["w_router"]], axis=1)
    wrt = jnp.pad(wrt, ((0, 0), (0, LANES - wrt.shape[1]))).astype(F32)
    wrt_hi = wrt.astype(BF16)
    wrt = jnp.concatenate([wrt_hi, (wrt - wrt_hi.astype(F32)).astype(BF16)], axis=1)
    brt = jnp.concatenate([lp["b_group"], lp["b_router"]])
    brt = jnp.pad(brt, (0, LANES - brt.shape[0])).astype(F32).reshape(1, LANES)
    return dict(w_in_r=w_in_r, wrt=wrt, brt=brt,
                wso=lp["w_ssd_out"].astype(BF16), wao=lp["w_attn_out"].astype(BF16), wo=lp["w_o"].astype(BF16),
                wg=lp["w_gate_e"].astype(BF16), wu=lp["w_up_e"].astype(BF16), wd=lp["w_down_e"].astype(BF16))


def _mods(mod, per_row_repeat):
    parts = jnp.split(mod, 6, axis=-1)
    if per_row_repeat:
        return [jnp.repeat(p, per_row_repeat, axis=0)[None] for p in parts]
    return [p[:, None, :] for p in parts]


def _layer_prompt(x, mod, lp, pw, final_w):
    bsz, seqlen, _ = x.shape
    sh1, sc1, gt1, sh2, sc2, gt2 = _mods(mod, 0)
    tables = _rope_tables(np.arange(seqlen))
    (z, xbc, dt, q, k, v, gs, ga, kmean, kaug, vaug) = _in_proj(x, sc1, sh1, lp["norm1_w"], pw["w_in_r"], tables, True)
    conv0 = jnp.zeros((bsz, SSD_CONV - 1, CONV_DIM), F32)
    ssm0 = jnp.zeros((bsz, SSD_HEADS, SSD_HEADDIM, SSD_STATE), F32)
    y_ssd, conv_new, ssm_new = _ssd(xbc, z, dt, conv0, ssm0, lp["conv_w"], lp["conv_b"], lp["dt_bias"],
                                    lp["a_log"], lp["d_skip"], lp["ssd_norm_w"])
    y_att = _moba_prompt(q, kmean, kaug, vaug)
    x1, hc = _mix(x, y_ssd, y_att, gs, ga, gt1, sc2, sh2, lp["norm2_w"],
                  pw["wso"], pw["wao"], pw["wo"], pw["wrt"], pw["brt"])
    y = _moe_routed(hc, x1, gt2, final_w, pw["wg"], pw["wu"], pw["wd"])
    kv_shape = (bsz, seqlen, ATTN_KV_HEADS, HEAD_DIM)
    return y, k.reshape(kv_shape), v.reshape(kv_shape), conv_new, ssm_new


def _layer_sample(x, mod, conv_buf, ssm_state, cache_k, cache_v, page_table, past_len, lp, pw, final_w):
    bsz, dec_seq, _ = x.shape
    ntok = bsz * dec_seq
    sh1, sc1, gt1, sh2, sc2, gt2 = _mods(mod, dec_seq)
    pos = past_len + (np.arange(ntok) % dec_seq)
    tables = _rope_tables(pos)
    xf = x.reshape(1, ntok, D_MODEL)
    z, xbc, dt, q, k, v, gs, ga = _in_proj(xf, sc1, sh1, lp["norm1_w"], pw["w_in_r"], tables, False)

    def per_b(t):
        return t.reshape(bsz, dec_seq, t.shape[-1])

    y_ssd, conv_new, ssm_new = _ssd(per_b(xbc), per_b(z), per_b(dt), conv_buf, ssm_state, lp["conv_w"],
                                    lp["conv_b"], lp["dt_bias"], lp["a_log"], lp["d_skip"], lp["ssd_norm_w"])
    n_pool, page = cache_k.shape[0], cache_k.shape[1]

    def position_minor(cache):
        return cache.transpose(0, 2, 3, 1).reshape(n_pool, ATTN_KV, page)

    y_att = _moba_sample(per_b(q), per_b(k), per_b(v), position_minor(cache_k), position_minor(cache_v),
                         page_table, past_len)
    x1, hc = _mix(xf, y_ssd.reshape(1, ntok, -1), y_att.reshape(1, ntok, -1), gs, ga, gt1, sc2, sh2,
                  lp["norm2_w"], pw["wso"], pw["wao"], pw["wo"], pw["wrt"], pw["brt"])
    y = _moe(hc, x1, gt2, final_w, pw["wg"], pw["wu"], pw["wd"])
    kv_shape = (bsz, dec_seq, ATTN_KV_HEADS, HEAD_DIM)
    return y.reshape(bsz, dec_seq, D_MODEL), k.reshape(kv_shape), v.reshape(kv_shape), conv_new, ssm_new


def kernel(x_prompt, x_sample, cache_k, cache_v, state_conv, state_ssm, page_table, c_prompt, c_sample,
           w_ada, b_ada, norm1_w, w_in, conv_w, conv_b, dt_bias, a_log, d_skip, ssd_norm_w,
           w_ssd_out, w_attn_out, w_o, norm2_w, w_group, b_group, w_router, b_router,
           w_gate_e, w_up_e, w_down_e, final_w):
    depth = w_in.shape[0]
    assert depth == 1, "the final RMSNorm is fused into the last layer's MoE kernel"
    nb_p, nb_s = c_prompt.shape[0], c_sample.shape[0]
    past_len = page_table.shape[1] * cache_k.shape[2]
    l = 0
    lp = dict(norm1_w=norm1_w[l], w_in=w_in[l], conv_w=conv_w[l], conv_b=conv_b[l], dt_bias=dt_bias[l],
              a_log=a_log[l], d_skip=d_skip[l], ssd_norm_w=ssd_norm_w[l], w_ssd_out=w_ssd_out[l],
              w_attn_out=w_attn_out[l], w_o=w_o[l], norm2_w=norm2_w[l], w_group=w_group[l], b_group=b_group[l],
              w_router=w_router[l], b_router=b_router[l], w_gate_e=w_gate_e[l], w_up_e=w_up_e[l],
              w_down_e=w_down_e[l])
    pw = _prep_weights(lp)
    c_all = jnp.concatenate([c_prompt, c_sample], axis=0)
    pad_rows = (-c_all.shape[0]) % SUBLANES
    mod = _ada_mod(jnp.pad(c_all, ((0, pad_rows), (0, 0))), w_ada[l], b_ada[l])
    yp, kp, vp, cp, sp = _layer_prompt(x_prompt, mod[:nb_p], lp, pw, final_w)
    ys, ks, vs, cs, ss = _layer_sample(x_sample, mod[nb_p:nb_p + nb_s], state_conv[l], state_ssm[l],
                                       cache_k[l], cache_v[l], page_table, past_len, lp, pw, final_w)
    return (yp, ys, kp[None], vp[None], cp[None], sp[None], ks[None], vs[None], cs[None], ss[None])
```

```python
import functools
import math

import numpy as np
import jax
import jax.numpy as jnp
from jax import lax
from jax.experimental import pallas as pl
from jax.experimental.pallas import tpu as pltpu

F32 = jnp.float32
BF16 = jnp.bfloat16
HIGHEST = lax.Precision.HIGHEST

D_MODEL = 1024
SSD_HEADDIM = 64
SSD_INNER = D_MODEL
SSD_HEADS = SSD_INNER // SSD_HEADDIM
SSD_GROUPS = 2
SSD_STATE = 128
SSD_CONV = 4
SSD_CHUNK = 128
SSD_BC = SSD_GROUPS * SSD_STATE
CONV_DIM = SSD_INNER + 2 * SSD_BC
ATTN_HEADS = 8
ATTN_KV_HEADS = 4
HEAD_DIM = 64
ATTN_Q = ATTN_HEADS * HEAD_DIM
ATTN_KV = ATTN_KV_HEADS * HEAD_DIM
ROT_DIM = HEAD_DIM // 4
ROPE_THETA = 500000.0
MOBA_BLOCK = 256
MOBA_TOPK = 3
N_EGROUPS = 4
EXPERTS_PER_GROUP = 4
N_EXPERTS = N_EGROUPS * EXPERTS_PER_GROUP
D_EXPERT = D_MODEL // 2
EPS = 1e-6

LANES = 128
SUBLANES = 8
VMEM_CAP = 56 << 20
NEG = -1e30
GRP = ATTN_HEADS // ATTN_KV_HEADS
ROUTE_OFF = N_EGROUPS

_SEG = {}
_off = 0
for _name, _w in (("z", SSD_INNER), ("xbc", CONV_DIM), ("q", ATTN_Q), ("k", ATTN_KV), ("v", ATTN_KV),
                  ("gs", D_MODEL), ("ga", D_MODEL), ("dt", LANES)):
    _SEG[_name] = (_off, _off + _w)
    _off += _w
IN_PAD = _off


def _cparams(semantics, vmem_bytes):
    return pltpu.CompilerParams(dimension_semantics=semantics,
                                vmem_limit_bytes=int(min(max(vmem_bytes, 16 << 20), VMEM_CAP)))


def _dot(a, b):
    return jnp.dot(a, b, preferred_element_type=F32)


def _dot_nt(a, b):
    return lax.dot_general(a, b, (((1,), (1,)), ((), ())), preferred_element_type=F32)


def _dot_exact(a, b):
    return jnp.dot(a, b, preferred_element_type=F32, precision=HIGHEST)


def _silu(x):
    return x * jax.nn.sigmoid(x)


def _mod_kernel(c_ref, w_ref, b_ref, o_ref):
    o_ref[...] = _dot_exact(_silu(c_ref[...]), w_ref[...]) + b_ref[...]


def _ada_mod(c_all, w_ada, b_ada):
    rows = c_all.shape[0]
    return pl.pallas_call(
        _mod_kernel,
        grid=(6,),
        in_specs=[pl.BlockSpec((rows, D_MODEL), lambda j: (0, 0)),
                  pl.BlockSpec((D_MODEL, D_MODEL), lambda j: (0, j)),
                  pl.BlockSpec((1, D_MODEL), lambda j: (0, j))],
        out_specs=pl.BlockSpec((rows, D_MODEL), lambda j: (0, j)),
        out_shape=jax.ShapeDtypeStruct((rows, 6 * D_MODEL), F32),
        compiler_params=_cparams(("arbitrary",), 24 << 20),
        name="ada_mod",
    )(c_all, w_ada, b_ada.reshape(1, -1))


def _rope_tables(pos):
    half = ROT_DIM // 2
    c = -2.0 * math.log(ROPE_THETA) / ROT_DIM
    c_hi = float(np.float32(c))
    j = jnp.arange(half, dtype=F32)
    inv_freq = jnp.exp(j * c_hi + j * (c - c_hi))
    ang = jnp.asarray(pos).astype(F32)[:, None] * inv_freq[None, :]
    cos, sin = jnp.cos(ang), jnp.sin(ang)
    n = pos.shape[0]
    one = jnp.ones((n, HEAD_DIM - ROT_DIM), F32)
    zero = jnp.zeros((n, half), F32)
    rest = jnp.zeros((n, HEAD_DIM - ROT_DIM), F32)
    a = jnp.concatenate([cos, cos, one], axis=1)
    s1 = jnp.concatenate([-sin, zero, rest], axis=1)
    s2 = jnp.concatenate([zero, sin, rest], axis=1)
    rep = LANES // HEAD_DIM
    return tuple(jnp.tile(t, (1, rep)) for t in (a, s1, s2))


def _inproj_kernel(*refs, attn_layout):
    (x_ref, sc_ref, sh_ref, nw_ref, w_ref, ra_ref, rs1_ref, rs2_ref,
     z_ref, xbc_ref, dt_ref, q_ref, k_ref, v_ref, gs_ref, ga_ref) = refs[:16]
    x = x_ref[0]
    h = x * lax.rsqrt(jnp.mean(x * x, axis=-1, keepdims=True) + EPS) * nw_ref[...]
    h = (h * (1.0 + sc_ref[0]) + sh_ref[0]).astype(BF16)

    def proj(name, out_ref):
        lo, hi = _SEG[name]
        step = 512
        for c in range(lo, hi, step):
            w = min(step, hi - c)
            out_ref[0, :, c - lo:c - lo + w] = _dot(h, w_ref[:, c:c + w])

    proj("z", z_ref)
    proj("xbc", xbc_ref)
    proj("gs", gs_ref)
    proj("ga", ga_ref)
    proj("dt", dt_ref)

    def rope(t):
        width = t.shape[1]
        reps = width // LANES
        a = jnp.concatenate([ra_ref[...]] * reps, axis=1)
        s1 = jnp.concatenate([rs1_ref[...]] * reps, axis=1)
        s2 = jnp.concatenate([rs2_ref[...]] * reps, axis=1)
        return t * a + pltpu.roll(t, width - ROT_DIM // 2, 1) * s1 + pltpu.roll(t, ROT_DIM // 2, 1) * s2

    lo, hi = _SEG["q"]
    q_ref[0] = rope(_dot(h, w_ref[:, lo:hi]))
    lo, hi = _SEG["k"]
    k = rope(_dot(h, w_ref[:, lo:hi]))
    k_ref[0] = k
    lo, hi = _SEG["v"]
    v = _dot(h, w_ref[:, lo:hi])
    v_ref[0] = v

    if attn_layout:
        kmean_ref, kaug_ref, vaug_ref = refs[16:19]
        tm = k.shape[0]
        blocks = tm // MOBA_BLOCK
        for hb in range(blocks):
            kmean_ref[0, hb] = jnp.mean(k[hb * MOBA_BLOCK:(hb + 1) * MOBA_BLOCK], axis=0, keepdims=True)
        lane = lax.broadcasted_iota(jnp.int32, (tm, HEAD_DIM), 1)
        row = lax.broadcasted_iota(jnp.int32, (tm, HEAD_DIM), 0)
        blk = blocks * pl.program_id(1)
        for hb in range(1, blocks):
            blk = blk + (row >= hb * MOBA_BLOCK).astype(jnp.int32)
        blk_onehot = (lane == blk).astype(F32)
        ones_col = (lane == 0).astype(F32)
        for kh in range(ATTN_KV_HEADS):
            sl = slice(kh * HEAD_DIM, (kh + 1) * HEAD_DIM)
            kaug_ref[0, kh] = jnp.concatenate([k[:, sl], blk_onehot], axis=1).astype(BF16)
            vaug_ref[0, kh] = jnp.concatenate([v[:, sl], ones_col], axis=1).astype(BF16)


def _in_proj(x, sc1, sh1, norm_w, w_in_r, tables, attn_layout):
    bsz, seqlen, _ = x.shape
    tm = 2 * MOBA_BLOCK if seqlen % (2 * MOBA_BLOCK) == 0 else MOBA_BLOCK
    assert seqlen % tm == 0
    nt = seqlen // tm
    blocks = tm // MOBA_BLOCK
    mod_rows = sc1.shape[1]
    if mod_rows == 1:
        mod_spec = pl.BlockSpec((1, 1, D_MODEL), lambda b, i: (b, 0, 0))
    else:
        mod_spec = pl.BlockSpec((1, tm, D_MODEL), lambda b, i: (b, i, 0))
    tab_spec = pl.BlockSpec((tm, LANES), lambda b, i: (i, 0))

    def tok(width):
        return pl.BlockSpec((1, tm, width), lambda b, i: (b, i, 0))

    def shp(width):
        return jax.ShapeDtypeStruct((bsz, seqlen, width), F32)

    widths = (SSD_INNER, CONV_DIM, LANES, ATTN_Q, ATTN_KV, ATTN_KV, D_MODEL, D_MODEL)
    out_specs = [tok(w) for w in widths]
    out_shape = [shp(w) for w in widths]
    if attn_layout:
        assert nt * blocks <= HEAD_DIM
        out_specs += [pl.BlockSpec((1, blocks, 1, ATTN_KV), lambda b, i: (b, i, 0, 0)),
                      pl.BlockSpec((1, ATTN_KV_HEADS, tm, LANES), lambda b, i: (b, 0, i, 0)),
                      pl.BlockSpec((1, ATTN_KV_HEADS, tm, LANES), lambda b, i: (b, 0, i, 0))]
        out_shape += [jax.ShapeDtypeStruct((bsz, nt * blocks, 1, ATTN_KV), F32),
                      jax.ShapeDtypeStruct((bsz, ATTN_KV_HEADS, seqlen, LANES), BF16),
                      jax.ShapeDtypeStruct((bsz, ATTN_KV_HEADS, seqlen, LANES), BF16)]
    vmem = D_MODEL * IN_PAD * 2 + 2 * tm * (D_MODEL + IN_PAD + 4 * LANES) * 4 + (10 << 20)
    return pl.pallas_call(
        functools.partial(_inproj_kernel, attn_layout=attn_layout),
        grid=(bsz, nt),
        in_specs=[tok(D_MODEL), mod_spec, mod_spec,
                  pl.BlockSpec((1, D_MODEL), lambda b, i: (0, 0)),
                  pl.BlockSpec((D_MODEL, IN_PAD), lambda b, i: (0, 0), pipeline_mode=pl.Buffered(1)),
                  tab_spec, tab_spec, tab_spec],
        out_specs=out_specs,
        out_shape=out_shape,
        compiler_params=_cparams(("parallel", "parallel"), vmem),
        name="in_proj",
    )(x, sc1, sh1, norm_w.reshape(1, -1), w_in_r, *tables)


def _cumsum_rows(a):
    row = lax.broadcasted_iota(jnp.int32, a.shape, 0)
    s = 1
    while s < a.shape[0]:
        a = a + jnp.where(row >= s, pltpu.roll(a, s, 0), 0.0)
        s *= 2
    return a


def _ssd_kernel(xbc_ref, z_ref, dt_ref, cbuf_ref, s0_ref, cw_ref, cb_ref, dtb_ref, alog_ref, dsk_ref, nw_ref,
                y_ref, cnew_ref, sout_ref, xext, st, ysc, *, rows):
    ch = SSD_CHUNK
    pad = SUBLANES
    c = pl.program_id(1)
    last = pl.num_programs(1) - 1

    @pl.when(c == 0)
    def _():
        xext[0:pad, :] = jnp.zeros((pad, CONV_DIM), F32)
        xext[pad - (SSD_CONV - 1):pad, :] = cbuf_ref[0]
        st[...] = s0_ref[0].T

    xext[pad:pad + rows, :] = xbc_ref[0]
    if rows < ch:
        xext[pad + rows:pad + ch, :] = jnp.zeros((ch - rows, CONV_DIM), F32)

    conv = cb_ref[...] + xext[pad:pad + ch, :] * cw_ref[SSD_CONV - 1:SSD_CONV, :]
    for w in range(SSD_CONV - 1):
        off = pad - (SSD_CONV - 1) + w
        conv = conv + xext[off:off + ch, :] * cw_ref[w:w + 1, :]
    act = _silu(conv)

    @pl.when(c == last)
    def _():
        cnew_ref[0] = xext[pad + rows - (SSD_CONV - 1):pad + rows, :]

    xext[0:pad, :] = xext[ch:ch + pad, :]

    xs = act[:, :SSD_INNER]
    dt_raw = dt_ref[0]
    if rows < ch:
        dt_raw = jnp.concatenate([dt_raw, jnp.zeros((ch - rows, LANES), F32)], axis=0)
    dt_in = dt_raw + dtb_ref[...]
    dt = jnp.maximum(dt_in, 0.0) + jnp.log1p(jnp.exp(-jnp.abs(dt_in)))
    row = lax.broadcasted_iota(jnp.int32, (ch, LANES), 0)
    col = lax.broadcasted_iota(jnp.int32, (ch, LANES), 1)
    if rows < ch:
        dt = jnp.where(row < rows, dt, 0.0)
    a_cs = _cumsum_rows(dt * (-jnp.exp(alog_ref[...])))
    a_cs_t = a_cs.T
    dt_t = dt.T
    causal = row >= col

    heads_per_pair = LANES // SSD_HEADDIM
    assert heads_per_pair == 2 and SSD_STATE == ch
    first_of_pair = col < SSD_HEADDIM
    for g in range(SSD_GROUPS):
        bg = act[:, SSD_INNER + g * SSD_STATE:SSD_INNER + (g + 1) * SSD_STATE]
        cg = act[:, SSD_INNER + SSD_BC + g * SSD_STATE:SSD_INNER + SSD_BC + (g + 1) * SSD_STATE]
        gmat = _dot_nt(cg.astype(BF16), bg.astype(BF16))
        bg_t = bg.T
        for pair in range(SSD_HEADS // SSD_GROUPS // heads_per_pair):
            hd0 = g * (SSD_HEADS // SSD_GROUPS) + pair * heads_per_pair
            sl = slice(hd0 * SSD_HEADDIM, hd0 * SSD_HEADDIM + LANES)
            x_pair = xs[:, sl]
            s_pair = st[:, sl]
            x_diag = [jnp.where(first_of_pair, x_pair, 0.0).astype(BF16),
                      jnp.where(first_of_pair, 0.0, x_pair).astype(BF16)]
            s_diag = [jnp.where(first_of_pair, s_pair, 0.0).astype(BF16),
                      jnp.where(first_of_pair, 0.0, s_pair).astype(BF16)]
            m_parts, c_parts, b_parts, decay = [], [], [], []
            for hd in range(hd0, hd0 + heads_per_pair):
                acs_col = a_cs[:, hd:hd + 1]
                acs_row = a_cs_t[hd:hd + 1, :]
                dt_row = dt_t[hd:hd + 1, :]
                acs_last = a_cs_t[hd:hd + 1, ch - 1:ch]
                lmat = jnp.where(causal, jnp.exp(acs_col - acs_row), 0.0)
                m_parts.append((gmat * lmat * dt_row).astype(BF16))
                c_parts.append((cg * jnp.exp(acs_col)).astype(BF16))
                b_parts.append((bg_t * (jnp.exp(acs_last - acs_row) * dt_row)).astype(BF16))
                decay.append(jnp.exp(acs_last))
            ysc[:, sl] = _dot(jnp.concatenate(m_parts + c_parts, axis=1), jnp.concatenate(x_diag + s_diag, axis=0))
            st[:, sl] = (jnp.where(first_of_pair[:1], decay[0], decay[1]) * s_pair
                         + _dot(jnp.concatenate(b_parts, axis=1), jnp.concatenate(x_diag, axis=0)))

    y = (ysc[0:rows, :] + xs[:rows] * dsk_ref[...]) * _silu(z_ref[0])
    gw = SSD_INNER // SSD_GROUPS
    for g in range(SSD_GROUPS):
        yg = y[:, g * gw:(g + 1) * gw]
        yg = yg * lax.rsqrt(jnp.mean(yg * yg, axis=-1, keepdims=True) + EPS)
        y_ref[0, :, g * gw:(g + 1) * gw] = yg * nw_ref[:, g * gw:(g + 1) * gw]

    @pl.when(c == last)
    def _():
        sout_ref[0] = st[...].T


def _ssd(xbc, z, dt, conv_buf, ssm_state, conv_w, conv_b, dt_bias, a_log, d_skip, norm_w):
    bsz, seqlen, _ = xbc.shape
    rows = SSD_CHUNK if seqlen % SSD_CHUNK == 0 else seqlen
    assert rows == seqlen or rows == SSD_CHUNK
    assert rows % SUBLANES == 0 and rows >= SSD_CONV - 1
    nc = seqlen // rows
    hp = SSD_HEADS * SSD_HEADDIM

    def tok(width):
        return pl.BlockSpec((1, rows, width), lambda b, c: (b, c, 0))

    def per_b(d1, d2):
        return pl.BlockSpec((1, d1, d2), lambda b, c: (b, 0, 0))

    def vec(width, r=1):
        return pl.BlockSpec((r, width), lambda b, c: (0, 0))

    def pad_lanes(v):
        return jnp.pad(v.astype(F32), (0, LANES - v.shape[0])).reshape(1, LANES)

    y, conv_new, s_new = pl.pallas_call(
        functools.partial(_ssd_kernel, rows=rows),
        grid=(bsz, nc),
        in_specs=[tok(CONV_DIM), tok(SSD_INNER), tok(LANES),
                  per_b(SSD_CONV - 1, CONV_DIM), per_b(hp, SSD_STATE),
                  vec(CONV_DIM, SSD_CONV), vec(CONV_DIM), vec(LANES), vec(LANES), vec(SSD_INNER), vec(SSD_INNER)],
        out_specs=[tok(SSD_INNER), per_b(SSD_CONV - 1, CONV_DIM), per_b(hp, SSD_STATE)],
        out_shape=[jax.ShapeDtypeStruct((bsz, seqlen, SSD_INNER), F32),
                   jax.ShapeDtypeStruct((bsz, SSD_CONV - 1, CONV_DIM), F32),
                   jax.ShapeDtypeStruct((bsz, hp, SSD_STATE), F32)],
        scratch_shapes=[pltpu.VMEM((SSD_CHUNK + SUBLANES, CONV_DIM), F32),
                        pltpu.VMEM((SSD_STATE, hp), F32),
                        pltpu.VMEM((SSD_CHUNK, SSD_INNER), F32)],
        compiler_params=_cparams(("parallel", "arbitrary"), 40 << 20),
        name="ssd",
    )(xbc, z, dt, conv_buf, ssm_state.reshape(bsz, hp, SSD_STATE),
      conv_w, conv_b.reshape(1, -1), pad_lanes(dt_bias), pad_lanes(a_log),
      jnp.repeat(d_skip.astype(F32), SSD_HEADDIM).reshape(1, -1), norm_w.reshape(1, -1))
    return y, conv_new, s_new.reshape(bsz, SSD_HEADS, SSD_HEADDIM, SSD_STATE)


def _select_topk(scores, valid, axis):
    sc = scores if valid is None else jnp.where(valid, scores, -jnp.inf)
    sel = jnp.zeros(scores.shape, F32)
    pos_f = lax.broadcasted_iota(jnp.int32, scores.shape, axis).astype(F32)
    for _ in range(MOBA_TOPK):
        m = jnp.max(sc, axis=axis, keepdims=True)
        idx = jnp.min(jnp.where(sc == m, pos_f, float(1 << 20)), axis=axis, keepdims=True)
        idx = jnp.where(m > -jnp.inf, idx, -1.0)
        hit = pos_f == idx
        sel = jnp.where(hit, 1.0, sel)
        sc = jnp.where(hit, -jnp.inf, sc)
    return sel > 0.5


def _moba_prompt_kernel(q_ref, km_ref, kaug_ref, vaug_ref, o_ref, s_a, s_b, *, qblocks):
    j = pl.program_id(2)
    tq = MOBA_BLOCK
    chunk = qblocks * tq
    half = GRP * tq
    q2 = q_ref[0]
    qs = jnp.concatenate([q2[b * tq:(b + 1) * tq, g * HEAD_DIM:(g + 1) * HEAD_DIM]
                          for b in range(qblocks) for g in range(GRP)], axis=0)
    rows = qs.shape[0]
    lane = lax.broadcasted_iota(jnp.int32, (rows, LANES), 1)
    scores_t = lax.dot_general(km_ref[0, 0], qs, (((1,), (1,)), ((), ())), preferred_element_type=F32,
                               precision=HIGHEST)
    nbp = scores_t.shape[0]
    blk = lax.broadcasted_iota(jnp.int32, scores_t.shape, 0)
    col = lax.broadcasted_iota(jnp.int32, scores_t.shape, 1)
    own = qblocks * j
    for b in range(1, qblocks):
        own = own + (col >= b * half).astype(jnp.int32)
    sel_t = _select_topk(scores_t, blk < own, 0)
    pen_t = jnp.concatenate([jnp.zeros((HEAD_DIM, rows), F32), jnp.where(sel_t, 0.0, NEG),
                             jnp.full((LANES - HEAD_DIM - nbp, rows), NEG, F32)], axis=0)
    pen = pen_t.T
    q_pad = jnp.concatenate([qs * (HEAD_DIM ** -0.5), jnp.zeros((rows, LANES - HEAD_DIM), F32)], axis=1)
    q_aug = jnp.where(lane < HEAD_DIM, q_pad, pen).astype(BF16)
    q_own = q_pad.astype(BF16)

    def update(m, acc, s, pv):
        m_new = jnp.maximum(m, jnp.max(s, axis=1, keepdims=True))
        p = jnp.exp(s - m_new)
        acc = jnp.exp(m - m_new) * acc + pv(p.astype(BF16))
        return m_new, acc

    total_chunks = kaug_ref.shape[2] // chunk
    npairs = (j + 2) // 2

    def logits(c):
        start = pl.multiple_of(jnp.minimum(c, total_chunks - 1) * chunk, chunk)
        return _dot_nt(q_aug, kaug_ref[0, 0, pl.ds(start, chunk), :])

    def values(c):
        vb = vaug_ref[0, 0, pl.ds(pl.multiple_of(c * chunk, chunk), chunk), :]
        return lambda p: _dot(p, vb)

    def body(t, carry):
        m, acc = carry
        c0 = 2 * t
        s_b[...] = logits(c0 + 1)
        m, acc = update(m, acc, s_a[...], values(c0))
        s_a[...] = logits(c0 + 2)
        m, acc = update(m, acc, s_b[...], values(c0 + 1))
        return m, acc

    def own_rows(ref, b):
        return ref[0, 0, pl.ds(pl.multiple_of((qblocks * j + b) * tq, tq), tq), :]

    s_own = jnp.concatenate([_dot_nt(q_own[b * half:(b + 1) * half], own_rows(kaug_ref, b))
                             for b in range(qblocks)], axis=0)
    r = lax.broadcasted_iota(jnp.int32, (rows, tq), 0)
    cidx = lax.broadcasted_iota(jnp.int32, (rows, tq), 1)
    s_own = jnp.where(cidx <= (r & (tq - 1)), s_own, NEG)

    def own_values(p):
        return jnp.concatenate([_dot(p[b * half:(b + 1) * half], own_rows(vaug_ref, b))
                                for b in range(qblocks)], axis=0)

    s_a[...] = logits(0)
    init = (jnp.full((rows, 1), -jnp.inf, F32), jnp.zeros((rows, LANES), F32))
    m, acc = lax.fori_loop(0, npairs, body, init)
    _, acc = update(m, acc, s_own, own_values)
    out = acc[:, :HEAD_DIM] / acc[:, HEAD_DIM:HEAD_DIM + 1]
    o_ref[0] = jnp.concatenate(
        [jnp.concatenate([out[(b * GRP + g) * tq:(b * GRP + g + 1) * tq] for g in range(GRP)], axis=1)
         for b in range(qblocks)], axis=0)


def _moba_prompt(q, kmean, kaug, vaug):
    bsz, seqlen, _ = q.shape
    nb = seqlen // MOBA_BLOCK
    assert GRP * HEAD_DIM == LANES and nb <= LANES - HEAD_DIM and MOBA_BLOCK & (MOBA_BLOCK - 1) == 0
    nbp = -(-nb // SUBLANES) * SUBLANES
    km = kmean.reshape(bsz, nb, ATTN_KV_HEADS, HEAD_DIM).transpose(0, 2, 1, 3)
    km = jnp.pad(km, ((0, 0), (0, 0), (0, nbp - nb), (0, 0)))
    kv_spec = pl.BlockSpec((1, 1, seqlen, LANES), lambda b, h, i: (b, h, 0, 0))
    qblocks = 4
    chunk = qblocks * MOBA_BLOCK
    assert seqlen % (2 * chunk) == 0, "the sweep walks the keys two chunks at a time"
    rows = qblocks * GRP * MOBA_BLOCK
    vmem = 2 * 2 * seqlen * LANES * 2 + 2 * rows * chunk * 4 + (24 << 20)
    return pl.pallas_call(
        functools.partial(_moba_prompt_kernel, qblocks=qblocks),
        scratch_shapes=[pltpu.VMEM((rows, chunk), F32), pltpu.VMEM((rows, chunk), F32)],
        grid=(bsz, ATTN_KV_HEADS, nb // qblocks),
        in_specs=[pl.BlockSpec((1, chunk, LANES), lambda b, h, i: (b, i, h)),
                  pl.BlockSpec((1, 1, nbp, HEAD_DIM), lambda b, h, i: (b, h, 0, 0)),
                  kv_spec, kv_spec],
        out_specs=pl.BlockSpec((1, chunk, LANES), lambda b, h, i: (b, i, h)),
        out_shape=jax.ShapeDtypeStruct((bsz, seqlen, ATTN_Q), F32),
        compiler_params=_cparams(("parallel", "parallel", "arbitrary"), vmem),
        name="moba_prompt",
    )(q, km, kaug, vaug)


def _moba_sample_kernel(pt_ref, qbt_ref, qb_ref, *refs, pages_per_step, nsteps, nblk, dec_seq):
    del pt_ref
    k_refs = refs[:pages_per_step]
    v_refs = refs[pages_per_step:2 * pages_per_step]
    kn_ref, vn_ref, o_ref, kt_sc, sct_sc, pen_sc, m_sc, l_sc, acc_sc = refs[2 * pages_per_step:]
    step = pl.program_id(1)
    rows = qb_ref.shape[1]
    pages_per_block = MOBA_BLOCK // LANES
    blocks_per_step = pages_per_step // pages_per_block

    @pl.when(step < nsteps)
    def _():
        qbt = qbt_ref[0]
        for jb in range(blocks_per_step):
            ksum = None
            for j in range(pages_per_block):
                idx = jb * pages_per_block + j
                kpage = k_refs[idx][0]
                kt_sc[step * pages_per_step + idx] = kpage.astype(BF16)
                ksum = kpage if ksum is None else ksum + kpage
            kmean = jnp.sum(ksum, axis=1, keepdims=True) * (1.0 / MOBA_BLOCK)
            sct_sc[pl.ds(step * blocks_per_step + jb, 1), :] = jnp.sum(qbt * kmean, axis=0, keepdims=True)

    @pl.when(step >= nsteps)
    def _():
        n = step - nsteps
        lane = lax.broadcasted_iota(jnp.int32, (rows, LANES), 1)

        @pl.when(n == 0)
        def _():
            sel_t = _select_topk(sct_sc[...], None, 0)
            pen_t = jnp.concatenate([jnp.where(sel_t, 0.0, NEG), jnp.full((LANES - nblk, LANES), NEG, F32)],
                                    axis=0)
            pen_sc[...] = pen_t.T[:rows]
            m_sc[...] = jnp.full(m_sc.shape, -jnp.inf, F32)
            l_sc[...] = jnp.zeros(l_sc.shape, F32)
            acc_sc[...] = jnp.zeros(acc_sc.shape, F32)

        qb = (qb_ref[0] * (HEAD_DIM ** -0.5)).astype(BF16)

        def update(s, pv_fn):
            m = m_sc[...]
            m_new = jnp.maximum(m, jnp.max(s, axis=1, keepdims=True))
            p = jnp.exp(s - m_new)
            alpha = jnp.exp(m - m_new)
            l_sc[...] = alpha * l_sc[...] + jnp.sum(p, axis=1, keepdims=True)
            acc_sc[...] = alpha * acc_sc[...] + pv_fn(p.astype(BF16))
            m_sc[...] = m_new

        pen = pen_sc[...]
        pen_cols = []
        for jb in range(blocks_per_step):
            pen_col = jnp.sum(jnp.where(lane == n * blocks_per_step + jb, pen, 0.0), axis=1, keepdims=True)
            pen_cols.append(jnp.broadcast_to(pen_col, (rows, MOBA_BLOCK)))
        kt_all = jnp.concatenate([kt_sc[n * pages_per_step + j] for j in range(pages_per_step)], axis=1)
        vt_all = jnp.concatenate([v_ref[0].astype(BF16) for v_ref in v_refs], axis=1)
        update(_dot(qb, kt_all) + jnp.concatenate(pen_cols, axis=1), lambda p: _dot_nt(p, vt_all))

        @pl.when(n == nsteps - 1)
        def _():
            zpad = jnp.zeros((LANES - dec_seq, kn_ref.shape[2]), F32)
            kn = jnp.concatenate([kn_ref[0], zpad], axis=0).astype(BF16)
            vn = jnp.concatenate([vn_ref[0], zpad], axis=0).astype(BF16)
            s_own = _dot_nt(qb, kn)
            r = lax.broadcasted_iota(jnp.int32, (rows, LANES), 0)
            qpos = r & (dec_seq - 1)
            s_own = jnp.where((lane <= qpos) & (lane < dec_seq), s_own, NEG)
            update(s_own, lambda p: _dot(p, vn))
            o_ref[0] = acc_sc[...] / l_sc[...]


def _moba_sample(q, k_new, v_new, cache_kt, cache_vt, page_table, past_len):
    bsz, dec_seq, _ = q.shape
    page = cache_kt.shape[2]
    assert page == LANES and MOBA_BLOCK % page == 0
    assert past_len % MOBA_BLOCK == 0 and dec_seq == SUBLANES
    nblk = past_len // MOBA_BLOCK
    assert MOBA_TOPK <= nblk <= LANES and nblk % SUBLANES == 0
    n_pages = past_len // page
    pages_per_step = 32
    while n_pages % pages_per_step:
        pages_per_step //= 2
    assert pages_per_step * page >= MOBA_BLOCK
    nsteps = n_pages // pages_per_step
    rows = ATTN_HEADS * dec_seq
    q4 = q.reshape(bsz, dec_seq, ATTN_HEADS, HEAD_DIM).transpose(0, 2, 1, 3)
    own = (np.arange(ATTN_HEADS)[:, None] // GRP == np.arange(ATTN_KV_HEADS)[None, :]).astype(np.float32)
    qb = (q4[:, :, :, None, :] * jnp.asarray(own)[None, :, None, :, None]).reshape(bsz, rows, ATTN_KV)
    qbt = jnp.pad(qb.transpose(0, 2, 1), ((0, 0), (0, 0), (0, LANES - rows)))

    def k_spec(j):
        return pl.BlockSpec((1, ATTN_KV, page),
                            lambda b, s, pt: (pt[b, pages_per_step * jnp.minimum(s, nsteps - 1) + j], 0, 0))

    def v_spec(j):
        return pl.BlockSpec((1, ATTN_KV, page),
                            lambda b, s, pt: (pt[b, pages_per_step * jnp.maximum(s - nsteps, 0) + j], 0, 0))

    def per_b(d1, d2):
        return pl.BlockSpec((1, d1, d2), lambda b, s, pt: (b, 0, 0))

    out = pl.pallas_call(
        functools.partial(_moba_sample_kernel, pages_per_step=pages_per_step, nsteps=nsteps, nblk=nblk,
                          dec_seq=dec_seq),
        grid_spec=pltpu.PrefetchScalarGridSpec(
            num_scalar_prefetch=1, grid=(bsz, 2 * nsteps),
            in_specs=[per_b(ATTN_KV, LANES), per_b(rows, ATTN_KV)]
                     + [k_spec(j) for j in range(pages_per_step)] + [v_spec(j) for j in range(pages_per_step)]
                     + [per_b(dec_seq, ATTN_KV), per_b(dec_seq, ATTN_KV)],
            out_specs=per_b(rows, ATTN_KV),
            scratch_shapes=[pltpu.VMEM((n_pages, ATTN_KV, page), BF16), pltpu.VMEM((nblk, LANES), F32),
                            pltpu.VMEM((rows, LANES), F32), pltpu.VMEM((rows, 1), F32), pltpu.VMEM((rows, 1), F32),
                            pltpu.VMEM((rows, ATTN_KV), F32)]),
        out_shape=jax.ShapeDtypeStruct((bsz, rows, ATTN_KV), F32),
        compiler_params=_cparams(("parallel", "arbitrary"), 32 << 20),
        name="moba_sample_attn",
    )(page_table, qbt, qb, *([cache_kt] * pages_per_step), *([cache_vt] * pages_per_step), k_new, v_new)
    o5 = out.reshape(bsz, ATTN_HEADS, dec_seq, ATTN_KV_HEADS, HEAD_DIM)
    o4 = jnp.sum(o5 * jnp.asarray(own)[None, :, None, :, None], axis=3)
    return o4.transpose(0, 2, 1, 3).reshape(bsz, dec_seq, ATTN_Q)


def _mix_kernel(x_ref, ys_ref, ya_ref, gs_ref, ga_ref, gt1_ref, sc2_ref, sh2_ref, nw_ref,
                wso_ref, wao_ref, wo_ref, wrt_ref, brt_ref, x1_ref, hc_ref):
    merged = (jax.nn.sigmoid(gs_ref[0]) * _dot(ys_ref[0].astype(BF16), wso_ref[...])
              + jax.nn.sigmoid(ga_ref[0]) * _dot(ya_ref[0].astype(BF16), wao_ref[...]))
    x1 = x_ref[0] + gt1_ref[0] * _dot(merged.astype(BF16), wo_ref[...])
    x1_ref[0] = x1
    h2 = x1 * lax.rsqrt(jnp.mean(x1 * x1, axis=-1, keepdims=True) + EPS) * nw_ref[...]
    h2 = h2 * (1.0 + sc2_ref[0]) + sh2_ref[0]
    hc_ref[0, :, :D_MODEL] = h2

    h_hi = h2.astype(BF16)
    h_lo = (h2 - h_hi.astype(F32)).astype(BF16)
    hw = _dot(h_hi, wrt_ref[...])
    lg = hw[:, :LANES] + hw[:, LANES:] + _dot(h_lo, wrt_ref[:, :LANES]) + brt_ref[...]
    lane = lax.broadcasted_iota(jnp.int32, lg.shape, 1).astype(F32)
    big = float(1 << 20)
    ninf = -jnp.inf
    gl = jnp.where(lane < N_EGROUPS, lg, ninf)
    gmax = jnp.max(gl, axis=1, keepdims=True)
    gidx = jnp.min(jnp.where(gl == gmax, lane, big), axis=1, keepdims=True)
    g_w = 1.0 / jnp.sum(jnp.exp(gl - gmax), axis=1, keepdims=True)
    lo = ROUTE_OFF + gidx * EXPERTS_PER_GROUP
    el = jnp.where((lane >= lo) & (lane < lo + EXPERTS_PER_GROUP), lg, ninf)
    m1 = jnp.max(el, axis=1, keepdims=True)
    i1 = jnp.min(jnp.where(el == m1, lane, big), axis=1, keepdims=True)
    el2 = jnp.where(lane == i1, ninf, el)
    m2 = jnp.max(el2, axis=1, keepdims=True)
    i2 = jnp.min(jnp.where(el2 == m2, lane, big), axis=1, keepdims=True)
    e2 = jnp.exp(m2 - m1)
    den = 1.0 / (1.0 + e2)
    comb = jnp.where(lane == i1, den * g_w, 0.0) + jnp.where(lane == i2, e2 * den * g_w, 0.0)
    hc_ref[0, :, D_MODEL:] = jnp.where(lane == 0.0, gidx, comb)


def _mix(x, y_ssd, y_att, gs, ga, gt1, sc2, sh2, norm_w, wso, wao, wo, wrt, brt):
    bsz, seqlen, _ = x.shape
    tm = 512 if seqlen % 512 == 0 else 256
    assert seqlen % tm == 0
    nt = seqlen // tm
    mod_rows = gt1.shape[1]
    if mod_rows == 1:
        mod_spec = pl.BlockSpec((1, 1, D_MODEL), lambda b, i: (b, 0, 0))
    else:
        mod_spec = pl.BlockSpec((1, tm, D_MODEL), lambda b, i: (b, i, 0))

    def tok(width):
        return pl.BlockSpec((1, tm, width), lambda b, i: (b, i, 0))

    def full(d1, d2):
        return pl.BlockSpec((d1, d2), lambda b, i: (0, 0))

    vmem = 2 * 2 * (2 * D_MODEL * D_MODEL + ATTN_Q * D_MODEL) + 2 * tm * (8 * D_MODEL) * 4 + (12 << 20)
    return pl.pallas_call(
        _mix_kernel,
        grid=(bsz, nt),
        in_specs=[tok(D_MODEL), tok(SSD_INNER), tok(ATTN_Q), tok(D_MODEL), tok(D_MODEL),
                  mod_spec, mod_spec, mod_spec, full(1, D_MODEL),
                  full(SSD_INNER, D_MODEL), full(ATTN_Q, D_MODEL), full(D_MODEL, D_MODEL),
                  full(D_MODEL, 2 * LANES), full(1, LANES)],
        out_specs=[tok(D_MODEL), tok(D_MODEL + LANES)],
        out_shape=[jax.ShapeDtypeStruct((bsz, seqlen, D_MODEL), F32),
                   jax.ShapeDtypeStruct((bsz, seqlen, D_MODEL + LANES), F32)],
        compiler_params=_cparams(("parallel", "parallel"), vmem),
        name="mix",
    )(x, y_ssd, y_att, gs, ga, gt1, sc2, sh2, norm_w.reshape(1, -1), wso, wao, wo, wrt, brt)


def _expert_step(h2, comb, expert, wg_ref, wu_ref, wd_ref):
    lane = lax.broadcasted_iota(jnp.int32, comb.shape, 1)
    cw = jnp.sum(jnp.where(lane == expert + ROUTE_OFF, comb, 0.0), axis=1, keepdims=True)
    a = _dot(h2, wg_ref[0])
    u = _dot(h2, wu_ref[0])
    act = (_silu(a) * u * cw).astype(BF16)
    return _dot(act, wd_ref[0])


def _final_norm(x1, gt2, ffn, fw):
    x2 = x1 + gt2 * ffn
    return x2 * lax.rsqrt(jnp.mean(x2 * x2, axis=-1, keepdims=True) + EPS) * fw


def _moe_kernel(hc_ref, x1_ref, gt2_ref, fw_ref, wg_ref, wu_ref, wd_ref, y_ref, acc):
    e = pl.program_id(2)

    @pl.when(e == 0)
    def _():
        acc[...] = jnp.zeros(acc.shape, F32)

    acc[...] += _expert_step(hc_ref[0, :, :D_MODEL].astype(BF16), hc_ref[0, :, D_MODEL:], e,
                             wg_ref, wu_ref, wd_ref)

    @pl.when(e == pl.num_programs(2) - 1)
    def _():
        y_ref[0] = _final_norm(x1_ref[0], gt2_ref[0], acc[...], fw_ref[...])


def _moe(hc, x1, gt2, final_w, wg, wu, wd):
    bsz, seqlen, _ = x1.shape
    tm = 1024 if seqlen % 1024 == 0 else 256
    assert seqlen % tm == 0
    nt = seqlen // tm
    mod_rows = gt2.shape[1]
    if mod_rows == 1:
        mod_spec = pl.BlockSpec((1, 1, D_MODEL), lambda b, i, e: (b, 0, 0))
    else:
        mod_spec = pl.BlockSpec((1, tm, D_MODEL), lambda b, i, e: (b, i, 0))

    def tok(width):
        return pl.BlockSpec((1, tm, width), lambda b, i, e: (b, i, 0))

    vmem = 2 * tm * D_MODEL * (2 + 4 + 4) + tm * D_MODEL * 4 + 2 * 3 * D_MODEL * D_EXPERT * 2 + (16 << 20)
    return pl.pallas_call(
        _moe_kernel,
        grid=(bsz, nt, N_EXPERTS),
        in_specs=[tok(D_MODEL + LANES), tok(D_MODEL), mod_spec,
                  pl.BlockSpec((1, D_MODEL), lambda b, i, e: (0, 0)),
                  pl.BlockSpec((1, D_MODEL, D_EXPERT), lambda b, i, e: (e, 0, 0)),
                  pl.BlockSpec((1, D_MODEL, D_EXPERT), lambda b, i, e: (e, 0, 0)),
                  pl.BlockSpec((1, D_EXPERT, D_MODEL), lambda b, i, e: (e, 0, 0))],
        out_specs=tok(D_MODEL),
        out_shape=jax.ShapeDtypeStruct((bsz, seqlen, D_MODEL), F32),
        scratch_shapes=[pltpu.VMEM((tm, D_MODEL), F32)],
        compiler_params=_cparams(("parallel", "parallel", "arbitrary"), vmem),
        name="moe",
    )(hc, x1, gt2, final_w.reshape(1, -1), wg, wu, wd)


ROUTE_TILE = 512
DISPATCH_TILE = 256


def _move_rows(src_of, dst_of, sem, whole_src, whole_dst):
    def body(r, carry):
        pltpu.make_async_copy(src_of(r), dst_of(r), sem).start()
        return carry

    lax.fori_loop(0, DISPATCH_TILE, body, 0, unroll=8)
    pltpu.make_async_copy(whole_src, whole_dst, sem).wait()


def _dispatch_kernel(pos_ref, hc_ref, xs_in, xs_ref, sem):
    del xs_in
    base = pl.program_id(0) * DISPATCH_TILE
    _move_rows(lambda r: hc_ref.at[pl.ds(r, 1), :], lambda r: xs_ref.at[pl.ds(pos_ref[base + r], 1), :],
               sem.at[0], hc_ref, xs_ref.at[pl.ds(0, DISPATCH_TILE), :])


def _moe_routed_kernel(tg_ref, xs_ref, wg_ref, wu_ref, wd_ref, ys_ref, xb, acc):
    t, e = pl.program_id(0), pl.program_id(1)

    @pl.when(e == 0)
    def _():
        xb[...] = xs_ref[:, :D_MODEL].astype(BF16)
        acc[...] = jnp.zeros(acc.shape, F32)

    expert = tg_ref[t] * EXPERTS_PER_GROUP + e
    acc[...] += _expert_step(xb[...], xs_ref[:, D_MODEL:], expert, wg_ref, wu_ref, wd_ref)

    @pl.when(e == pl.num_programs(1) - 1)
    def _():
        ys_ref[...] = acc[...]


def _combine_kernel(pos_ref, x1_ref, gt2_ref, fw_ref, ys_ref, y_ref, ybuf, sem):
    i = pl.program_id(0)
    slot = i % 2

    def request(step, buf):
        base = step * DISPATCH_TILE

        def body(r, carry):
            pltpu.make_async_copy(ys_ref.at[pl.ds(pos_ref[base + r], 1), :], ybuf.at[buf, pl.ds(r, 1), :],
                                  sem.at[buf]).start()
            return carry

        lax.fori_loop(0, DISPATCH_TILE, body, 0, unroll=8)

    @pl.when(i == 0)
    def _():
        request(0, 0)

    @pl.when(i + 1 < pl.num_programs(0))
    def _():
        request(i + 1, 1 - slot)

    pltpu.make_async_copy(ys_ref.at[pl.ds(0, DISPATCH_TILE), :], ybuf.at[slot], sem.at[slot]).wait()
    y_ref[...] = _final_norm(x1_ref[...], gt2_ref[0], ybuf[slot], fw_ref[...])


def _moe_routed(hc, x1, gt2, final_w, wg, wu, wd):
    bsz, seqlen, width = hc.shape
    ntok = bsz * seqlen
    assert seqlen % DISPATCH_TILE == 0 and ntok % ROUTE_TILE == 0 and gt2.shape[1] == 1
    hcf, x1f = hc.reshape(ntok, width), x1.reshape(ntok, D_MODEL)
    gidx = hcf[:, D_MODEL].astype(jnp.int32)
    onehot = (gidx[:, None] == jnp.arange(N_EGROUPS, dtype=jnp.int32)[None, :]).astype(jnp.int32)
    csum = jnp.cumsum(onehot, axis=0)
    padded = (csum[-1] + ROUTE_TILE - 1) // ROUTE_TILE * ROUTE_TILE
    ends = jnp.cumsum(padded)
    pos = jnp.sum(onehot * (csum - 1 + (ends - padded)[None, :]), axis=1).astype(jnp.int32)
    n_tiles = ntok // ROUTE_TILE + N_EGROUPS
    tile_start = jnp.arange(n_tiles, dtype=jnp.int32) * ROUTE_TILE
    tile_group = jnp.minimum(jnp.sum((tile_start[:, None] >= ends[None, :]).astype(jnp.int32), axis=1),
                             N_EGROUPS - 1).astype(jnp.int32)
    nrows = n_tiles * ROUTE_TILE
    nsteps = ntok // DISPATCH_TILE
    any_spec = pl.BlockSpec(memory_space=pl.ANY)

    xs = pl.pallas_call(
        _dispatch_kernel,
        grid_spec=pltpu.PrefetchScalarGridSpec(
            num_scalar_prefetch=1, grid=(nsteps,),
            in_specs=[pl.BlockSpec((DISPATCH_TILE, width), lambda i, pos: (i, 0)), any_spec],
            out_specs=any_spec,
            scratch_shapes=[pltpu.SemaphoreType.DMA((1,))]),
        out_shape=jax.ShapeDtypeStruct((nrows, width), F32),
        input_output_aliases={2: 0},
        compiler_params=_cparams(("arbitrary",), 16 << 20),
        name="moe_dispatch",
    )(pos, hcf, jnp.zeros((nrows, width), F32))

    def w_spec(d1, d2):
        return pl.BlockSpec((1, d1, d2), lambda t, e, tg: (tg[t] * EXPERTS_PER_GROUP + e, 0, 0))

    vmem = (2 * ROUTE_TILE * (width + D_MODEL) * 4 + ROUTE_TILE * D_MODEL * (2 + 4)
            + 2 * 3 * D_MODEL * D_EXPERT * 2 + (16 << 20))
    ys = pl.pallas_call(
        _moe_routed_kernel,
        grid_spec=pltpu.PrefetchScalarGridSpec(
            num_scalar_prefetch=1, grid=(n_tiles, EXPERTS_PER_GROUP),
            in_specs=[pl.BlockSpec((ROUTE_TILE, width), lambda t, e, tg: (t, 0)),
                      w_spec(D_MODEL, D_EXPERT), w_spec(D_MODEL, D_EXPERT), w_spec(D_EXPERT, D_MODEL)],
            out_specs=pl.BlockSpec((ROUTE_TILE, D_MODEL), lambda t, e, tg: (t, 0)),
            scratch_shapes=[pltpu.VMEM((ROUTE_TILE, D_MODEL), BF16), pltpu.VMEM((ROUTE_TILE, D_MODEL), F32)]),
        out_shape=jax.ShapeDtypeStruct((nrows, D_MODEL), F32),
        compiler_params=_cparams(("parallel", "arbitrary"), vmem),
        name="moe_routed",
    )(tile_group, xs, wg, wu, wd)

    steps_per_batch = seqlen // DISPATCH_TILE
    y = pl.pallas_call(
        _combine_kernel,
        grid_spec=pltpu.PrefetchScalarGridSpec(
            num_scalar_prefetch=1, grid=(nsteps,),
            in_specs=[pl.BlockSpec((DISPATCH_TILE, D_MODEL), lambda i, pos: (i, 0)),
                      pl.BlockSpec((1, 1, D_MODEL), lambda i, pos: (i // steps_per_batch, 0, 0)),
                      pl.BlockSpec((1, D_MODEL), lambda i, pos: (0, 0)), any_spec],
            out_specs=pl.BlockSpec((DISPATCH_TILE, D_MODEL), lambda i, pos: (i, 0)),
            scratch_shapes=[pltpu.VMEM((2, DISPATCH_TILE, D_MODEL), F32), pltpu.SemaphoreType.DMA((2,))]),
        out_shape=jax.ShapeDtypeStruct((ntok, D_MODEL), F32),
        compiler_params=_cparams(("arbitrary",), 16 << 20),
        name="moe_combine",
    )(pos, x1f, gt2, final_w.reshape(1, -1), ys)
    return y.reshape(bsz, seqlen, D_MODEL)


def _prep_weights(lp):
    w_in = lp["w_in"]
    sizes = (SSD_INNER, CONV_DIM, SSD_HEADS, ATTN_Q, ATTN_KV, ATTN_KV, D_MODEL, D_MODEL)
    pts = [int(p) for p in np.cumsum(sizes)[:-1]]
    z, xbc, dt, q, k, v, gs, ga = jnp.split(w_in, pts, axis=1)
    dt = jnp.pad(dt, ((0, 0), (0, LANES - SSD_HEADS)))
    w_in_r = jnp.concatenate([z, xbc, q, k, v, gs, ga, dt], axis=1).astype(BF16)
    wrt = jnp.concatenate([lp["w_group"], lp["w_router"]], axis=1)
    wrt = jnp.pad(wrt, ((0, 0), (0, LANES - wrt.shape[1]))).astype(F32)
    wrt_hi = wrt.astype(BF16)
    wrt = jnp.concatenate([wrt_hi, (wrt - wrt_hi.astype(F32)).astype(BF16)], axis=1)
    brt = jnp.concatenate([lp["b_group"], lp["b_router"]])
    brt = jnp.pad(brt, (0, LANES - brt.shape[0])).astype(F32).reshape(1, LANES)
    return dict(w_in_r=w_in_r, wrt=wrt, brt=brt,
                wso=lp["w_ssd_out"].astype(BF16), wao=lp["w_attn_out"].astype(BF16), wo=lp["w_o"].astype(BF16),
                wg=lp["w_gate_e"].astype(BF16), wu=lp["w_up_e"].astype(BF16), wd=lp["w_down_e"].astype(BF16))


def _mods(mod, per_row_repeat):
    parts = jnp.split(mod, 6, axis=-1)
    if per_row_repeat:
        return [jnp.repeat(p, per_row_repeat, axis=0)[None] for p in parts]
    return [p[:, None, :] for p in parts]


def _layer_prompt(x, mod, lp, pw, final_w):
    bsz, seqlen, _ = x.shape
    sh1, sc1, gt1, sh2, sc2, gt2 = _mods(mod, 0)
    tables = _rope_tables(np.arange(seqlen))
    (z, xbc, dt, q, k, v, gs, ga, kmean, kaug, vaug) = _in_proj(x, sc1, sh1, lp["norm1_w"], pw["w_in_r"], tables, True)
    conv0 = jnp.zeros((bsz, SSD_CONV - 1, CONV_DIM), F32)
    ssm0 = jnp.zeros((bsz, SSD_HEADS, SSD_HEADDIM, SSD_STATE), F32)
    y_ssd, conv_new, ssm_new = _ssd(xbc, z, dt, conv0, ssm0, lp["conv_w"], lp["conv_b"], lp["dt_bias"],
                                    lp["a_log"], lp["d_skip"], lp["ssd_norm_w"])
    y_att = _moba_prompt(q, kmean, kaug, vaug)
    x1, hc = _mix(x, y_ssd, y_att, gs, ga, gt1, sc2, sh2, lp["norm2_w"],
                  pw["wso"], pw["wao"], pw["wo"], pw["wrt"], pw["brt"])
    y = _moe_routed(hc, x1, gt2, final_w, pw["wg"], pw["wu"], pw["wd"])
    kv_shape = (bsz, seqlen, ATTN_KV_HEADS, HEAD_DIM)
    return y, k.reshape(kv_shape), v.reshape(kv_shape), conv_new, ssm_new


def _layer_sample(x, mod, conv_buf, ssm_state, cache_k, cache_v, page_table, past_len, lp, pw, final_w):
    bsz, dec_seq, _ = x.shape
    ntok = bsz * dec_seq
    sh1, sc1, gt1, sh2, sc2, gt2 = _mods(mod, dec_seq)
    pos = past_len + (np.arange(ntok) % dec_seq)
    tables = _rope_tables(pos)
    xf = x.reshape(1, ntok, D_MODEL)
    z, xbc, dt, q, k, v, gs, ga = _in_proj(xf, sc1, sh1, lp["norm1_w"], pw["w_in_r"], tables, False)

    def per_b(t):
        return t.reshape(bsz, dec_seq, t.shape[-1])

    y_ssd, conv_new, ssm_new = _ssd(per_b(xbc), per_b(z), per_b(dt), conv_buf, ssm_state, lp["conv_w"],
                                    lp["conv_b"], lp["dt_bias"], lp["a_log"], lp["d_skip"], lp["ssd_norm_w"])
    n_pool, page = cache_k.shape[0], cache_k.shape[1]

    def position_minor(cache):
        return cache.transpose(0, 2, 3, 1).reshape(n_pool, ATTN_KV, page)

    y_att = _moba_sample(per_b(q), per_b(k), per_b(v), position_minor(cache_k), position_minor(cache_v),
                         page_table, past_len)
    x1, hc = _mix(xf, y_ssd.reshape(1, ntok, -1), y_att.reshape(1, ntok, -1), gs, ga, gt1, sc2, sh2,
                  lp["norm2_w"], pw["wso"], pw["wao"], pw["wo"], pw["wrt"], pw["brt"])
    y = _moe(hc, x1, gt2, final_w, pw["wg"], pw["wu"], pw["wd"])
    kv_shape = (bsz, dec_seq, ATTN_KV_HEADS, HEAD_DIM)
    return y.reshape(bsz, dec_seq, D_MODEL), k.reshape(kv_shape), v.reshape(kv_shape), conv_new, ssm_new


def kernel(x_prompt, x_sample, cache_k, cache_v, state_conv, state_ssm, page_table, c_prompt, c_sample,
           w_ada, b_ada, norm1_w, w_in, conv_w, conv_b, dt_bias, a_log, d_skip, ssd_norm_w,
           w_ssd_out, w_attn_out, w_o, norm2_w, w_group, b_group, w_router, b_router,
           w_gate_e, w_up_e, w_down_e, final_w):
    depth = w_in.shape[0]
    assert depth == 1, "the final RMSNorm is fused into the last layer's MoE kernel"
    nb_p, nb_s = c_prompt.shape[0], c_sample.shape[0]
    past_len = page_table.shape[1] * cache_k.shape[2]
    l = 0
    lp = dict(norm1_w=norm1_w[l], w_in=w_in[l], conv_w=conv_w[l], conv_b=conv_b[l], dt_bias=dt_bias[l],
              a_log=a_log[l], d_skip=d_skip[l], ssd_norm_w=ssd_norm_w[l], w_ssd_out=w_ssd_out[l],
              w_attn_out=w_attn_out[l], w_o=w_o[l], norm2_w=norm2_w[l], w_group=w_group[l], b_group=b_group[l],
              w_router=w_router[l], b_router=b_router[l], w_gate_e=w_gate_e[l], w_up_e=w_up_e[l],
              w_down_e=w_down_e[l])
    pw = _prep_weights(lp)
    c_all = jnp.concatenate([c_prompt, c_sample], axis=0)
    pad_rows = (-c_all.shape[0]) % SUBLANES
    mod = _ada_mod(jnp.pad(c_all, ((0, pad_rows), (0, 0))), w_ada[l], b_ada[l])
    yp, kp, vp, cp, sp = _layer_prompt(x_prompt, mod[:nb_p], lp, pw, final_w)
    ys, ks, vs, cs, ss = _layer_sample(x_sample, mod[nb_p:nb_p + nb_s], state_conv[l], state_ssm[l],
                                       cache_k[l], cache_v[l], page_table, past_len, lp, pw, final_w)
    return (yp, ys, kp[None], vp[None], cp[None], sp[None], ks[None], vs[None], cs[None], ss[None])
```

```python
import functools
import math

import numpy as np
import jax
import jax.numpy as jnp
from jax import lax
from jax.experimental import pallas as pl
from jax.experimental.pallas import tpu as pltpu

F32 = jnp.float32
BF16 = jnp.bfloat16
HIGHEST = lax.Precision.HIGHEST

D_MODEL = 1024
SSD_HEADDIM = 64
SSD_INNER = D_MODEL
SSD_HEADS = SSD_INNER // SSD_HEADDIM
SSD_GROUPS = 2
SSD_STATE = 128
SSD_CONV = 4
SSD_CHUNK = 128
SSD_BC = SSD_GROUPS * SSD_STATE
CONV_DIM = SSD_INNER + 2 * SSD_BC
ATTN_HEADS = 8
ATTN_KV_HEADS = 4
HEAD_DIM = 64
ATTN_Q = ATTN_HEADS * HEAD_DIM
ATTN_KV = ATTN_KV_HEADS * HEAD_DIM
ROT_DIM = HEAD_DIM // 4
ROPE_THETA = 500000.0
MOBA_BLOCK = 256
MOBA_TOPK = 3
N_EGROUPS = 4
EXPERTS_PER_GROUP = 4
N_EXPERTS = N_EGROUPS * EXPERTS_PER_GROUP
D_EXPERT = D_MODEL // 2
EPS = 1e-6

LANES = 128
SUBLANES = 8
VMEM_CAP = 56 << 20
NEG = -1e30
GRP = ATTN_HEADS // ATTN_KV_HEADS
ROUTE_OFF = N_EGROUPS

_SEG = {}
_off = 0
for _name, _w in (("z", SSD_INNER), ("xbc", CONV_DIM), ("q", ATTN_Q), ("k", ATTN_KV), ("v", ATTN_KV),
                  ("gs", D_MODEL), ("ga", D_MODEL), ("dt", LANES)):
    _SEG[_name] = (_off, _off + _w)
    _off += _w
IN_PAD = _off


def _cparams(semantics, vmem_bytes):
    return pltpu.CompilerParams(dimension_semantics=semantics,
                                vmem_limit_bytes=int(min(max(vmem_bytes, 16 << 20), VMEM_CAP)))


def _dot(a, b):
    return jnp.dot(a, b, preferred_element_type=F32)


def _dot_nt(a, b):
    return lax.dot_general(a, b, (((1,), (1,)), ((), ())), preferred_element_type=F32)


def _dot_exact(a, b):
    return jnp.dot(a, b, preferred_element_type=F32, precision=HIGHEST)


def _silu(x):
    return x * jax.nn.sigmoid(x)


def _mod_kernel(c_ref, w_ref, b_ref, o_ref):
    o_ref[...] = _dot_exact(_silu(c_ref[...]), w_ref[...]) + b_ref[...]


def _ada_mod(c_all, w_ada, b_ada):
    rows = c_all.shape[0]
    return pl.pallas_call(
        _mod_kernel,
        grid=(6,),
        in_specs=[pl.BlockSpec((rows, D_MODEL), lambda j: (0, 0)),
                  pl.BlockSpec((D_MODEL, D_MODEL), lambda j: (0, j)),
                  pl.BlockSpec((1, D_MODEL), lambda j: (0, j))],
        out_specs=pl.BlockSpec((rows, D_MODEL), lambda j: (0, j)),
        out_shape=jax.ShapeDtypeStruct((rows, 6 * D_MODEL), F32),
        compiler_params=_cparams(("arbitrary",), 24 << 20),
        name="ada_mod",
    )(c_all, w_ada, b_ada.reshape(1, -1))


def _rope_tables(pos):
    half = ROT_DIM // 2
    c = -2.0 * math.log(ROPE_THETA) / ROT_DIM
    c_hi = float(np.float32(c))
    j = jnp.arange(half, dtype=F32)
    inv_freq = jnp.exp(j * c_hi + j * (c - c_hi))
    ang = jnp.asarray(pos).astype(F32)[:, None] * inv_freq[None, :]
    cos, sin = jnp.cos(ang), jnp.sin(ang)
    n = pos.shape[0]
    one = jnp.ones((n, HEAD_DIM - ROT_DIM), F32)
    zero = jnp.zeros((n, half), F32)
    rest = jnp.zeros((n, HEAD_DIM - ROT_DIM), F32)
    a = jnp.concatenate([cos, cos, one], axis=1)
    s1 = jnp.concatenate([-sin, zero, rest], axis=1)
    s2 = jnp.concatenate([zero, sin, rest], axis=1)
    rep = LANES // HEAD_DIM
    return tuple(jnp.tile(t, (1, rep)) for t in (a, s1, s2))


def _inproj_kernel(*refs, attn_layout):
    (x_ref, sc_ref, sh_ref, nw_ref, w_ref, ra_ref, rs1_ref, rs2_ref,
     z_ref, xbc_ref, dt_ref, q_ref, k_ref, v_ref, gs_ref, ga_ref) = refs[:16]
    x = x_ref[0]
    h = x * lax.rsqrt(jnp.mean(x * x, axis=-1, keepdims=True) + EPS) * nw_ref[...]
    h = (h * (1.0 + sc_ref[0]) + sh_ref[0]).astype(BF16)

    def proj(name, out_ref):
        lo, hi = _SEG[name]
        step = 512
        for c in range(lo, hi, step):
            w = min(step, hi - c)
            out_ref[0, :, c - lo:c - lo + w] = _dot(h, w_ref[:, c:c + w])

    proj("z", z_ref)
    proj("xbc", xbc_ref)
    proj("gs", gs_ref)
    proj("ga", ga_ref)
    proj("dt", dt_ref)

    def rope(t):
        width = t.shape[1]
        reps = width // LANES
        a = jnp.concatenate([ra_ref[...]] * reps, axis=1)
        s1 = jnp.concatenate([rs1_ref[...]] * reps, axis=1)
        s2 = jnp.concatenate([rs2_ref[...]] * reps, axis=1)
        return t * a + pltpu.roll(t, width - ROT_DIM // 2, 1) * s1 + pltpu.roll(t, ROT_DIM // 2, 1) * s2

    lo, hi = _SEG["q"]
    q_ref[0] = rope(_dot(h, w_ref[:, lo:hi]))
    lo, hi = _SEG["k"]
    k = rope(_dot(h, w_ref[:, lo:hi]))
    k_ref[0] = k
    lo, hi = _SEG["v"]
    v = _dot(h, w_ref[:, lo:hi])
    v_ref[0] = v

    if attn_layout:
        kmean_ref, kaug_ref, vaug_ref = refs[16:19]
        tm = k.shape[0]
        blocks = tm // MOBA_BLOCK
        for hb in range(blocks):
            kmean_ref[0, hb] = jnp.mean(k[hb * MOBA_BLOCK:(hb + 1) * MOBA_BLOCK], axis=0, keepdims=True)
        lane = lax.broadcasted_iota(jnp.int32, (tm, HEAD_DIM), 1)
        row = lax.broadcasted_iota(jnp.int32, (tm, HEAD_DIM), 0)
        blk = blocks * pl.program_id(1)
        for hb in range(1, blocks):
            blk = blk + (row >= hb * MOBA_BLOCK).astype(jnp.int32)
        blk_onehot = (lane == blk).astype(F32)
        ones_col = (lane == 0).astype(F32)
        for kh in range(ATTN_KV_HEADS):
            sl = slice(kh * HEAD_DIM, (kh + 1) * HEAD_DIM)
            kaug_ref[0, kh] = jnp.concatenate([k[:, sl], blk_onehot], axis=1).astype(BF16)
            vaug_ref[0, kh] = jnp.concatenate([v[:, sl], ones_col], axis=1).astype(BF16)


def _in_proj(x, sc1, sh1, norm_w, w_in_r, tables, attn_layout):
    bsz, seqlen, _ = x.shape
    tm = 2 * MOBA_BLOCK if seqlen % (2 * MOBA_BLOCK) == 0 else MOBA_BLOCK
    assert seqlen % tm == 0
    nt = seqlen // tm
    blocks = tm // MOBA_BLOCK
    mod_rows = sc1.shape[1]
    if mod_rows == 1:
        mod_spec = pl.BlockSpec((1, 1, D_MODEL), lambda b, i: (b, 0, 0))
    else:
        mod_spec = pl.BlockSpec((1, tm, D_MODEL), lambda b, i: (b, i, 0))
    tab_spec = pl.BlockSpec((tm, LANES), lambda b, i: (i, 0))

    def tok(width):
        return pl.BlockSpec((1, tm, width), lambda b, i: (b, i, 0))

    def shp(width):
        return jax.ShapeDtypeStruct((bsz, seqlen, width), F32)

    widths = (SSD_INNER, CONV_DIM, LANES, ATTN_Q, ATTN_KV, ATTN_KV, D_MODEL, D_MODEL)
    out_specs = [tok(w) for w in widths]
    out_shape = [shp(w) for w in widths]
    if attn_layout:
        assert nt * blocks <= HEAD_DIM
        out_specs += [pl.BlockSpec((1, blocks, 1, ATTN_KV), lambda b, i: (b, i, 0, 0)),
                      pl.BlockSpec((1, ATTN_KV_HEADS, tm, LANES), lambda b, i: (b, 0, i, 0)),
                      pl.BlockSpec((1, ATTN_KV_HEADS, tm, LANES), lambda b, i: (b, 0, i, 0))]
        out_shape += [jax.ShapeDtypeStruct((bsz, nt * blocks, 1, ATTN_KV), F32),
                      jax.ShapeDtypeStruct((bsz, ATTN_KV_HEADS, seqlen, LANES), BF16),
                      jax.ShapeDtypeStruct((bsz, ATTN_KV_HEADS, seqlen, LANES), BF16)]
    vmem = D_MODEL * IN_PAD * 2 + 2 * tm * (D_MODEL + IN_PAD + 4 * LANES) * 4 + (10 << 20)
    return pl.pallas_call(
        functools.partial(_inproj_kernel, attn_layout=attn_layout),
        grid=(bsz, nt),
        in_specs=[tok(D_MODEL), mod_spec, mod_spec,
                  pl.BlockSpec((1, D_MODEL), lambda b, i: (0, 0)),
                  pl.BlockSpec((D_MODEL, IN_PAD), lambda b, i: (0, 0), pipeline_mode=pl.Buffered(1)),
                  tab_spec, tab_spec, tab_spec],
        out_specs=out_specs,
        out_shape=out_shape,
        compiler_params=_cparams(("parallel", "parallel"), vmem),
        name="in_proj",
    )(x, sc1, sh1, norm_w.reshape(1, -1), w_in_r, *tables)


def _cumsum_rows(a):
    row = lax.broadcasted_iota(jnp.int32, a.shape, 0)
    s = 1
    while s < a.shape[0]:
        a = a + jnp.where(row >= s, pltpu.roll(a, s, 0), 0.0)
        s *= 2
    return a


def _ssd_kernel(xbc_ref, z_ref, dt_ref, cbuf_ref, s0_ref, cw_ref, cb_ref, dtb_ref, alog_ref, dsk_ref, nw_ref,
                y_ref, cnew_ref, sout_ref, xext, st, ysc, *, rows):
    ch = SSD_CHUNK
    pad = SUBLANES
    c = pl.program_id(1)
    last = pl.num_programs(1) - 1

    @pl.when(c == 0)
    def _():
        xext[0:pad, :] = jnp.zeros((pad, CONV_DIM), F32)
        xext[pad - (SSD_CONV - 1):pad, :] = cbuf_ref[0]
        st[...] = s0_ref[0].T

    xext[pad:pad + rows, :] = xbc_ref[0]
    if rows < ch:
        xext[pad + rows:pad + ch, :] = jnp.zeros((ch - rows, CONV_DIM), F32)

    conv = cb_ref[...] + xext[pad:pad + ch, :] * cw_ref[SSD_CONV - 1:SSD_CONV, :]
    for w in range(SSD_CONV - 1):
        off = pad - (SSD_CONV - 1) + w
        conv = conv + xext[off:off + ch, :] * cw_ref[w:w + 1, :]
    act = _silu(conv)

    @pl.when(c == last)
    def _():
        cnew_ref[0] = xext[pad + rows - (SSD_CONV - 1):pad + rows, :]

    xext[0:pad, :] = xext[ch:ch + pad, :]

    xs = act[:, :SSD_INNER]
    dt_raw = dt_ref[0]
    if rows < ch:
        dt_raw = jnp.concatenate([dt_raw, jnp.zeros((ch - rows, LANES), F32)], axis=0)
    dt_in = dt_raw + dtb_ref[...]
    dt = jnp.maximum(dt_in, 0.0) + jnp.log1p(jnp.exp(-jnp.abs(dt_in)))
    row = lax.broadcasted_iota(jnp.int32, (ch, LANES), 0)
    col = lax.broadcasted_iota(jnp.int32, (ch, LANES), 1)
    if rows < ch:
        dt = jnp.where(row < rows, dt, 0.0)
    a_cs = _cumsum_rows(dt * (-jnp.exp(alog_ref[...])))
    a_cs_t = a_cs.T
    dt_t = dt.T
    causal = row >= col

    heads_per_pair = LANES // SSD_HEADDIM
    assert heads_per_pair == 2 and SSD_STATE == ch
    first_of_pair = col < SSD_HEADDIM
    for g in range(SSD_GROUPS):
        bg = act[:, SSD_INNER + g * SSD_STATE:SSD_INNER + (g + 1) * SSD_STATE]
        cg = act[:, SSD_INNER + SSD_BC + g * SSD_STATE:SSD_INNER + SSD_BC + (g + 1) * SSD_STATE]
        gmat = _dot_nt(cg.astype(BF16), bg.astype(BF16))
        bg_t = bg.T
        for pair in range(SSD_HEADS // SSD_GROUPS // heads_per_pair):
            hd0 = g * (SSD_HEADS // SSD_GROUPS) + pair * heads_per_pair
            sl = slice(hd0 * SSD_HEADDIM, hd0 * SSD_HEADDIM + LANES)
            x_pair = xs[:, sl]
            s_pair = st[:, sl]
            x_diag = [jnp.where(first_of_pair, x_pair, 0.0).astype(BF16),
                      jnp.where(first_of_pair, 0.0, x_pair).astype(BF16)]
            s_diag = [jnp.where(first_of_pair, s_pair, 0.0).astype(BF16),
                      jnp.where(first_of_pair, 0.0, s_pair).astype(BF16)]
            m_parts, c_parts, b_parts, decay = [], [], [], []
            for hd in range(hd0, hd0 + heads_per_pair):
                acs_col = a_cs[:, hd:hd + 1]
                acs_row = a_cs_t[hd:hd + 1, :]
                dt_row = dt_t[hd:hd + 1, :]
                acs_last = a_cs_t[hd:hd + 1, ch - 1:ch]
                lmat = jnp.where(causal, jnp.exp(acs_col - acs_row), 0.0)
                m_parts.append((gmat * lmat * dt_row).astype(BF16))
                c_parts.append((cg * jnp.exp(acs_col)).astype(BF16))
                b_parts.append((bg_t * (jnp.exp(acs_last - acs_row) * dt_row)).astype(BF16))
                decay.append(jnp.exp(acs_last))
            ysc[:, sl] = _dot(jnp.concatenate(m_parts + c_parts, axis=1), jnp.concatenate(x_diag + s_diag, axis=0))
            st[:, sl] = (jnp.where(first_of_pair[:1], decay[0], decay[1]) * s_pair
                         + _dot(jnp.concatenate(b_parts, axis=1), jnp.concatenate(x_diag, axis=0)))

    y = (ysc[0:rows, :] + xs[:rows] * dsk_ref[...]) * _silu(z_ref[0])
    gw = SSD_INNER // SSD_GROUPS
    for g in range(SSD_GROUPS):
        yg = y[:, g * gw:(g + 1) * gw]
        yg = yg * lax.rsqrt(jnp.mean(yg * yg, axis=-1, keepdims=True) + EPS)
        y_ref[0, :, g * gw:(g + 1) * gw] = yg * nw_ref[:, g * gw:(g + 1) * gw]

    @pl.when(c == last)
    def _():
        sout_ref[0] = st[...].T


def _ssd(xbc, z, dt, conv_buf, ssm_state, conv_w, conv_b, dt_bias, a_log, d_skip, norm_w):
    bsz, seqlen, _ = xbc.shape
    rows = SSD_CHUNK if seqlen % SSD_CHUNK == 0 else seqlen
    assert rows == seqlen or rows == SSD_CHUNK
    assert rows % SUBLANES == 0 and rows >= SSD_CONV - 1
    nc = seqlen // rows
    hp = SSD_HEADS * SSD_HEADDIM

    def tok(width):
        return pl.BlockSpec((1, rows, width), lambda b, c: (b, c, 0))

    def per_b(d1, d2):
        return pl.BlockSpec((1, d1, d2), lambda b, c: (b, 0, 0))

    def vec(width, r=1):
        return pl.BlockSpec((r, width), lambda b, c: (0, 0))

    def pad_lanes(v):
        return jnp.pad(v.astype(F32), (0, LANES - v.shape[0])).reshape(1, LANES)

    y, conv_new, s_new = pl.pallas_call(
        functools.partial(_ssd_kernel, rows=rows),
        grid=(bsz, nc),
        in_specs=[tok(CONV_DIM), tok(SSD_INNER), tok(LANES),
                  per_b(SSD_CONV - 1, CONV_DIM), per_b(hp, SSD_STATE),
                  vec(CONV_DIM, SSD_CONV), vec(CONV_DIM), vec(LANES), vec(LANES), vec(SSD_INNER), vec(SSD_INNER)],
        out_specs=[tok(SSD_INNER), per_b(SSD_CONV - 1, CONV_DIM), per_b(hp, SSD_STATE)],
        out_shape=[jax.ShapeDtypeStruct((bsz, seqlen, SSD_INNER), F32),
                   jax.ShapeDtypeStruct((bsz, SSD_CONV - 1, CONV_DIM), F32),
                   jax.ShapeDtypeStruct((bsz, hp, SSD_STATE), F32)],
        scratch_shapes=[pltpu.VMEM((SSD_CHUNK + SUBLANES, CONV_DIM), F32),
                        pltpu.VMEM((SSD_STATE, hp), F32),
                        pltpu.VMEM((SSD_CHUNK, SSD_INNER), F32)],
        compiler_params=_cparams(("parallel", "arbitrary"), 40 << 20),
        name="ssd",
    )(xbc, z, dt, conv_buf, ssm_state.reshape(bsz, hp, SSD_STATE),
      conv_w, conv_b.reshape(1, -1), pad_lanes(dt_bias), pad_lanes(a_log),
      jnp.repeat(d_skip.astype(F32), SSD_HEADDIM).reshape(1, -1), norm_w.reshape(1, -1))
    return y, conv_new, s_new.reshape(bsz, SSD_HEADS, SSD_HEADDIM, SSD_STATE)


def _select_topk(scores, valid, axis):
    sc = scores if valid is None else jnp.where(valid, scores, -jnp.inf)
    sel = jnp.zeros(scores.shape, F32)
    pos_f = lax.broadcasted_iota(jnp.int32, scores.shape, axis).astype(F32)
    for _ in range(MOBA_TOPK):
        m = jnp.max(sc, axis=axis, keepdims=True)
        idx = jnp.min(jnp.where(sc == m, pos_f, float(1 << 20)), axis=axis, keepdims=True)
        idx = jnp.where(m > -jnp.inf, idx, -1.0)
        hit = pos_f == idx
        sel = jnp.where(hit, 1.0, sel)
        sc = jnp.where(hit, -jnp.inf, sc)
    return sel > 0.5


def _moba_prompt_kernel(q_ref, km_ref, kaug_ref, vaug_ref, o_ref, s_a, s_b, m_sc, acc_sc, *, qblocks):
    j = pl.program_id(2)
    tq = MOBA_BLOCK
    chunk = qblocks * tq
    half = GRP * tq
    q2 = q_ref[0]
    qs = jnp.concatenate([q2[b * tq:(b + 1) * tq, g * HEAD_DIM:(g + 1) * HEAD_DIM]
                          for b in range(qblocks) for g in range(GRP)], axis=0)
    rows = qs.shape[0]
    lane = lax.broadcasted_iota(jnp.int32, (rows, LANES), 1)
    scores_t = lax.dot_general(km_ref[0, 0], qs, (((1,), (1,)), ((), ())), preferred_element_type=F32,
                               precision=HIGHEST)
    nbp = scores_t.shape[0]
    blk = lax.broadcasted_iota(jnp.int32, scores_t.shape, 0)
    col = lax.broadcasted_iota(jnp.int32, scores_t.shape, 1)
    own = qblocks * j
    for b in range(1, qblocks):
        own = own + (col >= b * half).astype(jnp.int32)
    sel_t = _select_topk(scores_t, blk < own, 0)
    pen_t = jnp.concatenate([jnp.zeros((HEAD_DIM, rows), F32), jnp.where(sel_t, 0.0, NEG),
                             jnp.full((LANES - HEAD_DIM - nbp, rows), NEG, F32)], axis=0)
    pen = pen_t.T
    q_pad = jnp.concatenate([qs * (HEAD_DIM ** -0.5), jnp.zeros((rows, LANES - HEAD_DIM), F32)], axis=1)
    q_aug = jnp.where(lane < HEAD_DIM, q_pad, pen).astype(BF16)
    q_own = q_pad.astype(BF16)

    def update(m, acc, s, pv):
        m_new = jnp.maximum(m, jnp.max(s, axis=1, keepdims=True))
        p = jnp.exp(s - m_new)
        acc = jnp.exp(m - m_new) * acc + pv(p.astype(BF16))
        return m_new, acc

    total_chunks = kaug_ref.shape[2] // chunk
    nchunks = j + 1
    npairs = nchunks // 2

    def logits(c):
        start = pl.multiple_of(jnp.minimum(c, total_chunks - 1) * chunk, chunk)
        return _dot_nt(q_aug, kaug_ref[0, 0, pl.ds(start, chunk), :])

    def values(c):
        vb = vaug_ref[0, 0, pl.ds(pl.multiple_of(c * chunk, chunk), chunk), :]
        return lambda p: _dot(p, vb)

    def body(t, carry):
        m, acc = carry
        c0 = 2 * t
        s_b[...] = logits(c0 + 1)
        m, acc = update(m, acc, s_a[...], values(c0))
        s_a[...] = logits(c0 + 2)
        m, acc = update(m, acc, s_b[...], values(c0 + 1))
        return m, acc

    def own_rows(ref, b):
        return ref[0, 0, pl.ds(pl.multiple_of((qblocks * j + b) * tq, tq), tq), :]

    s_own = jnp.concatenate([_dot_nt(q_own[b * half:(b + 1) * half], own_rows(kaug_ref, b))
                             for b in range(qblocks)], axis=0)
    r = lax.broadcasted_iota(jnp.int32, (rows, tq), 0)
    cidx = lax.broadcasted_iota(jnp.int32, (rows, tq), 1)
    s_own = jnp.where(cidx <= (r & (tq - 1)), s_own, NEG)

    def own_values(p):
        return jnp.concatenate([_dot(p[b * half:(b + 1) * half], own_rows(vaug_ref, b))
                                for b in range(qblocks)], axis=0)

    s_a[...] = logits(0)
    init = (jnp.full((rows, 1), -jnp.inf, F32), jnp.zeros((rows, LANES), F32))
    m, acc = lax.fori_loop(0, npairs, body, init)
    m_sc[...] = m
    acc_sc[...] = acc

    @pl.when(nchunks % 2 == 1)
    def _():
        m_odd, acc_odd = update(m_sc[...], acc_sc[...], s_a[...], values(nchunks - 1))
        m_sc[...] = m_odd
        acc_sc[...] = acc_odd

    _, acc = update(m_sc[...], acc_sc[...], s_own, own_values)
    out = acc[:, :HEAD_DIM] / acc[:, HEAD_DIM:HEAD_DIM + 1]
    o_ref[0] = jnp.concatenate(
        [jnp.concatenate([out[(b * GRP + g) * tq:(b * GRP + g + 1) * tq] for g in range(GRP)], axis=1)
         for b in range(qblocks)], axis=0)


def _moba_prompt(q, kmean, kaug, vaug):
    bsz, seqlen, _ = q.shape
    nb = seqlen // MOBA_BLOCK
    assert GRP * HEAD_DIM == LANES and nb <= LANES - HEAD_DIM and MOBA_BLOCK & (MOBA_BLOCK - 1) == 0
    nbp = -(-nb // SUBLANES) * SUBLANES
    km = kmean.reshape(bsz, nb, ATTN_KV_HEADS, HEAD_DIM).transpose(0, 2, 1, 3)
    km = jnp.pad(km, ((0, 0), (0, 0), (0, nbp - nb), (0, 0)))
    kv_spec = pl.BlockSpec((1, 1, seqlen, LANES), lambda b, h, i: (b, h, 0, 0))
    qblocks = 4
    chunk = qblocks * MOBA_BLOCK
    assert seqlen % chunk == 0
    rows = qblocks * GRP * MOBA_BLOCK
    vmem = 2 * 2 * seqlen * LANES * 2 + 2 * rows * chunk * 4 + (24 << 20)
    return pl.pallas_call(
        functools.partial(_moba_prompt_kernel, qblocks=qblocks),
        scratch_shapes=[pltpu.VMEM((rows, chunk), F32), pltpu.VMEM((rows, chunk), F32),
                        pltpu.VMEM((rows, 1), F32), pltpu.VMEM((rows, LANES), F32)],
        grid=(bsz, ATTN_KV_HEADS, nb // qblocks),
        in_specs=[pl.BlockSpec((1, chunk, LANES), lambda b, h, i: (b, i, h)),
                  pl.BlockSpec((1, 1, nbp, HEAD_DIM), lambda b, h, i: (b, h, 0, 0)),
                  kv_spec, kv_spec],
        out_specs=pl.BlockSpec((1, chunk, LANES), lambda b, h, i: (b, i, h)),
        out_shape=jax.ShapeDtypeStruct((bsz, seqlen, ATTN_Q), F32),
        compiler_params=_cparams(("parallel", "parallel", "arbitrary"), vmem),
        name="moba_prompt",
    )(q, km, kaug, vaug)


def _moba_sample_kernel(pt_ref, qbt_ref, qb_ref, *refs, pages_per_step, nsteps, nblk, dec_seq):
    del pt_ref
    k_refs = refs[:pages_per_step]
    v_refs = refs[pages_per_step:2 * pages_per_step]
    kn_ref, vn_ref, o_ref, kt_sc, sct_sc, pen_sc, m_sc, l_sc, acc_sc = refs[2 * pages_per_step:]
    step = pl.program_id(1)
    rows = qb_ref.shape[1]
    pages_per_block = MOBA_BLOCK // LANES
    blocks_per_step = pages_per_step // pages_per_block

    @pl.when(step < nsteps)
    def _():
        qbt = qbt_ref[0]
        for jb in range(blocks_per_step):
            ksum = None
            for j in range(pages_per_block):
                idx = jb * pages_per_block + j
                kpage = k_refs[idx][0]
                kt_sc[step * pages_per_step + idx] = kpage.astype(BF16)
                ksum = kpage if ksum is None else ksum + kpage
            kmean = jnp.sum(ksum, axis=1, keepdims=True) * (1.0 / MOBA_BLOCK)
            sct_sc[pl.ds(step * blocks_per_step + jb, 1), :] = jnp.sum(qbt * kmean, axis=0, keepdims=True)

    @pl.when(step >= nsteps)
    def _():
        n = step - nsteps
        lane = lax.broadcasted_iota(jnp.int32, (rows, LANES), 1)

        @pl.when(n == 0)
        def _():
            sel_t = _select_topk(sct_sc[...], None, 0)
            pen_t = jnp.concatenate([jnp.where(sel_t, 0.0, NEG), jnp.full((LANES - nblk, LANES), NEG, F32)],
                                    axis=0)
            pen_sc[...] = pen_t.T[:rows]
            m_sc[...] = jnp.full(m_sc.shape, -jnp.inf, F32)
            l_sc[...] = jnp.zeros(l_sc.shape, F32)
            acc_sc[...] = jnp.zeros(acc_sc.shape, F32)

        qb = (qb_ref[0] * (HEAD_DIM ** -0.5)).astype(BF16)

        def update(s, pv_fn):
            m = m_sc[...]
            m_new = jnp.maximum(m, jnp.max(s, axis=1, keepdims=True))
            p = jnp.exp(s - m_new)
            alpha = jnp.exp(m - m_new)
            l_sc[...] = alpha * l_sc[...] + jnp.sum(p, axis=1, keepdims=True)
            acc_sc[...] = alpha * acc_sc[...] + pv_fn(p.astype(BF16))
            m_sc[...] = m_new

        pen = pen_sc[...]
        pen_cols = []
        for jb in range(blocks_per_step):
            pen_col = jnp.sum(jnp.where(lane == n * blocks_per_step + jb, pen, 0.0), axis=1, keepdims=True)
            pen_cols.append(jnp.broadcast_to(pen_col, (rows, MOBA_BLOCK)))
        kt_all = jnp.concatenate([kt_sc[n * pages_per_step + j] for j in range(pages_per_step)], axis=1)
        vt_all = jnp.concatenate([v_ref[0].astype(BF16) for v_ref in v_refs], axis=1)
        update(_dot(qb, kt_all) + jnp.concatenate(pen_cols, axis=1), lambda p: _dot_nt(p, vt_all))

        @pl.when(n == nsteps - 1)
        def _():
            zpad = jnp.zeros((LANES - dec_seq, kn_ref.shape[2]), F32)
            kn = jnp.concatenate([kn_ref[0], zpad], axis=0).astype(BF16)
            vn = jnp.concatenate([vn_ref[0], zpad], axis=0).astype(BF16)
            s_own = _dot_nt(qb, kn)
            r = lax.broadcasted_iota(jnp.int32, (rows, LANES), 0)
            qpos = r & (dec_seq - 1)
            s_own = jnp.where((lane <= qpos) & (lane < dec_seq), s_own, NEG)
            update(s_own, lambda p: _dot(p, vn))
            o_ref[0] = acc_sc[...] / l_sc[...]


def _moba_sample(q, k_new, v_new, cache_kt, cache_vt, page_table, past_len):
    bsz, dec_seq, _ = q.shape
    page = cache_kt.shape[2]
    assert page == LANES and MOBA_BLOCK % page == 0
    assert past_len % MOBA_BLOCK == 0 and dec_seq == SUBLANES
    nblk = past_len // MOBA_BLOCK
    assert MOBA_TOPK <= nblk <= LANES and nblk % SUBLANES == 0
    n_pages = past_len // page
    pages_per_step = 32
    while n_pages % pages_per_step:
        pages_per_step //= 2
    assert pages_per_step * page >= MOBA_BLOCK
    nsteps = n_pages // pages_per_step
    rows = ATTN_HEADS * dec_seq
    q4 = q.reshape(bsz, dec_seq, ATTN_HEADS, HEAD_DIM).transpose(0, 2, 1, 3)
    own = (np.arange(ATTN_HEADS)[:, None] // GRP == np.arange(ATTN_KV_HEADS)[None, :]).astype(np.float32)
    qb = (q4[:, :, :, None, :] * jnp.asarray(own)[None, :, None, :, None]).reshape(bsz, rows, ATTN_KV)
    qbt = jnp.pad(qb.transpose(0, 2, 1), ((0, 0), (0, 0), (0, LANES - rows)))

    def k_spec(j):
        return pl.BlockSpec((1, ATTN_KV, page),
                            lambda b, s, pt: (pt[b, pages_per_step * jnp.minimum(s, nsteps - 1) + j], 0, 0))

    def v_spec(j):
        return pl.BlockSpec((1, ATTN_KV, page),
                            lambda b, s, pt: (pt[b, pages_per_step * jnp.maximum(s - nsteps, 0) + j], 0, 0))

    def per_b(d1, d2):
        return pl.BlockSpec((1, d1, d2), lambda b, s, pt: (b, 0, 0))

    out = pl.pallas_call(
        functools.partial(_moba_sample_kernel, pages_per_step=pages_per_step, nsteps=nsteps, nblk=nblk,
                          dec_seq=dec_seq),
        grid_spec=pltpu.PrefetchScalarGridSpec(
            num_scalar_prefetch=1, grid=(bsz, 2 * nsteps),
            in_specs=[per_b(ATTN_KV, LANES), per_b(rows, ATTN_KV)]
                     + [k_spec(j) for j in range(pages_per_step)] + [v_spec(j) for j in range(pages_per_step)]
                     + [per_b(dec_seq, ATTN_KV), per_b(dec_seq, ATTN_KV)],
            out_specs=per_b(rows, ATTN_KV),
            scratch_shapes=[pltpu.VMEM((n_pages, ATTN_KV, page), BF16), pltpu.VMEM((nblk, LANES), F32),
                            pltpu.VMEM((rows, LANES), F32), pltpu.VMEM((rows, 1), F32), pltpu.VMEM((rows, 1), F32),
                            pltpu.VMEM((rows, ATTN_KV), F32)]),
        out_shape=jax.ShapeDtypeStruct((bsz, rows, ATTN_KV), F32),
        compiler_params=_cparams(("parallel", "arbitrary"), 32 << 20),
        name="moba_sample_attn",
    )(page_table, qbt, qb, *([cache_kt] * pages_per_step), *([cache_vt] * pages_per_step), k_new, v_new)
    o5 = out.reshape(bsz, ATTN_HEADS, dec_seq, ATTN_KV_HEADS, HEAD_DIM)
    o4 = jnp.sum(o5 * jnp.asarray(own)[None, :, None, :, None], axis=3)
    return o4.transpose(0, 2, 1, 3).reshape(bsz, dec_seq, ATTN_Q)


def _mix_kernel(x_ref, ys_ref, ya_ref, gs_ref, ga_ref, gt1_ref, sc2_ref, sh2_ref, nw_ref,
                wso_ref, wao_ref, wo_ref, wrt_ref, brt_ref, x1_ref, hc_ref):
    merged = (jax.nn.sigmoid(gs_ref[0]) * _dot(ys_ref[0].astype(BF16), wso_ref[...])
              + jax.nn.sigmoid(ga_ref[0]) * _dot(ya_ref[0].astype(BF16), wao_ref[...]))
    x1 = x_ref[0] + gt1_ref[0] * _dot(merged.astype(BF16), wo_ref[...])
    x1_ref[0] = x1
    h2 = x1 * lax.rsqrt(jnp.mean(x1 * x1, axis=-1, keepdims=True) + EPS) * nw_ref[...]
    h2 = h2 * (1.0 + sc2_ref[0]) + sh2_ref[0]
    hc_ref[0, :, :D_MODEL] = h2

    h_hi = h2.astype(BF16)
    h_lo = (h2 - h_hi.astype(F32)).astype(BF16)
    hw = _dot(h_hi, wrt_ref[...])
    lg = hw[:, :LANES] + hw[:, LANES:] + _dot(h_lo, wrt_ref[:, :LANES]) + brt_ref[...]
    lane = lax.broadcasted_iota(jnp.int32, lg.shape, 1).astype(F32)
    big = float(1 << 20)
    ninf = -jnp.inf
    gl = jnp.where(lane < N_EGROUPS, lg, ninf)
    gmax = jnp.max(gl, axis=1, keepdims=True)
    gidx = jnp.min(jnp.where(gl == gmax, lane, big), axis=1, keepdims=True)
    g_w = 1.0 / jnp.sum(jnp.exp(gl - gmax), axis=1, keepdims=True)
    lo = ROUTE_OFF + gidx * EXPERTS_PER_GROUP
    el = jnp.where((lane >= lo) & (lane < lo + EXPERTS_PER_GROUP), lg, ninf)
    m1 = jnp.max(el, axis=1, keepdims=True)
    i1 = jnp.min(jnp.where(el == m1, lane, big), axis=1, keepdims=True)
    el2 = jnp.where(lane == i1, ninf, el)
    m2 = jnp.max(el2, axis=1, keepdims=True)
    i2 = jnp.min(jnp.where(el2 == m2, lane, big), axis=1, keepdims=True)
    e2 = jnp.exp(m2 - m1)
    den = 1.0 / (1.0 + e2)
    comb = jnp.where(lane == i1, den * g_w, 0.0) + jnp.where(lane == i2, e2 * den * g_w, 0.0)
    hc_ref[0, :, D_MODEL:] = jnp.where(lane == 0.0, gidx, comb)


def _mix(x, y_ssd, y_att, gs, ga, gt1, sc2, sh2, norm_w, wso, wao, wo, wrt, brt):
    bsz, seqlen, _ = x.shape
    tm = 512 if seqlen % 512 == 0 else 256
    assert seqlen % tm == 0
    nt = seqlen // tm
    mod_rows = gt1.shape[1]
    if mod_rows == 1:
        mod_spec = pl.BlockSpec((1, 1, D_MODEL), lambda b, i: (b, 0, 0))
    else:
        mod_spec = pl.BlockSpec((1, tm, D_MODEL), lambda b, i: (b, i, 0))

    def tok(width):
        return pl.BlockSpec((1, tm, width), lambda b, i: (b, i, 0))

    def full(d1, d2):
        return pl.BlockSpec((d1, d2), lambda b, i: (0, 0))

    vmem = 2 * 2 * (2 * D_MODEL * D_MODEL + ATTN_Q * D_MODEL) + 2 * tm * (8 * D_MODEL) * 4 + (12 << 20)
    return pl.pallas_call(
        _mix_kernel,
        grid=(bsz, nt),
        in_specs=[tok(D_MODEL), tok(SSD_INNER), tok(ATTN_Q), tok(D_MODEL), tok(D_MODEL),
                  mod_spec, mod_spec, mod_spec, full(1, D_MODEL),
                  full(SSD_INNER, D_MODEL), full(ATTN_Q, D_MODEL), full(D_MODEL, D_MODEL),
                  full(D_MODEL, 2 * LANES), full(1, LANES)],
        out_specs=[tok(D_MODEL), tok(D_MODEL + LANES)],
        out_shape=[jax.ShapeDtypeStruct((bsz, seqlen, D_MODEL), F32),
                   jax.ShapeDtypeStruct((bsz, seqlen, D_MODEL + LANES), F32)],
        compiler_params=_cparams(("parallel", "parallel"), vmem),
        name="mix",
    )(x, y_ssd, y_att, gs, ga, gt1, sc2, sh2, norm_w.reshape(1, -1), wso, wao, wo, wrt, brt)


def _expert_step(h2, comb, expert, wg_ref, wu_ref, wd_ref):
    lane = lax.broadcasted_iota(jnp.int32, comb.shape, 1)
    cw = jnp.sum(jnp.where(lane == expert + ROUTE_OFF, comb, 0.0), axis=1, keepdims=True)
    a = _dot(h2, wg_ref[0])
    u = _dot(h2, wu_ref[0])
    act = (_silu(a) * u * cw).astype(BF16)
    return _dot(act, wd_ref[0])


def _final_norm(x1, gt2, ffn, fw):
    x2 = x1 + gt2 * ffn
    return x2 * lax.rsqrt(jnp.mean(x2 * x2, axis=-1, keepdims=True) + EPS) * fw


def _moe_kernel(hc_ref, x1_ref, gt2_ref, fw_ref, wg_ref, wu_ref, wd_ref, y_ref, acc):
    e = pl.program_id(2)

    @pl.when(e == 0)
    def _():
        acc[...] = jnp.zeros(acc.shape, F32)

    acc[...] += _expert_step(hc_ref[0, :, :D_MODEL].astype(BF16), hc_ref[0, :, D_MODEL:], e,
                             wg_ref, wu_ref, wd_ref)

    @pl.when(e == pl.num_programs(2) - 1)
    def _():
        y_ref[0] = _final_norm(x1_ref[0], gt2_ref[0], acc[...], fw_ref[...])


def _moe(hc, x1, gt2, final_w, wg, wu, wd):
    bsz, seqlen, _ = x1.shape
    tm = 1024 if seqlen % 1024 == 0 else 256
    assert seqlen % tm == 0
    nt = seqlen // tm
    mod_rows = gt2.shape[1]
    if mod_rows == 1:
        mod_spec = pl.BlockSpec((1, 1, D_MODEL), lambda b, i, e: (b, 0, 0))
    else:
        mod_spec = pl.BlockSpec((1, tm, D_MODEL), lambda b, i, e: (b, i, 0))

    def tok(width):
        return pl.BlockSpec((1, tm, width), lambda b, i, e: (b, i, 0))

    vmem = 2 * tm * D_MODEL * (2 + 4 + 4) + tm * D_MODEL * 4 + 2 * 3 * D_MODEL * D_EXPERT * 2 + (16 << 20)
    return pl.pallas_call(
        _moe_kernel,
        grid=(bsz, nt, N_EXPERTS),
        in_specs=[tok(D_MODEL + LANES), tok(D_MODEL), mod_spec,
                  pl.BlockSpec((1, D_MODEL), lambda b, i, e: (0, 0)),
                  pl.BlockSpec((1, D_MODEL, D_EXPERT), lambda b, i, e: (e, 0, 0)),
                  pl.BlockSpec((1, D_MODEL, D_EXPERT), lambda b, i, e: (e, 0, 0)),
                  pl.BlockSpec((1, D_EXPERT, D_MODEL), lambda b, i, e: (e, 0, 0))],
        out_specs=tok(D_MODEL),
        out_shape=jax.ShapeDtypeStruct((bsz, seqlen, D_MODEL), F32),
        scratch_shapes=[pltpu.VMEM((tm, D_MODEL), F32)],
        compiler_params=_cparams(("parallel", "parallel", "arbitrary"), vmem),
        name="moe",
    )(hc, x1, gt2, final_w.reshape(1, -1), wg, wu, wd)


ROUTE_TILE = 512
DISPATCH_TILE = 512


def _move_rows(src_of, dst_of, sem, whole_src, whole_dst):
    def body(r, carry):
        pltpu.make_async_copy(src_of(r), dst_of(r), sem).start()
        return carry

    lax.fori_loop(0, DISPATCH_TILE, body, 0, unroll=8)
    pltpu.make_async_copy(whole_src, whole_dst, sem).wait()


def _dispatch_kernel(pos_ref, hc_ref, xs_in, xs_ref, sem):
    del xs_in
    base = pl.program_id(0) * DISPATCH_TILE
    _move_rows(lambda r: hc_ref.at[pl.ds(r, 1), :], lambda r: xs_ref.at[pl.ds(pos_ref[base + r], 1), :],
               sem.at[0], hc_ref, xs_ref.at[pl.ds(0, DISPATCH_TILE), :])


def _moe_routed_kernel(tg_ref, xs_ref, wg_ref, wu_ref, wd_ref, ys_ref, xb, acc):
    t, e = pl.program_id(0), pl.program_id(1)

    @pl.when(e == 0)
    def _():
        xb[...] = xs_ref[:, :D_MODEL].astype(BF16)
        acc[...] = jnp.zeros(acc.shape, F32)

    expert = tg_ref[t] * EXPERTS_PER_GROUP + e
    acc[...] += _expert_step(xb[...], xs_ref[:, D_MODEL:], expert, wg_ref, wu_ref, wd_ref)

    @pl.when(e == pl.num_programs(1) - 1)
    def _():
        ys_ref[...] = acc[...]


def _combine_kernel(pos_ref, x1_ref, gt2_ref, fw_ref, ys_ref, y_ref, ybuf, sem):
    i = pl.program_id(0)
    slot = i % 2

    def request(step, buf):
        base = step * DISPATCH_TILE

        def body(r, carry):
            pltpu.make_async_copy(ys_ref.at[pl.ds(pos_ref[base + r], 1), :], ybuf.at[buf, pl.ds(r, 1), :],
                                  sem.at[buf]).start()
            return carry

        lax.fori_loop(0, DISPATCH_TILE, body, 0, unroll=8)

    @pl.when(i == 0)
    def _():
        request(0, 0)

    @pl.when(i + 1 < pl.num_programs(0))
    def _():
        request(i + 1, 1 - slot)

    pltpu.make_async_copy(ys_ref.at[pl.ds(0, DISPATCH_TILE), :], ybuf.at[slot], sem.at[slot]).wait()
    y_ref[...] = _final_norm(x1_ref[...], gt2_ref[0], ybuf[slot], fw_ref[...])


def _moe_routed(hc, x1, gt2, final_w, wg, wu, wd):
    bsz, seqlen, width = hc.shape
    ntok = bsz * seqlen
    assert seqlen % DISPATCH_TILE == 0 and ntok % ROUTE_TILE == 0 and gt2.shape[1] == 1
    hcf, x1f = hc.reshape(ntok, width), x1.reshape(ntok, D_MODEL)
    gidx = hcf[:, D_MODEL].astype(jnp.int32)
    onehot = (gidx[:, None] == jnp.arange(N_EGROUPS, dtype=jnp.int32)[None, :]).astype(jnp.int32)
    csum = jnp.cumsum(onehot, axis=0)
    padded = (csum[-1] + ROUTE_TILE - 1) // ROUTE_TILE * ROUTE_TILE
    ends = jnp.cumsum(padded)
    pos = jnp.sum(onehot * (csum - 1 + (ends - padded)[None, :]), axis=1).astype(jnp.int32)
    n_tiles = ntok // ROUTE_TILE + N_EGROUPS
    tile_start = jnp.arange(n_tiles, dtype=jnp.int32) * ROUTE_TILE
    tile_group = jnp.minimum(jnp.sum((tile_start[:, None] >= ends[None, :]).astype(jnp.int32), axis=1),
                             N_EGROUPS - 1).astype(jnp.int32)
    nrows = n_tiles * ROUTE_TILE
    nsteps = ntok // DISPATCH_TILE
    any_spec = pl.BlockSpec(memory_space=pl.ANY)

    xs = pl.pallas_call(
        _dispatch_kernel,
        grid_spec=pltpu.PrefetchScalarGridSpec(
            num_scalar_prefetch=1, grid=(nsteps,),
            in_specs=[pl.BlockSpec((DISPATCH_TILE, width), lambda i, pos: (i, 0)), any_spec],
            out_specs=any_spec,
            scratch_shapes=[pltpu.SemaphoreType.DMA((1,))]),
        out_shape=jax.ShapeDtypeStruct((nrows, width), F32),
        input_output_aliases={2: 0},
        compiler_params=_cparams(("arbitrary",), 16 << 20),
        name="moe_dispatch",
    )(pos, hcf, jnp.zeros((nrows, width), F32))

    def w_spec(d1, d2):
        return pl.BlockSpec((1, d1, d2), lambda t, e, tg: (tg[t] * EXPERTS_PER_GROUP + e, 0, 0))

    vmem = (2 * ROUTE_TILE * (width + D_MODEL) * 4 + ROUTE_TILE * D_MODEL * (2 + 4)
            + 2 * 3 * D_MODEL * D_EXPERT * 2 + (16 << 20))
    ys = pl.pallas_call(
        _moe_routed_kernel,
        grid_spec=pltpu.PrefetchScalarGridSpec(
            num_scalar_prefetch=1, grid=(n_tiles, EXPERTS_PER_GROUP),
            in_specs=[pl.BlockSpec((ROUTE_TILE, width), lambda t, e, tg: (t, 0)),
                      w_spec(D_MODEL, D_EXPERT), w_spec(D_MODEL, D_EXPERT), w_spec(D_EXPERT, D_MODEL)],
            out_specs=pl.BlockSpec((ROUTE_TILE, D_MODEL), lambda t, e, tg: (t, 0)),
            scratch_shapes=[pltpu.VMEM((ROUTE_TILE, D_MODEL), BF16), pltpu.VMEM((ROUTE_TILE, D_MODEL), F32)]),
        out_shape=jax.ShapeDtypeStruct((nrows, D_MODEL), F32),
        compiler_params=_cparams(("parallel", "arbitrary"), vmem),
        name="moe_routed",
    )(tile_group, xs, wg, wu, wd)

    steps_per_batch = seqlen // DISPATCH_TILE
    y = pl.pallas_call(
        _combine_kernel,
        grid_spec=pltpu.PrefetchScalarGridSpec(
            num_scalar_prefetch=1, grid=(nsteps,),
            in_specs=[pl.BlockSpec((DISPATCH_TILE, D_MODEL), lambda i, pos: (i, 0)),
                      pl.BlockSpec((1, 1, D_MODEL), lambda i, pos: (i // steps_per_batch, 0, 0)),
                      pl.BlockSpec((1, D_MODEL), lambda i, pos: (0, 0)), any_spec],
            out_specs=pl.BlockSpec((DISPATCH_TILE, D_MODEL), lambda i, pos: (i, 0)),
            scratch_shapes=[pltpu.VMEM((2, DISPATCH_TILE, D_MODEL), F32), pltpu.SemaphoreType.DMA((2,))]),
        out_shape=jax.ShapeDtypeStruct((ntok, D_MODEL), F32),
        compiler_params=_cparams(("arbitrary",), 16 << 20),
        name="moe_combine",
    )(pos, x1f, gt2, final_w.reshape(1, -1), ys)
    return y.reshape(bsz, seqlen, D_MODEL)


def _prep_weights(lp):
    w_in = lp["w_in"]
    sizes = (SSD_INNER, CONV_DIM, SSD_HEADS, ATTN_Q, ATTN_KV, ATTN_KV, D_MODEL, D_MODEL)
    pts = [int(p) for p in np.cumsum(sizes)[:-1]]
    z, xbc, dt, q, k, v, gs, ga = jnp.split(w_in, pts, axis=1)
    dt = jnp.pad(dt, ((0, 0), (0, LANES - SSD_HEADS)))
    w_in_r = jnp.concatenate([z, xbc, q, k, v, gs, ga, dt], axis=1).astype(BF16)
    wrt = jnp.concatenate([lp["w_group"], lp["w_router"]], axis=1)
    wrt = jnp.pad(wrt, ((0, 0), (0, LANES - wrt.shape[1]))).astype(F32)
    wrt_hi = wrt.astype(BF16)
    wrt = jnp.concatenate([wrt_hi, (wrt - wrt_hi.astype(F32)).astype(BF16)], axis=1)
    brt = jnp.concatenate([lp["b_group"], lp["b_router"]])
    brt = jnp.pad(brt, (0, LANES - brt.shape[0])).astype(F32).reshape(1, LANES)
    return dict(w_in_r=w_in_r, wrt=wrt, brt=brt,
                wso=lp["w_ssd_out"].astype(BF16), wao=lp["w_attn_out"].astype(BF16), wo=lp["w_o"].astype(BF16),
                wg=lp["w_gate_e"].astype(BF16), wu=lp["w_up_e"].astype(BF16), wd=lp["w_down_e"].astype(BF16))


def _mods(mod, per_row_repeat):
    parts = jnp.split(mod, 6, axis=-1)
    if per_row_repeat:
        return [jnp.repeat(p, per_row_repeat, axis=0)[None] for p in parts]
    return [p[:, None, :] for p in parts]


def _layer_prompt(x, mod, lp, pw, final_w):
    bsz, seqlen, _ = x.shape
    sh1, sc1, gt1, sh2, sc2, gt2 = _mods(mod, 0)
    tables = _rope_tables(np.arange(seqlen))
    (z, xbc, dt, q, k, v, gs, ga, kmean, kaug, vaug) = _in_proj(x, sc1, sh1, lp["norm1_w"], pw["w_in_r"], tables, True)
    conv0 = jnp.zeros((bsz, SSD_CONV - 1, CONV_DIM), F32)
    ssm0 = jnp.zeros((bsz, SSD_HEADS, SSD_HEADDIM, SSD_STATE), F32)
    y_ssd, conv_new, ssm_new = _ssd(xbc, z, dt, conv0, ssm0, lp["conv_w"], lp["conv_b"], lp["dt_bias"],
                                    lp["a_log"], lp["d_skip"], lp["ssd_norm_w"])
    y_att = _moba_prompt(q, kmean, kaug, vaug)
    x1, hc = _mix(x, y_ssd, y_att, gs, ga, gt1, sc2, sh2, lp["norm2_w"],
                  pw["wso"], pw["wao"], pw["wo"], pw["wrt"], pw["brt"])
    y = _moe_routed(hc, x1, gt2, final_w, pw["wg"], pw["wu"], pw["wd"])
    kv_shape = (bsz, seqlen, ATTN_KV_HEADS, HEAD_DIM)
    return y, k.reshape(kv_shape), v.reshape(kv_shape), conv_new, ssm_new


def _layer_sample(x, mod, conv_buf, ssm_state, cache_k, cache_v, page_table, past_len, lp, pw, final_w):
    bsz, dec_seq, _ = x.shape
    ntok = bsz * dec_seq
    sh1, sc1, gt1, sh2, sc2, gt2 = _mods(mod, dec_seq)
    pos = past_len + (np.arange(ntok) % dec_seq)
    tables = _rope_tables(pos)
    xf = x.reshape(1, ntok, D_MODEL)
    z, xbc, dt, q, k, v, gs, ga = _in_proj(xf, sc1, sh1, lp["norm1_w"], pw["w_in_r"], tables, False)

    def per_b(t):
        return t.reshape(bsz, dec_seq, t.shape[-1])

    y_ssd, conv_new, ssm_new = _ssd(per_b(xbc), per_b(z), per_b(dt), conv_buf, ssm_state, lp["conv_w"],
                                    lp["conv_b"], lp["dt_bias"], lp["a_log"], lp["d_skip"], lp["ssd_norm_w"])
    n_pool, page = cache_k.shape[0], cache_k.shape[1]

    def position_minor(cache):
        return cache.transpose(0, 2, 3, 1).reshape(n_pool, ATTN_KV, page)

    y_att = _moba_sample(per_b(q), per_b(k), per_b(v), position_minor(cache_k), position_minor(cache_v),
                         page_table, past_len)
    x1, hc = _mix(xf, y_ssd.reshape(1, ntok, -1), y_att.reshape(1, ntok, -1), gs, ga, gt1, sc2, sh2,
                  lp["norm2_w"], pw["wso"], pw["wao"], pw["wo"], pw["wrt"], pw["brt"])
    y = _moe(hc, x1, gt2, final_w, pw["wg"], pw["wu"], pw["wd"])
    kv_shape = (bsz, dec_seq, ATTN_KV_HEADS, HEAD_DIM)
    return y.reshape(bsz, dec_seq, D_MODEL), k.reshape(kv_shape), v.reshape(kv_shape), conv_new, ssm_new


def kernel(x_prompt, x_sample, cache_k, cache_v, state_conv, state_ssm, page_table, c_prompt, c_sample,
           w_ada, b_ada, norm1_w, w_in, conv_w, conv_b, dt_bias, a_log, d_skip, ssd_norm_w,
           w_ssd_out, w_attn_out, w_o, norm2_w, w_group, b_group, w_router, b_router,
           w_gate_e, w_up_e, w_down_e, final_w):
    depth = w_in.shape[0]
    assert depth == 1, "the final RMSNorm is fused into the last layer's MoE kernel"
    nb_p, nb_s = c_prompt.shape[0], c_sample.shape[0]
    past_len = page_table.shape[1] * cache_k.shape[2]
    l = 0
    lp = dict(norm1_w=norm1_w[l], w_in=w_in[l], conv_w=conv_w[l], conv_b=conv_b[l], dt_bias=dt_bias[l],
              a_log=a_log[l], d_skip=d_skip[l], ssd_norm_w=ssd_norm_w[l], w_ssd_out=w_ssd_out[l],
              w_attn_out=w_attn_out[l], w_o=w_o[l], norm2_w=norm2_w[l], w_group=w_group[l], b_group=b_group[l],
              w_router=w_router[l], b_router=b_router[l], w_gate_e=w_gate_e[l], w_up_e=w_up_e[l],
              w_down_e=w_down_e[l])
    pw = _prep_weights(lp)
    c_all = jnp.concatenate([c_prompt, c_sample], axis=0)
    pad_rows = (-c_all.shape[0]) % SUBLANES
    mod = _ada_mod(jnp.pad(c_all, ((0, pad_rows), (0, 0))), w_ada[l], b_ada[l])
    yp, kp, vp, cp, sp = _layer_prompt(x_prompt, mod[:nb_p], lp, pw, final_w)
    ys, ks, vs, cs, ss = _layer_sample(x_sample, mod[nb_p:nb_p + nb_s], state_conv[l], state_ssm[l],
                                       cache_k[l], cache_v[l], page_table, past_len, lp, pw, final_w)
    return (yp, ys, kp[None], vp[None], cp[None], sp[None], ks[None], vs[None], cs[None], ss[None])
```

```python
import functools
import math

import numpy as np
import jax
import jax.numpy as jnp
from jax import lax
from jax.experimental import pallas as pl
from jax.experimental.pallas import tpu as pltpu

F32 = jnp.float32
BF16 = jnp.bfloat16
HIGHEST = lax.Precision.HIGHEST

D_MODEL = 1024
SSD_HEADDIM = 64
SSD_INNER = D_MODEL
SSD_HEADS = SSD_INNER // SSD_HEADDIM
SSD_GROUPS = 2
SSD_STATE = 128
SSD_CONV = 4
SSD_CHUNK = 128
SSD_BC = SSD_GROUPS * SSD_STATE
CONV_DIM = SSD_INNER + 2 * SSD_BC
ATTN_HEADS = 8
ATTN_KV_HEADS = 4
HEAD_DIM = 64
ATTN_Q = ATTN_HEADS * HEAD_DIM
ATTN_KV = ATTN_KV_HEADS * HEAD_DIM
ROT_DIM = HEAD_DIM // 4
ROPE_THETA = 500000.0
MOBA_BLOCK = 256
MOBA_TOPK = 3
N_EGROUPS = 4
EXPERTS_PER_GROUP = 4
N_EXPERTS = N_EGROUPS * EXPERTS_PER_GROUP
D_EXPERT = D_MODEL // 2
EPS = 1e-6

LANES = 128
SUBLANES = 8
VMEM_CAP = 56 << 20
NEG = -1e30
GRP = ATTN_HEADS // ATTN_KV_HEADS
ROUTE_OFF = N_EGROUPS

_SEG = {}
_off = 0
for _name, _w in (("z", SSD_INNER), ("xbc", CONV_DIM), ("q", ATTN_Q), ("k", ATTN_KV), ("v", ATTN_KV),
                  ("gs", D_MODEL), ("ga", D_MODEL), ("dt", LANES)):
    _SEG[_name] = (_off, _off + _w)
    _off += _w
IN_PAD = _off


def _cparams(semantics, vmem_bytes):
    return pltpu.CompilerParams(dimension_semantics=semantics,
                                vmem_limit_bytes=int(min(max(vmem_bytes, 16 << 20), VMEM_CAP)))


def _dot(a, b):
    return jnp.dot(a, b, preferred_element_type=F32)


def _dot_nt(a, b):
    return lax.dot_general(a, b, (((1,), (1,)), ((), ())), preferred_element_type=F32)


def _dot_exact(a, b):
    return jnp.dot(a, b, preferred_element_type=F32, precision=HIGHEST)


def _silu(x):
    return x * jax.nn.sigmoid(x)


def _mod_kernel(c_ref, w_ref, b_ref, o_ref):
    o_ref[...] = _dot_exact(_silu(c_ref[...]), w_ref[...]) + b_ref[...]


def _ada_mod(c_all, w_ada, b_ada):
    rows = c_all.shape[0]
    return pl.pallas_call(
        _mod_kernel,
        grid=(6,),
        in_specs=[pl.BlockSpec((rows, D_MODEL), lambda j: (0, 0)),
                  pl.BlockSpec((D_MODEL, D_MODEL), lambda j: (0, j)),
                  pl.BlockSpec((1, D_MODEL), lambda j: (0, j))],
        out_specs=pl.BlockSpec((rows, D_MODEL), lambda j: (0, j)),
        out_shape=jax.ShapeDtypeStruct((rows, 6 * D_MODEL), F32),
        compiler_params=_cparams(("arbitrary",), 24 << 20),
        name="ada_mod",
    )(c_all, w_ada, b_ada.reshape(1, -1))


def _rope_tables(pos):
    half = ROT_DIM // 2
    c = -2.0 * math.log(ROPE_THETA) / ROT_DIM
    c_hi = float(np.float32(c))
    j = jnp.arange(half, dtype=F32)
    inv_freq = jnp.exp(j * c_hi + j * (c - c_hi))
    ang = jnp.asarray(pos).astype(F32)[:, None] * inv_freq[None, :]
    cos, sin = jnp.cos(ang), jnp.sin(ang)
    n = pos.shape[0]
    one = jnp.ones((n, HEAD_DIM - ROT_DIM), F32)
    zero = jnp.zeros((n, half), F32)
    rest = jnp.zeros((n, HEAD_DIM - ROT_DIM), F32)
    a = jnp.concatenate([cos, cos, one], axis=1)
    s1 = jnp.concatenate([-sin, zero, rest], axis=1)
    s2 = jnp.concatenate([zero, sin, rest], axis=1)
    rep = LANES // HEAD_DIM
    return tuple(jnp.tile(t, (1, rep)) for t in (a, s1, s2))


def _inproj_kernel(*refs, attn_layout):
    (x_ref, sc_ref, sh_ref, nw_ref, w_ref, ra_ref, rs1_ref, rs2_ref,
     z_ref, xbc_ref, dt_ref, q_ref, k_ref, v_ref, gs_ref, ga_ref) = refs[:16]
    x = x_ref[0]
    h = x * lax.rsqrt(jnp.mean(x * x, axis=-1, keepdims=True) + EPS) * nw_ref[...]
    h = (h * (1.0 + sc_ref[0]) + sh_ref[0]).astype(BF16)

    def proj(name, out_ref):
        lo, hi = _SEG[name]
        step = 512
        for c in range(lo, hi, step):
            w = min(step, hi - c)
            out_ref[0, :, c - lo:c - lo + w] = _dot(h, w_ref[:, c:c + w])

    proj("z", z_ref)
    proj("xbc", xbc_ref)
    proj("gs", gs_ref)
    proj("ga", ga_ref)
    proj("dt", dt_ref)

    def rope(t):
        width = t.shape[1]
        reps = width // LANES
        a = jnp.concatenate([ra_ref[...]] * reps, axis=1)
        s1 = jnp.concatenate([rs1_ref[...]] * reps, axis=1)
        s2 = jnp.concatenate([rs2_ref[...]] * reps, axis=1)
        return t * a + pltpu.roll(t, width - ROT_DIM // 2, 1) * s1 + pltpu.roll(t, ROT_DIM // 2, 1) * s2

    lo, hi = _SEG["q"]
    q_ref[0] = rope(_dot(h, w_ref[:, lo:hi]))
    lo, hi = _SEG["k"]
    k = rope(_dot(h, w_ref[:, lo:hi]))
    k_ref[0] = k
    lo, hi = _SEG["v"]
    v = _dot(h, w_ref[:, lo:hi])
    v_ref[0] = v

    if attn_layout:
        kmean_ref, kaug_ref, vaug_ref = refs[16:19]
        tm = k.shape[0]
        blocks = tm // MOBA_BLOCK
        for hb in range(blocks):
            kmean_ref[0, hb] = jnp.mean(k[hb * MOBA_BLOCK:(hb + 1) * MOBA_BLOCK], axis=0, keepdims=True)
        lane = lax.broadcasted_iota(jnp.int32, (tm, HEAD_DIM), 1)
        row = lax.broadcasted_iota(jnp.int32, (tm, HEAD_DIM), 0)
        blk = blocks * pl.program_id(1)
        for hb in range(1, blocks):
            blk = blk + (row >= hb * MOBA_BLOCK).astype(jnp.int32)
        blk_onehot = (lane == blk).astype(F32)
        ones_col = (lane == 0).astype(F32)
        for kh in range(ATTN_KV_HEADS):
            sl = slice(kh * HEAD_DIM, (kh + 1) * HEAD_DIM)
            kaug_ref[0, kh] = jnp.concatenate([k[:, sl], blk_onehot], axis=1).astype(BF16)
            vaug_ref[0, kh] = jnp.concatenate([v[:, sl], ones_col], axis=1).astype(BF16)


def _in_proj(x, sc1, sh1, norm_w, w_in_r, tables, attn_layout):
    bsz, seqlen, _ = x.shape
    tm = 2 * MOBA_BLOCK if seqlen % (2 * MOBA_BLOCK) == 0 else MOBA_BLOCK
    assert seqlen % tm == 0
    nt = seqlen // tm
    blocks = tm // MOBA_BLOCK
    mod_rows = sc1.shape[1]
    if mod_rows == 1:
        mod_spec = pl.BlockSpec((1, 1, D_MODEL), lambda b, i: (b, 0, 0))
    else:
        mod_spec = pl.BlockSpec((1, tm, D_MODEL), lambda b, i: (b, i, 0))
    tab_spec = pl.BlockSpec((tm, LANES), lambda b, i: (i, 0))

    def tok(width):
        return pl.BlockSpec((1, tm, width), lambda b, i: (b, i, 0))

    def shp(width):
        return jax.ShapeDtypeStruct((bsz, seqlen, width), F32)

    widths = (SSD_INNER, CONV_DIM, LANES, ATTN_Q, ATTN_KV, ATTN_KV, D_MODEL, D_MODEL)
    out_specs = [tok(w) for w in widths]
    out_shape = [shp(w) for w in widths]
    if attn_layout:
        assert nt * blocks <= HEAD_DIM
        out_specs += [pl.BlockSpec((1, blocks, 1, ATTN_KV), lambda b, i: (b, i, 0, 0)),
                      pl.BlockSpec((1, ATTN_KV_HEADS, tm, LANES), lambda b, i: (b, 0, i, 0)),
                      pl.BlockSpec((1, ATTN_KV_HEADS, tm, LANES), lambda b, i: (b, 0, i, 0))]
        out_shape += [jax.ShapeDtypeStruct((bsz, nt * blocks, 1, ATTN_KV), F32),
                      jax.ShapeDtypeStruct((bsz, ATTN_KV_HEADS, seqlen, LANES), BF16),
                      jax.ShapeDtypeStruct((bsz, ATTN_KV_HEADS, seqlen, LANES), BF16)]
    vmem = D_MODEL * IN_PAD * 2 + 2 * tm * (D_MODEL + IN_PAD + 4 * LANES) * 4 + (10 << 20)
    return pl.pallas_call(
        functools.partial(_inproj_kernel, attn_layout=attn_layout),
        grid=(bsz, nt),
        in_specs=[tok(D_MODEL), mod_spec, mod_spec,
                  pl.BlockSpec((1, D_MODEL), lambda b, i: (0, 0)),
                  pl.BlockSpec((D_MODEL, IN_PAD), lambda b, i: (0, 0), pipeline_mode=pl.Buffered(1)),
                  tab_spec, tab_spec, tab_spec],
        out_specs=out_specs,
        out_shape=out_shape,
        compiler_params=_cparams(("parallel", "parallel"), vmem),
        name="in_proj",
    )(x, sc1, sh1, norm_w.reshape(1, -1), w_in_r, *tables)


def _cumsum_rows(a):
    row = lax.broadcasted_iota(jnp.int32, a.shape, 0)
    s = 1
    while s < a.shape[0]:
        a = a + jnp.where(row >= s, pltpu.roll(a, s, 0), 0.0)
        s *= 2
    return a


def _ssd_kernel(xbc_ref, z_ref, dt_ref, cbuf_ref, s0_ref, cw_ref, cb_ref, dtb_ref, alog_ref, dsk_ref, nw_ref,
                y_ref, cnew_ref, sout_ref, xext, st, ysc, *, rows):
    ch = SSD_CHUNK
    pad = SUBLANES
    c = pl.program_id(1)
    last = pl.num_programs(1) - 1

    @pl.when(c == 0)
    def _():
        xext[0:pad, :] = jnp.zeros((pad, CONV_DIM), F32)
        xext[pad - (SSD_CONV - 1):pad, :] = cbuf_ref[0]
        st[...] = s0_ref[0].T

    xext[pad:pad + rows, :] = xbc_ref[0]
    if rows < ch:
        xext[pad + rows:pad + ch, :] = jnp.zeros((ch - rows, CONV_DIM), F32)

    conv = cb_ref[...] + xext[pad:pad + ch, :] * cw_ref[SSD_CONV - 1:SSD_CONV, :]
    for w in range(SSD_CONV - 1):
        off = pad - (SSD_CONV - 1) + w
        conv = conv + xext[off:off + ch, :] * cw_ref[w:w + 1, :]
    act = _silu(conv)

    @pl.when(c == last)
    def _():
        cnew_ref[0] = xext[pad + rows - (SSD_CONV - 1):pad + rows, :]

    xext[0:pad, :] = xext[ch:ch + pad, :]

    xs = act[:, :SSD_INNER]
    dt_raw = dt_ref[0]
    if rows < ch:
        dt_raw = jnp.concatenate([dt_raw, jnp.zeros((ch - rows, LANES), F32)], axis=0)
    dt_in = dt_raw + dtb_ref[...]
    dt = jnp.maximum(dt_in, 0.0) + jnp.log1p(jnp.exp(-jnp.abs(dt_in)))
    row = lax.broadcasted_iota(jnp.int32, (ch, LANES), 0)
    col = lax.broadcasted_iota(jnp.int32, (ch, LANES), 1)
    if rows < ch:
        dt = jnp.where(row < rows, dt, 0.0)
    a_cs = _cumsum_rows(dt * (-jnp.exp(alog_ref[...])))
    a_cs_t = a_cs.T
    dt_t = dt.T
    causal = row >= col

    heads_per_pair = LANES // SSD_HEADDIM
    assert heads_per_pair == 2 and SSD_STATE == ch
    first_of_pair = col < SSD_HEADDIM
    for g in range(SSD_GROUPS):
        bg = act[:, SSD_INNER + g * SSD_STATE:SSD_INNER + (g + 1) * SSD_STATE]
        cg = act[:, SSD_INNER + SSD_BC + g * SSD_STATE:SSD_INNER + SSD_BC + (g + 1) * SSD_STATE]
        gmat = _dot_nt(cg.astype(BF16), bg.astype(BF16))
        bg_t = bg.T
        for pair in range(SSD_HEADS // SSD_GROUPS // heads_per_pair):
            hd0 = g * (SSD_HEADS // SSD_GROUPS) + pair * heads_per_pair
            sl = slice(hd0 * SSD_HEADDIM, hd0 * SSD_HEADDIM + LANES)
            x_pair = xs[:, sl]
            s_pair = st[:, sl]
            x_diag = [jnp.where(first_of_pair, x_pair, 0.0).astype(BF16),
                      jnp.where(first_of_pair, 0.0, x_pair).astype(BF16)]
            s_diag = [jnp.where(first_of_pair, s_pair, 0.0).astype(BF16),
                      jnp.where(first_of_pair, 0.0, s_pair).astype(BF16)]
            m_parts, c_parts, b_parts, decay = [], [], [], []
            for hd in range(hd0, hd0 + heads_per_pair):
                acs_col = a_cs[:, hd:hd + 1]
                acs_row = a_cs_t[hd:hd + 1, :]
                dt_row = dt_t[hd:hd + 1, :]
                acs_last = a_cs_t[hd:hd + 1, ch - 1:ch]
                lmat = jnp.where(causal, jnp.exp(acs_col - acs_row), 0.0)
                m_parts.append((gmat * lmat * dt_row).astype(BF16))
                c_parts.append((cg * jnp.exp(acs_col)).astype(BF16))
                b_parts.append((bg_t * (jnp.exp(acs_last - acs_row) * dt_row)).astype(BF16))
                decay.append(jnp.exp(acs_last))
            ysc[:, sl] = _dot(jnp.concatenate(m_parts + c_parts, axis=1), jnp.concatenate(x_diag + s_diag, axis=0))
            st[:, sl] = (jnp.where(first_of_pair[:1], decay[0], decay[1]) * s_pair
                         + _dot(jnp.concatenate(b_parts, axis=1), jnp.concatenate(x_diag, axis=0)))

    y = (ysc[0:rows, :] + xs[:rows] * dsk_ref[...]) * _silu(z_ref[0])
    gw = SSD_INNER // SSD_GROUPS
    for g in range(SSD_GROUPS):
        yg = y[:, g * gw:(g + 1) * gw]
        yg = yg * lax.rsqrt(jnp.mean(yg * yg, axis=-1, keepdims=True) + EPS)
        y_ref[0, :, g * gw:(g + 1) * gw] = yg * nw_ref[:, g * gw:(g + 1) * gw]

    @pl.when(c == last)
    def _():
        sout_ref[0] = st[...].T


def _ssd(xbc, z, dt, conv_buf, ssm_state, conv_w, conv_b, dt_bias, a_log, d_skip, norm_w):
    bsz, seqlen, _ = xbc.shape
    rows = SSD_CHUNK if seqlen % SSD_CHUNK == 0 else seqlen
    assert rows == seqlen or rows == SSD_CHUNK
    assert rows % SUBLANES == 0 and rows >= SSD_CONV - 1
    nc = seqlen // rows
    hp = SSD_HEADS * SSD_HEADDIM

    def tok(width):
        return pl.BlockSpec((1, rows, width), lambda b, c: (b, c, 0))

    def per_b(d1, d2):
        return pl.BlockSpec((1, d1, d2), lambda b, c: (b, 0, 0))

    def vec(width, r=1):
        return pl.BlockSpec((r, width), lambda b, c: (0, 0))

    def pad_lanes(v):
        return jnp.pad(v.astype(F32), (0, LANES - v.shape[0])).reshape(1, LANES)

    y, conv_new, s_new = pl.pallas_call(
        functools.partial(_ssd_kernel, rows=rows),
        grid=(bsz, nc),
        in_specs=[tok(CONV_DIM), tok(SSD_INNER), tok(LANES),
                  per_b(SSD_CONV - 1, CONV_DIM), per_b(hp, SSD_STATE),
                  vec(CONV_DIM, SSD_CONV), vec(CONV_DIM), vec(LANES), vec(LANES), vec(SSD_INNER), vec(SSD_INNER)],
        out_specs=[tok(SSD_INNER), per_b(SSD_CONV - 1, CONV_DIM), per_b(hp, SSD_STATE)],
        out_shape=[jax.ShapeDtypeStruct((bsz, seqlen, SSD_INNER), F32),
                   jax.ShapeDtypeStruct((bsz, SSD_CONV - 1, CONV_DIM), F32),
                   jax.ShapeDtypeStruct((bsz, hp, SSD_STATE), F32)],
        scratch_shapes=[pltpu.VMEM((SSD_CHUNK + SUBLANES, CONV_DIM), F32),
                        pltpu.VMEM((SSD_STATE, hp), F32),
                        pltpu.VMEM((SSD_CHUNK, SSD_INNER), F32)],
        compiler_params=_cparams(("parallel", "arbitrary"), 40 << 20),
        name="ssd",
    )(xbc, z, dt, conv_buf, ssm_state.reshape(bsz, hp, SSD_STATE),
      conv_w, conv_b.reshape(1, -1), pad_lanes(dt_bias), pad_lanes(a_log),
      jnp.repeat(d_skip.astype(F32), SSD_HEADDIM).reshape(1, -1), norm_w.reshape(1, -1))
    return y, conv_new, s_new.reshape(bsz, SSD_HEADS, SSD_HEADDIM, SSD_STATE)


def _select_topk(scores, valid, axis):
    sc = scores if valid is None else jnp.where(valid, scores, -jnp.inf)
    sel = jnp.zeros(scores.shape, F32)
    pos_f = lax.broadcasted_iota(jnp.int32, scores.shape, axis).astype(F32)
    for _ in range(MOBA_TOPK):
        m = jnp.max(sc, axis=axis, keepdims=True)
        idx = jnp.min(jnp.where(sc == m, pos_f, float(1 << 20)), axis=axis, keepdims=True)
        idx = jnp.where(m > -jnp.inf, idx, -1.0)
        hit = pos_f == idx
        sel = jnp.where(hit, 1.0, sel)
        sc = jnp.where(hit, -jnp.inf, sc)
    return sel > 0.5


def _moba_prompt_kernel(q_ref, km_ref, kaug_ref, vaug_ref, o_ref, s_a, s_b, m_sc, acc_sc, *, qblocks):
    j = pl.program_id(2)
    tq = MOBA_BLOCK
    chunk = qblocks * tq
    half = GRP * tq
    q2 = q_ref[0]
    qs = jnp.concatenate([q2[b * tq:(b + 1) * tq, g * HEAD_DIM:(g + 1) * HEAD_DIM]
                          for b in range(qblocks) for g in range(GRP)], axis=0)
    rows = qs.shape[0]
    lane = lax.broadcasted_iota(jnp.int32, (rows, LANES), 1)
    scores_t = lax.dot_general(km_ref[0, 0], qs, (((1,), (1,)), ((), ())), preferred_element_type=F32,
                               precision=HIGHEST)
    nbp = scores_t.shape[0]
    blk = lax.broadcasted_iota(jnp.int32, scores_t.shape, 0)
    col = lax.broadcasted_iota(jnp.int32, scores_t.shape, 1)
    own = qblocks * j
    for b in range(1, qblocks):
        own = own + (col >= b * half).astype(jnp.int32)
    sel_t = _select_topk(scores_t, blk < own, 0)
    pen_t = jnp.concatenate([jnp.zeros((HEAD_DIM, rows), F32), jnp.where(sel_t, 0.0, NEG),
                             jnp.full((LANES - HEAD_DIM - nbp, rows), NEG, F32)], axis=0)
    pen = pen_t.T
    q_pad = jnp.concatenate([qs * (HEAD_DIM ** -0.5), jnp.zeros((rows, LANES - HEAD_DIM), F32)], axis=1)
    q_aug = jnp.where(lane < HEAD_DIM, q_pad, pen).astype(BF16)
    q_own = q_pad.astype(BF16)

    def update(m, acc, s, pv):
        m_new = jnp.maximum(m, jnp.max(s, axis=1, keepdims=True))
        p = jnp.exp(s - m_new)
        acc = jnp.exp(m - m_new) * acc + pv(p.astype(BF16))
        return m_new, acc

    total_chunks = kaug_ref.shape[2] // chunk
    nchunks = j + 1
    npairs = nchunks // 2

    def logits(c):
        start = pl.multiple_of(jnp.minimum(c, total_chunks - 1) * chunk, chunk)
        return _dot_nt(q_aug, kaug_ref[0, 0, pl.ds(start, chunk), :])

    def values(c):
        vb = vaug_ref[0, 0, pl.ds(pl.multiple_of(c * chunk, chunk), chunk), :]
        return lambda p: _dot(p, vb)

    def body(t, carry):
        m, acc = carry
        c0 = 2 * t
        s_b[...] = logits(c0 + 1)
        m, acc = update(m, acc, s_a[...], values(c0))
        s_a[...] = logits(c0 + 2)
        m, acc = update(m, acc, s_b[...], values(c0 + 1))
        return m, acc

    def own_rows(ref, b):
        return ref[0, 0, pl.ds(pl.multiple_of((qblocks * j + b) * tq, tq), tq), :]

    s_own = jnp.concatenate([_dot_nt(q_own[b * half:(b + 1) * half], own_rows(kaug_ref, b))
                             for b in range(qblocks)], axis=0)
    r = lax.broadcasted_iota(jnp.int32, (rows, tq), 0)
    cidx = lax.broadcasted_iota(jnp.int32, (rows, tq), 1)
    s_own = jnp.where(cidx <= (r & (tq - 1)), s_own, NEG)

    def own_values(p):
        return jnp.concatenate([_dot(p[b * half:(b + 1) * half], own_rows(vaug_ref, b))
                                for b in range(qblocks)], axis=0)

    s_a[...] = logits(0)
    init = (jnp.full((rows, 1), -jnp.inf, F32), jnp.zeros((rows, LANES), F32))
    m, acc = lax.fori_loop(0, npairs, body, init)
    m_sc[...] = m
    acc_sc[...] = acc

    @pl.when(nchunks % 2 == 1)
    def _():
        m_odd, acc_odd = update(m_sc[...], acc_sc[...], s_a[...], values(nchunks - 1))
        m_sc[...] = m_odd
        acc_sc[...] = acc_odd

    _, acc = update(m_sc[...], acc_sc[...], s_own, own_values)
    out = acc[:, :HEAD_DIM] / acc[:, HEAD_DIM:HEAD_DIM + 1]
    o_ref[0] = jnp.concatenate(
        [jnp.concatenate([out[(b * GRP + g) * tq:(b * GRP + g + 1) * tq] for g in range(GRP)], axis=1)
         for b in range(qblocks)], axis=0)


def _moba_prompt(q, kmean, kaug, vaug):
    bsz, seqlen, _ = q.shape
    nb = seqlen // MOBA_BLOCK
    assert GRP * HEAD_DIM == LANES and nb <= LANES - HEAD_DIM and MOBA_BLOCK & (MOBA_BLOCK - 1) == 0
    nbp = -(-nb // SUBLANES) * SUBLANES
    km = kmean.reshape(bsz, nb, ATTN_KV_HEADS, HEAD_DIM).transpose(0, 2, 1, 3)
    km = jnp.pad(km, ((0, 0), (0, 0), (0, nbp - nb), (0, 0)))
    kv_spec = pl.BlockSpec((1, 1, seqlen, LANES), lambda b, h, i: (b, h, 0, 0))
    qblocks = 4
    chunk = qblocks * MOBA_BLOCK
    assert seqlen % chunk == 0
    rows = qblocks * GRP * MOBA_BLOCK
    vmem = 2 * 2 * seqlen * LANES * 2 + 2 * rows * chunk * 4 + (24 << 20)
    return pl.pallas_call(
        functools.partial(_moba_prompt_kernel, qblocks=qblocks),
        scratch_shapes=[pltpu.VMEM((rows, chunk), F32), pltpu.VMEM((rows, chunk), F32),
                        pltpu.VMEM((rows, 1), F32), pltpu.VMEM((rows, LANES), F32)],
        grid=(bsz, ATTN_KV_HEADS, nb // qblocks),
        in_specs=[pl.BlockSpec((1, chunk, LANES), lambda b, h, i: (b, i, h)),
                  pl.BlockSpec((1, 1, nbp, HEAD_DIM), lambda b, h, i: (b, h, 0, 0)),
                  kv_spec, kv_spec],
        out_specs=pl.BlockSpec((1, chunk, LANES), lambda b, h, i: (b, i, h)),
        out_shape=jax.ShapeDtypeStruct((bsz, seqlen, ATTN_Q), F32),
        compiler_params=_cparams(("parallel", "parallel", "arbitrary"), vmem),
        name="moba_prompt",
    )(q, km, kaug, vaug)


def _moba_sample_kernel(pt_ref, qbt_ref, qb_ref, *refs, pages_per_step, nsteps, nblk, dec_seq):
    del pt_ref
    k_refs = refs[:pages_per_step]
    v_refs = refs[pages_per_step:2 * pages_per_step]
    kn_ref, vn_ref, o_ref, s_sc, sct_sc, pen_sc, m_sc, l_sc, acc_sc = refs[2 * pages_per_step:]
    step = pl.program_id(1)
    rows = qb_ref.shape[1]
    pages_per_block = MOBA_BLOCK // LANES
    blocks_per_step = pages_per_step // pages_per_block
    qb = (qb_ref[0] * (HEAD_DIM ** -0.5)).astype(BF16)

    @pl.when(step < nsteps)
    def _():
        qbt = qbt_ref[0]
        for jb in range(blocks_per_step):
            ksum = None
            for j in range(pages_per_block):
                kpage = k_refs[jb * pages_per_block + j][0]
                ksum = kpage if ksum is None else ksum + kpage
            kmean = jnp.sum(ksum, axis=1, keepdims=True) * (1.0 / MOBA_BLOCK)
            sct_sc[pl.ds(step * blocks_per_step + jb, 1), :] = jnp.sum(qbt * kmean, axis=0, keepdims=True)
        s_sc[step] = _dot(qb, jnp.concatenate([k_ref[0].astype(BF16) for k_ref in k_refs], axis=1))

    @pl.when(step >= nsteps)
    def _():
        n = step - nsteps
        lane = lax.broadcasted_iota(jnp.int32, (rows, LANES), 1)

        @pl.when(n == 0)
        def _():
            sel_t = _select_topk(sct_sc[...], None, 0)
            pen_t = jnp.concatenate([jnp.where(sel_t, 0.0, NEG), jnp.full((LANES - nblk, LANES), NEG, F32)],
                                    axis=0)
            pen_sc[...] = pen_t.T[:rows]
            m_sc[...] = jnp.full(m_sc.shape, -jnp.inf, F32)
            l_sc[...] = jnp.zeros(l_sc.shape, F32)
            acc_sc[...] = jnp.zeros(acc_sc.shape, F32)

        def update(s, pv_fn):
            m = m_sc[...]
            m_new = jnp.maximum(m, jnp.max(s, axis=1, keepdims=True))
            p = jnp.exp(s - m_new)
            alpha = jnp.exp(m - m_new)
            l_sc[...] = alpha * l_sc[...] + jnp.sum(p, axis=1, keepdims=True)
            acc_sc[...] = alpha * acc_sc[...] + pv_fn(p.astype(BF16))
            m_sc[...] = m_new

        pen = pen_sc[...]
        pen_cols = []
        for jb in range(blocks_per_step):
            pen_col = jnp.sum(jnp.where(lane == n * blocks_per_step + jb, pen, 0.0), axis=1, keepdims=True)
            pen_cols.append(jnp.broadcast_to(pen_col, (rows, MOBA_BLOCK)))
        vt_all = jnp.concatenate([v_ref[0].astype(BF16) for v_ref in v_refs], axis=1)
        update(s_sc[n] + jnp.concatenate(pen_cols, axis=1), lambda p: _dot_nt(p, vt_all))

        @pl.when(n == nsteps - 1)
        def _():
            zpad = jnp.zeros((LANES - dec_seq, kn_ref.shape[2]), F32)
            kn = jnp.concatenate([kn_ref[0], zpad], axis=0).astype(BF16)
            vn = jnp.concatenate([vn_ref[0], zpad], axis=0).astype(BF16)
            s_own = _dot_nt(qb, kn)
            r = lax.broadcasted_iota(jnp.int32, (rows, LANES), 0)
            qpos = r & (dec_seq - 1)
            s_own = jnp.where((lane <= qpos) & (lane < dec_seq), s_own, NEG)
            update(s_own, lambda p: _dot(p, vn))
            o_ref[0] = acc_sc[...] / l_sc[...]


def _moba_sample(q, k_new, v_new, cache_kt, cache_vt, page_table, past_len):
    bsz, dec_seq, _ = q.shape
    page = cache_kt.shape[2]
    assert page == LANES and MOBA_BLOCK % page == 0
    assert past_len % MOBA_BLOCK == 0 and dec_seq == SUBLANES
    nblk = past_len // MOBA_BLOCK
    assert MOBA_TOPK <= nblk <= LANES and nblk % SUBLANES == 0
    n_pages = past_len // page
    pages_per_step = 32
    while n_pages % pages_per_step:
        pages_per_step //= 2
    assert pages_per_step * page >= MOBA_BLOCK
    nsteps = n_pages // pages_per_step
    rows = ATTN_HEADS * dec_seq
    q4 = q.reshape(bsz, dec_seq, ATTN_HEADS, HEAD_DIM).transpose(0, 2, 1, 3)
    own = (np.arange(ATTN_HEADS)[:, None] // GRP == np.arange(ATTN_KV_HEADS)[None, :]).astype(np.float32)
    qb = (q4[:, :, :, None, :] * jnp.asarray(own)[None, :, None, :, None]).reshape(bsz, rows, ATTN_KV)
    qbt = jnp.pad(qb.transpose(0, 2, 1), ((0, 0), (0, 0), (0, LANES - rows)))

    def k_spec(j):
        return pl.BlockSpec((1, ATTN_KV, page),
                            lambda b, s, pt: (pt[b, pages_per_step * jnp.minimum(s, nsteps - 1) + j], 0, 0))

    def v_spec(j):
        return pl.BlockSpec((1, ATTN_KV, page),
                            lambda b, s, pt: (pt[b, pages_per_step * jnp.maximum(s - nsteps, 0) + j], 0, 0))

    def per_b(d1, d2):
        return pl.BlockSpec((1, d1, d2), lambda b, s, pt: (b, 0, 0))

    out = pl.pallas_call(
        functools.partial(_moba_sample_kernel, pages_per_step=pages_per_step, nsteps=nsteps, nblk=nblk,
                          dec_seq=dec_seq),
        grid_spec=pltpu.PrefetchScalarGridSpec(
            num_scalar_prefetch=1, grid=(bsz, 2 * nsteps),
            in_specs=[per_b(ATTN_KV, LANES), per_b(rows, ATTN_KV)]
                     + [k_spec(j) for j in range(pages_per_step)] + [v_spec(j) for j in range(pages_per_step)]
                     + [per_b(dec_seq, ATTN_KV), per_b(dec_seq, ATTN_KV)],
            out_specs=per_b(rows, ATTN_KV),
            scratch_shapes=[pltpu.VMEM((nsteps, rows, pages_per_step * page), F32), pltpu.VMEM((nblk, LANES), F32),
                            pltpu.VMEM((rows, LANES), F32), pltpu.VMEM((rows, 1), F32), pltpu.VMEM((rows, 1), F32),
                            pltpu.VMEM((rows, ATTN_KV), F32)]),
        out_shape=jax.ShapeDtypeStruct((bsz, rows, ATTN_KV), F32),
        compiler_params=_cparams(("parallel", "arbitrary"), 32 << 20),
        name="moba_sample_attn",
    )(page_table, qbt, qb, *([cache_kt] * pages_per_step), *([cache_vt] * pages_per_step), k_new, v_new)
    o5 = out.reshape(bsz, ATTN_HEADS, dec_seq, ATTN_KV_HEADS, HEAD_DIM)
    o4 = jnp.sum(o5 * jnp.asarray(own)[None, :, None, :, None], axis=3)
    return o4.transpose(0, 2, 1, 3).reshape(bsz, dec_seq, ATTN_Q)


def _mix_kernel(x_ref, ys_ref, ya_ref, gs_ref, ga_ref, gt1_ref, sc2_ref, sh2_ref, nw_ref,
                wso_ref, wao_ref, wo_ref, wrt_ref, brt_ref, x1_ref, hc_ref):
    merged = (jax.nn.sigmoid(gs_ref[0]) * _dot(ys_ref[0].astype(BF16), wso_ref[...])
              + jax.nn.sigmoid(ga_ref[0]) * _dot(ya_ref[0].astype(BF16), wao_ref[...]))
    x1 = x_ref[0] + gt1_ref[0] * _dot(merged.astype(BF16), wo_ref[...])
    x1_ref[0] = x1
    h2 = x1 * lax.rsqrt(jnp.mean(x1 * x1, axis=-1, keepdims=True) + EPS) * nw_ref[...]
    h2 = h2 * (1.0 + sc2_ref[0]) + sh2_ref[0]
    hc_ref[0, :, :D_MODEL] = h2

    h_hi = h2.astype(BF16)
    h_lo = (h2 - h_hi.astype(F32)).astype(BF16)
    hw = _dot(h_hi, wrt_ref[...])
    lg = hw[:, :LANES] + hw[:, LANES:] + _dot(h_lo, wrt_ref[:, :LANES]) + brt_ref[...]
    lane = lax.broadcasted_iota(jnp.int32, lg.shape, 1).astype(F32)
    big = float(1 << 20)
    ninf = -jnp.inf
    gl = jnp.where(lane < N_EGROUPS, lg, ninf)
    gmax = jnp.max(gl, axis=1, keepdims=True)
    gidx = jnp.min(jnp.where(gl == gmax, lane, big), axis=1, keepdims=True)
    g_w = 1.0 / jnp.sum(jnp.exp(gl - gmax), axis=1, keepdims=True)
    lo = ROUTE_OFF + gidx * EXPERTS_PER_GROUP
    el = jnp.where((lane >= lo) & (lane < lo + EXPERTS_PER_GROUP), lg, ninf)
    m1 = jnp.max(el, axis=1, keepdims=True)
    i1 = jnp.min(jnp.where(el == m1, lane, big), axis=1, keepdims=True)
    el2 = jnp.where(lane == i1, ninf, el)
    m2 = jnp.max(el2, axis=1, keepdims=True)
    i2 = jnp.min(jnp.where(el2 == m2, lane, big), axis=1, keepdims=True)
    e2 = jnp.exp(m2 - m1)
    den = 1.0 / (1.0 + e2)
    comb = jnp.where(lane == i1, den * g_w, 0.0) + jnp.where(lane == i2, e2 * den * g_w, 0.0)
    hc_ref[0, :, D_MODEL:] = jnp.where(lane == 0.0, gidx, comb)


def _mix(x, y_ssd, y_att, gs, ga, gt1, sc2, sh2, norm_w, wso, wao, wo, wrt, brt):
    bsz, seqlen, _ = x.shape
    tm = 512 if seqlen % 512 == 0 else 256
    assert seqlen % tm == 0
    nt = seqlen // tm
    mod_rows = gt1.shape[1]
    if mod_rows == 1:
        mod_spec = pl.BlockSpec((1, 1, D_MODEL), lambda b, i: (b, 0, 0))
    else:
        mod_spec = pl.BlockSpec((1, tm, D_MODEL), lambda b, i: (b, i, 0))

    def tok(width):
        return pl.BlockSpec((1, tm, width), lambda b, i: (b, i, 0))

    def full(d1, d2):
        return pl.BlockSpec((d1, d2), lambda b, i: (0, 0))

    vmem = 2 * 2 * (2 * D_MODEL * D_MODEL + ATTN_Q * D_MODEL) + 2 * tm * (8 * D_MODEL) * 4 + (12 << 20)
    return pl.pallas_call(
        _mix_kernel,
        grid=(bsz, nt),
        in_specs=[tok(D_MODEL), tok(SSD_INNER), tok(ATTN_Q), tok(D_MODEL), tok(D_MODEL),
                  mod_spec, mod_spec, mod_spec, full(1, D_MODEL),
                  full(SSD_INNER, D_MODEL), full(ATTN_Q, D_MODEL), full(D_MODEL, D_MODEL),
                  full(D_MODEL, 2 * LANES), full(1, LANES)],
        out_specs=[tok(D_MODEL), tok(D_MODEL + LANES)],
        out_shape=[jax.ShapeDtypeStruct((bsz, seqlen, D_MODEL), F32),
                   jax.ShapeDtypeStruct((bsz, seqlen, D_MODEL + LANES), F32)],
        compiler_params=_cparams(("parallel", "parallel"), vmem),
        name="mix",
    )(x, y_ssd, y_att, gs, ga, gt1, sc2, sh2, norm_w.reshape(1, -1), wso, wao, wo, wrt, brt)


def _expert_step(h2, comb, expert, wg_ref, wu_ref, wd_ref):
    lane = lax.broadcasted_iota(jnp.int32, comb.shape, 1)
    cw = jnp.sum(jnp.where(lane == expert + ROUTE_OFF, comb, 0.0), axis=1, keepdims=True)
    a = _dot(h2, wg_ref[0])
    u = _dot(h2, wu_ref[0])
    act = (_silu(a) * u * cw).astype(BF16)
    return _dot(act, wd_ref[0])


def _final_norm(x1, gt2, ffn, fw):
    x2 = x1 + gt2 * ffn
    return x2 * lax.rsqrt(jnp.mean(x2 * x2, axis=-1, keepdims=True) + EPS) * fw


def _moe_kernel(hc_ref, x1_ref, gt2_ref, fw_ref, wg_ref, wu_ref, wd_ref, y_ref, acc):
    e = pl.program_id(2)

    @pl.when(e == 0)
    def _():
        acc[...] = jnp.zeros(acc.shape, F32)

    acc[...] += _expert_step(hc_ref[0, :, :D_MODEL].astype(BF16), hc_ref[0, :, D_MODEL:], e,
                             wg_ref, wu_ref, wd_ref)

    @pl.when(e == pl.num_programs(2) - 1)
    def _():
        y_ref[0] = _final_norm(x1_ref[0], gt2_ref[0], acc[...], fw_ref[...])


def _moe(hc, x1, gt2, final_w, wg, wu, wd):
    bsz, seqlen, _ = x1.shape
    tm = 1024 if seqlen % 1024 == 0 else 256
    assert seqlen % tm == 0
    nt = seqlen // tm
    mod_rows = gt2.shape[1]
    if mod_rows == 1:
        mod_spec = pl.BlockSpec((1, 1, D_MODEL), lambda b, i, e: (b, 0, 0))
    else:
        mod_spec = pl.BlockSpec((1, tm, D_MODEL), lambda b, i, e: (b, i, 0))

    def tok(width):
        return pl.BlockSpec((1, tm, width), lambda b, i, e: (b, i, 0))

    vmem = 2 * tm * D_MODEL * (2 + 4 + 4) + tm * D_MODEL * 4 + 2 * 3 * D_MODEL * D_EXPERT * 2 + (16 << 20)
    return pl.pallas_call(
        _moe_kernel,
        grid=(bsz, nt, N_EXPERTS),
        in_specs=[tok(D_MODEL + LANES), tok(D_MODEL), mod_spec,
                  pl.BlockSpec((1, D_MODEL), lambda b, i, e: (0, 0)),
                  pl.BlockSpec((1, D_MODEL, D_EXPERT), lambda b, i, e: (e, 0, 0)),
                  pl.BlockSpec((1, D_MODEL, D_EXPERT), lambda b, i, e: (e, 0, 0)),
                  pl.BlockSpec((1, D_EXPERT, D_MODEL), lambda b, i, e: (e, 0, 0))],
        out_specs=tok(D_MODEL),
        out_shape=jax.ShapeDtypeStruct((bsz, seqlen, D_MODEL), F32),
        scratch_shapes=[pltpu.VMEM((tm, D_MODEL), F32)],
        compiler_params=_cparams(("parallel", "parallel", "arbitrary"), vmem),
        name="moe",
    )(hc, x1, gt2, final_w.reshape(1, -1), wg, wu, wd)


ROUTE_TILE = 512
DISPATCH_TILE = 512


def _move_rows(src_of, dst_of, sem, whole_src, whole_dst):
    def body(r, carry):
        pltpu.make_async_copy(src_of(r), dst_of(r), sem).start()
        return carry

    lax.fori_loop(0, DISPATCH_TILE, body, 0, unroll=8)
    pltpu.make_async_copy(whole_src, whole_dst, sem).wait()


def _dispatch_kernel(pos_ref, hc_ref, xs_in, xs_ref, sem):
    del xs_in
    base = pl.program_id(0) * DISPATCH_TILE
    _move_rows(lambda r: hc_ref.at[pl.ds(r, 1), :], lambda r: xs_ref.at[pl.ds(pos_ref[base + r], 1), :],
               sem.at[0], hc_ref, xs_ref.at[pl.ds(0, DISPATCH_TILE), :])


def _moe_routed_kernel(tg_ref, xs_ref, wg_ref, wu_ref, wd_ref, ys_ref, xb, acc):
    t, e = pl.program_id(0), pl.program_id(1)

    @pl.when(e == 0)
    def _():
        xb[...] = xs_ref[:, :D_MODEL].astype(BF16)
        acc[...] = jnp.zeros(acc.shape, F32)

    expert = tg_ref[t] * EXPERTS_PER_GROUP + e
    acc[...] += _expert_step(xb[...], xs_ref[:, D_MODEL:], expert, wg_ref, wu_ref, wd_ref)

    @pl.when(e == pl.num_programs(1) - 1)
    def _():
        ys_ref[...] = acc[...]


def _combine_kernel(pos_ref, x1_ref, gt2_ref, fw_ref, ys_ref, y_ref, ybuf, sem):
    i = pl.program_id(0)
    slot = i % 2

    def request(step, buf):
        base = step * DISPATCH_TILE

        def body(r, carry):
            pltpu.make_async_copy(ys_ref.at[pl.ds(pos_ref[base + r], 1), :], ybuf.at[buf, pl.ds(r, 1), :],
                                  sem.at[buf]).start()
            return carry

        lax.fori_loop(0, DISPATCH_TILE, body, 0, unroll=8)

    @pl.when(i == 0)
    def _():
        request(0, 0)

    @pl.when(i + 1 < pl.num_programs(0))
    def _():
        request(i + 1, 1 - slot)

    pltpu.make_async_copy(ys_ref.at[pl.ds(0, DISPATCH_TILE), :], ybuf.at[slot], sem.at[slot]).wait()
    y_ref[...] = _final_norm(x1_ref[...], gt2_ref[0], ybuf[slot], fw_ref[...])


def _moe_routed(hc, x1, gt2, final_w, wg, wu, wd):
    bsz, seqlen, width = hc.shape
    ntok = bsz * seqlen
    assert seqlen % DISPATCH_TILE == 0 and ntok % ROUTE_TILE == 0 and gt2.shape[1] == 1
    hcf, x1f = hc.reshape(ntok, width), x1.reshape(ntok, D_MODEL)
    gidx = hcf[:, D_MODEL].astype(jnp.int32)
    onehot = (gidx[:, None] == jnp.arange(N_EGROUPS, dtype=jnp.int32)[None, :]).astype(jnp.int32)
    csum = jnp.cumsum(onehot, axis=0)
    padded = (csum[-1] + ROUTE_TILE - 1) // ROUTE_TILE * ROUTE_TILE
    ends = jnp.cumsum(padded)
    pos = jnp.sum(onehot * (csum - 1 + (ends - padded)[None, :]), axis=1).astype(jnp.int32)
    n_tiles = ntok // ROUTE_TILE + N_EGROUPS
    tile_start = jnp.arange(n_tiles, dtype=jnp.int32) * ROUTE_TILE
    tile_group = jnp.minimum(jnp.sum((tile_start[:, None] >= ends[None, :]).astype(jnp.int32), axis=1),
                             N_EGROUPS - 1).astype(jnp.int32)
    nrows = n_tiles * ROUTE_TILE
    nsteps = ntok // DISPATCH_TILE
    any_spec = pl.BlockSpec(memory_space=pl.ANY)

    xs = pl.pallas_call(
        _dispatch_kernel,
        grid_spec=pltpu.PrefetchScalarGridSpec(
            num_scalar_prefetch=1, grid=(nsteps,),
            in_specs=[pl.BlockSpec((DISPATCH_TILE, width), lambda i, pos: (i, 0)), any_spec],
            out_specs=any_spec,
            scratch_shapes=[pltpu.SemaphoreType.DMA((1,))]),
        out_shape=jax.ShapeDtypeStruct((nrows, width), F32),
        input_output_aliases={2: 0},
        compiler_params=_cparams(("arbitrary",), 16 << 20),
        name="moe_dispatch",
    )(pos, hcf, jnp.zeros((nrows, width), F32))

    def w_spec(d1, d2):
        return pl.BlockSpec((1, d1, d2), lambda t, e, tg: (tg[t] * EXPERTS_PER_GROUP + e, 0, 0))

    vmem = (2 * ROUTE_TILE * (width + D_MODEL) * 4 + ROUTE_TILE * D_MODEL * (2 + 4)
            + 2 * 3 * D_MODEL * D_EXPERT * 2 + (16 << 20))
    ys = pl.pallas_call(
        _moe_routed_kernel,
        grid_spec=pltpu.PrefetchScalarGridSpec(
            num_scalar_prefetch=1, grid=(n_tiles, EXPERTS_PER_GROUP),
            in_specs=[pl.BlockSpec((ROUTE_TILE, width), lambda t, e, tg: (t, 0)),
                      w_spec(D_MODEL, D_EXPERT), w_spec(D_MODEL, D_EXPERT), w_spec(D_EXPERT, D_MODEL)],
            out_specs=pl.BlockSpec((ROUTE_TILE, D_MODEL), lambda t, e, tg: (t, 0)),
            scratch_shapes=[pltpu.VMEM((ROUTE_TILE, D_MODEL), BF16), pltpu.VMEM((ROUTE_TILE, D_MODEL), F32)]),
        out_shape=jax.ShapeDtypeStruct((nrows, D_MODEL), F32),
        compiler_params=_cparams(("parallel", "arbitrary"), vmem),
        name="moe_routed",
    )(tile_group, xs, wg, wu, wd)

    steps_per_batch = seqlen // DISPATCH_TILE
    y = pl.pallas_call(
        _combine_kernel,
        grid_spec=pltpu.PrefetchScalarGridSpec(
            num_scalar_prefetch=1, grid=(nsteps,),
            in_specs=[pl.BlockSpec((DISPATCH_TILE, D_MODEL), lambda i, pos: (i, 0)),
                      pl.BlockSpec((1, 1, D_MODEL), lambda i, pos: (i // steps_per_batch, 0, 0)),
                      pl.BlockSpec((1, D_MODEL), lambda i, pos: (0, 0)), any_spec],
            out_specs=pl.BlockSpec((DISPATCH_TILE, D_MODEL), lambda i, pos: (i, 0)),
            scratch_shapes=[pltpu.VMEM((2, DISPATCH_TILE, D_MODEL), F32), pltpu.SemaphoreType.DMA((2,))]),
        out_shape=jax.ShapeDtypeStruct((ntok, D_MODEL), F32),
        compiler_params=_cparams(("arbitrary",), 16 << 20),
        name="moe_combine",
    )(pos, x1f, gt2, final_w.reshape(1, -1), ys)
    return y.reshape(bsz, seqlen, D_MODEL)


def _prep_weights(lp):
    w_in = lp["w_in"]
    sizes = (SSD_INNER, CONV_DIM, SSD_HEADS, ATTN_Q, ATTN_KV, ATTN_KV, D_MODEL, D_MODEL)
    pts = [int(p) for p in np.cumsum(sizes)[:-1]]
    z, xbc, dt, q, k, v, gs, ga = jnp.split(w_in, pts, axis=1)
    dt = jnp.pad(dt, ((0, 0), (0, LANES - SSD_HEADS)))
    w_in_r = jnp.concatenate([z, xbc, q, k, v, gs, ga, dt], axis=1).astype(BF16)
    wrt = jnp.concatenate([lp["w_group"], lp["w_router"]], axis=1)
    wrt = jnp.pad(wrt, ((0, 0), (0, LANES - wrt.shape[1]))).astype(F32)
    wrt_hi = wrt.astype(BF16)
    wrt = jnp.concatenate([wrt_hi, (wrt - wrt_hi.astype(F32)).astype(BF16)], axis=1)
    brt = jnp.concatenate([lp["b_group"], lp["b_router"]])
    brt = jnp.pad(brt, (0, LANES - brt.shape[0])).astype(F32).reshape(1, LANES)
    return dict(w_in_r=w_in_r, wrt=wrt, brt=brt,
                wso=lp["w_ssd_out"].astype(BF16), wao=lp["w_attn_out"].astype(BF16), wo=lp["w_o"].astype(BF16),
                wg=lp["w_gate_e"].astype(BF16), wu=lp["w_up_e"].astype(BF16), wd=lp["w_down_e"].astype(BF16))


def _mods(mod, per_row_repeat):
    parts = jnp.split(mod, 6, axis=-1)
    if per_row_repeat:
        return [jnp.repeat(p, per_row_repeat, axis=0)[None] for p in parts]
    return [p[:, None, :] for p in parts]


def _layer_prompt(x, mod, lp, pw, final_w):
    bsz, seqlen, _ = x.shape
    sh1, sc1, gt1, sh2, sc2, gt2 = _mods(mod, 0)
    tables = _rope_tables(np.arange(seqlen))
    (z, xbc, dt, q, k, v, gs, ga, kmean, kaug, vaug) = _in_proj(x, sc1, sh1, lp["norm1_w"], pw["w_in_r"], tables, True)
    conv0 = jnp.zeros((bsz, SSD_CONV - 1, CONV_DIM), F32)
    ssm0 = jnp.zeros((bsz, SSD_HEADS, SSD_HEADDIM, SSD_STATE), F32)
    y_ssd, conv_new, ssm_new = _ssd(xbc, z, dt, conv0, ssm0, lp["conv_w"], lp["conv_b"], lp["dt_bias"],
                                    lp["a_log"], lp["d_skip"], lp["ssd_norm_w"])
    y_att = _moba_prompt(q, kmean, kaug, vaug)
    x1, hc = _mix(x, y_ssd, y_att, gs, ga, gt1, sc2, sh2, lp["norm2_w"],
                  pw["wso"], pw["wao"], pw["wo"], pw["wrt"], pw["brt"])
    y = _moe_routed(hc, x1, gt2, final_w, pw["wg"], pw["wu"], pw["wd"])
    kv_shape = (bsz, seqlen, ATTN_KV_HEADS, HEAD_DIM)
    return y, k.reshape(kv_shape), v.reshape(kv_shape), conv_new, ssm_new


def _layer_sample(x, mod, conv_buf, ssm_state, cache_k, cache_v, page_table, past_len, lp, pw, final_w):
    bsz, dec_seq, _ = x.shape
    ntok = bsz * dec_seq
    sh1, sc1, gt1, sh2, sc2, gt2 = _mods(mod, dec_seq)
    pos = past_len + (np.arange(ntok) % dec_seq)
    tables = _rope_tables(pos)
    xf = x.reshape(1, ntok, D_MODEL)
    z, xbc, dt, q, k, v, gs, ga = _in_proj(xf, sc1, sh1, lp["norm1_w"], pw["w_in_r"], tables, False)

    def per_b(t):
        return t.reshape(bsz, dec_seq, t.shape[-1])

    y_ssd, conv_new, ssm_new = _ssd(per_b(xbc), per_b(z), per_b(dt), conv_buf, ssm_state, lp["conv_w"],
                                    lp["conv_b"], lp["dt_bias"], lp["a_log"], lp["d_skip"], lp["ssd_norm_w"])
    n_pool, page = cache_k.shape[0], cache_k.shape[1]

    def position_minor(cache):
        return cache.transpose(0, 2, 3, 1).reshape(n_pool, ATTN_KV, page)

    y_att = _moba_sample(per_b(q), per_b(k), per_b(v), position_minor(cache_k), position_minor(cache_v),
                         page_table, past_len)
    x1, hc = _mix(xf, y_ssd.reshape(1, ntok, -1), y_att.reshape(1, ntok, -1), gs, ga, gt1, sc2, sh2,
                  lp["norm2_w"], pw["wso"], pw["wao"], pw["wo"], pw["wrt"], pw["brt"])
    y = _moe(hc, x1, gt2, final_w, pw["wg"], pw["wu"], pw["wd"])
    kv_shape = (bsz, dec_seq, ATTN_KV_HEADS, HEAD_DIM)
    return y.reshape(bsz, dec_seq, D_MODEL), k.reshape(kv_shape), v.reshape(kv_shape), conv_new, ssm_new


def kernel(x_prompt, x_sample, cache_k, cache_v, state_conv, state_ssm, page_table, c_prompt, c_sample,
           w_ada, b_ada, norm1_w, w_in, conv_w, conv_b, dt_bias, a_log, d_skip, ssd_norm_w,
           w_ssd_out, w_attn_out, w_o, norm2_w, w_group, b_group, w_router, b_router,
           w_gate_e, w_up_e, w_down_e, final_w):
    depth = w_in.shape[0]
    assert depth == 1, "the final RMSNorm is fused into the last layer's MoE kernel"
    nb_p, nb_s = c_prompt.shape[0], c_sample.shape[0]
    past_len = page_table.shape[1] * cache_k.shape[2]
    l = 0
    lp = dict(norm1_w=norm1_w[l], w_in=w_in[l], conv_w=conv_w[l], conv_b=conv_b[l], dt_bias=dt_bias[l],
              a_log=a_log[l], d_skip=d_skip[l], ssd_norm_w=ssd_norm_w[l], w_ssd_out=w_ssd_out[l],
              w_attn_out=w_attn_out[l], w_o=w_o[l], norm2_w=norm2_w[l], w_group=w_group[l], b_group=b_group[l],
              w_router=w_router[l], b_router=b_router[l], w_gate_e=w_gate_e[l], w_up_e=w_up_e[l],
              w_down_e=w_down_e[l])
    pw = _prep_weights(lp)
    c_all = jnp.concatenate([c_prompt, c_sample], axis=0)
    pad_rows = (-c_all.shape[0]) % SUBLANES
    mod = _ada_mod(jnp.pad(c_all, ((0, pad_rows), (0, 0))), w_ada[l], b_ada[l])
    yp, kp, vp, cp, sp = _layer_prompt(x_prompt, mod[:nb_p], lp, pw, final_w)
    ys, ks, vs, cs, ss = _layer_sample(x_sample, mod[nb_p:nb_p + nb_s], state_conv[l], state_ssm[l],
                                       cache_k[l], cache_v[l], page_table, past_len, lp, pw, final_w)
    return (yp, ys, kp[None], vp[None], cp[None], sp[None], ks[None], vs[None], cs[None], ss[None])
```

```python
import functools
import math

import numpy as np
import jax
import jax.numpy as jnp
from jax import lax
from jax.experimental import pallas as pl
from jax.experimental.pallas import tpu as pltpu

F32 = jnp.float32
BF16 = jnp.bfloat16
HIGHEST = lax.Precision.HIGHEST

D_MODEL = 1024
SSD_HEADDIM = 64
SSD_INNER = D_MODEL
SSD_HEADS = SSD_INNER // SSD_HEADDIM
SSD_GROUPS = 2
SSD_STATE = 128
SSD_CONV = 4
SSD_CHUNK = 128
SSD_BC = SSD_GROUPS * SSD_STATE
CONV_DIM = SSD_INNER + 2 * SSD_BC
ATTN_HEADS = 8
ATTN_KV_HEADS = 4
HEAD_DIM = 64
ATTN_Q = ATTN_HEADS * HEAD_DIM
ATTN_KV = ATTN_KV_HEADS * HEAD_DIM
ROT_DIM = HEAD_DIM // 4
ROPE_THETA = 500000.0
MOBA_BLOCK = 256
MOBA_TOPK = 3
N_EGROUPS = 4
EXPERTS_PER_GROUP = 4
N_EXPERTS = N_EGROUPS * EXPERTS_PER_GROUP
D_EXPERT = D_MODEL // 2
EPS = 1e-6

LANES = 128
SUBLANES = 8
VMEM_CAP = 56 << 20
NEG = -1e30
GRP = ATTN_HEADS // ATTN_KV_HEADS
ROUTE_OFF = N_EGROUPS

_SEG = {}
_off = 0
for _name, _w in (("z", SSD_INNER), ("xbc", CONV_DIM), ("q", ATTN_Q), ("k", ATTN_KV), ("v", ATTN_KV),
                  ("gs", D_MODEL), ("ga", D_MODEL), ("dt", LANES)):
    _SEG[_name] = (_off, _off + _w)
    _off += _w
IN_PAD = _off


def _cparams(semantics, vmem_bytes):
    return pltpu.CompilerParams(dimension_semantics=semantics,
                                vmem_limit_bytes=int(min(max(vmem_bytes, 16 << 20), VMEM_CAP)))


def _dot(a, b):
    return jnp.dot(a, b, preferred_element_type=F32)


def _dot_nt(a, b):
    return lax.dot_general(a, b, (((1,), (1,)), ((), ())), preferred_element_type=F32)


def _dot_exact(a, b):
    return jnp.dot(a, b, preferred_element_type=F32, precision=HIGHEST)


def _silu(x):
    return x * jax.nn.sigmoid(x)


def _mod_kernel(c_ref, w_ref, b_ref, o_ref):
    o_ref[...] = _dot_exact(_silu(c_ref[...]), w_ref[...]) + b_ref[...]


def _ada_mod(c_all, w_ada, b_ada):
    rows = c_all.shape[0]
    return pl.pallas_call(
        _mod_kernel,
        grid=(6,),
        in_specs=[pl.BlockSpec((rows, D_MODEL), lambda j: (0, 0)),
                  pl.BlockSpec((D_MODEL, D_MODEL), lambda j: (0, j)),
                  pl.BlockSpec((1, D_MODEL), lambda j: (0, j))],
        out_specs=pl.BlockSpec((rows, D_MODEL), lambda j: (0, j)),
        out_shape=jax.ShapeDtypeStruct((rows, 6 * D_MODEL), F32),
        compiler_params=_cparams(("arbitrary",), 24 << 20),
        name="ada_mod",
    )(c_all, w_ada, b_ada.reshape(1, -1))


def _rope_tables(pos):
    half = ROT_DIM // 2
    c = -2.0 * math.log(ROPE_THETA) / ROT_DIM
    c_hi = float(np.float32(c))
    j = jnp.arange(half, dtype=F32)
    inv_freq = jnp.exp(j * c_hi + j * (c - c_hi))
    ang = jnp.asarray(pos).astype(F32)[:, None] * inv_freq[None, :]
    cos, sin = jnp.cos(ang), jnp.sin(ang)
    n = pos.shape[0]
    one = jnp.ones((n, HEAD_DIM - ROT_DIM), F32)
    zero = jnp.zeros((n, half), F32)
    rest = jnp.zeros((n, HEAD_DIM - ROT_DIM), F32)
    a = jnp.concatenate([cos, cos, one], axis=1)
    s1 = jnp.concatenate([-sin, zero, rest], axis=1)
    s2 = jnp.concatenate([zero, sin, rest], axis=1)
    rep = LANES // HEAD_DIM
    return tuple(jnp.tile(t, (1, rep)) for t in (a, s1, s2))


def _inproj_kernel(*refs, attn_layout):
    (x_ref, sc_ref, sh_ref, nw_ref, w_ref, ra_ref, rs1_ref, rs2_ref,
     z_ref, xbc_ref, dt_ref, q_ref, k_ref, v_ref, gs_ref, ga_ref) = refs[:16]
    x = x_ref[0]
    h = x * lax.rsqrt(jnp.mean(x * x, axis=-1, keepdims=True) + EPS) * nw_ref[...]
    h = (h * (1.0 + sc_ref[0]) + sh_ref[0]).astype(BF16)

    def proj(name, out_ref):
        lo, hi = _SEG[name]
        step = 512
        for c in range(lo, hi, step):
            w = min(step, hi - c)
            out_ref[0, :, c - lo:c - lo + w] = _dot(h, w_ref[:, c:c + w])

    proj("z", z_ref)
    proj("xbc", xbc_ref)
    proj("gs", gs_ref)
    proj("ga", ga_ref)
    proj("dt", dt_ref)

    def rope(t):
        width = t.shape[1]
        reps = width // LANES
        a = jnp.concatenate([ra_ref[...]] * reps, axis=1)
        s1 = jnp.concatenate([rs1_ref[...]] * reps, axis=1)
        s2 = jnp.concatenate([rs2_ref[...]] * reps, axis=1)
        return t * a + pltpu.roll(t, width - ROT_DIM // 2, 1) * s1 + pltpu.roll(t, ROT_DIM // 2, 1) * s2

    lo, hi = _SEG["q"]
    q_ref[0] = rope(_dot(h, w_ref[:, lo:hi]))
    lo, hi = _SEG["k"]
    k = rope(_dot(h, w_ref[:, lo:hi]))
    k_ref[0] = k
    lo, hi = _SEG["v"]
    v = _dot(h, w_ref[:, lo:hi])
    v_ref[0] = v

    if attn_layout:
        kmean_ref, kaug_ref, vaug_ref = refs[16:19]
        tm = k.shape[0]
        blocks = tm // MOBA_BLOCK
        for hb in range(blocks):
            kmean_ref[0, hb] = jnp.mean(k[hb * MOBA_BLOCK:(hb + 1) * MOBA_BLOCK], axis=0, keepdims=True)
        lane = lax.broadcasted_iota(jnp.int32, (tm, HEAD_DIM), 1)
        row = lax.broadcasted_iota(jnp.int32, (tm, HEAD_DIM), 0)
        blk = blocks * pl.program_id(1)
        for hb in range(1, blocks):
            blk = blk + (row >= hb * MOBA_BLOCK).astype(jnp.int32)
        blk_onehot = (lane == blk).astype(F32)
        ones_col = (lane == 0).astype(F32)
        for kh in range(ATTN_KV_HEADS):
            sl = slice(kh * HEAD_DIM, (kh + 1) * HEAD_DIM)
            kaug_ref[0, kh] = jnp.concatenate([k[:, sl], blk_onehot], axis=1).astype(BF16)
            vaug_ref[0, kh] = jnp.concatenate([v[:, sl], ones_col], axis=1).astype(BF16)


def _in_proj(x, sc1, sh1, norm_w, w_in_r, tables, attn_layout):
    bsz, seqlen, _ = x.shape
    tm = 2 * MOBA_BLOCK if seqlen % (2 * MOBA_BLOCK) == 0 else MOBA_BLOCK
    assert seqlen % tm == 0
    nt = seqlen // tm
    blocks = tm // MOBA_BLOCK
    mod_rows = sc1.shape[1]
    if mod_rows == 1:
        mod_spec = pl.BlockSpec((1, 1, D_MODEL), lambda b, i: (b, 0, 0))
    else:
        mod_spec = pl.BlockSpec((1, tm, D_MODEL), lambda b, i: (b, i, 0))
    tab_spec = pl.BlockSpec((tm, LANES), lambda b, i: (i, 0))

    def tok(width):
        return pl.BlockSpec((1, tm, width), lambda b, i: (b, i, 0))

    def shp(width):
        return jax.ShapeDtypeStruct((bsz, seqlen, width), F32)

    widths = (SSD_INNER, CONV_DIM, LANES, ATTN_Q, ATTN_KV, ATTN_KV, D_MODEL, D_MODEL)
    out_specs = [tok(w) for w in widths]
    out_shape = [shp(w) for w in widths]
    if attn_layout:
        assert nt * blocks <= HEAD_DIM
        out_specs += [pl.BlockSpec((1, blocks, 1, ATTN_KV), lambda b, i: (b, i, 0, 0)),
                      pl.BlockSpec((1, ATTN_KV_HEADS, tm, LANES), lambda b, i: (b, 0, i, 0)),
                      pl.BlockSpec((1, ATTN_KV_HEADS, tm, LANES), lambda b, i: (b, 0, i, 0))]
        out_shape += [jax.ShapeDtypeStruct((bsz, nt * blocks, 1, ATTN_KV), F32),
                      jax.ShapeDtypeStruct((bsz, ATTN_KV_HEADS, seqlen, LANES), BF16),
                      jax.ShapeDtypeStruct((bsz, ATTN_KV_HEADS, seqlen, LANES), BF16)]
    vmem = D_MODEL * IN_PAD * 2 + 2 * tm * (D_MODEL + IN_PAD + 4 * LANES) * 4 + (10 << 20)
    return pl.pallas_call(
        functools.partial(_inproj_kernel, attn_layout=attn_layout),
        grid=(bsz, nt),
        in_specs=[tok(D_MODEL), mod_spec, mod_spec,
                  pl.BlockSpec((1, D_MODEL), lambda b, i: (0, 0)),
                  pl.BlockSpec((D_MODEL, IN_PAD), lambda b, i: (0, 0), pipeline_mode=pl.Buffered(1)),
                  tab_spec, tab_spec, tab_spec],
        out_specs=out_specs,
        out_shape=out_shape,
        compiler_params=_cparams(("parallel", "parallel"), vmem),
        name="in_proj",
    )(x, sc1, sh1, norm_w.reshape(1, -1), w_in_r, *tables)


def _cumsum_rows(a):
    row = lax.broadcasted_iota(jnp.int32, a.shape, 0)
    s = 1
    while s < a.shape[0]:
        a = a + jnp.where(row >= s, pltpu.roll(a, s, 0), 0.0)
        s *= 2
    return a


def _ssd_kernel(xbc_ref, z_ref, dt_ref, cbuf_ref, s0_ref, cw_ref, cb_ref, dtb_ref, alog_ref, dsk_ref, nw_ref,
                y_ref, cnew_ref, sout_ref, xext, st, ysc, *, rows):
    ch = SSD_CHUNK
    pad = SUBLANES
    c = pl.program_id(1)
    last = pl.num_programs(1) - 1

    @pl.when(c == 0)
    def _():
        xext[0:pad, :] = jnp.zeros((pad, CONV_DIM), F32)
        xext[pad - (SSD_CONV - 1):pad, :] = cbuf_ref[0]
        st[...] = s0_ref[0].T

    xext[pad:pad + rows, :] = xbc_ref[0]
    if rows < ch:
        xext[pad + rows:pad + ch, :] = jnp.zeros((ch - rows, CONV_DIM), F32)

    conv = cb_ref[...] + xext[pad:pad + ch, :] * cw_ref[SSD_CONV - 1:SSD_CONV, :]
    for w in range(SSD_CONV - 1):
        off = pad - (SSD_CONV - 1) + w
        conv = conv + xext[off:off + ch, :] * cw_ref[w:w + 1, :]
    act = _silu(conv)

    @pl.when(c == last)
    def _():
        cnew_ref[0] = xext[pad + rows - (SSD_CONV - 1):pad + rows, :]

    xext[0:pad, :] = xext[ch:ch + pad, :]

    xs = act[:, :SSD_INNER]
    dt_raw = dt_ref[0]
    if rows < ch:
        dt_raw = jnp.concatenate([dt_raw, jnp.zeros((ch - rows, LANES), F32)], axis=0)
    dt_in = dt_raw + dtb_ref[...]
    dt = jnp.maximum(dt_in, 0.0) + jnp.log1p(jnp.exp(-jnp.abs(dt_in)))
    row = lax.broadcasted_iota(jnp.int32, (ch, LANES), 0)
    col = lax.broadcasted_iota(jnp.int32, (ch, LANES), 1)
    if rows < ch:
        dt = jnp.where(row < rows, dt, 0.0)
    a_cs = _cumsum_rows(dt * (-jnp.exp(alog_ref[...])))
    a_cs_t = a_cs.T
    dt_t = dt.T
    causal = row >= col

    heads_per_pair = LANES // SSD_HEADDIM
    assert heads_per_pair == 2 and SSD_STATE == ch
    first_of_pair = col < SSD_HEADDIM
    for g in range(SSD_GROUPS):
        bg = act[:, SSD_INNER + g * SSD_STATE:SSD_INNER + (g + 1) * SSD_STATE]
        cg = act[:, SSD_INNER + SSD_BC + g * SSD_STATE:SSD_INNER + SSD_BC + (g + 1) * SSD_STATE]
        gmat = _dot_nt(cg.astype(BF16), bg.astype(BF16))
        bg_t = bg.T
        for pair in range(SSD_HEADS // SSD_GROUPS // heads_per_pair):
            hd0 = g * (SSD_HEADS // SSD_GROUPS) + pair * heads_per_pair
            sl = slice(hd0 * SSD_HEADDIM, hd0 * SSD_HEADDIM + LANES)
            x_pair = xs[:, sl]
            s_pair = st[:, sl]
            x_diag = [jnp.where(first_of_pair, x_pair, 0.0).astype(BF16),
                      jnp.where(first_of_pair, 0.0, x_pair).astype(BF16)]
            s_diag = [jnp.where(first_of_pair, s_pair, 0.0).astype(BF16),
                      jnp.where(first_of_pair, 0.0, s_pair).astype(BF16)]
            m_parts, c_parts, b_parts, decay = [], [], [], []
            for hd in range(hd0, hd0 + heads_per_pair):
                acs_col = a_cs[:, hd:hd + 1]
                acs_row = a_cs_t[hd:hd + 1, :]
                dt_row = dt_t[hd:hd + 1, :]
                acs_last = a_cs_t[hd:hd + 1, ch - 1:ch]
                lmat = jnp.where(causal, jnp.exp(acs_col - acs_row), 0.0)
                m_parts.append((gmat * lmat * dt_row).astype(BF16))
                c_parts.append((cg * jnp.exp(acs_col)).astype(BF16))
                b_parts.append((bg_t * (jnp.exp(acs_last - acs_row) * dt_row)).astype(BF16))
                decay.append(jnp.exp(acs_last))
            ysc[:, sl] = _dot(jnp.concatenate(m_parts + c_parts, axis=1), jnp.concatenate(x_diag + s_diag, axis=0))
            st[:, sl] = (jnp.where(first_of_pair[:1], decay[0], decay[1]) * s_pair
                         + _dot(jnp.concatenate(b_parts, axis=1), jnp.concatenate(x_diag, axis=0)))

    y = (ysc[0:rows, :] + xs[:rows] * dsk_ref[...]) * _silu(z_ref[0])
    gw = SSD_INNER // SSD_GROUPS
    for g in range(SSD_GROUPS):
        yg = y[:, g * gw:(g + 1) * gw]
        yg = yg * lax.rsqrt(jnp.mean(yg * yg, axis=-1, keepdims=True) + EPS)
        y_ref[0, :, g * gw:(g + 1) * gw] = yg * nw_ref[:, g * gw:(g + 1) * gw]

    @pl.when(c == last)
    def _():
        sout_ref[0] = st[...].T


def _ssd(xbc, z, dt, conv_buf, ssm_state, conv_w, conv_b, dt_bias, a_log, d_skip, norm_w):
    bsz, seqlen, _ = xbc.shape
    rows = SSD_CHUNK if seqlen % SSD_CHUNK == 0 else seqlen
    assert rows == seqlen or rows == SSD_CHUNK
    assert rows % SUBLANES == 0 and rows >= SSD_CONV - 1
    nc = seqlen // rows
    hp = SSD_HEADS * SSD_HEADDIM

    def tok(width):
        return pl.BlockSpec((1, rows, width), lambda b, c: (b, c, 0))

    def per_b(d1, d2):
        return pl.BlockSpec((1, d1, d2), lambda b, c: (b, 0, 0))

    def vec(width, r=1):
        return pl.BlockSpec((r, width), lambda b, c: (0, 0))

    def pad_lanes(v):
        return jnp.pad(v.astype(F32), (0, LANES - v.shape[0])).reshape(1, LANES)

    y, conv_new, s_new = pl.pallas_call(
        functools.partial(_ssd_kernel, rows=rows),
        grid=(bsz, nc),
        in_specs=[tok(CONV_DIM), tok(SSD_INNER), tok(LANES),
                  per_b(SSD_CONV - 1, CONV_DIM), per_b(hp, SSD_STATE),
                  vec(CONV_DIM, SSD_CONV), vec(CONV_DIM), vec(LANES), vec(LANES), vec(SSD_INNER), vec(SSD_INNER)],
        out_specs=[tok(SSD_INNER), per_b(SSD_CONV - 1, CONV_DIM), per_b(hp, SSD_STATE)],
        out_shape=[jax.ShapeDtypeStruct((bsz, seqlen, SSD_INNER), F32),
                   jax.ShapeDtypeStruct((bsz, SSD_CONV - 1, CONV_DIM), F32),
                   jax.ShapeDtypeStruct((bsz, hp, SSD_STATE), F32)],
        scratch_shapes=[pltpu.VMEM((SSD_CHUNK + SUBLANES, CONV_DIM), F32),
                        pltpu.VMEM((SSD_STATE, hp), F32),
                        pltpu.VMEM((SSD_CHUNK, SSD_INNER), F32)],
        compiler_params=_cparams(("parallel", "arbitrary"), 40 << 20),
        name="ssd",
    )(xbc, z, dt, conv_buf, ssm_state.reshape(bsz, hp, SSD_STATE),
      conv_w, conv_b.reshape(1, -1), pad_lanes(dt_bias), pad_lanes(a_log),
      jnp.repeat(d_skip.astype(F32), SSD_HEADDIM).reshape(1, -1), norm_w.reshape(1, -1))
    return y, conv_new, s_new.reshape(bsz, SSD_HEADS, SSD_HEADDIM, SSD_STATE)


def _select_topk(scores, valid, axis):
    sc = scores if valid is None else jnp.where(valid, scores, -jnp.inf)
    sel = jnp.zeros(scores.shape, F32)
    pos_f = lax.broadcasted_iota(jnp.int32, scores.shape, axis).astype(F32)
    for _ in range(MOBA_TOPK):
        m = jnp.max(sc, axis=axis, keepdims=True)
        idx = jnp.min(jnp.where(sc == m, pos_f, float(1 << 20)), axis=axis, keepdims=True)
        idx = jnp.where(m > -jnp.inf, idx, -1.0)
        hit = pos_f == idx
        sel = jnp.where(hit, 1.0, sel)
        sc = jnp.where(hit, -jnp.inf, sc)
    return sel > 0.5


def _moba_prompt_kernel(q_ref, km_ref, kaug_ref, vaug_ref, o_ref, s_a, s_b, m_sc, acc_sc, *, qblocks):
    j = pl.program_id(2)
    tq = MOBA_BLOCK
    chunk = qblocks * tq
    half = GRP * tq
    q2 = q_ref[0]
    qs = jnp.concatenate([q2[b * tq:(b + 1) * tq, g * HEAD_DIM:(g + 1) * HEAD_DIM]
                          for b in range(qblocks) for g in range(GRP)], axis=0)
    rows = qs.shape[0]
    lane = lax.broadcasted_iota(jnp.int32, (rows, LANES), 1)
    scores_t = lax.dot_general(km_ref[0, 0], qs, (((1,), (1,)), ((), ())), preferred_element_type=F32,
                               precision=HIGHEST)
    nbp = scores_t.shape[0]
    blk = lax.broadcasted_iota(jnp.int32, scores_t.shape, 0)
    col = lax.broadcasted_iota(jnp.int32, scores_t.shape, 1)
    own = qblocks * j
    for b in range(1, qblocks):
        own = own + (col >= b * half).astype(jnp.int32)
    sel_t = _select_topk(scores_t, blk < own, 0)
    pen_t = jnp.concatenate([jnp.zeros((HEAD_DIM, rows), F32), jnp.where(sel_t, 0.0, NEG),
                             jnp.full((LANES - HEAD_DIM - nbp, rows), NEG, F32)], axis=0)
    pen = pen_t.T
    q_pad = jnp.concatenate([qs * (HEAD_DIM ** -0.5), jnp.zeros((rows, LANES - HEAD_DIM), F32)], axis=1)
    q_aug = jnp.where(lane < HEAD_DIM, q_pad, pen).astype(BF16)
    q_own = q_pad.astype(BF16)

    def update(m, acc, s, pv):
        m_new = jnp.maximum(m, jnp.max(s, axis=1, keepdims=True))
        p = jnp.exp(s - m_new)
        acc = jnp.exp(m - m_new) * acc + pv(p.astype(BF16))
        return m_new, acc

    total_chunks = kaug_ref.shape[2] // chunk
    nchunks = j + 1
    npairs = nchunks // 2

    def logits(c):
        start = pl.multiple_of(jnp.minimum(c, total_chunks - 1) * chunk, chunk)
        return _dot_nt(q_aug, kaug_ref[0, 0, pl.ds(start, chunk), :])

    def values(c):
        vb = vaug_ref[0, 0, pl.ds(pl.multiple_of(c * chunk, chunk), chunk), :]
        return lambda p: _dot(p, vb)

    def body(t, carry):
        m, acc = carry
        c0 = 2 * t
        s_b[...] = logits(c0 + 1)
        m, acc = update(m, acc, s_a[...], values(c0))
        s_a[...] = logits(c0 + 2)
        m, acc = update(m, acc, s_b[...], values(c0 + 1))
        return m, acc

    def own_rows(ref, b):
        return ref[0, 0, pl.ds(pl.multiple_of((qblocks * j + b) * tq, tq), tq), :]

    s_own = jnp.concatenate([_dot_nt(q_own[b * half:(b + 1) * half], own_rows(kaug_ref, b))
                             for b in range(qblocks)], axis=0)
    r = lax.broadcasted_iota(jnp.int32, (rows, tq), 0)
    cidx = lax.broadcasted_iota(jnp.int32, (rows, tq), 1)
    s_own = jnp.where(cidx <= (r & (tq - 1)), s_own, NEG)

    def own_values(p):
        return jnp.concatenate([_dot(p[b * half:(b + 1) * half], own_rows(vaug_ref, b))
                                for b in range(qblocks)], axis=0)

    s_a[...] = logits(0)
    init = (jnp.full((rows, 1), -jnp.inf, F32), jnp.zeros((rows, LANES), F32))
    m, acc = lax.fori_loop(0, npairs, body, init)
    m_sc[...] = m
    acc_sc[...] = acc

    @pl.when(nchunks % 2 == 1)
    def _():
        m_odd, acc_odd = update(m_sc[...], acc_sc[...], s_a[...], values(nchunks - 1))
        m_sc[...] = m_odd
        acc_sc[...] = acc_odd

    _, acc = update(m_sc[...], acc_sc[...], s_own, own_values)
    out = acc[:, :HEAD_DIM] / acc[:, HEAD_DIM:HEAD_DIM + 1]
    o_ref[0] = jnp.concatenate(
        [jnp.concatenate([out[(b * GRP + g) * tq:(b * GRP + g + 1) * tq] for g in range(GRP)], axis=1)
         for b in range(qblocks)], axis=0)


def _moba_prompt(q, kmean, kaug, vaug):
    bsz, seqlen, _ = q.shape
    nb = seqlen // MOBA_BLOCK
    assert GRP * HEAD_DIM == LANES and nb <= LANES - HEAD_DIM and MOBA_BLOCK & (MOBA_BLOCK - 1) == 0
    nbp = -(-nb // SUBLANES) * SUBLANES
    km = kmean.reshape(bsz, nb, ATTN_KV_HEADS, HEAD_DIM).transpose(0, 2, 1, 3)
    km = jnp.pad(km, ((0, 0), (0, 0), (0, nbp - nb), (0, 0)))
    kv_spec = pl.BlockSpec((1, 1, seqlen, LANES), lambda b, h, i: (b, h, 0, 0))
    qblocks = 4
    chunk = qblocks * MOBA_BLOCK
    assert seqlen % chunk == 0
    rows = qblocks * GRP * MOBA_BLOCK
    vmem = 2 * 2 * seqlen * LANES * 2 + 2 * rows * chunk * 4 + (24 << 20)
    return pl.pallas_call(
        functools.partial(_moba_prompt_kernel, qblocks=qblocks),
        scratch_shapes=[pltpu.VMEM((rows, chunk), F32), pltpu.VMEM((rows, chunk), F32),
                        pltpu.VMEM((rows, 1), F32), pltpu.VMEM((rows, LANES), F32)],
        grid=(bsz, ATTN_KV_HEADS, nb // qblocks),
        in_specs=[pl.BlockSpec((1, chunk, LANES), lambda b, h, i: (b, i, h)),
                  pl.BlockSpec((1, 1, nbp, HEAD_DIM), lambda b, h, i: (b, h, 0, 0)),
                  kv_spec, kv_spec],
        out_specs=pl.BlockSpec((1, chunk, LANES), lambda b, h, i: (b, i, h)),
        out_shape=jax.ShapeDtypeStruct((bsz, seqlen, ATTN_Q), F32),
        compiler_params=_cparams(("parallel", "parallel", "arbitrary"), vmem),
        name="moba_prompt",
    )(q, km, kaug, vaug)


def _moba_sample_kernel(pt_ref, qbt_ref, qb_ref, *refs, pages_per_step, nsteps, nblk, dec_seq):
    del pt_ref
    k_refs = refs[:pages_per_step]
    v_refs = refs[pages_per_step:2 * pages_per_step]
    kn_ref, vn_ref, o_ref, s_sc, sct_sc, pen_sc, m_sc, l_sc, acc_sc = refs[2 * pages_per_step:]
    step = pl.program_id(1)
    rows = qb_ref.shape[1]
    pages_per_block = MOBA_BLOCK // LANES
    blocks_per_step = pages_per_step // pages_per_block
    qb = (qb_ref[0] * (HEAD_DIM ** -0.5)).astype(BF16)

    @pl.when(step < nsteps)
    def _():
        qbt = qbt_ref[0]
        for jb in range(blocks_per_step):
            ksum = None
            for j in range(pages_per_block):
                kpage = k_refs[jb * pages_per_block + j][0]
                ksum = kpage if ksum is None else ksum + kpage
            kmean = jnp.sum(ksum, axis=1, keepdims=True) * (1.0 / MOBA_BLOCK)
            sct_sc[pl.ds(step * blocks_per_step + jb, 1), :] = jnp.sum(qbt * kmean, axis=0, keepdims=True)
        s_sc[step] = _dot(qb, jnp.concatenate([k_ref[0].astype(BF16) for k_ref in k_refs], axis=1))

    @pl.when(step >= nsteps)
    def _():
        n = step - nsteps
        lane = lax.broadcasted_iota(jnp.int32, (rows, LANES), 1)

        @pl.when(n == 0)
        def _():
            sel_t = _select_topk(sct_sc[...], None, 0)
            pen_t = jnp.concatenate([jnp.where(sel_t, 0.0, NEG), jnp.full((LANES - nblk, LANES), NEG, F32)],
                                    axis=0)
            pen_sc[...] = pen_t.T[:rows]
            m_sc[...] = jnp.full(m_sc.shape, -jnp.inf, F32)
            l_sc[...] = jnp.zeros(l_sc.shape, F32)
            acc_sc[...] = jnp.zeros(acc_sc.shape, F32)

        def update(s, pv_fn):
            m = m_sc[...]
            m_new = jnp.maximum(m, jnp.max(s, axis=1, keepdims=True))
            p = jnp.exp(s - m_new)
            alpha = jnp.exp(m - m_new)
            l_sc[...] = alpha * l_sc[...] + jnp.sum(p, axis=1, keepdims=True)
            acc_sc[...] = alpha * acc_sc[...] + pv_fn(p.astype(BF16))
            m_sc[...] = m_new

        pen = pen_sc[...]
        pen_cols = []
        for jb in range(blocks_per_step):
            pen_col = jnp.sum(jnp.where(lane == n * blocks_per_step + jb, pen, 0.0), axis=1, keepdims=True)
            pen_cols.append(jnp.broadcast_to(pen_col, (rows, MOBA_BLOCK)))
        vt_all = jnp.concatenate([v_ref[0].astype(BF16) for v_ref in v_refs], axis=1)
        update(s_sc[n] + jnp.concatenate(pen_cols, axis=1), lambda p: _dot_nt(p, vt_all))

        @pl.when(n == nsteps - 1)
        def _():
            zpad = jnp.zeros((LANES - dec_seq, kn_ref.shape[2]), F32)
            kn = jnp.concatenate([kn_ref[0], zpad], axis=0).astype(BF16)
            vn = jnp.concatenate([vn_ref[0], zpad], axis=0).astype(BF16)
            s_own = _dot_nt(qb, kn)
            r = lax.broadcasted_iota(jnp.int32, (rows, LANES), 0)
            qpos = r & (dec_seq - 1)
            s_own = jnp.where((lane <= qpos) & (lane < dec_seq), s_own, NEG)
            update(s_own, lambda p: _dot(p, vn))
            o_ref[0] = acc_sc[...] / l_sc[...]


def _moba_sample(q, k_new, v_new, cache_kt, cache_vt, page_table, past_len):
    bsz, dec_seq, _ = q.shape
    page = cache_kt.shape[2]
    assert page == LANES and MOBA_BLOCK % page == 0
    assert past_len % MOBA_BLOCK == 0 and dec_seq == SUBLANES
    nblk = past_len // MOBA_BLOCK
    assert MOBA_TOPK <= nblk <= LANES and nblk % SUBLANES == 0
    n_pages = past_len // page
    pages_per_step = 64
    while n_pages % pages_per_step:
        pages_per_step //= 2
    assert pages_per_step * page >= MOBA_BLOCK
    nsteps = n_pages // pages_per_step
    rows = ATTN_HEADS * dec_seq
    q4 = q.reshape(bsz, dec_seq, ATTN_HEADS, HEAD_DIM).transpose(0, 2, 1, 3)
    own = (np.arange(ATTN_HEADS)[:, None] // GRP == np.arange(ATTN_KV_HEADS)[None, :]).astype(np.float32)
    qb = (q4[:, :, :, None, :] * jnp.asarray(own)[None, :, None, :, None]).reshape(bsz, rows, ATTN_KV)
    qbt = jnp.pad(qb.transpose(0, 2, 1), ((0, 0), (0, 0), (0, LANES - rows)))

    def k_spec(j):
        return pl.BlockSpec((1, ATTN_KV, page),
                            lambda b, s, pt: (pt[b, pages_per_step * jnp.minimum(s, nsteps - 1) + j], 0, 0))

    def v_spec(j):
        return pl.BlockSpec((1, ATTN_KV, page),
                            lambda b, s, pt: (pt[b, pages_per_step * jnp.maximum(s - nsteps, 0) + j], 0, 0))

    def per_b(d1, d2):
        return pl.BlockSpec((1, d1, d2), lambda b, s, pt: (b, 0, 0))

    out = pl.pallas_call(
        functools.partial(_moba_sample_kernel, pages_per_step=pages_per_step, nsteps=nsteps, nblk=nblk,
                          dec_seq=dec_seq),
        grid_spec=pltpu.PrefetchScalarGridSpec(
            num_scalar_prefetch=1, grid=(bsz, 2 * nsteps),
            in_specs=[per_b(ATTN_KV, LANES), per_b(rows, ATTN_KV)]
                     + [k_spec(j) for j in range(pages_per_step)] + [v_spec(j) for j in range(pages_per_step)]
                     + [per_b(dec_seq, ATTN_KV), per_b(dec_seq, ATTN_KV)],
            out_specs=per_b(rows, ATTN_KV),
            scratch_shapes=[pltpu.VMEM((nsteps, rows, pages_per_step * page), F32), pltpu.VMEM((nblk, LANES), F32),
                            pltpu.VMEM((rows, LANES), F32), pltpu.VMEM((rows, 1), F32), pltpu.VMEM((rows, 1), F32),
                            pltpu.VMEM((rows, ATTN_KV), F32)]),
        out_shape=jax.ShapeDtypeStruct((bsz, rows, ATTN_KV), F32),
        compiler_params=_cparams(("parallel", "arbitrary"),
                                 2 * 2 * pages_per_step * ATTN_KV * page * 4 + nsteps * rows * pages_per_step * page * 4
                                 + (16 << 20)),
        name="moba_sample_attn",
    )(page_table, qbt, qb, *([cache_kt] * pages_per_step), *([cache_vt] * pages_per_step), k_new, v_new)
    o5 = out.reshape(bsz, ATTN_HEADS, dec_seq, ATTN_KV_HEADS, HEAD_DIM)
    o4 = jnp.sum(o5 * jnp.asarray(own)[None, :, None, :, None], axis=3)
    return o4.transpose(0, 2, 1, 3).reshape(bsz, dec_seq, ATTN_Q)


def _mix_kernel(x_ref, ys_ref, ya_ref, gs_ref, ga_ref, gt1_ref, sc2_ref, sh2_ref, nw_ref,
                wso_ref, wao_ref, wo_ref, wrt_ref, brt_ref, x1_ref, hc_ref):
    merged = (jax.nn.sigmoid(gs_ref[0]) * _dot(ys_ref[0].astype(BF16), wso_ref[...])
              + jax.nn.sigmoid(ga_ref[0]) * _dot(ya_ref[0].astype(BF16), wao_ref[...]))
    x1 = x_ref[0] + gt1_ref[0] * _dot(merged.astype(BF16), wo_ref[...])
    x1_ref[0] = x1
    h2 = x1 * lax.rsqrt(jnp.mean(x1 * x1, axis=-1, keepdims=True) + EPS) * nw_ref[...]
    h2 = h2 * (1.0 + sc2_ref[0]) + sh2_ref[0]
    hc_ref[0, :, :D_MODEL] = h2

    h_hi = h2.astype(BF16)
    h_lo = (h2 - h_hi.astype(F32)).astype(BF16)
    hw = _dot(h_hi, wrt_ref[...])
    lg = hw[:, :LANES] + hw[:, LANES:] + _dot(h_lo, wrt_ref[:, :LANES]) + brt_ref[...]
    lane = lax.broadcasted_iota(jnp.int32, lg.shape, 1).astype(F32)
    big = float(1 << 20)
    ninf = -jnp.inf
    gl = jnp.where(lane < N_EGROUPS, lg, ninf)
    gmax = jnp.max(gl, axis=1, keepdims=True)
    gidx = jnp.min(jnp.where(gl == gmax, lane, big), axis=1, keepdims=True)
    g_w = 1.0 / jnp.sum(jnp.exp(gl - gmax), axis=1, keepdims=True)
    lo = ROUTE_OFF + gidx * EXPERTS_PER_GROUP
    el = jnp.where((lane >= lo) & (lane < lo + EXPERTS_PER_GROUP), lg, ninf)
    m1 = jnp.max(el, axis=1, keepdims=True)
    i1 = jnp.min(jnp.where(el == m1, lane, big), axis=1, keepdims=True)
    el2 = jnp.where(lane == i1, ninf, el)
    m2 = jnp.max(el2, axis=1, keepdims=True)
    i2 = jnp.min(jnp.where(el2 == m2, lane, big), axis=1, keepdims=True)
    e2 = jnp.exp(m2 - m1)
    den = 1.0 / (1.0 + e2)
    comb = jnp.where(lane == i1, den * g_w, 0.0) + jnp.where(lane == i2, e2 * den * g_w, 0.0)
    hc_ref[0, :, D_MODEL:] = jnp.where(lane == 0.0, gidx, comb)


def _mix(x, y_ssd, y_att, gs, ga, gt1, sc2, sh2, norm_w, wso, wao, wo, wrt, brt):
    bsz, seqlen, _ = x.shape
    tm = 512 if seqlen % 512 == 0 else 256
    assert seqlen % tm == 0
    nt = seqlen // tm
    mod_rows = gt1.shape[1]
    if mod_rows == 1:
        mod_spec = pl.BlockSpec((1, 1, D_MODEL), lambda b, i: (b, 0, 0))
    else:
        mod_spec = pl.BlockSpec((1, tm, D_MODEL), lambda b, i: (b, i, 0))

    def tok(width):
        return pl.BlockSpec((1, tm, width), lambda b, i: (b, i, 0))

    def full(d1, d2):
        return pl.BlockSpec((d1, d2), lambda b, i: (0, 0))

    vmem = 2 * 2 * (2 * D_MODEL * D_MODEL + ATTN_Q * D_MODEL) + 2 * tm * (8 * D_MODEL) * 4 + (12 << 20)
    return pl.pallas_call(
        _mix_kernel,
        grid=(bsz, nt),
        in_specs=[tok(D_MODEL), tok(SSD_INNER), tok(ATTN_Q), tok(D_MODEL), tok(D_MODEL),
                  mod_spec, mod_spec, mod_spec, full(1, D_MODEL),
                  full(SSD_INNER, D_MODEL), full(ATTN_Q, D_MODEL), full(D_MODEL, D_MODEL),
                  full(D_MODEL, 2 * LANES), full(1, LANES)],
        out_specs=[tok(D_MODEL), tok(D_MODEL + LANES)],
        out_shape=[jax.ShapeDtypeStruct((bsz, seqlen, D_MODEL), F32),
                   jax.ShapeDtypeStruct((bsz, seqlen, D_MODEL + LANES), F32)],
        compiler_params=_cparams(("parallel", "parallel"), vmem),
        name="mix",
    )(x, y_ssd, y_att, gs, ga, gt1, sc2, sh2, norm_w.reshape(1, -1), wso, wao, wo, wrt, brt)


def _expert_step(h2, comb, expert, wg_ref, wu_ref, wd_ref):
    lane = lax.broadcasted_iota(jnp.int32, comb.shape, 1)
    cw = jnp.sum(jnp.where(lane == expert + ROUTE_OFF, comb, 0.0), axis=1, keepdims=True)
    a = _dot(h2, wg_ref[0])
    u = _dot(h2, wu_ref[0])
    act = (_silu(a) * u * cw).astype(BF16)
    return _dot(act, wd_ref[0])


def _final_norm(x1, gt2, ffn, fw):
    x2 = x1 + gt2 * ffn
    return x2 * lax.rsqrt(jnp.mean(x2 * x2, axis=-1, keepdims=True) + EPS) * fw


def _moe_kernel(hc_ref, x1_ref, gt2_ref, fw_ref, wg_ref, wu_ref, wd_ref, y_ref, acc):
    e = pl.program_id(2)

    @pl.when(e == 0)
    def _():
        acc[...] = jnp.zeros(acc.shape, F32)

    acc[...] += _expert_step(hc_ref[0, :, :D_MODEL].astype(BF16), hc_ref[0, :, D_MODEL:], e,
                             wg_ref, wu_ref, wd_ref)

    @pl.when(e == pl.num_programs(2) - 1)
    def _():
        y_ref[0] = _final_norm(x1_ref[0], gt2_ref[0], acc[...], fw_ref[...])


def _moe(hc, x1, gt2, final_w, wg, wu, wd):
    bsz, seqlen, _ = x1.shape
    tm = 1024 if seqlen % 1024 == 0 else 256
    assert seqlen % tm == 0
    nt = seqlen // tm
    mod_rows = gt2.shape[1]
    if mod_rows == 1:
        mod_spec = pl.BlockSpec((1, 1, D_MODEL), lambda b, i, e: (b, 0, 0))
    else:
        mod_spec = pl.BlockSpec((1, tm, D_MODEL), lambda b, i, e: (b, i, 0))

    def tok(width):
        return pl.BlockSpec((1, tm, width), lambda b, i, e: (b, i, 0))

    vmem = 2 * tm * D_MODEL * (2 + 4 + 4) + tm * D_MODEL * 4 + 2 * 3 * D_MODEL * D_EXPERT * 2 + (16 << 20)
    return pl.pallas_call(
        _moe_kernel,
        grid=(bsz, nt, N_EXPERTS),
        in_specs=[tok(D_MODEL + LANES), tok(D_MODEL), mod_spec,
                  pl.BlockSpec((1, D_MODEL), lambda b, i, e: (0, 0)),
                  pl.BlockSpec((1, D_MODEL, D_EXPERT), lambda b, i, e: (e, 0, 0)),
                  pl.BlockSpec((1, D_MODEL, D_EXPERT), lambda b, i, e: (e, 0, 0)),
                  pl.BlockSpec((1, D_EXPERT, D_MODEL), lambda b, i, e: (e, 0, 0))],
        out_specs=tok(D_MODEL),
        out_shape=jax.ShapeDtypeStruct((bsz, seqlen, D_MODEL), F32),
        scratch_shapes=[pltpu.VMEM((tm, D_MODEL), F32)],
        compiler_params=_cparams(("parallel", "parallel", "arbitrary"), vmem),
        name="moe",
    )(hc, x1, gt2, final_w.reshape(1, -1), wg, wu, wd)


ROUTE_TILE = 512
DISPATCH_TILE = 512


def _move_rows(src_of, dst_of, sem, whole_src, whole_dst):
    def body(r, carry):
        pltpu.make_async_copy(src_of(r), dst_of(r), sem).start()
        return carry

    lax.fori_loop(0, DISPATCH_TILE, body, 0, unroll=8)
    pltpu.make_async_copy(whole_src, whole_dst, sem).wait()


def _dispatch_kernel(pos_ref, hc_ref, xs_in, xs_ref, sem):
    del xs_in
    base = pl.program_id(0) * DISPATCH_TILE
    _move_rows(lambda r: hc_ref.at[pl.ds(r, 1), :], lambda r: xs_ref.at[pl.ds(pos_ref[base + r], 1), :],
               sem.at[0], hc_ref, xs_ref.at[pl.ds(0, DISPATCH_TILE), :])


def _moe_routed_kernel(tg_ref, xs_ref, wg_ref, wu_ref, wd_ref, ys_ref, xb, acc):
    t, e = pl.program_id(0), pl.program_id(1)

    @pl.when(e == 0)
    def _():
        xb[...] = xs_ref[:, :D_MODEL].astype(BF16)
        acc[...] = jnp.zeros(acc.shape, F32)

    expert = tg_ref[t] * EXPERTS_PER_GROUP + e
    acc[...] += _expert_step(xb[...], xs_ref[:, D_MODEL:], expert, wg_ref, wu_ref, wd_ref)

    @pl.when(e == pl.num_programs(1) - 1)
    def _():
        ys_ref[...] = acc[...]


def _combine_kernel(pos_ref, x1_ref, gt2_ref, fw_ref, ys_ref, y_ref, ybuf, sem):
    i = pl.program_id(0)
    slot = i % 2

    def request(step, buf):
        base = step * DISPATCH_TILE

        def body(r, carry):
            pltpu.make_async_copy(ys_ref.at[pl.ds(pos_ref[base + r], 1), :], ybuf.at[buf, pl.ds(r, 1), :],
                                  sem.at[buf]).start()
            return carry

        lax.fori_loop(0, DISPATCH_TILE, body, 0, unroll=8)

    @pl.when(i == 0)
    def _():
        request(0, 0)

    @pl.when(i + 1 < pl.num_programs(0))
    def _():
        request(i + 1, 1 - slot)

    pltpu.make_async_copy(ys_ref.at[pl.ds(0, DISPATCH_TILE), :], ybuf.at[slot], sem.at[slot]).wait()
    y_ref[...] = _final_norm(x1_ref[...], gt2_ref[0], ybuf[slot], fw_ref[...])


def _moe_routed(hc, x1, gt2, final_w, wg, wu, wd):
    bsz, seqlen, width = hc.shape
    ntok = bsz * seqlen
    assert seqlen % DISPATCH_TILE == 0 and ntok % ROUTE_TILE == 0 and gt2.shape[1] == 1
    hcf, x1f = hc.reshape(ntok, width), x1.reshape(ntok, D_MODEL)
    gidx = hcf[:, D_MODEL].astype(jnp.int32)
    onehot = (gidx[:, None] == jnp.arange(N_EGROUPS, dtype=jnp.int32)[None, :]).astype(jnp.int32)
    csum = jnp.cumsum(onehot, axis=0)
    padded = (csum[-1] + ROUTE_TILE - 1) // ROUTE_TILE * ROUTE_TILE
    ends = jnp.cumsum(padded)
    pos = jnp.sum(onehot * (csum - 1 + (ends - padded)[None, :]), axis=1).astype(jnp.int32)
    n_tiles = ntok // ROUTE_TILE + N_EGROUPS
    tile_start = jnp.arange(n_tiles, dtype=jnp.int32) * ROUTE_TILE
    tile_group = jnp.minimum(jnp.sum((tile_start[:, None] >= ends[None, :]).astype(jnp.int32), axis=1),
                             N_EGROUPS - 1).astype(jnp.int32)
    nrows = n_tiles * ROUTE_TILE
    nsteps = ntok // DISPATCH_TILE
    any_spec = pl.BlockSpec(memory_space=pl.ANY)

    xs = pl.pallas_call(
        _dispatch_kernel,
        grid_spec=pltpu.PrefetchScalarGridSpec(
            num_scalar_prefetch=1, grid=(nsteps,),
            in_specs=[pl.BlockSpec((DISPATCH_TILE, width), lambda i, pos: (i, 0)), any_spec],
            out_specs=any_spec,
            scratch_shapes=[pltpu.SemaphoreType.DMA((1,))]),
        out_shape=jax.ShapeDtypeStruct((nrows, width), F32),
        input_output_aliases={2: 0},
        compiler_params=_cparams(("arbitrary",), 16 << 20),
        name="moe_dispatch",
    )(pos, hcf, jnp.zeros((nrows, width), F32))

    def w_spec(d1, d2):
        return pl.BlockSpec((1, d1, d2), lambda t, e, tg: (tg[t] * EXPERTS_PER_GROUP + e, 0, 0))

    vmem = (2 * ROUTE_TILE * (width + D_MODEL) * 4 + ROUTE_TILE * D_MODEL * (2 + 4)
            + 2 * 3 * D_MODEL * D_EXPERT * 2 + (16 << 20))
    ys = pl.pallas_call(
        _moe_routed_kernel,
        grid_spec=pltpu.PrefetchScalarGridSpec(
            num_scalar_prefetch=1, grid=(n_tiles, EXPERTS_PER_GROUP),
            in_specs=[pl.BlockSpec((ROUTE_TILE, width), lambda t, e, tg: (t, 0)),
                      w_spec(D_MODEL, D_EXPERT), w_spec(D_MODEL, D_EXPERT), w_spec(D_EXPERT, D_MODEL)],
            out_specs=pl.BlockSpec((ROUTE_TILE, D_MODEL), lambda t, e, tg: (t, 0)),
            scratch_shapes=[pltpu.VMEM((ROUTE_TILE, D_MODEL), BF16), pltpu.VMEM((ROUTE_TILE, D_MODEL), F32)]),
        out_shape=jax.ShapeDtypeStruct((nrows, D_MODEL), F32),
        compiler_params=_cparams(("parallel", "arbitrary"), vmem),
        name="moe_routed",
    )(tile_group, xs, wg, wu, wd)

    steps_per_batch = seqlen // DISPATCH_TILE
    y = pl.pallas_call(
        _combine_kernel,
        grid_spec=pltpu.PrefetchScalarGridSpec(
            num_scalar_prefetch=1, grid=(nsteps,),
            in_specs=[pl.BlockSpec((DISPATCH_TILE, D_MODEL), lambda i, pos: (i, 0)),
                      pl.BlockSpec((1, 1, D_MODEL), lambda i, pos: (i // steps_per_batch, 0, 0)),
                      pl.BlockSpec((1, D_MODEL), lambda i, pos: (0, 0)), any_spec],
            out_specs=pl.BlockSpec((DISPATCH_TILE, D_MODEL), lambda i, pos: (i, 0)),
            scratch_shapes=[pltpu.VMEM((2, DISPATCH_TILE, D_MODEL), F32), pltpu.SemaphoreType.DMA((2,))]),
        out_shape=jax.ShapeDtypeStruct((ntok, D_MODEL), F32),
        compiler_params=_cparams(("arbitrary",), 16 << 20),
        name="moe_combine",
    )(pos, x1f, gt2, final_w.reshape(1, -1), ys)
    return y.reshape(bsz, seqlen, D_MODEL)


def _prep_weights(lp):
    w_in = lp["w_in"]
    sizes = (SSD_INNER, CONV_DIM, SSD_HEADS, ATTN_Q, ATTN_KV, ATTN_KV, D_MODEL, D_MODEL)
    pts = [int(p) for p in np.cumsum(sizes)[:-1]]
    z, xbc, dt, q, k, v, gs, ga = jnp.split(w_in, pts, axis=1)
    dt = jnp.pad(dt, ((0, 0), (0, LANES - SSD_HEADS)))
    w_in_r = jnp.concatenate([z, xbc, q, k, v, gs, ga, dt], axis=1).astype(BF16)
    wrt = jnp.concatenate([lp["w_group"], lp["w_router"]], axis=1)
    wrt = jnp.pad(wrt, ((0, 0), (0, LANES - wrt.shape[1]))).astype(F32)
    wrt_hi = wrt.astype(BF16)
    wrt = jnp.concatenate([wrt_hi, (wrt - wrt_hi.astype(F32)).astype(BF16)], axis=1)
    brt = jnp.concatenate([lp["b_group"], lp["b_router"]])
    brt = jnp.pad(brt, (0, LANES - brt.shape[0])).astype(F32).reshape(1, LANES)
    return dict(w_in_r=w_in_r, wrt=wrt, brt=brt,
                wso=lp["w_ssd_out"].astype(BF16), wao=lp["w_attn_out"].astype(BF16), wo=lp["w_o"].astype(BF16),
                wg=lp["w_gate_e"].astype(BF16), wu=lp["w_up_e"].astype(BF16), wd=lp["w_down_e"].astype(BF16))


def _mods(mod, per_row_repeat):
    parts = jnp.split(mod, 6, axis=-1)
    if per_row_repeat:
        return [jnp.repeat(p, per_row_repeat, axis=0)[None] for p in parts]
    return [p[:, None, :] for p in parts]


def _layer_prompt(x, mod, lp, pw, final_w):
    bsz, seqlen, _ = x.shape
    sh1, sc1, gt1, sh2, sc2, gt2 = _mods(mod, 0)
    tables = _rope_tables(np.arange(seqlen))
    (z, xbc, dt, q, k, v, gs, ga, kmean, kaug, vaug) = _in_proj(x, sc1, sh1, lp["norm1_w"], pw["w_in_r"], tables, True)
    conv0 = jnp.zeros((bsz, SSD_CONV - 1, CONV_DIM), F32)
    ssm0 = jnp.zeros((bsz, SSD_HEADS, SSD_HEADDIM, SSD_STATE), F32)
    y_ssd, conv_new, ssm_new = _ssd(xbc, z, dt, conv0, ssm0, lp["conv_w"], lp["conv_b"], lp["dt_bias"],
                                    lp["a_log"], lp["d_skip"], lp["ssd_norm_w"])
    y_att = _moba_prompt(q, kmean, kaug, vaug)
    x1, hc = _mix(x, y_ssd, y_att, gs, ga, gt1, sc2, sh2, lp["norm2_w"],
                  pw["wso"], pw["wao"], pw["wo"], pw["wrt"], pw["brt"])
    y = _moe_routed(hc, x1, gt2, final_w, pw["wg"], pw["wu"], pw["wd"])
    kv_shape = (bsz, seqlen, ATTN_KV_HEADS, HEAD_DIM)
    return y, k.reshape(kv_shape), v.reshape(kv_shape), conv_new, ssm_new


def _layer_sample(x, mod, conv_buf, ssm_state, cache_k, cache_v, page_table, past_len, lp, pw, final_w):
    bsz, dec_seq, _ = x.shape
    ntok = bsz * dec_seq
    sh1, sc1, gt1, sh2, sc2, gt2 = _mods(mod, dec_seq)
    pos = past_len + (np.arange(ntok) % dec_seq)
    tables = _rope_tables(pos)
    xf = x.reshape(1, ntok, D_MODEL)
    z, xbc, dt, q, k, v, gs, ga = _in_proj(xf, sc1, sh1, lp["norm1_w"], pw["w_in_r"], tables, False)

    def per_b(t):
        return t.reshape(bsz, dec_seq, t.shape[-1])

    y_ssd, conv_new, ssm_new = _ssd(per_b(xbc), per_b(z), per_b(dt), conv_buf, ssm_state, lp["conv_w"],
                                    lp["conv_b"], lp["dt_bias"], lp["a_log"], lp["d_skip"], lp["ssd_norm_w"])
    n_pool, page = cache_k.shape[0], cache_k.shape[1]

    def position_minor(cache):
        return cache.transpose(0, 2, 3, 1).reshape(n_pool, ATTN_KV, page)

    y_att = _moba_sample(per_b(q), per_b(k), per_b(v), position_minor(cache_k), position_minor(cache_v),
                         page_table, past_len)
    x1, hc = _mix(xf, y_ssd.reshape(1, ntok, -1), y_att.reshape(1, ntok, -1), gs, ga, gt1, sc2, sh2,
                  lp["norm2_w"], pw["wso"], pw["wao"], pw["wo"], pw["wrt"], pw["brt"])
    y = _moe(hc, x1, gt2, final_w, pw["wg"], pw["wu"], pw["wd"])
    kv_shape = (bsz, dec_seq, ATTN_KV_HEADS, HEAD_DIM)
    return y.reshape(bsz, dec_seq, D_MODEL), k.reshape(kv_shape), v.reshape(kv_shape), conv_new, ssm_new


def kernel(x_prompt, x_sample, cache_k, cache_v, state_conv, state_ssm, page_table, c_prompt, c_sample,
           w_ada, b_ada, norm1_w, w_in, conv_w, conv_b, dt_bias, a_log, d_skip, ssd_norm_w,
           w_ssd_out, w_attn_out, w_o, norm2_w, w_group, b_group, w_router, b_router,
           w_gate_e, w_up_e, w_down_e, final_w):
    depth = w_in.shape[0]
    assert depth == 1, "the final RMSNorm is fused into the last layer's MoE kernel"
    nb_p, nb_s = c_prompt.shape[0], c_sample.shape[0]
    past_len = page_table.shape[1] * cache_k.shape[2]
    l = 0
    lp = dict(norm1_w=norm1_w[l], w_in=w_in[l], conv_w=conv_w[l], conv_b=conv_b[l], dt_bias=dt_bias[l],
              a_log=a_log[l], d_skip=d_skip[l], ssd_norm_w=ssd_norm_w[l], w_ssd_out=w_ssd_out[l],
              w_attn_out=w_attn_out[l], w_o=w_o[l], norm2_w=norm2_w[l], w_group=w_group[l], b_group=b_group[l],
              w_router=w_router[l], b_router=b_router[l], w_gate_e=w_gate_e[l], w_up_e=w_up_e[l],
              w_down_e=w_down_e[l])
    pw = _prep_weights(lp)
    c_all = jnp.concatenate([c_prompt, c_sample], axis=0)
    pad_rows = (-c_all.shape[0]) % SUBLANES
    mod = _ada_mod(jnp.pad(c_all, ((0, pad_rows), (0, 0))), w_ada[l], b_ada[l])
    yp, kp, vp, cp, sp = _layer_prompt(x_prompt, mod[:nb_p], lp, pw, final_w)
    ys, ks, vs, cs, ss = _layer_sample(x_sample, mod[nb_p:nb_p + nb_s], state_conv[l], state_ssm[l],
                                       cache_k[l], cache_v[l], page_table, past_len, lp, pw, final_w)
    return (yp, ys, kp[None], vp[None], cp[None], sp[None], ks[None], vs[None], cs[None], ss[None])
```

```python
import functools
import math

import numpy as np
import jax
import jax.numpy as jnp
from jax import lax
from jax.experimental import pallas as pl
from jax.experimental.pallas import tpu as pltpu

F32 = jnp.float32
BF16 = jnp.bfloat16
HIGHEST = lax.Precision.HIGHEST

D_MODEL = 1024
SSD_HEADDIM = 64
SSD_INNER = D_MODEL
SSD_HEADS = SSD_INNER // SSD_HEADDIM
SSD_GROUPS = 2
SSD_STATE = 128
SSD_CONV = 4
SSD_CHUNK = 128
SSD_BC = SSD_GROUPS * SSD_STATE
CONV_DIM = SSD_INNER + 2 * SSD_BC
ATTN_HEADS = 8
ATTN_KV_HEADS = 4
HEAD_DIM = 64
ATTN_Q = ATTN_HEADS * HEAD_DIM
ATTN_KV = ATTN_KV_HEADS * HEAD_DIM
ROT_DIM = HEAD_DIM // 4
ROPE_THETA = 500000.0
MOBA_BLOCK = 256
MOBA_TOPK = 3
N_EGROUPS = 4
EXPERTS_PER_GROUP = 4
N_EXPERTS = N_EGROUPS * EXPERTS_PER_GROUP
D_EXPERT = D_MODEL // 2
EPS = 1e-6

LANES = 128
SUBLANES = 8
VMEM_CAP = 56 << 20
NEG = -1e30
GRP = ATTN_HEADS // ATTN_KV_HEADS
ROUTE_OFF = N_EGROUPS

_SEG = {}
_off = 0
for _name, _w in (("z", SSD_INNER), ("xbc", CONV_DIM), ("q", ATTN_Q), ("k", ATTN_KV), ("v", ATTN_KV),
                  ("gs", D_MODEL), ("ga", D_MODEL), ("dt", LANES)):
    _SEG[_name] = (_off, _off + _w)
    _off += _w
IN_PAD = _off


def _cparams(semantics, vmem_bytes):
    return pltpu.CompilerParams(dimension_semantics=semantics,
                                vmem_limit_bytes=int(min(max(vmem_bytes, 16 << 20), VMEM_CAP)))


def _dot(a, b):
    return jnp.dot(a, b, preferred_element_type=F32)


def _dot_nt(a, b):
    return lax.dot_general(a, b, (((1,), (1,)), ((), ())), preferred_element_type=F32)


def _dot_exact(a, b):
    return jnp.dot(a, b, preferred_element_type=F32, precision=HIGHEST)


def _silu(x):
    return x * jax.nn.sigmoid(x)


def _mod_kernel(c_ref, w_ref, b_ref, o_ref):
    o_ref[...] = _dot_exact(_silu(c_ref[...]), w_ref[...]) + b_ref[...]


def _ada_mod(c_all, w_ada, b_ada):
    rows = c_all.shape[0]
    return pl.pallas_call(
        _mod_kernel,
        grid=(6,),
        in_specs=[pl.BlockSpec((rows, D_MODEL), lambda j: (0, 0)),
                  pl.BlockSpec((D_MODEL, D_MODEL), lambda j: (0, j)),
                  pl.BlockSpec((1, D_MODEL), lambda j: (0, j))],
        out_specs=pl.BlockSpec((rows, D_MODEL), lambda j: (0, j)),
        out_shape=jax.ShapeDtypeStruct((rows, 6 * D_MODEL), F32),
        compiler_params=_cparams(("arbitrary",), 24 << 20),
        name="ada_mod",
    )(c_all, w_ada, b_ada.reshape(1, -1))


def _rope_tables(pos):
    half = ROT_DIM // 2
    c = -2.0 * math.log(ROPE_THETA) / ROT_DIM
    c_hi = float(np.float32(c))
    j = jnp.arange(half, dtype=F32)
    inv_freq = jnp.exp(j * c_hi + j * (c - c_hi))
    ang = jnp.asarray(pos).astype(F32)[:, None] * inv_freq[None, :]
    cos, sin = jnp.cos(ang), jnp.sin(ang)
    n = pos.shape[0]
    one = jnp.ones((n, HEAD_DIM - ROT_DIM), F32)
    zero = jnp.zeros((n, half), F32)
    rest = jnp.zeros((n, HEAD_DIM - ROT_DIM), F32)
    a = jnp.concatenate([cos, cos, one], axis=1)
    s1 = jnp.concatenate([-sin, zero, rest], axis=1)
    s2 = jnp.concatenate([zero, sin, rest], axis=1)
    rep = LANES // HEAD_DIM
    return tuple(jnp.tile(t, (1, rep)) for t in (a, s1, s2))


def _inproj_kernel(*refs, attn_layout):
    (x_ref, sc_ref, sh_ref, nw_ref, w_ref, ra_ref, rs1_ref, rs2_ref,
     z_ref, xbc_ref, dt_ref, q_ref, k_ref, v_ref, gs_ref, ga_ref) = refs[:16]
    x = x_ref[0]
    h = x * lax.rsqrt(jnp.mean(x * x, axis=-1, keepdims=True) + EPS) * nw_ref[...]
    h = (h * (1.0 + sc_ref[0]) + sh_ref[0]).astype(BF16)

    def proj(name, out_ref):
        lo, hi = _SEG[name]
        step = 512
        for c in range(lo, hi, step):
            w = min(step, hi - c)
            out_ref[0, :, c - lo:c - lo + w] = _dot(h, w_ref[:, c:c + w])

    proj("z", z_ref)
    proj("xbc", xbc_ref)
    proj("gs", gs_ref)
    proj("ga", ga_ref)
    proj("dt", dt_ref)

    def rope(t):
        width = t.shape[1]
        reps = width // LANES
        a = jnp.concatenate([ra_ref[...]] * reps, axis=1)
        s1 = jnp.concatenate([rs1_ref[...]] * reps, axis=1)
        s2 = jnp.concatenate([rs2_ref[...]] * reps, axis=1)
        return t * a + pltpu.roll(t, width - ROT_DIM // 2, 1) * s1 + pltpu.roll(t, ROT_DIM // 2, 1) * s2

    lo, hi = _SEG["q"]
    q_ref[0] = rope(_dot(h, w_ref[:, lo:hi]))
    lo, hi = _SEG["k"]
    k = rope(_dot(h, w_ref[:, lo:hi]))
    k_ref[0] = k
    lo, hi = _SEG["v"]
    v = _dot(h, w_ref[:, lo:hi])
    v_ref[0] = v

    if attn_layout:
        kmean_ref, kaug_ref, vaug_ref = refs[16:19]
        tm = k.shape[0]
        blocks = tm // MOBA_BLOCK
        for hb in range(blocks):
            kmean_ref[0, hb] = jnp.mean(k[hb * MOBA_BLOCK:(hb + 1) * MOBA_BLOCK], axis=0, keepdims=True)
        lane = lax.broadcasted_iota(jnp.int32, (tm, HEAD_DIM), 1)
        row = lax.broadcasted_iota(jnp.int32, (tm, HEAD_DIM), 0)
        blk = blocks * pl.program_id(1)
        for hb in range(1, blocks):
            blk = blk + (row >= hb * MOBA_BLOCK).astype(jnp.int32)
        blk_onehot = (lane == blk).astype(F32)
        ones_col = (lane == 0).astype(F32)
        for kh in range(ATTN_KV_HEADS):
            sl = slice(kh * HEAD_DIM, (kh + 1) * HEAD_DIM)
            kaug_ref[0, kh] = jnp.concatenate([k[:, sl], blk_onehot], axis=1).astype(BF16)
            vaug_ref[0, kh] = jnp.concatenate([v[:, sl], ones_col], axis=1).astype(BF16)


def _in_proj(x, sc1, sh1, norm_w, w_in_r, tables, attn_layout):
    bsz, seqlen, _ = x.shape
    tm = 2 * MOBA_BLOCK if seqlen % (2 * MOBA_BLOCK) == 0 else MOBA_BLOCK
    assert seqlen % tm == 0
    nt = seqlen // tm
    blocks = tm // MOBA_BLOCK
    mod_rows = sc1.shape[1]
    if mod_rows == 1:
        mod_spec = pl.BlockSpec((1, 1, D_MODEL), lambda b, i: (b, 0, 0))
    else:
        mod_spec = pl.BlockSpec((1, tm, D_MODEL), lambda b, i: (b, i, 0))
    tab_spec = pl.BlockSpec((tm, LANES), lambda b, i: (i, 0))

    def tok(width):
        return pl.BlockSpec((1, tm, width), lambda b, i: (b, i, 0))

    def shp(width):
        return jax.ShapeDtypeStruct((bsz, seqlen, width), F32)

    widths = (SSD_INNER, CONV_DIM, LANES, ATTN_Q, ATTN_KV, ATTN_KV, D_MODEL, D_MODEL)
    out_specs = [tok(w) for w in widths]
    out_shape = [shp(w) for w in widths]
    if attn_layout:
        assert nt * blocks <= HEAD_DIM
        out_specs += [pl.BlockSpec((1, blocks, 1, ATTN_KV), lambda b, i: (b, i, 0, 0)),
                      pl.BlockSpec((1, ATTN_KV_HEADS, tm, LANES), lambda b, i: (b, 0, i, 0)),
                      pl.BlockSpec((1, ATTN_KV_HEADS, tm, LANES), lambda b, i: (b, 0, i, 0))]
        out_shape += [jax.ShapeDtypeStruct((bsz, nt * blocks, 1, ATTN_KV), F32),
                      jax.ShapeDtypeStruct((bsz, ATTN_KV_HEADS, seqlen, LANES), BF16),
                      jax.ShapeDtypeStruct((bsz, ATTN_KV_HEADS, seqlen, LANES), BF16)]
    vmem = D_MODEL * IN_PAD * 2 + 2 * tm * (D_MODEL + IN_PAD + 4 * LANES) * 4 + (10 << 20)
    return pl.pallas_call(
        functools.partial(_inproj_kernel, attn_layout=attn_layout),
        grid=(bsz, nt),
        in_specs=[tok(D_MODEL), mod_spec, mod_spec,
                  pl.BlockSpec((1, D_MODEL), lambda b, i: (0, 0)),
                  pl.BlockSpec((D_MODEL, IN_PAD), lambda b, i: (0, 0), pipeline_mode=pl.Buffered(1)),
                  tab_spec, tab_spec, tab_spec],
        out_specs=out_specs,
        out_shape=out_shape,
        compiler_params=_cparams(("parallel", "parallel"), vmem),
        name="in_proj",
    )(x, sc1, sh1, norm_w.reshape(1, -1), w_in_r, *tables)


def _cumsum_rows(a):
    row = lax.broadcasted_iota(jnp.int32, a.shape, 0)
    s = 1
    while s < a.shape[0]:
        a = a + jnp.where(row >= s, pltpu.roll(a, s, 0), 0.0)
        s *= 2
    return a


def _ssd_kernel(xbc_ref, z_ref, dt_ref, cbuf_ref, s0_ref, cw_ref, cb_ref, dtb_ref, alog_ref, dsk_ref, nw_ref,
                y_ref, cnew_ref, sout_ref, xext, st, ysc, *, rows):
    ch = SSD_CHUNK
    pad = SUBLANES
    c = pl.program_id(1)
    last = pl.num_programs(1) - 1

    @pl.when(c == 0)
    def _():
        xext[0:pad, :] = jnp.zeros((pad, CONV_DIM), F32)
        xext[pad - (SSD_CONV - 1):pad, :] = cbuf_ref[0]
        st[...] = s0_ref[0].T

    xext[pad:pad + rows, :] = xbc_ref[0]
    if rows < ch:
        xext[pad + rows:pad + ch, :] = jnp.zeros((ch - rows, CONV_DIM), F32)

    conv = cb_ref[...] + xext[pad:pad + ch, :] * cw_ref[SSD_CONV - 1:SSD_CONV, :]
    for w in range(SSD_CONV - 1):
        off = pad - (SSD_CONV - 1) + w
        conv = conv + xext[off:off + ch, :] * cw_ref[w:w + 1, :]
    act = _silu(conv)

    @pl.when(c == last)
    def _():
        cnew_ref[0] = xext[pad + rows - (SSD_CONV - 1):pad + rows, :]

    xext[0:pad, :] = xext[ch:ch + pad, :]

    xs = act[:, :SSD_INNER]
    dt_raw = dt_ref[0]
    if rows < ch:
        dt_raw = jnp.concatenate([dt_raw, jnp.zeros((ch - rows, LANES), F32)], axis=0)
    dt_in = dt_raw + dtb_ref[...]
    dt = jnp.maximum(dt_in, 0.0) + jnp.log1p(jnp.exp(-jnp.abs(dt_in)))
    row = lax.broadcasted_iota(jnp.int32, (ch, LANES), 0)
    col = lax.broadcasted_iota(jnp.int32, (ch, LANES), 1)
    if rows < ch:
        dt = jnp.where(row < rows, dt, 0.0)
    a_cs = _cumsum_rows(dt * (-jnp.exp(alog_ref[...])))
    a_cs_t = a_cs.T
    dt_t = dt.T
    causal = row >= col

    heads_per_pair = LANES // SSD_HEADDIM
    assert heads_per_pair == 2 and SSD_STATE == ch
    first_of_pair = col < SSD_HEADDIM
    for g in range(SSD_GROUPS):
        bg = act[:, SSD_INNER + g * SSD_STATE:SSD_INNER + (g + 1) * SSD_STATE]
        cg = act[:, SSD_INNER + SSD_BC + g * SSD_STATE:SSD_INNER + SSD_BC + (g + 1) * SSD_STATE]
        gmat = _dot_nt(cg.astype(BF16), bg.astype(BF16))
        bg_t = bg.T
        for pair in range(SSD_HEADS // SSD_GROUPS // heads_per_pair):
            hd0 = g * (SSD_HEADS // SSD_GROUPS) + pair * heads_per_pair
            sl = slice(hd0 * SSD_HEADDIM, hd0 * SSD_HEADDIM + LANES)
            x_pair = xs[:, sl]
            s_pair = st[:, sl]
            x_diag = [jnp.where(first_of_pair, x_pair, 0.0).astype(BF16),
                      jnp.where(first_of_pair, 0.0, x_pair).astype(BF16)]
            s_diag = [jnp.where(first_of_pair, s_pair, 0.0).astype(BF16),
                      jnp.where(first_of_pair, 0.0, s_pair).astype(BF16)]
            m_parts, c_parts, b_parts, decay = [], [], [], []
            for hd in range(hd0, hd0 + heads_per_pair):
                acs_col = a_cs[:, hd:hd + 1]
                acs_row = a_cs_t[hd:hd + 1, :]
                dt_row = dt_t[hd:hd + 1, :]
                acs_last = a_cs_t[hd:hd + 1, ch - 1:ch]
                lmat = jnp.where(causal, jnp.exp(acs_col - acs_row), 0.0)
                m_parts.append((gmat * lmat * dt_row).astype(BF16))
                c_parts.append((cg * jnp.exp(acs_col)).astype(BF16))
                b_parts.append((bg_t * (jnp.exp(acs_last - acs_row) * dt_row)).astype(BF16))
                decay.append(jnp.exp(acs_last))
            ysc[:, sl] = _dot(jnp.concatenate(m_parts + c_parts, axis=1), jnp.concatenate(x_diag + s_diag, axis=0))
            st[:, sl] = (jnp.where(first_of_pair[:1], decay[0], decay[1]) * s_pair
                         + _dot(jnp.concatenate(b_parts, axis=1), jnp.concatenate(x_diag, axis=0)))

    y = (ysc[0:rows, :] + xs[:rows] * dsk_ref[...]) * _silu(z_ref[0])
    gw = SSD_INNER // SSD_GROUPS
    for g in range(SSD_GROUPS):
        yg = y[:, g * gw:(g + 1) * gw]
        yg = yg * lax.rsqrt(jnp.mean(yg * yg, axis=-1, keepdims=True) + EPS)
        y_ref[0, :, g * gw:(g + 1) * gw] = yg * nw_ref[:, g * gw:(g + 1) * gw]

    @pl.when(c == last)
    def _():
        sout_ref[0] = st[...].T


def _ssd(xbc, z, dt, conv_buf, ssm_state, conv_w, conv_b, dt_bias, a_log, d_skip, norm_w):
    bsz, seqlen, _ = xbc.shape
    rows = SSD_CHUNK if seqlen % SSD_CHUNK == 0 else seqlen
    assert rows == seqlen or rows == SSD_CHUNK
    assert rows % SUBLANES == 0 and rows >= SSD_CONV - 1
    nc = seqlen // rows
    hp = SSD_HEADS * SSD_HEADDIM

    def tok(width):
        return pl.BlockSpec((1, rows, width), lambda b, c: (b, c, 0))

    def per_b(d1, d2):
        return pl.BlockSpec((1, d1, d2), lambda b, c: (b, 0, 0))

    def vec(width, r=1):
        return pl.BlockSpec((r, width), lambda b, c: (0, 0))

    def pad_lanes(v):
        return jnp.pad(v.astype(F32), (0, LANES - v.shape[0])).reshape(1, LANES)

    y, conv_new, s_new = pl.pallas_call(
        functools.partial(_ssd_kernel, rows=rows),
        grid=(bsz, nc),
        in_specs=[tok(CONV_DIM), tok(SSD_INNER), tok(LANES),
                  per_b(SSD_CONV - 1, CONV_DIM), per_b(hp, SSD_STATE),
                  vec(CONV_DIM, SSD_CONV), vec(CONV_DIM), vec(LANES), vec(LANES), vec(SSD_INNER), vec(SSD_INNER)],
        out_specs=[tok(SSD_INNER), per_b(SSD_CONV - 1, CONV_DIM), per_b(hp, SSD_STATE)],
        out_shape=[jax.ShapeDtypeStruct((bsz, seqlen, SSD_INNER), F32),
                   jax.ShapeDtypeStruct((bsz, SSD_CONV - 1, CONV_DIM), F32),
                   jax.ShapeDtypeStruct((bsz, hp, SSD_STATE), F32)],
        scratch_shapes=[pltpu.VMEM((SSD_CHUNK + SUBLANES, CONV_DIM), F32),
                        pltpu.VMEM((SSD_STATE, hp), F32),
                        pltpu.VMEM((SSD_CHUNK, SSD_INNER), F32)],
        compiler_params=_cparams(("parallel", "arbitrary"), 40 << 20),
        name="ssd",
    )(xbc, z, dt, conv_buf, ssm_state.reshape(bsz, hp, SSD_STATE),
      conv_w, conv_b.reshape(1, -1), pad_lanes(dt_bias), pad_lanes(a_log),
      jnp.repeat(d_skip.astype(F32), SSD_HEADDIM).reshape(1, -1), norm_w.reshape(1, -1))
    return y, conv_new, s_new.reshape(bsz, SSD_HEADS, SSD_HEADDIM, SSD_STATE)


def _select_topk(scores, valid, axis):
    sc = scores if valid is None else jnp.where(valid, scores, -jnp.inf)
    sel = jnp.zeros(scores.shape, F32)
    pos_f = lax.broadcasted_iota(jnp.int32, scores.shape, axis).astype(F32)
    for _ in range(MOBA_TOPK):
        m = jnp.max(sc, axis=axis, keepdims=True)
        idx = jnp.min(jnp.where(sc == m, pos_f, float(1 << 20)), axis=axis, keepdims=True)
        idx = jnp.where(m > -jnp.inf, idx, -1.0)
        hit = pos_f == idx
        sel = jnp.where(hit, 1.0, sel)
        sc = jnp.where(hit, -jnp.inf, sc)
    return sel > 0.5


def _moba_prompt_kernel(q_ref, km_ref, kaug_ref, vaug_ref, o_ref, s_a, s_b, m_sc, acc_sc, *, qblocks):
    j = pl.program_id(2)
    tq = MOBA_BLOCK
    chunk = qblocks * tq
    half = GRP * tq
    q2 = q_ref[0]
    qs = jnp.concatenate([q2[b * tq:(b + 1) * tq, g * HEAD_DIM:(g + 1) * HEAD_DIM]
                          for b in range(qblocks) for g in range(GRP)], axis=0)
    rows = qs.shape[0]
    lane = lax.broadcasted_iota(jnp.int32, (rows, LANES), 1)
    scores_t = lax.dot_general(km_ref[0, 0], qs, (((1,), (1,)), ((), ())), preferred_element_type=F32,
                               precision=HIGHEST)
    nbp = scores_t.shape[0]
    blk = lax.broadcasted_iota(jnp.int32, scores_t.shape, 0)
    col = lax.broadcasted_iota(jnp.int32, scores_t.shape, 1)
    own = qblocks * j
    for b in range(1, qblocks):
        own = own + (col >= b * half).astype(jnp.int32)
    sel_t = _select_topk(scores_t, blk < own, 0)
    pen_t = jnp.concatenate([jnp.zeros((HEAD_DIM, rows), F32), jnp.where(sel_t, 0.0, NEG),
                             jnp.full((LANES - HEAD_DIM - nbp, rows), NEG, F32)], axis=0)
    pen = pen_t.T
    q_pad = jnp.concatenate([qs * (HEAD_DIM ** -0.5), jnp.zeros((rows, LANES - HEAD_DIM), F32)], axis=1)
    q_aug = jnp.where(lane < HEAD_DIM, q_pad, pen).astype(BF16)
    q_own = q_pad.astype(BF16)

    def update(m, acc, s, pv):
        m_new = jnp.maximum(m, jnp.max(s, axis=1, keepdims=True))
        p = jnp.exp(s - m_new)
        acc = jnp.exp(m - m_new) * acc + pv(p.astype(BF16))
        return m_new, acc

    total_chunks = kaug_ref.shape[2] // chunk
    nchunks = j + 1
    npairs = nchunks // 2

    def logits(c):
        start = pl.multiple_of(jnp.minimum(c, total_chunks - 1) * chunk, chunk)
        return _dot_nt(q_aug, kaug_ref[0, 0, pl.ds(start, chunk), :])

    def values(c):
        vb = vaug_ref[0, 0, pl.ds(pl.multiple_of(c * chunk, chunk), chunk), :]
        return lambda p: _dot(p, vb)

    def body(t, carry):
        m, acc = carry
        c0 = 2 * t
        s_b[...] = logits(c0 + 1)
        m, acc = update(m, acc, s_a[...], values(c0))
        s_a[...] = logits(c0 + 2)
        m, acc = update(m, acc, s_b[...], values(c0 + 1))
        return m, acc

    def own_rows(ref, b):
        return ref[0, 0, pl.ds(pl.multiple_of((qblocks * j + b) * tq, tq), tq), :]

    s_own = jnp.concatenate([_dot_nt(q_own[b * half:(b + 1) * half], own_rows(kaug_ref, b))
                             for b in range(qblocks)], axis=0)
    r = lax.broadcasted_iota(jnp.int32, (rows, tq), 0)
    cidx = lax.broadcasted_iota(jnp.int32, (rows, tq), 1)
    s_own = jnp.where(cidx <= (r & (tq - 1)), s_own, NEG)

    def own_values(p):
        return jnp.concatenate([_dot(p[b * half:(b + 1) * half], own_rows(vaug_ref, b))
                                for b in range(qblocks)], axis=0)

    s_a[...] = logits(0)
    init = (jnp.full((rows, 1), -jnp.inf, F32), jnp.zeros((rows, LANES), F32))
    m, acc = lax.fori_loop(0, npairs, body, init)
    m_sc[...] = m
    acc_sc[...] = acc

    @pl.when(nchunks % 2 == 1)
    def _():
        m_odd, acc_odd = update(m_sc[...], acc_sc[...], s_a[...], values(nchunks - 1))
        m_sc[...] = m_odd
        acc_sc[...] = acc_odd

    _, acc = update(m_sc[...], acc_sc[...], s_own, own_values)
    out = acc[:, :HEAD_DIM] / acc[:, HEAD_DIM:HEAD_DIM + 1]
    o_ref[0] = jnp.concatenate(
        [jnp.concatenate([out[(b * GRP + g) * tq:(b * GRP + g + 1) * tq] for g in range(GRP)], axis=1)
         for b in range(qblocks)], axis=0)


def _moba_prompt(q, kmean, kaug, vaug):
    bsz, seqlen, _ = q.shape
    nb = seqlen // MOBA_BLOCK
    assert GRP * HEAD_DIM == LANES and nb <= LANES - HEAD_DIM and MOBA_BLOCK & (MOBA_BLOCK - 1) == 0
    nbp = -(-nb // SUBLANES) * SUBLANES
    km = kmean.reshape(bsz, nb, ATTN_KV_HEADS, HEAD_DIM).transpose(0, 2, 1, 3)
    km = jnp.pad(km, ((0, 0), (0, 0), (0, nbp - nb), (0, 0)))
    kv_spec = pl.BlockSpec((1, 1, seqlen, LANES), lambda b, h, i: (b, h, 0, 0))
    qblocks = 4
    chunk = qblocks * MOBA_BLOCK
    assert seqlen % chunk == 0
    rows = qblocks * GRP * MOBA_BLOCK
    vmem = 2 * 2 * seqlen * LANES * 2 + 2 * rows * chunk * 4 + (24 << 20)
    return pl.pallas_call(
        functools.partial(_moba_prompt_kernel, qblocks=qblocks),
        scratch_shapes=[pltpu.VMEM((rows, chunk), F32), pltpu.VMEM((rows, chunk), F32),
                        pltpu.VMEM((rows, 1), F32), pltpu.VMEM((rows, LANES), F32)],
        grid=(bsz, ATTN_KV_HEADS, nb // qblocks),
        in_specs=[pl.BlockSpec((1, chunk, LANES), lambda b, h, i: (b, i, h)),
                  pl.BlockSpec((1, 1, nbp, HEAD_DIM), lambda b, h, i: (b, h, 0, 0)),
                  kv_spec, kv_spec],
        out_specs=pl.BlockSpec((1, chunk, LANES), lambda b, h, i: (b, i, h)),
        out_shape=jax.ShapeDtypeStruct((bsz, seqlen, ATTN_Q), F32),
        compiler_params=_cparams(("parallel", "parallel", "arbitrary"), vmem),
        name="moba_prompt",
    )(q, km, kaug, vaug)


def _moba_sample_kernel(pt_ref, qbt_ref, qb_ref, *refs, pages_per_step, nsteps, nblk, dec_seq):
    del pt_ref
    k_refs = refs[:pages_per_step]
    v_refs = refs[pages_per_step:2 * pages_per_step]
    kn_ref, vn_ref, o_ref, s_sc, sct_sc, pen_sc, m_sc, l_sc, acc_sc = refs[2 * pages_per_step:]
    step = pl.program_id(1)
    rows = qb_ref.shape[1]
    pages_per_block = MOBA_BLOCK // LANES
    blocks_per_step = pages_per_step // pages_per_block
    qb = (qb_ref[0] * (HEAD_DIM ** -0.5)).astype(BF16)

    @pl.when(step < nsteps)
    def _():
        qbt = qbt_ref[0]
        for jb in range(blocks_per_step):
            ksum = None
            for j in range(pages_per_block):
                kpage = k_refs[jb * pages_per_block + j][0]
                ksum = kpage if ksum is None else ksum + kpage
            kmean = jnp.sum(ksum, axis=1, keepdims=True) * (1.0 / MOBA_BLOCK)
            sct_sc[pl.ds(step * blocks_per_step + jb, 1), :] = jnp.sum(qbt * kmean, axis=0, keepdims=True)
        s_sc[step] = _dot(qb, jnp.concatenate([k_ref[0].astype(BF16) for k_ref in k_refs], axis=1))

    @pl.when(step >= nsteps)
    def _():
        n = step - nsteps
        lane = lax.broadcasted_iota(jnp.int32, (rows, LANES), 1)

        @pl.when(n == 0)
        def _():
            sel_t = _select_topk(sct_sc[...], None, 0)
            pen_t = jnp.concatenate([jnp.where(sel_t, 0.0, NEG), jnp.full((LANES - nblk, LANES), NEG, F32)],
                                    axis=0)
            pen_sc[...] = pen_t.T[:rows]
            m_sc[...] = jnp.full(m_sc.shape, -jnp.inf, F32)
            l_sc[...] = jnp.zeros(l_sc.shape, F32)
            acc_sc[...] = jnp.zeros(acc_sc.shape, F32)

        def update(s, pv_fn):
            m = m_sc[...]
            m_new = jnp.maximum(m, jnp.max(s, axis=1, keepdims=True))
            p = jnp.exp(s - m_new)
            alpha = jnp.exp(m - m_new)
            l_sc[...] = alpha * l_sc[...] + jnp.sum(p, axis=1, keepdims=True)
            acc_sc[...] = alpha * acc_sc[...] + pv_fn(p.astype(BF16))
            m_sc[...] = m_new

        pen = pen_sc[...]
        pen_cols = []
        for jb in range(blocks_per_step):
            pen_col = jnp.sum(jnp.where(lane == n * blocks_per_step + jb, pen, 0.0), axis=1, keepdims=True)
            pen_cols.append(jnp.broadcast_to(pen_col, (rows, MOBA_BLOCK)))
        vt_all = jnp.concatenate([v_ref[0].astype(BF16) for v_ref in v_refs], axis=1)
        update(s_sc[n] + jnp.concatenate(pen_cols, axis=1), lambda p: _dot_nt(p, vt_all))

        @pl.when(n == nsteps - 1)
        def _():
            zpad = jnp.zeros((LANES - dec_seq, kn_ref.shape[2]), F32)
            kn = jnp.concatenate([kn_ref[0], zpad], axis=0).astype(BF16)
            vn = jnp.concatenate([vn_ref[0], zpad], axis=0).astype(BF16)
            s_own = _dot_nt(qb, kn)
            r = lax.broadcasted_iota(jnp.int32, (rows, LANES), 0)
            qpos = r & (dec_seq - 1)
            s_own = jnp.where((lane <= qpos) & (lane < dec_seq), s_own, NEG)
            update(s_own, lambda p: _dot(p, vn))
            o_ref[0] = acc_sc[...] / l_sc[...]


def _moba_sample(q, k_new, v_new, cache_kt, cache_vt, page_table, past_len):
    bsz, dec_seq, _ = q.shape
    page = cache_kt.shape[2]
    assert page == LANES and MOBA_BLOCK % page == 0
    assert past_len % MOBA_BLOCK == 0 and dec_seq == SUBLANES
    nblk = past_len // MOBA_BLOCK
    assert MOBA_TOPK <= nblk <= LANES and nblk % SUBLANES == 0
    n_pages = past_len // page
    pages_per_step = 64
    while n_pages % pages_per_step:
        pages_per_step //= 2
    assert pages_per_step * page >= MOBA_BLOCK
    nsteps = n_pages // pages_per_step
    rows = ATTN_HEADS * dec_seq
    q4 = q.reshape(bsz, dec_seq, ATTN_HEADS, HEAD_DIM).transpose(0, 2, 1, 3)
    own = (np.arange(ATTN_HEADS)[:, None] // GRP == np.arange(ATTN_KV_HEADS)[None, :]).astype(np.float32)
    qb = (q4[:, :, :, None, :] * jnp.asarray(own)[None, :, None, :, None]).reshape(bsz, rows, ATTN_KV)
    qbt = jnp.pad(qb.transpose(0, 2, 1), ((0, 0), (0, 0), (0, LANES - rows)))

    def k_spec(j):
        return pl.BlockSpec((1, ATTN_KV, page),
                            lambda b, s, pt: (pt[b, pages_per_step * jnp.minimum(s, nsteps - 1) + j], 0, 0))

    def v_spec(j):
        return pl.BlockSpec((1, ATTN_KV, page),
                            lambda b, s, pt: (pt[b, pages_per_step * jnp.maximum(s - nsteps, 0) + j], 0, 0))

    def per_b(d1, d2):
        return pl.BlockSpec((1, d1, d2), lambda b, s, pt: (b, 0, 0))

    out = pl.pallas_call(
        functools.partial(_moba_sample_kernel, pages_per_step=pages_per_step, nsteps=nsteps, nblk=nblk,
                          dec_seq=dec_seq),
        grid_spec=pltpu.PrefetchScalarGridSpec(
            num_scalar_prefetch=1, grid=(bsz, 2 * nsteps),
            in_specs=[per_b(ATTN_KV, LANES), per_b(rows, ATTN_KV)]
                     + [k_spec(j) for j in range(pages_per_step)] + [v_spec(j) for j in range(pages_per_step)]
                     + [per_b(dec_seq, ATTN_KV), per_b(dec_seq, ATTN_KV)],
            out_specs=per_b(rows, ATTN_KV),
            scratch_shapes=[pltpu.VMEM((nsteps, rows, pages_per_step * page), F32), pltpu.VMEM((nblk, LANES), F32),
                            pltpu.VMEM((rows, LANES), F32), pltpu.VMEM((rows, 1), F32), pltpu.VMEM((rows, 1), F32),
                            pltpu.VMEM((rows, ATTN_KV), F32)]),
        out_shape=jax.ShapeDtypeStruct((bsz, rows, ATTN_KV), F32),
        compiler_params=_cparams(("parallel", "arbitrary"),
                                 2 * 2 * pages_per_step * ATTN_KV * page * 4 + nsteps * rows * pages_per_step * page * 4
                                 + (16 << 20)),
        name="moba_sample_attn",
    )(page_table, qbt, qb, *([cache_kt] * pages_per_step), *([cache_vt] * pages_per_step), k_new, v_new)
    o5 = out.reshape(bsz, ATTN_HEADS, dec_seq, ATTN_KV_HEADS, HEAD_DIM)
    o4 = jnp.sum(o5 * jnp.asarray(own)[None, :, None, :, None], axis=3)
    return o4.transpose(0, 2, 1, 3).reshape(bsz, dec_seq, ATTN_Q)


def _mix_kernel(x_ref, ys_ref, ya_ref, gs_ref, ga_ref, gt1_ref, sc2_ref, sh2_ref, nw_ref,
                wso_ref, wao_ref, wo_ref, wrt_ref, brt_ref, x1_ref, hc_ref):
    merged = (jax.nn.sigmoid(gs_ref[0]) * _dot(ys_ref[0].astype(BF16), wso_ref[...])
              + jax.nn.sigmoid(ga_ref[0]) * _dot(ya_ref[0].astype(BF16), wao_ref[...]))
    x1 = x_ref[0] + gt1_ref[0] * _dot(merged.astype(BF16), wo_ref[...])
    x1_ref[0] = x1
    h2 = x1 * lax.rsqrt(jnp.mean(x1 * x1, axis=-1, keepdims=True) + EPS) * nw_ref[...]
    h2 = h2 * (1.0 + sc2_ref[0]) + sh2_ref[0]
    hc_ref[0, :, :D_MODEL] = h2

    h_hi = h2.astype(BF16)
    h_lo = (h2 - h_hi.astype(F32)).astype(BF16)
    hw = _dot(h_hi, wrt_ref[...])
    lg = hw[:, :LANES] + hw[:, LANES:] + _dot(h_lo, wrt_ref[:, :LANES]) + brt_ref[...]
    lane = lax.broadcasted_iota(jnp.int32, lg.shape, 1).astype(F32)
    big = float(1 << 20)
    ninf = -jnp.inf
    gl = jnp.where(lane < N_EGROUPS, lg, ninf)
    gmax = jnp.max(gl, axis=1, keepdims=True)
    gidx = jnp.min(jnp.where(gl == gmax, lane, big), axis=1, keepdims=True)
    g_w = 1.0 / jnp.sum(jnp.exp(gl - gmax), axis=1, keepdims=True)
    lo = ROUTE_OFF + gidx * EXPERTS_PER_GROUP
    el = jnp.where((lane >= lo) & (lane < lo + EXPERTS_PER_GROUP), lg, ninf)
    m1 = jnp.max(el, axis=1, keepdims=True)
    i1 = jnp.min(jnp.where(el == m1, lane, big), axis=1, keepdims=True)
    el2 = jnp.where(lane == i1, ninf, el)
    m2 = jnp.max(el2, axis=1, keepdims=True)
    i2 = jnp.min(jnp.where(el2 == m2, lane, big), axis=1, keepdims=True)
    e2 = jnp.exp(m2 - m1)
    den = 1.0 / (1.0 + e2)
    comb = jnp.where(lane == i1, den * g_w, 0.0) + jnp.where(lane == i2, e2 * den * g_w, 0.0)
    hc_ref[0, :, D_MODEL:] = jnp.where(lane == 0.0, gidx, comb)


def _mix(x, y_ssd, y_att, gs, ga, gt1, sc2, sh2, norm_w, wso, wao, wo, wrt, brt):
    bsz, seqlen, _ = x.shape
    tm = 512 if seqlen % 512 == 0 else 256
    assert seqlen % tm == 0
    nt = seqlen // tm
    mod_rows = gt1.shape[1]
    if mod_rows == 1:
        mod_spec = pl.BlockSpec((1, 1, D_MODEL), lambda b, i: (b, 0, 0))
    else:
        mod_spec = pl.BlockSpec((1, tm, D_MODEL), lambda b, i: (b, i, 0))

    def tok(width):
        return pl.BlockSpec((1, tm, width), lambda b, i: (b, i, 0))

    def full(d1, d2):
        return pl.BlockSpec((d1, d2), lambda b, i: (0, 0))

    vmem = 2 * 2 * (2 * D_MODEL * D_MODEL + ATTN_Q * D_MODEL) + 2 * tm * (8 * D_MODEL) * 4 + (12 << 20)
    return pl.pallas_call(
        _mix_kernel,
        grid=(bsz, nt),
        in_specs=[tok(D_MODEL), tok(SSD_INNER), tok(ATTN_Q), tok(D_MODEL), tok(D_MODEL),
                  mod_spec, mod_spec, mod_spec, full(1, D_MODEL),
                  full(SSD_INNER, D_MODEL), full(ATTN_Q, D_MODEL), full(D_MODEL, D_MODEL),
                  full(D_MODEL, 2 * LANES), full(1, LANES)],
        out_specs=[tok(D_MODEL), tok(D_MODEL + LANES)],
        out_shape=[jax.ShapeDtypeStruct((bsz, seqlen, D_MODEL), F32),
                   jax.ShapeDtypeStruct((bsz, seqlen, D_MODEL + LANES), F32)],
        compiler_params=_cparams(("parallel", "parallel"), vmem),
        name="mix",
    )(x, y_ssd, y_att, gs, ga, gt1, sc2, sh2, norm_w.reshape(1, -1), wso, wao, wo, wrt, brt)


def _expert_step(h2, comb, expert, wg_ref, wu_ref, wd_ref):
    lane = lax.broadcasted_iota(jnp.int32, comb.shape, 1)
    cw = jnp.sum(jnp.where(lane == expert + ROUTE_OFF, comb, 0.0), axis=1, keepdims=True)
    a = _dot(h2, wg_ref[0])
    u = _dot(h2, wu_ref[0])
    act = (_silu(a) * u * cw).astype(BF16)
    return _dot(act, wd_ref[0])


def _final_norm(x1, gt2, ffn, fw):
    x2 = x1 + gt2 * ffn
    return x2 * lax.rsqrt(jnp.mean(x2 * x2, axis=-1, keepdims=True) + EPS) * fw


def _moe_kernel(hc_ref, x1_ref, gt2_ref, fw_ref, wg_ref, wu_ref, wd_ref, y_ref, acc):
    e = pl.program_id(2)

    @pl.when(e == 0)
    def _():
        acc[...] = jnp.zeros(acc.shape, F32)

    acc[...] += _expert_step(hc_ref[0, :, :D_MODEL].astype(BF16), hc_ref[0, :, D_MODEL:], e,
                             wg_ref, wu_ref, wd_ref)

    @pl.when(e == pl.num_programs(2) - 1)
    def _():
        y_ref[0] = _final_norm(x1_ref[0], gt2_ref[0], acc[...], fw_ref[...])


def _moe(hc, x1, gt2, final_w, wg, wu, wd):
    bsz, seqlen, _ = x1.shape
    tm = 1024 if seqlen % 1024 == 0 else 256
    assert seqlen % tm == 0
    nt = seqlen // tm
    mod_rows = gt2.shape[1]
    if mod_rows == 1:
        mod_spec = pl.BlockSpec((1, 1, D_MODEL), lambda b, i, e: (b, 0, 0))
    else:
        mod_spec = pl.BlockSpec((1, tm, D_MODEL), lambda b, i, e: (b, i, 0))

    def tok(width):
        return pl.BlockSpec((1, tm, width), lambda b, i, e: (b, i, 0))

    vmem = 2 * tm * D_MODEL * (2 + 4 + 4) + tm * D_MODEL * 4 + 2 * 3 * D_MODEL * D_EXPERT * 2 + (16 << 20)
    return pl.pallas_call(
        _moe_kernel,
        grid=(bsz, nt, N_EXPERTS),
        in_specs=[tok(D_MODEL + LANES), tok(D_MODEL), mod_spec,
                  pl.BlockSpec((1, D_MODEL), lambda b, i, e: (0, 0)),
                  pl.BlockSpec((1, D_MODEL, D_EXPERT), lambda b, i, e: (e, 0, 0)),
                  pl.BlockSpec((1, D_MODEL, D_EXPERT), lambda b, i, e: (e, 0, 0)),
                  pl.BlockSpec((1, D_EXPERT, D_MODEL), lambda b, i, e: (e, 0, 0))],
        out_specs=tok(D_MODEL),
        out_shape=jax.ShapeDtypeStruct((bsz, seqlen, D_MODEL), F32),
        scratch_shapes=[pltpu.VMEM((tm, D_MODEL), F32)],
        compiler_params=_cparams(("parallel", "parallel", "arbitrary"), vmem),
        name="moe",
    )(hc, x1, gt2, final_w.reshape(1, -1), wg, wu, wd)


ROUTE_TILE = 512
DISPATCH_TILE = 512


def _move_rows(src_of, dst_of, sem, whole_src, whole_dst):
    def body(h, carry):
        for prio in range(2):
            r = 2 * h + prio
            pltpu.make_async_copy(src_of(r), dst_of(r), sem).start(priority=prio)
        return carry

    lax.fori_loop(0, DISPATCH_TILE // 2, body, 0, unroll=4)
    pltpu.make_async_copy(whole_src, whole_dst, sem).wait()


def _dispatch_kernel(pos_ref, hc_ref, xs_in, xs_ref, sem):
    del xs_in
    base = pl.program_id(0) * DISPATCH_TILE
    _move_rows(lambda r: hc_ref.at[pl.ds(r, 1), :], lambda r: xs_ref.at[pl.ds(pos_ref[base + r], 1), :],
               sem.at[0], hc_ref, xs_ref.at[pl.ds(0, DISPATCH_TILE), :])


def _moe_routed_kernel(tg_ref, xs_ref, wg_ref, wu_ref, wd_ref, ys_ref, xb, acc):
    t, e = pl.program_id(0), pl.program_id(1)

    @pl.when(e == 0)
    def _():
        xb[...] = xs_ref[:, :D_MODEL].astype(BF16)
        acc[...] = jnp.zeros(acc.shape, F32)

    expert = tg_ref[t] * EXPERTS_PER_GROUP + e
    acc[...] += _expert_step(xb[...], xs_ref[:, D_MODEL:], expert, wg_ref, wu_ref, wd_ref)

    @pl.when(e == pl.num_programs(1) - 1)
    def _():
        ys_ref[...] = acc[...]


def _combine_kernel(pos_ref, x1_ref, gt2_ref, fw_ref, ys_ref, y_ref, ybuf, sem):
    i = pl.program_id(0)
    slot = i % 2

    def request(step, buf):
        base = step * DISPATCH_TILE

        def body(h, carry):
            for prio in range(2):
                r = 2 * h + prio
                pltpu.make_async_copy(ys_ref.at[pl.ds(pos_ref[base + r], 1), :], ybuf.at[buf, pl.ds(r, 1), :],
                                      sem.at[buf]).start(priority=prio)
            return carry

        lax.fori_loop(0, DISPATCH_TILE // 2, body, 0, unroll=4)

    @pl.when(i == 0)
    def _():
        request(0, 0)

    @pl.when(i + 1 < pl.num_programs(0))
    def _():
        request(i + 1, 1 - slot)

    pltpu.make_async_copy(ys_ref.at[pl.ds(0, DISPATCH_TILE), :], ybuf.at[slot], sem.at[slot]).wait()
    y_ref[...] = _final_norm(x1_ref[...], gt2_ref[0], ybuf[slot], fw_ref[...])


def _moe_routed(hc, x1, gt2, final_w, wg, wu, wd):
    bsz, seqlen, width = hc.shape
    ntok = bsz * seqlen
    assert seqlen % DISPATCH_TILE == 0 and ntok % ROUTE_TILE == 0 and gt2.shape[1] == 1
    hcf, x1f = hc.reshape(ntok, width), x1.reshape(ntok, D_MODEL)
    gidx = hcf[:, D_MODEL].astype(jnp.int32)
    onehot = (gidx[:, None] == jnp.arange(N_EGROUPS, dtype=jnp.int32)[None, :]).astype(jnp.int32)
    csum = jnp.cumsum(onehot, axis=0)
    padded = (csum[-1] + ROUTE_TILE - 1) // ROUTE_TILE * ROUTE_TILE
    ends = jnp.cumsum(padded)
    pos = jnp.sum(onehot * (csum - 1 + (ends - padded)[None, :]), axis=1).astype(jnp.int32)
    n_tiles = ntok // ROUTE_TILE + N_EGROUPS
    tile_start = jnp.arange(n_tiles, dtype=jnp.int32) * ROUTE_TILE
    tile_group = jnp.minimum(jnp.sum((tile_start[:, None] >= ends[None, :]).astype(jnp.int32), axis=1),
                             N_EGROUPS - 1).astype(jnp.int32)
    nrows = n_tiles * ROUTE_TILE
    nsteps = ntok // DISPATCH_TILE
    any_spec = pl.BlockSpec(memory_space=pl.ANY)

    xs = pl.pallas_call(
        _dispatch_kernel,
        grid_spec=pltpu.PrefetchScalarGridSpec(
            num_scalar_prefetch=1, grid=(nsteps,),
            in_specs=[pl.BlockSpec((DISPATCH_TILE, width), lambda i, pos: (i, 0)), any_spec],
            out_specs=any_spec,
            scratch_shapes=[pltpu.SemaphoreType.DMA((1,))]),
        out_shape=jax.ShapeDtypeStruct((nrows, width), F32),
        input_output_aliases={2: 0},
        compiler_params=_cparams(("arbitrary",), 16 << 20),
        name="moe_dispatch",
    )(pos, hcf, jnp.zeros((nrows, width), F32))

    def w_spec(d1, d2):
        return pl.BlockSpec((1, d1, d2), lambda t, e, tg: (tg[t] * EXPERTS_PER_GROUP + e, 0, 0))

    vmem = (2 * ROUTE_TILE * (width + D_MODEL) * 4 + ROUTE_TILE * D_MODEL * (2 + 4)
            + 2 * 3 * D_MODEL * D_EXPERT * 2 + (16 << 20))
    ys = pl.pallas_call(
        _moe_routed_kernel,
        grid_spec=pltpu.PrefetchScalarGridSpec(
            num_scalar_prefetch=1, grid=(n_tiles, EXPERTS_PER_GROUP),
            in_specs=[pl.BlockSpec((ROUTE_TILE, width), lambda t, e, tg: (t, 0)),
                      w_spec(D_MODEL, D_EXPERT), w_spec(D_MODEL, D_EXPERT), w_spec(D_EXPERT, D_MODEL)],
            out_specs=pl.BlockSpec((ROUTE_TILE, D_MODEL), lambda t, e, tg: (t, 0)),
            scratch_shapes=[pltpu.VMEM((ROUTE_TILE, D_MODEL), BF16), pltpu.VMEM((ROUTE_TILE, D_MODEL), F32)]),
        out_shape=jax.ShapeDtypeStruct((nrows, D_MODEL), F32),
        compiler_params=_cparams(("parallel", "arbitrary"), vmem),
        name="moe_routed",
    )(tile_group, xs, wg, wu, wd)

    steps_per_batch = seqlen // DISPATCH_TILE
    y = pl.pallas_call(
        _combine_kernel,
        grid_spec=pltpu.PrefetchScalarGridSpec(
            num_scalar_prefetch=1, grid=(nsteps,),
            in_specs=[pl.BlockSpec((DISPATCH_TILE, D_MODEL), lambda i, pos: (i, 0)),
                      pl.BlockSpec((1, 1, D_MODEL), lambda i, pos: (i // steps_per_batch, 0, 0)),
                      pl.BlockSpec((1, D_MODEL), lambda i, pos: (0, 0)), any_spec],
            out_specs=pl.BlockSpec((DISPATCH_TILE, D_MODEL), lambda i, pos: (i, 0)),
            scratch_shapes=[pltpu.VMEM((2, DISPATCH_TILE, D_MODEL), F32), pltpu.SemaphoreType.DMA((2,))]),
        out_shape=jax.ShapeDtypeStruct((ntok, D_MODEL), F32),
        compiler_params=_cparams(("arbitrary",), 16 << 20),
        name="moe_combine",
    )(pos, x1f, gt2, final_w.reshape(1, -1), ys)
    return y.reshape(bsz, seqlen, D_MODEL)


def _prep_weights(lp):
    w_in = lp["w_in"]
    sizes = (SSD_INNER, CONV_DIM, SSD_HEADS, ATTN_Q, ATTN_KV, ATTN_KV, D_MODEL, D_MODEL)
    pts = [int(p) for p in np.cumsum(sizes)[:-1]]
    z, xbc, dt, q, k, v, gs, ga = jnp.split(w_in, pts, axis=1)
    dt = jnp.pad(dt, ((0, 0), (0, LANES - SSD_HEADS)))
    w_in_r = jnp.concatenate([z, xbc, q, k, v, gs, ga, dt], axis=1).astype(BF16)
    wrt = jnp.concatenate([lp["w_group"], lp["w_router"]], axis=1)
    wrt = jnp.pad(wrt, ((0, 0), (0, LANES - wrt.shape[1]))).astype(F32)
    wrt_hi = wrt.astype(BF16)
    wrt = jnp.concatenate([wrt_hi, (wrt - wrt_hi.astype(F32)).astype(BF16)], axis=1)
    brt = jnp.concatenate([lp["b_group"], lp["b_router"]])
    brt = jnp.pad(brt, (0, LANES - brt.shape[0])).astype(F32).reshape(1, LANES)
    return dict(w_in_r=w_in_r, wrt=wrt, brt=brt,
                wso=lp["w_ssd_out"].astype(BF16), wao=lp["w_attn_out"].astype(BF16), wo=lp["w_o"].astype(BF16),
                wg=lp["w_gate_e"].astype(BF16), wu=lp["w_up_e"].astype(BF16), wd=lp["w_down_e"].astype(BF16))


def _mods(mod, per_row_repeat):
    parts = jnp.split(mod, 6, axis=-1)
    if per_row_repeat:
        return [jnp.repeat(p, per_row_repeat, axis=0)[None] for p in parts]
    return [p[:, None, :] for p in parts]


def _layer_prompt(x, mod, lp, pw, final_w):
    bsz, seqlen, _ = x.shape
    sh1, sc1, gt1, sh2, sc2, gt2 = _mods(mod, 0)
    tables = _rope_tables(np.arange(seqlen))
    (z, xbc, dt, q, k, v, gs, ga, kmean, kaug, vaug) = _in_proj(x, sc1, sh1, lp["norm1_w"], pw["w_in_r"], tables, True)
    conv0 = jnp.zeros((bsz, SSD_CONV - 1, CONV_DIM), F32)
    ssm0 = jnp.zeros((bsz, SSD_HEADS, SSD_HEADDIM, SSD_STATE), F32)
    y_ssd, conv_new, ssm_new = _ssd(xbc, z, dt, conv0, ssm0, lp["conv_w"], lp["conv_b"], lp["dt_bias"],
                                    lp["a_log"], lp["d_skip"], lp["ssd_norm_w"])
    y_att = _moba_prompt(q, kmean, kaug, vaug)
    x1, hc = _mix(x, y_ssd, y_att, gs, ga, gt1, sc2, sh2, lp["norm2_w"],
                  pw["wso"], pw["wao"], pw["wo"], pw["wrt"], pw["brt"])
    y = _moe_routed(hc, x1, gt2, final_w, pw["wg"], pw["wu"], pw["wd"])
    kv_shape = (bsz, seqlen, ATTN_KV_HEADS, HEAD_DIM)
    return y, k.reshape(kv_shape), v.reshape(kv_shape), conv_new, ssm_new


def _layer_sample(x, mod, conv_buf, ssm_state, cache_k, cache_v, page_table, past_len, lp, pw, final_w):
    bsz, dec_seq, _ = x.shape
    ntok = bsz * dec_seq
    sh1, sc1, gt1, sh2, sc2, gt2 = _mods(mod, dec_seq)
    pos = past_len + (np.arange(ntok) % dec_seq)
    tables = _rope_tables(pos)
    xf = x.reshape(1, ntok, D_MODEL)
    z, xbc, dt, q, k, v, gs, ga = _in_proj(xf, sc1, sh1, lp["norm1_w"], pw["w_in_r"], tables, False)

    def per_b(t):
        return t.reshape(bsz, dec_seq, t.shape[-1])

    y_ssd, conv_new, ssm_new = _ssd(per_b(xbc), per_b(z), per_b(dt), conv_buf, ssm_state, lp["conv_w"],
                                    lp["conv_b"], lp["dt_bias"], lp["a_log"], lp["d_skip"], lp["ssd_norm_w"])
    n_pool, page = cache_k.shape[0], cache_k.shape[1]

    def position_minor(cache):
        return cache.transpose(0, 2, 3, 1).reshape(n_pool, ATTN_KV, page)

    y_att = _moba_sample(per_b(q), per_b(k), per_b(v), position_minor(cache_k), position_minor(cache_v),
                         page_table, past_len)
    x1, hc = _mix(xf, y_ssd.reshape(1, ntok, -1), y_att.reshape(1, ntok, -1), gs, ga, gt1, sc2, sh2,
                  lp["norm2_w"], pw["wso"], pw["wao"], pw["wo"], pw["wrt"], pw["brt"])
    y = _moe(hc, x1, gt2, final_w, pw["wg"], pw["wu"], pw["wd"])
    kv_shape = (bsz, dec_seq, ATTN_KV_HEADS, HEAD_DIM)
    return y.reshape(bsz, dec_seq, D_MODEL), k.reshape(kv_shape), v.reshape(kv_shape), conv_new, ssm_new


def kernel(x_prompt, x_sample, cache_k, cache_v, state_conv, state_ssm, page_table, c_prompt, c_sample,
           w_ada, b_ada, norm1_w, w_in, conv_w, conv_b, dt_bias, a_log, d_skip, ssd_norm_w,
           w_ssd_out, w_attn_out, w_o, norm2_w, w_group, b_group, w_router, b_router,
           w_gate_e, w_up_e, w_down_e, final_w):
    depth = w_in.shape[0]
    assert depth == 1, "the final RMSNorm is fused into the last layer's MoE kernel"
    nb_p, nb_s = c_prompt.shape[0], c_sample.shape[0]
    past_len = page_table.shape[1] * cache_k.shape[2]
    l = 0
    lp = dict(norm1_w=norm1_w[l], w_in=w_in[l], conv_w=conv_w[l], conv_b=conv_b[l], dt_bias=dt_bias[l],
              a_log=a_log[l], d_skip=d_skip[l], ssd_norm_w=ssd_norm_w[l], w_ssd_out=w_ssd_out[l],
              w_attn_out=w_attn_out[l], w_o=w_o[l], norm2_w=norm2_w[l], w_group=w_group[l], b_group=b_group[l],
              w_router=w_router[l], b_router=b_router[l], w_gate_e=w_gate_e[l], w_up_e=w_up_e[l],
              w_down_e=w_down_e[l])
    pw = _prep_weights(lp)
    c_all = jnp.concatenate([c_prompt, c_sample], axis=0)
    pad_rows = (-c_all.shape[0]) % SUBLANES
    mod = _ada_mod(jnp.pad(c_all, ((0, pad_rows), (0, 0))), w_ada[l], b_ada[l])
    yp, kp, vp, cp, sp = _layer_prompt(x_prompt, mod[:nb_p], lp, pw, final_w)
    ys, ks, vs, cs, ss = _layer_sample(x_sample, mod[nb_p:nb_p + nb_s], state_conv[l], state_ssm[l],
                                       cache_k[l], cache_v[l], page_table, past_len, lp, pw, final_w)
    return (yp, ys, kp[None], vp[None], cp[None], sp[None], ks[None], vs[None], cs[None], ss[None])
```
